```python
import math
import jax, jax.numpy as jnp
from jax import lax
import numpy as np

D_MODEL = 2048
BATCH = 4
SEQ = 2048
DEPTH = 1
DEC_BATCH = 128
DEC_SEQ = 1
PAST_LEN = 16384
PAGE_SIZE = 128

D_S5 = D_MODEL // 2
S5_GROUP = 16
S5_GROUPS = D_S5 // S5_GROUP
S5_STATE = 64
D_LRU = D_MODEL // 2
LRU_HEADS = 16
LRU_HEAD_DIM = D_LRU // LRU_HEADS
CONV_W = 4
LRU_C = 8.0
D_IN = D_S5 + 2 * D_LRU
D_MIX = D_S5 + D_LRU
D_FF = 11 * D_MODEL // 4
EPS = 1e-6

kernel_name = "hymba_s5_rglru_macaron_step"


def _rmsnorm(x, g):
    xf = x.astype(jnp.float32)
    y = xf * lax.rsqrt(jnp.mean(xf * xf, axis=-1, keepdims=True) + EPS)
    return (y * g.astype(jnp.float32)).astype(x.dtype)


def _swiglu(x, w1, w3, w2):
    return (jax.nn.silu(x @ w1) * (x @ w3)) @ w2


def _linear_scan(a, b):
    def comb(e1, e2):
        a1, b1 = e1
        a2, b2 = e2
        return a2 * a1, a2 * b1 + b2
    return lax.associative_scan(comb, (a, b), axis=0)


def _s5(u, h0, lam_re, lam_im, log_dt, b_re, b_im, c_re, c_im, d_skip, w_glu, b_glu):
    f32 = jnp.float32
    bsz, L, _ = u.shape
    uf = u.astype(f32)
    lam = lax.complex(lam_re.astype(f32), lam_im.astype(f32))
    dt = jnp.exp(log_dt.astype(f32))[:, None]
    lam_bar = jnp.exp(lam * dt)
    b_mat = lax.complex(b_re.astype(f32), b_im.astype(f32))
    b_bar = ((lam_bar - 1.0) / lam)[..., None] * b_mat
    ug = uf.reshape(bsz, L, S5_GROUPS, S5_GROUP).transpose(1, 0, 2, 3)
    bu = jnp.einsum('lbgc,gpc->lbgp', ug.astype(jnp.complex64), b_bar)
    a = jnp.broadcast_to(lam_bar, (L, 1) + lam_bar.shape)
    a_cum, h = _linear_scan(a, bu)
    h = h + a_cum * h0[None]
    c_mat = lax.complex(c_re.astype(f32), c_im.astype(f32))
    y = jnp.real(jnp.einsum('lbgp,gcp->lbgc', h, c_mat))
    y = y.transpose(1, 0, 2, 3).reshape(bsz, L, D_S5) + d_skip.astype(f32) * uf
    g = jax.nn.gelu(y).astype(u.dtype)
    out = g * jax.nn.sigmoid(g @ w_glu + b_glu)
    return out, h[-1]


def _rglru(x, gate, conv_buf, h0, conv_w, conv_b, w_a, b_a, w_x, b_x, lam_l):
    f32 = jnp.float32
    bsz, L, _ = x.shape
    xp = jnp.concatenate([conv_buf.astype(x.dtype), x], axis=1)
    xc = conv_b + sum(xp[:, k:k + L] * conv_w[k] for k in range(CONV_W))
    new_buf = xp[:, L:]
    xh = xc.reshape(bsz, L, LRU_HEADS, LRU_HEAD_DIM)
    r = jax.nn.sigmoid(jnp.einsum('blhi,hij->blhj', xh, w_a).reshape(bsz, L, D_LRU) + b_a)
    i = jax.nn.sigmoid(jnp.einsum('blhi,hij->blhj', xh, w_x).reshape(bsz, L, D_LRU) + b_x)
    log_a = -LRU_C * r.astype(f32) * jax.nn.softplus(-lam_l.astype(f32))
    a = jnp.exp(log_a)
    bx = jnp.sqrt(-jnp.expm1(2.0 * log_a)) * (i.astype(f32) * xc.astype(f32))
    a_cum, h = _linear_scan(a.transpose(1, 0, 2), bx.transpose(1, 0, 2))
    h = h + a_cum * h0.astype(f32)[None]
    y = h.transpose(1, 0, 2).astype(x.dtype) * jax.nn.gelu(gate)
    return y, h[-1], new_buf


def _layer(x, s5_h0, lru_h0, conv_buf,
           g_ffn1, w1_a, w3_a, w2_a, g_mix, w_in,
           lam_re, lam_im, log_dt, b_re, b_im, c_re, c_im, d_skip, w_glu, b_glu,
           conv_w, conv_b, w_a, b_a, w_x, b_x, lam_l,
           g_out_s5, g_out_lru, w_out, g_ffn2, w1_b, w3_b, w2_b):
    x = x + 0.5 * _swiglu(_rmsnorm(x, g_ffn1), w1_a, w3_a, w2_a)
    xn = _rmsnorm(x, g_mix)
    proj = xn @ w_in
    u_s5 = proj[..., :D_S5]
    x_lru = proj[..., D_S5:D_S5 + D_LRU]
    gate = proj[..., D_S5 + D_LRU:]
    s5_out, s5_h = _s5(u_s5, s5_h0, lam_re, lam_im, log_dt, b_re, b_im, c_re, c_im, d_skip, w_glu, b_glu)
    lru_out, lru_h, new_buf = _rglru(x_lru, gate, conv_buf, lru_h0, conv_w, conv_b, w_a, b_a, w_x, b_x, lam_l)
    merged = jnp.concatenate([_rmsnorm(s5_out, g_out_s5), _rmsnorm(lru_out, g_out_lru)], axis=-1)
    x = x + merged @ w_out
    x = x + 0.5 * _swiglu(_rmsnorm(x, g_ffn2), w1_b, w3_b, w2_b)
    return x, s5_h, lru_h, new_buf


def setup_inputs(seed: int = 0) -> dict:
    key = jax.random.key(seed)
    ks = jax.random.split(key, 40)
    f32 = jnp.float32
    nrm = lambda k, shape, s: jax.random.normal(k, shape, f32) * s
    gain = lambda k, shape: 1.0 + 0.01 * jax.random.normal(k, shape, f32)
    n = jnp.arange(S5_STATE, dtype=f32)
    a0 = jax.random.uniform(ks[30], (DEPTH, D_LRU), f32, 0.9, 0.999)
    sig = a0 ** (1.0 / LRU_C)
    return {
        "x_prompt": nrm(ks[0], (BATCH, SEQ, D_MODEL), 1.0),
        "x_sample": nrm(ks[1], (DEC_BATCH, DEC_SEQ, D_MODEL), 1.0),
        "state_s5_re": nrm(ks[2], (DEPTH, DEC_BATCH, S5_GROUPS, S5_STATE), 0.1),
        "state_s5_im": nrm(ks[3], (DEPTH, DEC_BATCH, S5_GROUPS, S5_STATE), 0.1),
        "state_lru_h": nrm(ks[4], (DEPTH, DEC_BATCH, D_LRU), 0.5),
        "state_lru_conv": nrm(ks[5], (DEPTH, DEC_BATCH, CONV_W - 1, D_LRU), 1.0),
        "g_ffn1": gain(ks[6], (DEPTH, D_MODEL)),
        "w1_a": nrm(ks[7], (DEPTH, D_MODEL, D_FF), D_MODEL ** -0.5),
        "w3_a": nrm(ks[8], (DEPTH, D_MODEL, D_FF), D_MODEL ** -0.5),
        "w2_a": nrm(ks[9], (DEPTH, D_FF, D_MODEL), D_FF ** -0.5),
        "g_mix": gain(ks[10], (DEPTH, D_MODEL)),
        "w_in": nrm(ks[11], (DEPTH, D_MODEL, D_IN), D_MODEL ** -0.5),
        "lam_re": -0.5 + nrm(ks[12], (DEPTH, S5_GROUPS, S5_STATE), 0.01),
        "lam_im": math.pi * n + nrm(ks[13], (DEPTH, S5_GROUPS, S5_STATE), 0.01),
        "log_dt": jax.random.uniform(ks[14], (DEPTH, S5_GROUPS), f32, math.log(1e-3), math.log(1e-1)),
        "b_re": nrm(ks[15], (DEPTH, S5_GROUPS, S5_STATE, S5_GROUP), (2 * S5_GROUP) ** -0.5),
        "b_im": nrm(ks[16], (DEPTH, S5_GROUPS, S5_STATE, S5_GROUP), (2 * S5_GROUP) ** -0.5),
        "c_re": nrm(ks[17], (DEPTH, S5_GROUPS, S5_GROUP, S5_STATE), (2 * S5_STATE) ** -0.5),
        "c_im": nrm(ks[18], (DEPTH, S5_GROUPS, S5_GROUP, S5_STATE), (2 * S5_STATE) ** -0.5),
        "d_skip": nrm(ks[19], (DEPTH, D_S5), 1.0),
        "w_glu": nrm(ks[20], (DEPTH, D_S5, D_S5), D_S5 ** -0.5),
        "b_glu": nrm(ks[21], (DEPTH, D_S5), 0.01),
        "conv_w": nrm(ks[22], (DEPTH, CONV_W, D_LRU), CONV_W ** -0.5),
        "conv_b": nrm(ks[23], (DEPTH, D_LRU), 0.01),
        "w_a": nrm(ks[24], (DEPTH, LRU_HEADS, LRU_HEAD_DIM, LRU_HEAD_DIM), LRU_HEAD_DIM ** -0.5),
        "b_a": nrm(ks[25], (DEPTH, D_LRU), 0.01),
        "w_x": nrm(ks[26], (DEPTH, LRU_HEADS, LRU_HEAD_DIM, LRU_HEAD_DIM), LRU_HEAD_DIM ** -0.5),
        "b_x": nrm(ks[27], (DEPTH, D_LRU), 0.01),
        "lam_l": jnp.log(sig) - jnp.log1p(-sig),
        "g_out_s5": gain(ks[28], (DEPTH, D_S5)),
        "g_out_lru": gain(ks[29], (DEPTH, D_LRU)),
        "w_out": nrm(ks[31], (DEPTH, D_MIX, D_MODEL), D_MIX ** -0.5),
        "g_ffn2": gain(ks[32], (DEPTH, D_MODEL)),
        "w1_b": nrm(ks[33], (DEPTH, D_MODEL, D_FF), D_MODEL ** -0.5),
        "w3_b": nrm(ks[34], (DEPTH, D_MODEL, D_FF), D_MODEL ** -0.5),
        "w2_b": nrm(ks[35], (DEPTH, D_FF, D_MODEL), D_FF ** -0.5),
        "g_final": gain(ks[36], (D_MODEL,)),
    }


def reference(x_prompt, x_sample, state_s5_re, state_s5_im, state_lru_h, state_lru_conv,
              g_ffn1, w1_a, w3_a, w2_a, g_mix, w_in,
              lam_re, lam_im, log_dt, b_re, b_im, c_re, c_im, d_skip, w_glu, b_glu,
              conv_w, conv_b, w_a, b_a, w_x, b_x, lam_l,
              g_out_s5, g_out_lru, w_out, g_ffn2, w1_b, w3_b, w2_b, g_final):
    f32 = jnp.float32

    def run(x, s5_re0, s5_im0, lru0, conv0):
        s5_re_new, s5_im_new, lru_new, conv_new = [], [], [], []
        for l in range(DEPTH):
            h0 = lax.complex(s5_re0[l].astype(f32), s5_im0[l].astype(f32))
            x, s5_h, lru_h, buf = _layer(
                x, h0, lru0[l], conv0[l],
                g_ffn1[l], w1_a[l], w3_a[l], w2_a[l], g_mix[l], w_in[l],
                lam_re[l], lam_im[l], log_dt[l], b_re[l], b_im[l], c_re[l], c_im[l], d_skip[l], w_glu[l], b_glu[l],
                conv_w[l], conv_b[l], w_a[l], b_a[l], w_x[l], b_x[l], lam_l[l],
                g_out_s5[l], g_out_lru[l], w_out[l], g_ffn2[l], w1_b[l], w3_b[l], w2_b[l])
            s5_re_new.append(jnp.real(s5_h).astype(s5_re0.dtype))
            s5_im_new.append(jnp.imag(s5_h).astype(s5_im0.dtype))
            lru_new.append(lru_h.astype(lru0.dtype))
            conv_new.append(buf.astype(conv0.dtype))
        y = _rmsnorm(x, g_final)
        return y, jnp.stack(s5_re_new), jnp.stack(s5_im_new), jnp.stack(lru_new), jnp.stack(conv_new)

    bp = x_prompt.shape[0]
    dt_p = x_prompt.dtype
    zeros_s5 = jnp.zeros((DEPTH, bp, S5_GROUPS, S5_STATE), dt_p)
    zeros_lru = jnp.zeros((DEPTH, bp, D_LRU), dt_p)
    zeros_conv = jnp.zeros((DEPTH, bp, CONV_W - 1, D_LRU), dt_p)
    y_prompt, p_s5_re, p_s5_im, p_lru_h, p_lru_conv = run(x_prompt, zeros_s5, zeros_s5, zeros_lru, zeros_conv)
    y_sample, s_s5_re, s_s5_im, s_lru_h, s_lru_conv = run(x_sample, state_s5_re, state_s5_im, state_lru_h, state_lru_conv)
    return (y_prompt, y_sample, p_s5_re, p_s5_im, p_lru_h, p_lru_conv, s_s5_re, s_s5_im, s_lru_h, s_lru_conv)
```

```python
import functools

import jax
import jax.numpy as jnp
from jax import lax
from jax.experimental import pallas as pl
from jax.experimental.pallas import tpu as pltpu

F32 = jnp.float32
BF16 = jnp.bfloat16

D_MODEL = 2048
D_S5 = 1024
S5_GROUP = 16
S5_GROUPS = 64
S5_STATE = 64
D_LRU = 1024
LRU_HEADS = 16
LRU_HEAD_DIM = 64
CONV_W = 4
LRU_C = 8.0
D_FF = 5632
EPS = 1e-6

CHUNK = 16
CW = CHUNK * S5_GROUP
N_DOUBLINGS = 7

VMEM_LIMIT = 56 * 1024 * 1024

NN = (((1,), (0,)), ((), ()))
NT = (((1,), (1,)), ((), ()))


def _rms(x, g):
    return x * lax.rsqrt(jnp.mean(x * x, axis=-1, keepdims=True) + EPS) * g


def _split(x):
    hi = x.astype(BF16)
    lo = (x - hi.astype(F32)).astype(BF16)
    return hi, lo


def _dot3(a, b, dims=NN):
    ah, al = _split(a)
    bh, bl = _split(b)
    d = functools.partial(lax.dot_general, dimension_numbers=dims, preferred_element_type=F32)
    return d(ah, bh) + d(al, bh) + d(ah, bl)


def _gelu(x):
    return jax.nn.gelu(x, approximate=True)


def _ffn_body(*refs, final_norm):
    if final_norm:
        x_ref, g_ref, w1_ref, w3_ref, w2_ref, gf_ref, o_ref, xn_ref = refs
    else:
        x_ref, g_ref, w1_ref, w3_ref, w2_ref, o_ref, xn_ref = refs
    f = pl.program_id(1)

    @pl.when(f == 0)
    def _():
        x = x_ref[...]
        xn_ref[...] = _rms(x, g_ref[...]).astype(BF16)
        o_ref[...] = x

    xn = xn_ref[...]
    a = jnp.dot(xn, w1_ref[...], preferred_element_type=F32)
    b = jnp.dot(xn, w3_ref[...], preferred_element_type=F32)
    h = (a * jax.nn.sigmoid(a) * b).astype(BF16)
    o_ref[...] += 0.5 * jnp.dot(h, w2_ref[...], preferred_element_type=F32)

    if final_norm:
        @pl.when(f == pl.num_programs(1) - 1)
        def _():
            o_ref[...] = _rms(o_ref[...], gf_ref[...])


def _ffn(x, g, w1, w3, w2, g_final=None, *, tm=832, tf=512):
    n = x.shape[0]
    final_norm = g_final is not None
    in_specs = [
        pl.BlockSpec((tm, D_MODEL), lambda i, f: (i, 0)),
        pl.BlockSpec((1, D_MODEL), lambda i, f: (0, 0)),
        pl.BlockSpec((D_MODEL, tf), lambda i, f: (0, f)),
        pl.BlockSpec((D_MODEL, tf), lambda i, f: (0, f)),
        pl.BlockSpec((tf, D_MODEL), lambda i, f: (f, 0)),
    ]
    args = [x, g, w1, w3, w2]
    if final_norm:
        in_specs.append(pl.BlockSpec((1, D_MODEL), lambda i, f: (0, 0)))
        args.append(g_final)
    return pl.pallas_call(
        functools.partial(_ffn_body, final_norm=final_norm),
        grid=(n // tm, D_FF // tf),
        in_specs=in_specs,
        out_specs=pl.BlockSpec((tm, D_MODEL), lambda i, f: (i, 0)),
        out_shape=jax.ShapeDtypeStruct((n, D_MODEL), F32),
        scratch_shapes=[pltpu.VMEM((tm, D_MODEL), BF16)],
        compiler_params=pltpu.CompilerParams(
            dimension_semantics=("parallel", "arbitrary"), vmem_limit_bytes=VMEM_LIMIT),
        name="ffn_final" if final_norm else "ffn",
    )(*args)


def _inproj_body(x_ref, g_ref, w_ref, o_ref, xn_ref):
    @pl.when(pl.program_id(1) == 0)
    def _():
        xn_ref[...] = _rms(x_ref[...], g_ref[...]).astype(BF16)

    o_ref[...] = jnp.dot(xn_ref[...], w_ref[...], preferred_element_type=F32)


def _inproj(x, g, w, *, tm=832, tn=1024):
    n = x.shape[0]
    d_out = w.shape[1]
    return pl.pallas_call(
        _inproj_body,
        grid=(n // tm, d_out // tn),
        in_specs=[
            pl.BlockSpec((tm, D_MODEL), lambda i, j: (i, 0)),
            pl.BlockSpec((1, D_MODEL), lambda i, j: (0, 0)),
            pl.BlockSpec((D_MODEL, tn), lambda i, j: (0, j)),
        ],
        out_specs=pl.BlockSpec((tm, tn), lambda i, j: (i, j)),
        out_shape=jax.ShapeDtypeStruct((n, d_out), F32),
        scratch_shapes=[pltpu.VMEM((tm, D_MODEL), BF16)],
        compiler_params=pltpu.CompilerParams(
            dimension_semantics=("parallel", "arbitrary"), vmem_limit_bytes=VMEM_LIMIT),
        name="inproj",
    )(x, g, w)


def _s5_prep_body(lre_ref, lim_ref, ldt_ref, bre_ref, bim_ref, cre_ref, cim_ref,
                  are_ref, aim_ref, kt_ref, cpre_ref, cpimn_ref,
                  qr_ref, qi_ref, lbr_ref, lbi_ref):
    lam_re = lre_ref[...]
    lam_im = lim_ref[...]
    dt = jnp.exp(ldt_ref[...])
    mag = jnp.exp(lam_re * dt)
    ang = lam_im * dt
    lbr = mag * jnp.cos(ang)
    lbi = mag * jnp.sin(ang)
    lbr_ref[...] = lbr
    lbi_ref[...] = lbi
    nr = lbr - 1.0
    den = lam_re * lam_re + lam_im * lam_im
    cr = (nr * lam_re + lbi * lam_im) / den
    ci = (lbi * lam_re - nr * lam_im) / den
    b_re = bre_ref[...]
    b_im = bim_ref[...]
    bbr = cr * b_re - ci * b_im
    bbi = cr * b_im + ci * b_re
    c_re = cre_ref[...]
    c_im = cim_ref[...]

    pr = jnp.ones_like(lbr)
    pi = jnp.zeros_like(lbr)
    for d in range(CHUNK):
        rows = slice(d * S5_GROUP, (d + 1) * S5_GROUP)
        are_ref[:, rows, :] = bbr * pr - bbi * pi
        aim_ref[:, rows, :] = bbr * pi + bbi * pr
        pr, pi = pr * lbr - pi * lbi, pr * lbi + pi * lbr
        cpre_ref[:, rows, :] = c_re * pr - c_im * pi
        cpimn_ref[:, rows, :] = -(c_re * pi + c_im * pr)

    qr, qi = pr, pi
    for j in range(N_DOUBLINGS):
        qr_ref[:, j:j + 1, :] = qr
        qi_ref[:, j:j + 1, :] = qi
        qr, qi = qr * qr - qi * qi, 2.0 * qr * qi
    qr_ref[:, N_DOUBLINGS:, :] = jnp.zeros_like(qr)
    qi_ref[:, N_DOUBLINGS:, :] = jnp.zeros_like(qi)

    def body(g, carry):
        a_re = are_ref[g]
        a_im = aim_ref[g]
        k = (lax.dot_general(a_re, cre_ref[g], NT, precision=lax.Precision.HIGHEST,
                             preferred_element_type=F32)
             - lax.dot_general(a_im, cim_ref[g], NT, precision=lax.Precision.HIGHEST,
                               preferred_element_type=F32))
        kt_ref[g] = k
        return carry

    lax.fori_loop(0, S5_GROUPS, body, 0)


def _s5_prep(lam_re, lam_im, log_dt, b_re, b_im, c_re, c_im):
    g, p = S5_GROUPS, S5_STATE
    lre = lam_re.reshape(g, 1, p)
    lim = lam_im.reshape(g, 1, p)
    ldt = jnp.broadcast_to(log_dt.reshape(g, 1, 1), (g, 1, p))
    bre = jnp.transpose(b_re, (0, 2, 1))
    bim = jnp.transpose(b_im, (0, 2, 1))
    sd = jax.ShapeDtypeStruct
    outs = pl.pallas_call(
        _s5_prep_body,
        out_shape=(
            sd((g, CW, p), F32), sd((g, CW, p), F32),
            sd((g, CW, S5_GROUP), F32),
            sd((g, CW, p), F32), sd((g, CW, p), F32),
            sd((g, 8, p), F32), sd((g, 8, p), F32),
            sd((g, 1, p), F32), sd((g, 1, p), F32),
        ),
        compiler_params=pltpu.CompilerParams(vmem_limit_bytes=VMEM_LIMIT),
        name="s5_prep",
    )(lre, lim, ldt, bre, bim, c_re, c_im)
    return outs


def _cmul_add(h, hs, ar, ai):
    return h + hs * ar + pltpu.roll(hs, S5_STATE, axis=1) * ai


def _s5p_body(u_ref, m_ref, bp_ref, cp_ref, ar_ref, ai_ref, y_ref, hf_ref, *, gb, nb, nk):
    rows = lax.broadcasted_iota(jnp.int32, (nb * nk, 2 * S5_STATE), 0) & (nk - 1)
    for j in range(gb):
        u = u_ref[j]
        uh, ul = _split(u)
        mh, ml = _split(m_ref[j])
        bh, bl = _split(bp_ref[j])
        d = functools.partial(jnp.dot, preferred_element_type=F32)
        y = d(uh, mh) + d(ul, mh) + d(uh, ml)
        h = d(uh, bh) + d(ul, bh) + d(uh, bl)
        ar = ar_ref[j]
        ai = ai_ref[j]
        for k in range(N_DOUBLINGS):
            sh = 1 << k
            hs = jnp.where(rows >= sh, pltpu.roll(h, sh, axis=0), 0.0)
            h = _cmul_add(h, hs, ar[k:k + 1], ai[k:k + 1])
        for b in range(nb):
            hf_ref[j, b:b + 1, :] = h[(b + 1) * nk - 1:(b + 1) * nk, :]
        hprev = jnp.where(rows >= 1, pltpu.roll(h, 1, axis=0), 0.0)
        y_ref[j] = y + _dot3(hprev, cp_ref[j], NT)


def _s5_prompt(ug, m, bpow, cpt, ar, ai, *, nb, nk, gb=8):
    g = S5_GROUPS
    rows = nb * nk
    blk = lambda r, c: pl.BlockSpec((gb, r, c), lambda i: (i, 0, 0))
    return pl.pallas_call(
        functools.partial(_s5p_body, gb=gb, nb=nb, nk=nk),
        grid=(g // gb,),
        in_specs=[blk(rows, CW), blk(CW, CW), blk(CW, 2 * S5_STATE), blk(CW, 2 * S5_STATE),
                  blk(8, 2 * S5_STATE), blk(8, 2 * S5_STATE)],
        out_specs=(blk(rows, CW), blk(nb, 2 * S5_STATE)),
        out_shape=(jax.ShapeDtypeStruct((g, rows, CW), F32),
                   jax.ShapeDtypeStruct((g, nb, 2 * S5_STATE), F32)),
        compiler_params=pltpu.CompilerParams(
            dimension_semantics=("parallel",), vmem_limit_bytes=VMEM_LIMIT),
        name="s5_prompt",
    )(ug, m, bpow, cpt, ar, ai)


def _s5d_body(u_ref, h0_ref, bd_ref, cd_ref, lr_ref, li_ref, y_ref, h_ref, *, gb):
    for j in range(gb):
        h0 = h0_ref[j]
        bu = _dot3(u_ref[j], bd_ref[j])
        h = _cmul_add(bu, h0, lr_ref[j], li_ref[j])
        h_ref[j] = h
        y_ref[j] = _dot3(h, cd_ref[j], NT)


def _s5_decode(ud, h0, bd, cd, lr, li, *, gb=8):
    g, nbatch = S5_GROUPS, ud.shape[1]
    blk = lambda r, c: pl.BlockSpec((gb, r, c), lambda i: (i, 0, 0))
    return pl.pallas_call(
        functools.partial(_s5d_body, gb=gb),
        grid=(g // gb,),
        in_specs=[blk(nbatch, S5_GROUP), blk(nbatch, 2 * S5_STATE), blk(S5_GROUP, 2 * S5_STATE),
                  blk(S5_GROUP, 2 * S5_STATE), blk(1, 2 * S5_STATE), blk(1, 2 * S5_STATE)],
        out_specs=(blk(nbatch, S5_GROUP), blk(nbatch, 2 * S5_STATE)),
        out_shape=(jax.ShapeDtypeStruct((g, nbatch, S5_GROUP), F32),
                   jax.ShapeDtypeStruct((g, nbatch, 2 * S5_STATE), F32)),
        compiler_params=pltpu.CompilerParams(dimension_semantics=("parallel",)),
        name="s5_decode",
    )(ud, h0, bd, cd, lr, li)


def _lru_gates(xc, wa_ref, wx_ref, ba, bx, lam):
    xcb = xc.astype(BF16)
    nblk = D_LRU // 256
    r_parts, i_parts = [], []
    for k in range(nblk):
        xk = xcb[:, k * 256:(k + 1) * 256]
        r_parts.append(jnp.dot(xk, wa_ref[k], preferred_element_type=F32))
        i_parts.append(jnp.dot(xk, wx_ref[k], preferred_element_type=F32))
    r = jax.nn.sigmoid(jnp.concatenate(r_parts, axis=1) + ba)
    i = jax.nn.sigmoid(jnp.concatenate(i_parts, axis=1) + bx)
    z = -lam
    softplus = jnp.maximum(z, 0.0) + jnp.log1p(jnp.exp(-jnp.abs(z)))
    log_a = (-LRU_C * softplus) * r
    a = jnp.exp(log_a)
    mult = jnp.sqrt(-jnp.tanh(log_a) * (a * a + 1.0))
    return a, mult * (i * xc)


def _lru_prompt_body(xl_ref, gate_ref, cw_ref, cb_ref, wa_ref, wx_ref, ba_ref, bx_ref, lam_ref,
                     o_ref, hl_ref, xbuf_ref, carry_ref, *, tt):
    t = pl.program_id(1)

    @pl.when(t == 0)
    def _():
        xbuf_ref[0:8, :] = jnp.zeros((8, D_LRU), F32)
        carry_ref[...] = jnp.zeros((8, D_LRU), F32)

    x = xl_ref[...]
    xbuf_ref[8:8 + tt, :] = x
    cw = cw_ref[...]
    xc = (cb_ref[...] + xbuf_ref[5:5 + tt, :] * cw[0:1] + xbuf_ref[6:6 + tt, :] * cw[1:2]
          + xbuf_ref[7:7 + tt, :] * cw[2:3] + x * cw[3:4])
    xbuf_ref[0:8, :] = x[tt - 8:tt, :]

    a, b = _lru_gates(xc, wa_ref, wx_ref, ba_ref[...], bx_ref[...], lam_ref[...])

    rows = lax.broadcasted_iota(jnp.int32, (tt, D_LRU), 0)
    sh = 1
    while sh < tt:
        if sh < 8:
            m = rows >= sh
            b = jnp.where(m, b + a * pltpu.roll(b, sh, axis=0), b)
            a = jnp.where(m, a * pltpu.roll(a, sh, axis=0), a)
        else:
            b = jnp.concatenate([b[:sh], b[sh:] + a[sh:] * b[:tt - sh]], axis=0)
            a = jnp.concatenate([a[:sh], a[sh:] * a[:tt - sh]], axis=0)
        sh *= 2
    h = b + a * carry_ref[0:1, :]
    hlast = h[tt - 1:tt, :]
    carry_ref[...] = jnp.broadcast_to(hlast, (8, D_LRU))
    hl_ref[0] = hlast
    o_ref[...] = h * _gelu(gate_ref[...])


def _lru_prompt(proj, cw, cb, wa, wx, ba, bx, lam, *, nb, seq, tt=256):
    nt = seq // tt
    vec = lambda r: pl.BlockSpec((r, D_LRU), lambda b, t: (0, 0))
    wspec = pl.BlockSpec((D_LRU // 256, 256, 256), lambda b, t: (0, 0, 0))
    return pl.pallas_call(
        functools.partial(_lru_prompt_body, tt=tt),
        grid=(nb, nt),
        in_specs=[
            pl.BlockSpec((tt, D_LRU), lambda b, t: (b * nt + t, 1)),
            pl.BlockSpec((tt, D_LRU), lambda b, t: (b * nt + t, 2)),
            vec(CONV_W), vec(1), wspec, wspec, vec(1), vec(1), vec(1),
        ],
        out_specs=(pl.BlockSpec((tt, D_LRU), lambda b, t: (b * nt + t, 0)),
                   pl.BlockSpec((1, 1, D_LRU), lambda b, t: (b, 0, 0))),
        out_shape=(jax.ShapeDtypeStruct((nb * seq, D_LRU), F32),
                   jax.ShapeDtypeStruct((nb, 1, D_LRU), F32)),
        scratch_shapes=[pltpu.VMEM((tt + 8, D_LRU), F32), pltpu.VMEM((8, D_LRU), F32)],
        compiler_params=pltpu.CompilerParams(
            dimension_semantics=("parallel", "arbitrary"), vmem_limit_bytes=VMEM_LIMIT),
        name="lru_prompt",
    )(proj, proj, cw, cb, wa, wx, ba, bx, lam)


def _lru_decode_body(xl_ref, gate_ref, c0_ref, c1_ref, c2_ref, h0_ref, cw_ref, cb_ref,
                     wa_ref, wx_ref, ba_ref, bx_ref, lam_ref, o_ref, h_ref):
    x = xl_ref[...]
    cw = cw_ref[...]
    xc = (cb_ref[...] + c0_ref[...] * cw[0:1] + c1_ref[...] * cw[1:2]
          + c2_ref[...] * cw[2:3] + x * cw[3:4])
    a, b = _lru_gates(xc, wa_ref, wx_ref, ba_ref[...], bx_ref[...], lam_ref[...])
    h = a * h0_ref[...] + b
    h_ref[...] = h
    o_ref[...] = h * _gelu(gate_ref[...])


def _lru_decode(proj_d, c0, c1, c2, h0, cw, cb, wa, wx, ba, bx, lam):
    nd = proj_d.shape[0]
    full = lambda r: pl.BlockSpec((r, D_LRU), lambda i: (0, 0))
    wspec = pl.BlockSpec((D_LRU // 256, 256, 256), lambda i: (0, 0, 0))
    return pl.pallas_call(
        _lru_decode_body,
        grid=(1,),
        in_specs=[
            pl.BlockSpec((nd, D_LRU), lambda i: (0, 1)),
            pl.BlockSpec((nd, D_LRU), lambda i: (0, 2)),
            full(nd), full(nd), full(nd), full(nd),
            full(CONV_W), full(1), wspec, wspec, full(1), full(1), full(1),
        ],
        out_specs=(full(nd), full(nd)),
        out_shape=(jax.ShapeDtypeStruct((nd, D_LRU), F32),
                   jax.ShapeDtypeStruct((nd, D_LRU), F32)),
        name="lru_decode",
    )(proj_d, proj_d, c0, c1, c2, h0, cw, cb, wa, wx, ba, bx, lam)


def _mix_body(ys_ref, u_ref, lru_ref, x_ref, dsk_ref, wg_ref, bg_ref, gs_ref, gl_ref, wo_ref,
              o_ref):
    yy = ys_ref[...] + dsk_ref[...] * u_ref[...]
    g = _gelu(yy)
    z = jnp.dot(g.astype(BF16), wg_ref[...], preferred_element_type=F32) + bg_ref[...]
    s5o = g * jax.nn.sigmoid(z)
    n1 = _rms(s5o, gs_ref[...]).astype(BF16)
    n2 = _rms(lru_ref[...], gl_ref[...]).astype(BF16)
    o_ref[...] = (x_ref[...]
                  + jnp.dot(n1, wo_ref[0:D_S5, :], preferred_element_type=F32)
                  + jnp.dot(n2, wo_ref[D_S5:, :], preferred_element_type=F32))


def _mix(ys, proj, lru, x1, dsk, wg, bg, gs, gl, wo, *, tm=416):
    n = x1.shape[0]
    row = lambda c: pl.BlockSpec((tm, c), lambda i: (i, 0))
    vec = lambda c: pl.BlockSpec((1, c), lambda i: (0, 0))
    return pl.pallas_call(
        _mix_body,
        grid=(n // tm,),
        in_specs=[row(D_S5), row(D_S5), row(D_LRU), row(D_MODEL), vec(D_S5),
                  pl.BlockSpec((D_S5, D_S5), lambda i: (0, 0)), vec(D_S5), vec(D_S5), vec(D_LRU),
                  pl.BlockSpec((D_MODEL, D_MODEL), lambda i: (0, 0))],
        out_specs=row(D_MODEL),
        out_shape=jax.ShapeDtypeStruct((n, D_MODEL), F32),
        compiler_params=pltpu.CompilerParams(
            dimension_semantics=("parallel",), vmem_limit_bytes=VMEM_LIMIT),
        name="mix",
    )(ys, proj, lru, x1, dsk, wg, bg, gs, gl, wo)


def _block_diag4(w):
    w4 = w.reshape(LRU_HEADS // 4, 4, LRU_HEAD_DIM, LRU_HEAD_DIM)
    eye = jnp.eye(4, dtype=w.dtype)
    return jnp.einsum("kaij,ab->kaibj", w4, eye).reshape(LRU_HEADS // 4, 256, 256)


def kernel(x_prompt, x_sample, state_s5_re, state_s5_im, state_lru_h, state_lru_conv, g_ffn1, w1_a, w3_a, w2_a, g_mix, w_in, lam_re, lam_im, log_dt, b_re, b_im, c_re, c_im, d_skip, w_glu, b_glu, conv_w, conv_b, w_a, b_a, w_x, b_x, lam_l, g_out_s5, g_out_lru, w_out, g_ffn2, w1_b, w3_b, w2_b, g_final):
    nb, seq, _ = x_prompt.shape
    nd = x_sample.shape[0]
    n_p = nb * seq
    nk = seq // CHUNK
    g, p = S5_GROUPS, S5_STATE
    row = lambda v: v.reshape(1, -1)

    x = jnp.concatenate([x_prompt.reshape(n_p, D_MODEL), x_sample.reshape(nd, D_MODEL)], axis=0)

    x1 = _ffn(x, row(g_ffn1[0]), w1_a[0].astype(BF16), w3_a[0].astype(BF16), w2_a[0].astype(BF16))
    proj = _inproj(x1, row(g_mix[0]), w_in[0].astype(BF16))

    (a_re, a_im, kt, cp_re, cp_imn, q_re, q_im, lb_re, lb_im) = _s5_prep(
        lam_re[0], lam_im[0], log_dt[0], b_re[0], b_im[0], c_re[0], c_im[0])
    a_cat = jnp.concatenate([a_re, a_im], axis=-1)
    bpow = a_cat.reshape(g, CHUNK, S5_GROUP, 2 * p)[:, ::-1].reshape(g, CW, 2 * p)
    cpt = jnp.concatenate([cp_re, cp_imn], axis=-1)
    kt4 = kt.reshape(g, CHUNK, S5_GROUP, S5_GROUP)
    lag = jnp.arange(CHUNK)[None, :] - jnp.arange(CHUNK)[:, None]
    m5 = jnp.where((lag >= 0)[None, :, :, None, None], kt4[:, jnp.clip(lag, 0, CHUNK - 1)], 0.0)
    m_mat = jnp.transpose(m5, (0, 1, 3, 2, 4)).reshape(g, CW, CW)
    ar = jnp.concatenate([q_re, q_re], axis=-1)
    ai = jnp.concatenate([-q_im, q_im], axis=-1)

    u_p = proj[:n_p, :D_S5].reshape(nb * nk, CHUNK, g, S5_GROUP)
    ug = jnp.transpose(u_p, (2, 0, 1, 3)).reshape(g, nb * nk, CW)
    yg, hfin = _s5_prompt(ug, m_mat, bpow, cpt, ar, ai, nb=nb, nk=nk)
    y_p = jnp.transpose(yg.reshape(g, nb * nk, CHUNK, S5_GROUP), (1, 2, 0, 3)).reshape(n_p, D_S5)

    ud = jnp.transpose(proj[n_p:, :D_S5].reshape(nd, g, S5_GROUP), (1, 0, 2))
    h0d = jnp.concatenate([jnp.transpose(state_s5_re[0], (1, 0, 2)),
                           jnp.transpose(state_s5_im[0], (1, 0, 2))], axis=-1)
    bd = a_cat[:, :S5_GROUP, :]
    cd = jnp.concatenate([c_re[0], -c_im[0]], axis=-1)
    lr = jnp.concatenate([lb_re, lb_re], axis=-1)
    li = jnp.concatenate([-lb_im, lb_im], axis=-1)
    ydg, hd = _s5_decode(ud, h0d, bd, cd, lr, li)
    y_d = jnp.transpose(ydg, (1, 0, 2)).reshape(nd, D_S5)
    y_s5 = jnp.concatenate([y_p, y_d], axis=0)

    wa_bd = _block_diag4(w_a[0]).astype(BF16)
    wx_bd = _block_diag4(w_x[0]).astype(BF16)
    lru_args = (conv_w[0], row(conv_b[0]), wa_bd, wx_bd, row(b_a[0]), row(b_x[0]), row(lam_l[0]))
    lru_p, hl_p = _lru_prompt(proj, *lru_args, nb=nb, seq=seq)
    conv0 = state_lru_conv[0]
    proj_d = proj[n_p:]
    lru_d, hl_d = _lru_decode(proj_d, conv0[:, 0], conv0[:, 1], conv0[:, 2], state_lru_h[0],
                              *lru_args)
    lru = jnp.concatenate([lru_p, lru_d], axis=0)

    x2 = _mix(y_s5, proj, lru, x1, row(d_skip[0]), w_glu[0].astype(BF16), row(b_glu[0]),
              row(g_out_s5[0]), row(g_out_lru[0]), w_out[0].astype(BF16))
    y = _ffn(x2, row(g_ffn2[0]), w1_b[0].astype(BF16), w3_b[0].astype(BF16), w2_b[0].astype(BF16),
             row(g_final))

    xl_p = proj[:n_p, D_S5:D_S5 + D_LRU].reshape(nb, seq, D_LRU)
    xl_d = proj_d[:, D_S5:D_S5 + D_LRU]
    return (
        y[:n_p].reshape(nb, seq, D_MODEL),
        y[n_p:].reshape(nd, 1, D_MODEL),
        jnp.transpose(hfin[:, :, :p], (1, 0, 2))[None],
        jnp.transpose(hfin[:, :, p:], (1, 0, 2))[None],
        hl_p.reshape(1, nb, D_LRU),
        xl_p[:, seq - (CONV_W - 1):, :][None],
        jnp.transpose(hd[:, :, :p], (1, 0, 2))[None],
        jnp.transpose(hd[:, :, p:], (1, 0, 2))[None],
        hl_d[None],
        jnp.stack([conv0[:, 1], conv0[:, 2], xl_d], axis=1)[None],
    )
```

```python
import functools

import jax
import jax.numpy as jnp
from jax import lax
from jax.experimental import pallas as pl
from jax.experimental.pallas import tpu as pltpu

F32 = jnp.float32
BF16 = jnp.bfloat16

D_MODEL = 2048
D_S5 = 1024
S5_GROUP = 16
S5_GROUPS = 64
S5_STATE = 64
D_LRU = 1024
LRU_HEADS = 16
LRU_HEAD_DIM = 64
CONV_W = 4
LRU_C = 8.0
D_FF = 5632
EPS = 1e-6

CHUNK = 16
CW = CHUNK * S5_GROUP
N_DOUBLINGS = 7

VMEM_LIMIT = 58 * 1024 * 1024

NN = (((1,), (0,)), ((), ()))
NT = (((1,), (1,)), ((), ()))


def _rms(x, g):
    return x * lax.rsqrt(jnp.mean(x * x, axis=-1, keepdims=True) + EPS) * g


def _split(x):
    hi = x.astype(BF16)
    lo = (x - hi.astype(F32)).astype(BF16)
    return hi, lo


def _dot3(a, b, dims=NN):
    ah, al = _split(a)
    bh, bl = _split(b)
    d = functools.partial(lax.dot_general, dimension_numbers=dims, preferred_element_type=F32)
    return d(ah, bh) + d(al, bh) + d(ah, bl)


def _gelu(x):
    return jax.nn.gelu(x, approximate=True)


def _ffn_body(*refs, final_norm, n_tiles, nd):
    if final_norm:
        xp_ref, xd_ref, g_ref, w1_ref, w3_ref, w2_ref, gf_ref, op_ref, od_ref, xn_ref = refs
    else:
        xp_ref, xd_ref, g_ref, w1_ref, w3_ref, w2_ref, op_ref, od_ref, xn_ref = refs
    i = pl.program_id(0)
    f = pl.program_id(1)

    def run(x_ref, o_ref, rows):
        @pl.when(f == 0)
        def _():
            x = x_ref[...]
            xn_ref[0:rows, :] = _rms(x, g_ref[...]).astype(BF16)
            o_ref[...] = x

        xn = xn_ref[0:rows, :]
        a = jnp.dot(xn, w1_ref[...], preferred_element_type=F32)
        b = jnp.dot(xn, w3_ref[...], preferred_element_type=F32)
        h = (a * jax.nn.sigmoid(a) * b).astype(BF16)
        o_ref[...] += 0.5 * jnp.dot(h, w2_ref[...], preferred_element_type=F32)

        if final_norm:
            @pl.when(f == pl.num_programs(1) - 1)
            def _():
                o_ref[...] = _rms(o_ref[...], gf_ref[...])

    @pl.when(i < n_tiles)
    def _():
        run(xp_ref, op_ref, xp_ref.shape[0])

    @pl.when(i == n_tiles)
    def _():
        run(xd_ref, od_ref, nd)


def _ffn(xp, xd, g, w1, w3, w2, g_final=None, *, tm=1024, tf=512):
    n_p, nd = xp.shape[0], xd.shape[0]
    n_tiles = n_p // tm
    final_norm = g_final is not None
    pspec = pl.BlockSpec((tm, D_MODEL), lambda i, f: (jnp.minimum(i, n_tiles - 1), 0))
    dspec = pl.BlockSpec((nd, D_MODEL), lambda i, f: (0, 0))
    in_specs = [
        pspec, dspec,
        pl.BlockSpec((1, D_MODEL), lambda i, f: (0, 0)),
        pl.BlockSpec((D_MODEL, tf), lambda i, f: (0, f)),
        pl.BlockSpec((D_MODEL, tf), lambda i, f: (0, f)),
        pl.BlockSpec((tf, D_MODEL), lambda i, f: (f, 0)),
    ]
    args = [xp, xd, g, w1, w3, w2]
    if final_norm:
        in_specs.append(pl.BlockSpec((1, D_MODEL), lambda i, f: (0, 0)))
        args.append(g_final)
    return pl.pallas_call(
        functools.partial(_ffn_body, final_norm=final_norm, n_tiles=n_tiles, nd=nd),
        grid=(n_tiles + 1, D_FF // tf),
        in_specs=in_specs,
        out_specs=(pspec, dspec),
        out_shape=(jax.ShapeDtypeStruct((n_p, D_MODEL), F32),
                   jax.ShapeDtypeStruct((nd, D_MODEL), F32)),
        scratch_shapes=[pltpu.VMEM((tm, D_MODEL), BF16)],
        compiler_params=pltpu.CompilerParams(
            dimension_semantics=("arbitrary", "arbitrary"), vmem_limit_bytes=VMEM_LIMIT),
        name="ffn_final" if final_norm else "ffn",
    )(*args)


def _inproj_body(xp_ref, xd_ref, g_ref, w_ref, op_ref, od_ref, xn_ref, *, n_tiles, nd):
    i = pl.program_id(0)
    j = pl.program_id(1)

    def run(x_ref, o_ref, rows):
        @pl.when(j == 0)
        def _():
            xn_ref[0:rows, :] = _rms(x_ref[...], g_ref[...]).astype(BF16)

        o_ref[...] = jnp.dot(xn_ref[0:rows, :], w_ref[...], preferred_element_type=F32)

    @pl.when(i < n_tiles)
    def _():
        run(xp_ref, op_ref, xp_ref.shape[0])

    @pl.when(i == n_tiles)
    def _():
        run(xd_ref, od_ref, nd)


def _inproj(xp, xd, g, w, *, tm=1024, tn=1024):
    n_p, nd = xp.shape[0], xd.shape[0]
    n_tiles = n_p // tm
    d_out = w.shape[1]
    nj = d_out // tn
    return pl.pallas_call(
        functools.partial(_inproj_body, n_tiles=n_tiles, nd=nd),
        grid=(n_tiles + 1, nj),
        in_specs=[
            pl.BlockSpec((tm, D_MODEL), lambda i, j: (jnp.minimum(i, n_tiles - 1), 0)),
            pl.BlockSpec((nd, D_MODEL), lambda i, j: (0, 0)),
            pl.BlockSpec((1, D_MODEL), lambda i, j: (0, 0)),
            pl.BlockSpec((D_MODEL, tn), lambda i, j: (0, j)),
        ],
        out_specs=(
            pl.BlockSpec((tm, tn), lambda i, j: (jnp.minimum(i, n_tiles - 1),
                                                 jnp.where(i < n_tiles, j, nj - 1))),
            pl.BlockSpec((nd, tn), lambda i, j: (0, jnp.where(i < n_tiles, 0, j))),
        ),
        out_shape=(jax.ShapeDtypeStruct((n_p, d_out), F32),
                   jax.ShapeDtypeStruct((nd, d_out), F32)),
        scratch_shapes=[pltpu.VMEM((tm, D_MODEL), BF16)],
        compiler_params=pltpu.CompilerParams(
            dimension_semantics=("arbitrary", "arbitrary"), vmem_limit_bytes=VMEM_LIMIT),
        name="inproj",
    )(xp, xd, g, w)


def _s5_prep_body(lre_ref, lim_ref, ldt_ref, bre_ref, bim_ref, cre_ref, cim_ref,
                  are_ref, aim_ref, kt_ref, cpre_ref, cpimn_ref,
                  qr_ref, qi_ref, lbr_ref, lbi_ref):
    lam_re = lre_ref[...]
    lam_im = lim_ref[...]
    dt = jnp.exp(ldt_ref[...])
    mag = jnp.exp(lam_re * dt)
    ang = lam_im * dt
    lbr = mag * jnp.cos(ang)
    lbi = mag * jnp.sin(ang)
    lbr_ref[...] = lbr
    lbi_ref[...] = lbi
    nr = lbr - 1.0
    den = lam_re * lam_re + lam_im * lam_im
    cr = (nr * lam_re + lbi * lam_im) / den
    ci = (lbi * lam_re - nr * lam_im) / den
    b_re = bre_ref[...]
    b_im = bim_ref[...]
    bbr = cr * b_re - ci * b_im
    bbi = cr * b_im + ci * b_re
    c_re = cre_ref[...]
    c_im = cim_ref[...]

    pr = jnp.ones_like(lbr)
    pi = jnp.zeros_like(lbr)
    for d in range(CHUNK):
        rows = slice(d * S5_GROUP, (d + 1) * S5_GROUP)
        are_ref[:, rows, :] = bbr * pr - bbi * pi
        aim_ref[:, rows, :] = bbr * pi + bbi * pr
        pr, pi = pr * lbr - pi * lbi, pr * lbi + pi * lbr
        cpre_ref[:, rows, :] = c_re * pr - c_im * pi
        cpimn_ref[:, rows, :] = -(c_re * pi + c_im * pr)

    qr, qi = pr, pi
    for j in range(N_DOUBLINGS):
        qr_ref[:, j:j + 1, :] = qr
        qi_ref[:, j:j + 1, :] = qi
        qr, qi = qr * qr - qi * qi, 2.0 * qr * qi
    qr_ref[:, N_DOUBLINGS:, :] = jnp.zeros_like(qr)
    qi_ref[:, N_DOUBLINGS:, :] = jnp.zeros_like(qi)

    def body(g, carry):
        a_re = are_ref[g]
        a_im = aim_ref[g]
        k = (lax.dot_general(a_re, cre_ref[g], NT, precision=lax.Precision.HIGHEST,
                             preferred_element_type=F32)
             - lax.dot_general(a_im, cim_ref[g], NT, precision=lax.Precision.HIGHEST,
                               preferred_element_type=F32))
        kt_ref[g] = k
        return carry

    lax.fori_loop(0, S5_GROUPS, body, 0)


def _s5_prep(lam_re, lam_im, log_dt, b_re, b_im, c_re, c_im):
    g, p = S5_GROUPS, S5_STATE
    lre = lam_re.reshape(g, 1, p)
    lim = lam_im.reshape(g, 1, p)
    ldt = jnp.broadcast_to(log_dt.reshape(g, 1, 1), (g, 1, p))
    bre = jnp.transpose(b_re, (0, 2, 1))
    bim = jnp.transpose(b_im, (0, 2, 1))
    sd = jax.ShapeDtypeStruct
    outs = pl.pallas_call(
        _s5_prep_body,
        out_shape=(
            sd((g, CW, p), F32), sd((g, CW, p), F32),
            sd((g, CW, S5_GROUP), F32),
            sd((g, CW, p), F32), sd((g, CW, p), F32),
            sd((g, 8, p), F32), sd((g, 8, p), F32),
            sd((g, 1, p), F32), sd((g, 1, p), F32),
        ),
        compiler_params=pltpu.CompilerParams(vmem_limit_bytes=VMEM_LIMIT),
        name="s5_prep",
    )(lre, lim, ldt, bre, bim, c_re, c_im)
    return outs


def _cmul_add(h, hs, ar, ai):
    return h + hs * ar + pltpu.roll(hs, S5_STATE, axis=1) * ai


GROUPS_PER_TILE = 128 // S5_GROUP


def _block_transpose(xs, inverse=False):
    lane = lax.broadcasted_iota(jnp.int32, xs[0].shape, 1)
    stages = [(4, 64), (2, 32), (1, 16)]
    if inverse:
        stages = stages[::-1]
    xs = list(xs)
    for dist, s in stages:
        first = (lane & (2 * s - 1)) < s
        new = list(xs)
        for i in range(len(xs)):
            if i & dist:
                continue
            p, q = xs[i], xs[i + dist]
            new[i] = jnp.where(first, p, pltpu.roll(q, s, axis=1))
            new[i + dist] = jnp.where(first, pltpu.roll(p, 128 - s, axis=1), q)
        xs = new
    return xs


def _s5p_body(u_ref, m_ref, bp_ref, cp_ref, ar_ref, ai_ref, y_ref, hf_ref, *, nb, nk):
    gb = GROUPS_PER_TILE
    nrow = nb * nk
    d = functools.partial(jnp.dot, preferred_element_type=F32)

    halves = []
    for h in range(CHUNK // 8):
        xs = [u_ref[pl.ds(8 * h + i, nrow, stride=CHUNK), :] for i in range(8)]
        halves.append(_block_transpose(xs))

    ys, hs = [], []
    for j in range(gb):
        u = jnp.concatenate([half[j] for half in halves], axis=1)
        uh, ul = _split(u)
        mh, ml = _split(m_ref[j])
        bh, bl = _split(bp_ref[j])
        ys.append(d(uh, mh) + d(ul, mh) + d(uh, ml))
        hs.append(d(uh, bh) + d(ul, bh) + d(uh, bl))

    rows = lax.broadcasted_iota(jnp.int32, (nrow, 2 * S5_STATE), 0) & (nk - 1)
    for k in range(N_DOUBLINGS):
        sh = 1 << k
        for j in range(gb):
            h = hs[j]
            shifted = jnp.where(rows >= sh, pltpu.roll(h, sh, axis=0), 0.0)
            hs[j] = _cmul_add(h, shifted, ar_ref[j, k:k + 1, :], ai_ref[j, k:k + 1, :])

    for j in range(gb):
        h = hs[j]
        for b in range(nb):
            hf_ref[j, b:b + 1, :] = h[(b + 1) * nk - 1:(b + 1) * nk, :]
        hprev = jnp.where(rows >= 1, pltpu.roll(h, 1, axis=0), 0.0)
        ys[j] = ys[j] + _dot3(hprev, cp_ref[j], NT)

    for h in range(CHUNK // 8):
        outs = _block_transpose([ys[j][:, 128 * h:128 * (h + 1)] for j in range(gb)], inverse=True)
        for i in range(8):
            y_ref[pl.ds(8 * h + i, nrow, stride=CHUNK), :] = outs[i]


def _s5_prompt(proj_p, m, bpow, cpt, ar, ai, *, nb, seq):
    g, gb = S5_GROUPS, GROUPS_PER_TILE
    n_p = nb * seq
    nk = seq // CHUNK
    blk = lambda r, c: pl.BlockSpec((gb, r, c), lambda i: (i, 0, 0))
    return pl.pallas_call(
        functools.partial(_s5p_body, nb=nb, nk=nk),
        grid=(g // gb,),
        in_specs=[pl.BlockSpec((n_p, 128), lambda i: (0, i)),
                  blk(CW, CW), blk(CW, 2 * S5_STATE), blk(CW, 2 * S5_STATE),
                  blk(8, 2 * S5_STATE), blk(8, 2 * S5_STATE)],
        out_specs=(pl.BlockSpec((n_p, 128), lambda i: (0, i)), blk(nb, 2 * S5_STATE)),
        out_shape=(jax.ShapeDtypeStruct((n_p, D_S5), F32),
                   jax.ShapeDtypeStruct((g, nb, 2 * S5_STATE), F32)),
        compiler_params=pltpu.CompilerParams(
            dimension_semantics=("parallel",), vmem_limit_bytes=VMEM_LIMIT),
        name="s5_prompt",
    )(proj_p, m, bpow, cpt, ar, ai)


def _s5d_body(u_ref, h0_ref, bd_ref, cd_ref, lr_ref, li_ref, y_ref, h_ref, *, gb):
    for j in range(gb):
        h0 = h0_ref[j]
        bu = _dot3(u_ref[j], bd_ref[j])
        h = _cmul_add(bu, h0, lr_ref[j], li_ref[j])
        h_ref[j] = h
        y_ref[j] = _dot3(h, cd_ref[j], NT)


def _s5_decode(ud, h0, bd, cd, lr, li, *, gb=8):
    g, nbatch = S5_GROUPS, ud.shape[1]
    blk = lambda r, c: pl.BlockSpec((gb, r, c), lambda i: (i, 0, 0))
    return pl.pallas_call(
        functools.partial(_s5d_body, gb=gb),
        grid=(g // gb,),
        in_specs=[blk(nbatch, S5_GROUP), blk(nbatch, 2 * S5_STATE), blk(S5_GROUP, 2 * S5_STATE),
                  blk(S5_GROUP, 2 * S5_STATE), blk(1, 2 * S5_STATE), blk(1, 2 * S5_STATE)],
        out_specs=(blk(nbatch, S5_GROUP), blk(nbatch, 2 * S5_STATE)),
        out_shape=(jax.ShapeDtypeStruct((g, nbatch, S5_GROUP), F32),
                   jax.ShapeDtypeStruct((g, nbatch, 2 * S5_STATE), F32)),
        compiler_params=pltpu.CompilerParams(dimension_semantics=("parallel",)),
        name="s5_decode",
    )(ud, h0, bd, cd, lr, li)


def _lru_gates(xc, wa_ref, wx_ref, ba, bx, lam):
    xcb = xc.astype(BF16)
    nblk = D_LRU // 256
    r_parts, i_parts = [], []
    for k in range(nblk):
        xk = xcb[:, k * 256:(k + 1) * 256]
        r_parts.append(jnp.dot(xk, wa_ref[k], preferred_element_type=F32))
        i_parts.append(jnp.dot(xk, wx_ref[k], preferred_element_type=F32))
    r = jax.nn.sigmoid(jnp.concatenate(r_parts, axis=1) + ba)
    i = jax.nn.sigmoid(jnp.concatenate(i_parts, axis=1) + bx)
    z = -lam
    softplus = jnp.maximum(z, 0.0) + jnp.log1p(jnp.exp(-jnp.abs(z)))
    log_a = (-LRU_C * softplus) * r
    a = jnp.exp(log_a)
    mult = jnp.sqrt(-jnp.tanh(log_a) * (a * a + 1.0))
    return a, mult * (i * xc)


def _lru_prompt_body(xl_ref, gate_ref, cw_ref, cb_ref, wa_ref, wx_ref, ba_ref, bx_ref, lam_ref,
                     o_ref, hl_ref, xbuf_ref, carry_ref, *, tt):
    t = pl.program_id(1)

    @pl.when(t == 0)
    def _():
        xbuf_ref[0:8, :] = jnp.zeros((8, D_LRU), F32)
        carry_ref[...] = jnp.zeros((8, D_LRU), F32)

    x = xl_ref[...]
    xbuf_ref[8:8 + tt, :] = x
    cw = cw_ref[...]
    xc = (cb_ref[...] + xbuf_ref[5:5 + tt, :] * cw[0:1] + xbuf_ref[6:6 + tt, :] * cw[1:2]
          + xbuf_ref[7:7 + tt, :] * cw[2:3] + x * cw[3:4])
    xbuf_ref[0:8, :] = x[tt - 8:tt, :]

    a, b = _lru_gates(xc, wa_ref, wx_ref, ba_ref[...], bx_ref[...], lam_ref[...])

    rows = lax.broadcasted_iota(jnp.int32, (tt, D_LRU), 0)
    sh = 1
    while sh < tt:
        if sh < 8:
            m = rows >= sh
            b = jnp.where(m, b + a * pltpu.roll(b, sh, axis=0), b)
            a = jnp.where(m, a * pltpu.roll(a, sh, axis=0), a)
        else:
            b = jnp.concatenate([b[:sh], b[sh:] + a[sh:] * b[:tt - sh]], axis=0)
            a = jnp.concatenate([a[:sh], a[sh:] * a[:tt - sh]], axis=0)
        sh *= 2
    h = b + a * carry_ref[0:1, :]
    hlast = h[tt - 1:tt, :]
    carry_ref[...] = jnp.broadcast_to(hlast, (8, D_LRU))
    hl_ref[0] = hlast
    o_ref[...] = h * _gelu(gate_ref[...])


def _lru_prompt(proj, cw, cb, wa, wx, ba, bx, lam, *, nb, seq, tt=256):
    nt = seq // tt
    vec = lambda r: pl.BlockSpec((r, D_LRU), lambda b, t: (0, 0))
    wspec = pl.BlockSpec((D_LRU // 256, 256, 256), lambda b, t: (0, 0, 0))
    return pl.pallas_call(
        functools.partial(_lru_prompt_body, tt=tt),
        grid=(nb, nt),
        in_specs=[
            pl.BlockSpec((tt, D_LRU), lambda b, t: (b * nt + t, 1)),
            pl.BlockSpec((tt, D_LRU), lambda b, t: (b * nt + t, 2)),
            vec(CONV_W), vec(1), wspec, wspec, vec(1), vec(1), vec(1),
        ],
        out_specs=(pl.BlockSpec((tt, D_LRU), lambda b, t: (b * nt + t, 0)),
                   pl.BlockSpec((1, 1, D_LRU), lambda b, t: (b, 0, 0))),
        out_shape=(jax.ShapeDtypeStruct((nb * seq, D_LRU), F32),
                   jax.ShapeDtypeStruct((nb, 1, D_LRU), F32)),
        scratch_shapes=[pltpu.VMEM((tt + 8, D_LRU), F32), pltpu.VMEM((8, D_LRU), F32)],
        compiler_params=pltpu.CompilerParams(
            dimension_semantics=("parallel", "arbitrary"), vmem_limit_bytes=VMEM_LIMIT),
        name="lru_prompt",
    )(proj, proj, cw, cb, wa, wx, ba, bx, lam)


def _lru_decode_body(xl_ref, gate_ref, c0_ref, c1_ref, c2_ref, h0_ref, cw_ref, cb_ref,
                     wa_ref, wx_ref, ba_ref, bx_ref, lam_ref, o_ref, h_ref):
    x = xl_ref[...]
    cw = cw_ref[...]
    xc = (cb_ref[...] + c0_ref[...] * cw[0:1] + c1_ref[...] * cw[1:2]
          + c2_ref[...] * cw[2:3] + x * cw[3:4])
    a, b = _lru_gates(xc, wa_ref, wx_ref, ba_ref[...], bx_ref[...], lam_ref[...])
    h = a * h0_ref[...] + b
    h_ref[...] = h
    o_ref[...] = h * _gelu(gate_ref[...])


def _lru_decode(proj_d, c0, c1, c2, h0, cw, cb, wa, wx, ba, bx, lam):
    nd = proj_d.shape[0]
    full = lambda r: pl.BlockSpec((r, D_LRU), lambda i: (0, 0))
    wspec = pl.BlockSpec((D_LRU // 256, 256, 256), lambda i: (0, 0, 0))
    return pl.pallas_call(
        _lru_decode_body,
        grid=(1,),
        in_specs=[
            pl.BlockSpec((nd, D_LRU), lambda i: (0, 1)),
            pl.BlockSpec((nd, D_LRU), lambda i: (0, 2)),
            full(nd), full(nd), full(nd), full(nd),
            full(CONV_W), full(1), wspec, wspec, full(1), full(1), full(1),
        ],
        out_specs=(full(nd), full(nd)),
        out_shape=(jax.ShapeDtypeStruct((nd, D_LRU), F32),
                   jax.ShapeDtypeStruct((nd, D_LRU), F32)),
        name="lru_decode",
    )(proj_d, proj_d, c0, c1, c2, h0, cw, cb, wa, wx, ba, bx, lam)


def _mix_body(ysp_ref, up_ref, lrup_ref, xp_ref, ysd_ref, ud_ref, lrud_ref, xd_ref,
              dsk_ref, wg_ref, bg_ref, gs_ref, gl_ref, wo_ref, op_ref, od_ref, *, n_tiles):
    i = pl.program_id(0)

    def run(ys_ref, u_ref, lru_ref, x_ref, o_ref):
        yy = ys_ref[...] + dsk_ref[...] * u_ref[...]
        g = _gelu(yy)
        z = jnp.dot(g.astype(BF16), wg_ref[...], preferred_element_type=F32) + bg_ref[...]
        s5o = g * jax.nn.sigmoid(z)
        n1 = _rms(s5o, gs_ref[...]).astype(BF16)
        n2 = _rms(lru_ref[...], gl_ref[...]).astype(BF16)
        o_ref[...] = (x_ref[...]
                      + jnp.dot(n1, wo_ref[0:D_S5, :], preferred_element_type=F32)
                      + jnp.dot(n2, wo_ref[D_S5:, :], preferred_element_type=F32))

    @pl.when(i < n_tiles)
    def _():
        run(ysp_ref, up_ref, lrup_ref, xp_ref, op_ref)

    @pl.when(i == n_tiles)
    def _():
        run(ysd_ref, ud_ref, lrud_ref, xd_ref, od_ref)


def _mix(ys_p, proj_p, lru_p, x1_p, ys_d, proj_d, lru_d, x1_d, dsk, wg, bg, gs, gl, wo, *, tm=512):
    n_p, nd = x1_p.shape[0], x1_d.shape[0]
    n_tiles = n_p // tm
    prow = lambda c: pl.BlockSpec((tm, c), lambda i: (jnp.minimum(i, n_tiles - 1), 0))
    drow = lambda c: pl.BlockSpec((nd, c), lambda i: (0, 0))
    vec = lambda c: pl.BlockSpec((1, c), lambda i: (0, 0))
    return pl.pallas_call(
        functools.partial(_mix_body, n_tiles=n_tiles),
        grid=(n_tiles + 1,),
        in_specs=[prow(D_S5), prow(D_S5), prow(D_LRU), prow(D_MODEL),
                  drow(D_S5), drow(D_S5), drow(D_LRU), drow(D_MODEL), vec(D_S5),
                  pl.BlockSpec((D_S5, D_S5), lambda i: (0, 0)), vec(D_S5), vec(D_S5), vec(D_LRU),
                  pl.BlockSpec((D_MODEL, D_MODEL), lambda i: (0, 0))],
        out_specs=(prow(D_MODEL), drow(D_MODEL)),
        out_shape=(jax.ShapeDtypeStruct((n_p, D_MODEL), F32),
                   jax.ShapeDtypeStruct((nd, D_MODEL), F32)),
        compiler_params=pltpu.CompilerParams(
            dimension_semantics=("arbitrary",), vmem_limit_bytes=VMEM_LIMIT),
        name="mix",
    )(ys_p, proj_p, lru_p, x1_p, ys_d, proj_d, lru_d, x1_d, dsk, wg, bg, gs, gl, wo)


def _block_diag4(w):
    w4 = w.reshape(LRU_HEADS // 4, 4, LRU_HEAD_DIM, LRU_HEAD_DIM)
    eye = jnp.eye(4, dtype=w.dtype)
    return jnp.einsum("kaij,ab->kaibj", w4, eye).reshape(LRU_HEADS // 4, 256, 256)


def kernel(x_prompt, x_sample, state_s5_re, state_s5_im, state_lru_h, state_lru_conv, g_ffn1, w1_a, w3_a, w2_a, g_mix, w_in, lam_re, lam_im, log_dt, b_re, b_im, c_re, c_im, d_skip, w_glu, b_glu, conv_w, conv_b, w_a, b_a, w_x, b_x, lam_l, g_out_s5, g_out_lru, w_out, g_ffn2, w1_b, w3_b, w2_b, g_final):
    nb, seq, _ = x_prompt.shape
    nd = x_sample.shape[0]
    n_p = nb * seq
    nk = seq // CHUNK
    g, p = S5_GROUPS, S5_STATE
    row = lambda v: v.reshape(1, -1)

    xp = x_prompt.reshape(n_p, D_MODEL)
    xd = x_sample.reshape(nd, D_MODEL)

    x1_p, x1_d = _ffn(xp, xd, row(g_ffn1[0]), w1_a[0].astype(BF16), w3_a[0].astype(BF16),
                      w2_a[0].astype(BF16))
    proj_p, proj_d = _inproj(x1_p, x1_d, row(g_mix[0]), w_in[0].astype(BF16))

    (a_re, a_im, kt, cp_re, cp_imn, q_re, q_im, lb_re, lb_im) = _s5_prep(
        lam_re[0], lam_im[0], log_dt[0], b_re[0], b_im[0], c_re[0], c_im[0])
    a_cat = jnp.concatenate([a_re, a_im], axis=-1)
    bpow = a_cat.reshape(g, CHUNK, S5_GROUP, 2 * p)[:, ::-1].reshape(g, CW, 2 * p)
    cpt = jnp.concatenate([cp_re, cp_imn], axis=-1)
    kt4 = kt.reshape(g, CHUNK, S5_GROUP, S5_GROUP)
    lag = jnp.arange(CHUNK)[None, :] - jnp.arange(CHUNK)[:, None]
    m5 = jnp.where((lag >= 0)[None, :, :, None, None], kt4[:, jnp.clip(lag, 0, CHUNK - 1)], 0.0)
    m_mat = jnp.transpose(m5, (0, 1, 3, 2, 4)).reshape(g, CW, CW)
    ar = jnp.concatenate([q_re, q_re], axis=-1)
    ai = jnp.concatenate([-q_im, q_im], axis=-1)

    ys_p, hfin = _s5_prompt(proj_p, m_mat, bpow, cpt, ar, ai, nb=nb, seq=seq)

    ud = jnp.transpose(proj_d[:, :D_S5].reshape(nd, g, S5_GROUP), (1, 0, 2))
    h0d = jnp.concatenate([jnp.transpose(state_s5_re[0], (1, 0, 2)),
                           jnp.transpose(state_s5_im[0], (1, 0, 2))], axis=-1)
    bd = a_cat[:, :S5_GROUP, :]
    cd = jnp.concatenate([c_re[0], -c_im[0]], axis=-1)
    lr = jnp.concatenate([lb_re, lb_re], axis=-1)
    li = jnp.concatenate([-lb_im, lb_im], axis=-1)
    ydg, hd = _s5_decode(ud, h0d, bd, cd, lr, li)
    ys_d = jnp.transpose(ydg, (1, 0, 2)).reshape(nd, D_S5)

    wa_bd = _block_diag4(w_a[0]).astype(BF16)
    wx_bd = _block_diag4(w_x[0]).astype(BF16)
    lru_args = (conv_w[0], row(conv_b[0]), wa_bd, wx_bd, row(b_a[0]), row(b_x[0]), row(lam_l[0]))
    lru_p, hl_p = _lru_prompt(proj_p, *lru_args, nb=nb, seq=seq)
    conv0 = state_lru_conv[0]
    lru_d, hl_d = _lru_decode(proj_d, conv0[:, 0], conv0[:, 1], conv0[:, 2], state_lru_h[0],
                              *lru_args)

    x2_p, x2_d = _mix(ys_p, proj_p, lru_p, x1_p, ys_d, proj_d, lru_d, x1_d,
                      row(d_skip[0]), w_glu[0].astype(BF16), row(b_glu[0]),
                      row(g_out_s5[0]), row(g_out_lru[0]), w_out[0].astype(BF16))
    y_p, y_d = _ffn(x2_p, x2_d, row(g_ffn2[0]), w1_b[0].astype(BF16), w3_b[0].astype(BF16),
                    w2_b[0].astype(BF16), row(g_final))

    xl_p = proj_p[:, D_S5:D_S5 + D_LRU].reshape(nb, seq, D_LRU)
    xl_d = proj_d[:, D_S5:D_S5 + D_LRU]
    return (
        y_p.reshape(nb, seq, D_MODEL),
        y_d.reshape(nd, 1, D_MODEL),
        jnp.transpose(hfin[:, :, :p], (1, 0, 2))[None],
        jnp.transpose(hfin[:, :, p:], (1, 0, 2))[None],
        hl_p.reshape(1, nb, D_LRU),
        xl_p[:, seq - (CONV_W - 1):, :][None],
        jnp.transpose(hd[:, :, :p], (1, 0, 2))[None],
        jnp.transpose(hd[:, :, p:], (1, 0, 2))[None],
        hl_d[None],
        jnp.stack([conv0[:, 1], conv0[:, 2], xl_d], axis=1)[None],
    )
```

```python
import functools

import jax
import jax.numpy as jnp
from jax import lax
from jax.experimental import pallas as pl
from jax.experimental.pallas import tpu as pltpu

F32 = jnp.float32
BF16 = jnp.bfloat16

D_MODEL = 2048
D_S5 = 1024
S5_GROUP = 16
S5_GROUPS = 64
S5_STATE = 64
D_LRU = 1024
LRU_HEADS = 16
LRU_HEAD_DIM = 64
CONV_W = 4
LRU_C = 8.0
D_FF = 5632
EPS = 1e-6

CHUNK = 16
CW = CHUNK * S5_GROUP
N_DOUBLINGS = 7

VMEM_LIMIT = 58 * 1024 * 1024

NN = (((1,), (0,)), ((), ()))
NT = (((1,), (1,)), ((), ()))


def _rms(x, g):
    return x * lax.rsqrt(jnp.mean(x * x, axis=-1, keepdims=True) + EPS) * g


def _split(x):
    hi = x.astype(BF16)
    lo = (x - hi.astype(F32)).astype(BF16)
    return hi, lo


def _dot3(a, b, dims=NN):
    ah, al = _split(a)
    bh, bl = _split(b)
    d = functools.partial(lax.dot_general, dimension_numbers=dims, preferred_element_type=F32)
    return d(ah, bh) + d(al, bh) + d(ah, bl)


def _gelu(x):
    return jax.nn.gelu(x, approximate=True)


def _ffn_body(*refs, final_norm, n_tiles, nd):
    if final_norm:
        xp_ref, xd_ref, g_ref, w1_ref, w3_ref, w2_ref, gf_ref, op_ref, od_ref, xn_ref = refs
    else:
        xp_ref, xd_ref, g_ref, w1_ref, w3_ref, w2_ref, op_ref, od_ref, xn_ref = refs
    i = pl.program_id(0)
    f = pl.program_id(1)

    def run(x_ref, o_ref, rows):
        @pl.when(f == 0)
        def _():
            x = x_ref[...]
            xn_ref[0:rows, :] = _rms(x, g_ref[...]).astype(BF16)
            o_ref[...] = x

        xn = xn_ref[0:rows, :]
        a = jnp.dot(xn, w1_ref[...], preferred_element_type=F32)
        b = jnp.dot(xn, w3_ref[...], preferred_element_type=F32)
        h = (a * jax.nn.sigmoid(a) * b).astype(BF16)
        o_ref[...] += 0.5 * jnp.dot(h, w2_ref[...], preferred_element_type=F32)

        if final_norm:
            @pl.when(f == pl.num_programs(1) - 1)
            def _():
                o_ref[...] = _rms(o_ref[...], gf_ref[...])

    @pl.when(i < n_tiles)
    def _():
        run(xp_ref, op_ref, xp_ref.shape[0])

    @pl.when(i == n_tiles)
    def _():
        run(xd_ref, od_ref, nd)


def _ffn(xp, xd, g, w1, w3, w2, g_final=None, *, tm=1024, tf=512):
    n_p, nd = xp.shape[0], xd.shape[0]
    n_tiles = n_p // tm
    final_norm = g_final is not None
    pspec = pl.BlockSpec((tm, D_MODEL), lambda i, f: (jnp.minimum(i, n_tiles - 1), 0))
    dspec = pl.BlockSpec((nd, D_MODEL), lambda i, f: (0, 0))
    in_specs = [
        pspec, dspec,
        pl.BlockSpec((1, D_MODEL), lambda i, f: (0, 0)),
        pl.BlockSpec((D_MODEL, tf), lambda i, f: (0, f)),
        pl.BlockSpec((D_MODEL, tf), lambda i, f: (0, f)),
        pl.BlockSpec((tf, D_MODEL), lambda i, f: (f, 0)),
    ]
    args = [xp, xd, g, w1, w3, w2]
    if final_norm:
        in_specs.append(pl.BlockSpec((1, D_MODEL), lambda i, f: (0, 0)))
        args.append(g_final)
    return pl.pallas_call(
        functools.partial(_ffn_body, final_norm=final_norm, n_tiles=n_tiles, nd=nd),
        grid=(n_tiles + 1, D_FF // tf),
        in_specs=in_specs,
        out_specs=(pspec, dspec),
        out_shape=(jax.ShapeDtypeStruct((n_p, D_MODEL), F32),
                   jax.ShapeDtypeStruct((nd, D_MODEL), F32)),
        scratch_shapes=[pltpu.VMEM((tm, D_MODEL), BF16)],
        compiler_params=pltpu.CompilerParams(
            dimension_semantics=("arbitrary", "arbitrary"), vmem_limit_bytes=VMEM_LIMIT),
        name="ffn_final" if final_norm else "ffn",
    )(*args)


def _inproj_body(xp_ref, xd_ref, g_ref, w_ref, op_ref, od_ref, xn_ref, *, n_tiles, nd):
    i = pl.program_id(0)
    j = pl.program_id(1)

    def run(x_ref, o_ref, rows):
        @pl.when(j == 0)
        def _():
            xn_ref[0:rows, :] = _rms(x_ref[...], g_ref[...]).astype(BF16)

        o_ref[...] = jnp.dot(xn_ref[0:rows, :], w_ref[...], preferred_element_type=F32)

    @pl.when(i < n_tiles)
    def _():
        run(xp_ref, op_ref, xp_ref.shape[0])

    @pl.when(i == n_tiles)
    def _():
        run(xd_ref, od_ref, nd)


def _inproj(xp, xd, g, w, *, tm=1024, tn=1024):
    n_p, nd = xp.shape[0], xd.shape[0]
    n_tiles = n_p // tm
    d_out = w.shape[1]
    nj = d_out // tn
    return pl.pallas_call(
        functools.partial(_inproj_body, n_tiles=n_tiles, nd=nd),
        grid=(n_tiles + 1, nj),
        in_specs=[
            pl.BlockSpec((tm, D_MODEL), lambda i, j: (jnp.minimum(i, n_tiles - 1), 0)),
            pl.BlockSpec((nd, D_MODEL), lambda i, j: (0, 0)),
            pl.BlockSpec((1, D_MODEL), lambda i, j: (0, 0)),
            pl.BlockSpec((D_MODEL, tn), lambda i, j: (0, j)),
        ],
        out_specs=(
            pl.BlockSpec((tm, tn), lambda i, j: (jnp.minimum(i, n_tiles - 1),
                                                 jnp.where(i < n_tiles, j, nj - 1))),
            pl.BlockSpec((nd, tn), lambda i, j: (0, jnp.where(i < n_tiles, 0, j))),
        ),
        out_shape=(jax.ShapeDtypeStruct((n_p, d_out), F32),
                   jax.ShapeDtypeStruct((nd, d_out), F32)),
        scratch_shapes=[pltpu.VMEM((tm, D_MODEL), BF16)],
        compiler_params=pltpu.CompilerParams(
            dimension_semantics=("arbitrary", "arbitrary"), vmem_limit_bytes=VMEM_LIMIT),
        name="inproj",
    )(xp, xd, g, w)


GROUPS_PER_TILE = 128 // S5_GROUP
PAIRS_PER_TILE = GROUPS_PER_TILE // 2
L_ROWS = CW + 2 * S5_STATE


def _s5_prep_body(lre_ref, lim_ref, ldt_ref, bre_ref, bim_ref, cre_ref, cim_ref,
                  l_ref, cpre_ref, cpim_ref, bd_ref, cd_ref, ar_ref, ai_ref, lr_ref, li_ref,
                  bp_ref):
    gb, half = GROUPS_PER_TILE, PAIRS_PER_TILE
    p = S5_STATE
    lo, hi = slice(0, p), slice(p, 2 * p)
    lam_re = lre_ref[...]
    lam_im = lim_ref[...]
    dt = jnp.exp(ldt_ref[...])
    mag = jnp.exp(lam_re * dt)
    ang = lam_im * dt
    lbr = mag * jnp.cos(ang)
    lbi = mag * jnp.sin(ang)
    lr_ref[:, :, lo] = lbr
    lr_ref[:, :, hi] = lbr
    li_ref[:, :, lo] = -lbi
    li_ref[:, :, hi] = lbi
    nr = lbr - 1.0
    den = lam_re * lam_re + lam_im * lam_im
    cr = (nr * lam_re + lbi * lam_im) / den
    ci = (lbi * lam_re - nr * lam_im) / den
    b_re = bre_ref[...]
    b_im = bim_ref[...]
    bbr = cr * b_re - ci * b_im
    bbi = cr * b_im + ci * b_re
    bd_ref[:, :, lo] = bbr
    bd_ref[:, :, hi] = bbi
    c_re = cre_ref[...]
    c_im = cim_ref[...]
    cd_ref[:, :, lo] = c_re
    cd_ref[:, :, hi] = -c_im

    zeros = jnp.zeros((half, S5_GROUP, p), F32)
    pr = jnp.ones_like(lbr)
    pi = jnp.zeros_like(lbr)
    for d in range(CHUNK):
        rows = slice(d * S5_GROUP, (d + 1) * S5_GROUP)
        back = slice((CHUNK - 1 - d) * S5_GROUP, (CHUNK - d) * S5_GROUP)
        bp_ref[:, back, lo] = bbr * pr - bbi * pi
        bp_ref[:, back, hi] = bbr * pi + bbi * pr
        pr, pi = pr * lbr - pi * lbi, pr * lbi + pi * lbr
        cp_r = c_re * pr - c_im * pi
        cp_i = -(c_re * pi + c_im * pr)
        cpre_ref[0:half, rows, lo] = cp_r[0:half]
        cpre_ref[0:half, rows, hi] = zeros
        cpre_ref[half:gb, rows, lo] = zeros
        cpre_ref[half:gb, rows, hi] = cp_r[half:gb]
        cpim_ref[0:half, rows, lo] = cp_i[0:half]
        cpim_ref[0:half, rows, hi] = zeros
        cpim_ref[half:gb, rows, lo] = zeros
        cpim_ref[half:gb, rows, hi] = cp_i[half:gb]

    qr, qi = pr, pi
    for j in range(N_DOUBLINGS):
        ar_ref[:, j:j + 1, lo] = qr[0:half]
        ar_ref[:, j:j + 1, hi] = qr[half:gb]
        ai_ref[:, j:j + 1, lo] = qi[0:half]
        ai_ref[:, j:j + 1, hi] = qi[half:gb]
        qr, qi = qr * qr - qi * qi, 2.0 * qr * qi
    ar_ref[:, N_DOUBLINGS:, :] = jnp.zeros((half, 8 - N_DOUBLINGS, 2 * p), F32)
    ai_ref[:, N_DOUBLINGS:, :] = jnp.zeros((half, 8 - N_DOUBLINGS, 2 * p), F32)

    lane = lax.broadcasted_iota(jnp.int32, (S5_GROUP, 128), 1)
    for j in range(gb):
        w = _dot3(cd_ref[j], bp_ref[j], NT)
        w0, w1 = w[:, :128], w[:, 128:]
        for t in range(CHUNK):
            rows = slice(t * S5_GROUP, (t + 1) * S5_GROUP)
            shift = (CHUNK - 1 - t) * S5_GROUP
            keep = 128 - shift % 128
            if shift == 0:
                left, right = w0, w1
            elif shift < 128:
                r0 = pltpu.roll(w0, keep, axis=1)
                r1 = pltpu.roll(w1, keep, axis=1)
                left = jnp.where(lane < keep, r0, r1)
                right = jnp.where(lane < keep, r1, 0.0)
            elif shift == 128:
                left, right = w1, jnp.zeros_like(w1)
            else:
                left = jnp.where(lane < keep, pltpu.roll(w1, keep, axis=1), 0.0)
                right = jnp.zeros_like(w1)
            l_ref[j, rows, 0:128] = left
            l_ref[j, rows, 128:256] = right
        l_ref[j, CW:L_ROWS, :] = bp_ref[j].T


def _s5_prep(lam_re, lam_im, log_dt, b_re, b_im, c_re, c_im):
    g, p, gb, half = S5_GROUPS, S5_STATE, GROUPS_PER_TILE, PAIRS_PER_TILE
    lre = lam_re.reshape(g, 1, p)
    lim = lam_im.reshape(g, 1, p)
    ldt = jnp.broadcast_to(log_dt.reshape(g, 1, 1), (g, 1, p))
    bre = jnp.transpose(b_re, (0, 2, 1))
    bim = jnp.transpose(b_im, (0, 2, 1))
    sd = jax.ShapeDtypeStruct
    blk = lambda n, r, c: pl.BlockSpec((n, r, c), lambda i: (i, 0, 0))
    return pl.pallas_call(
        _s5_prep_body,
        grid=(g // gb,),
        in_specs=[blk(gb, 1, p)] * 3 + [blk(gb, S5_GROUP, p)] * 4,
        out_specs=(
            blk(gb, L_ROWS, CW), blk(gb, CW, 2 * p), blk(gb, CW, 2 * p),
            blk(gb, S5_GROUP, 2 * p), blk(gb, S5_GROUP, 2 * p),
            blk(half, 8, 2 * p), blk(half, 8, 2 * p), blk(gb, 1, 2 * p), blk(gb, 1, 2 * p),
        ),
        out_shape=(
            sd((g, L_ROWS, CW), F32),
            sd((g, CW, 2 * p), F32),
            sd((g, CW, 2 * p), F32),
            sd((g, S5_GROUP, 2 * p), F32),
            sd((g, S5_GROUP, 2 * p), F32),
            sd((g // 2, 8, 2 * p), F32),
            sd((g // 2, 8, 2 * p), F32),
            sd((g, 1, 2 * p), F32),
            sd((g, 1, 2 * p), F32),
        ),
        scratch_shapes=[pltpu.VMEM((gb, CW, 2 * p), F32)],
        compiler_params=pltpu.CompilerParams(dimension_semantics=("parallel",)),
        name="s5_prep",
    )(lre, lim, ldt, bre, bim, c_re, c_im)


def _cmul_add(h, hs, ar, ai):
    return h + hs * ar + pltpu.roll(hs, S5_STATE, axis=1) * ai


def _s5p_body(u_ref, l_ref, cpre_ref, cpim_ref, ar_ref, ai_ref, y_ref, hre_ref, him_ref,
              ut_ref, yt_ref, *, nb, nk):
    gb, half = GROUPS_PER_TILE, PAIRS_PER_TILE
    p = S5_STATE
    nrow = nb * nk
    d = functools.partial(jnp.dot, preferred_element_type=F32)

    for t in range(CHUNK):
        xt = u_ref[pl.ds(t, nrow, stride=CHUNK), :].T
        for j in range(gb):
            ut_ref[j, t * S5_GROUP:(t + 1) * S5_GROUP, :] = xt[j * S5_GROUP:(j + 1) * S5_GROUP, :]

    s_re, s_im = [], []
    for j in range(gb):
        uh, ul = _split(ut_ref[j])
        lh, ll = _split(l_ref[j])
        r = d(lh, uh) + d(lh, ul) + d(ll, uh)
        yt_ref[j] = r[0:CW]
        s_re.append(r[CW:CW + p])
        s_im.append(r[CW + p:L_ROWS])

    rows = lax.broadcasted_iota(jnp.int32, (nrow, 2 * p), 0) & (nk - 1)
    for q in range(half):
        re = jnp.concatenate([s_re[q], s_re[q + half]], axis=0).T
        im = jnp.concatenate([s_im[q], s_im[q + half]], axis=0).T
        for k in range(N_DOUBLINGS):
            sh = 1 << k
            keep = rows >= sh
            rs = jnp.where(keep, pltpu.roll(re, sh, axis=0), 0.0)
            js = jnp.where(keep, pltpu.roll(im, sh, axis=0), 0.0)
            ar = ar_ref[q, k:k + 1, :]
            ai = ai_ref[q, k:k + 1, :]
            re, im = re + ar * rs - ai * js, im + ar * js + ai * rs
        for b in range(nb):
            hre_ref[q, b:b + 1, :] = re[(b + 1) * nk - 1:(b + 1) * nk, :]
            him_ref[q, b:b + 1, :] = im[(b + 1) * nk - 1:(b + 1) * nk, :]
        pre = jnp.where(rows >= 1, pltpu.roll(re, 1, axis=0), 0.0)
        pim = jnp.where(rows >= 1, pltpu.roll(im, 1, axis=0), 0.0)
        for j in (q, q + half):
            yt_ref[j] = yt_ref[j] + _dot3(cpre_ref[j], pre, NT) + _dot3(cpim_ref[j], pim, NT)

    for t in range(CHUNK):
        yt = jnp.concatenate(
            [yt_ref[j, t * S5_GROUP:(t + 1) * S5_GROUP, :] for j in range(gb)], axis=0)
        y_ref[pl.ds(t, nrow, stride=CHUNK), :] = yt.T


def _s5_prompt(proj_p, lmat, cpre, cpim, ar, ai, *, nb, seq):
    g, gb, half = S5_GROUPS, GROUPS_PER_TILE, PAIRS_PER_TILE
    n_p = nb * seq
    nk = seq // CHUNK
    blk = lambda n, r, c: pl.BlockSpec((n, r, c), lambda i: (i, 0, 0))
    return pl.pallas_call(
        functools.partial(_s5p_body, nb=nb, nk=nk),
        grid=(g // gb,),
        in_specs=[pl.BlockSpec((n_p, 128), lambda i: (0, i)),
                  blk(gb, L_ROWS, CW), blk(gb, CW, 2 * S5_STATE), blk(gb, CW, 2 * S5_STATE),
                  blk(half, 8, 2 * S5_STATE), blk(half, 8, 2 * S5_STATE)],
        out_specs=(pl.BlockSpec((n_p, 128), lambda i: (0, i)),
                   blk(half, nb, 2 * S5_STATE), blk(half, nb, 2 * S5_STATE)),
        out_shape=(jax.ShapeDtypeStruct((n_p, D_S5), F32),
                   jax.ShapeDtypeStruct((g // 2, nb, 2 * S5_STATE), F32),
                   jax.ShapeDtypeStruct((g // 2, nb, 2 * S5_STATE), F32)),
        scratch_shapes=[pltpu.VMEM((gb, CW, nb * nk), F32), pltpu.VMEM((gb, CW, nb * nk), F32)],
        compiler_params=pltpu.CompilerParams(
            dimension_semantics=("parallel",), vmem_limit_bytes=VMEM_LIMIT),
        name="s5_prompt",
    )(proj_p, lmat, cpre, cpim, ar, ai)


def _s5d_body(u_ref, h0_ref, bd_ref, cd_ref, lr_ref, li_ref, y_ref, h_ref, *, gb):
    for j in range(gb):
        h0 = h0_ref[j]
        bu = _dot3(u_ref[j], bd_ref[j])
        h = _cmul_add(bu, h0, lr_ref[j], li_ref[j])
        h_ref[j] = h
        y_ref[j] = _dot3(h, cd_ref[j], NT)


def _s5_decode(ud, h0, bd, cd, lr, li, *, gb=8):
    g, nbatch = S5_GROUPS, ud.shape[1]
    blk = lambda r, c: pl.BlockSpec((gb, r, c), lambda i: (i, 0, 0))
    return pl.pallas_call(
        functools.partial(_s5d_body, gb=gb),
        grid=(g // gb,),
        in_specs=[blk(nbatch, S5_GROUP), blk(nbatch, 2 * S5_STATE), blk(S5_GROUP, 2 * S5_STATE),
                  blk(S5_GROUP, 2 * S5_STATE), blk(1, 2 * S5_STATE), blk(1, 2 * S5_STATE)],
        out_specs=(blk(nbatch, S5_GROUP), blk(nbatch, 2 * S5_STATE)),
        out_shape=(jax.ShapeDtypeStruct((g, nbatch, S5_GROUP), F32),
                   jax.ShapeDtypeStruct((g, nbatch, 2 * S5_STATE), F32)),
        compiler_params=pltpu.CompilerParams(dimension_semantics=("parallel",)),
        name="s5_decode",
    )(ud, h0, bd, cd, lr, li)


def _lru_gates(xc, wa_ref, wx_ref, ba, bx, lam):
    xcb = xc.astype(BF16)
    nblk = D_LRU // 256
    r_parts, i_parts = [], []
    for k in range(nblk):
        xk = xcb[:, k * 256:(k + 1) * 256]
        r_parts.append(jnp.dot(xk, wa_ref[k], preferred_element_type=F32))
        i_parts.append(jnp.dot(xk, wx_ref[k], preferred_element_type=F32))
    r = jax.nn.sigmoid(jnp.concatenate(r_parts, axis=1) + ba)
    i = jax.nn.sigmoid(jnp.concatenate(i_parts, axis=1) + bx)
    z = -lam
    softplus = jnp.maximum(z, 0.0) + jnp.log1p(jnp.exp(-jnp.abs(z)))
    log_a = (-LRU_C * softplus) * r
    a = jnp.exp(log_a)
    mult = jnp.sqrt(-jnp.tanh(log_a) * (a * a + 1.0))
    return a, mult * (i * xc)


def _lru_prompt_body(xl_ref, gate_ref, cw_ref, cb_ref, wa_ref, wx_ref, ba_ref, bx_ref, lam_ref,
                     o_ref, hl_ref, xbuf_ref, carry_ref, *, tt):
    t = pl.program_id(1)

    @pl.when(t == 0)
    def _():
        xbuf_ref[0:8, :] = jnp.zeros((8, D_LRU), F32)
        carry_ref[...] = jnp.zeros((8, D_LRU), F32)

    x = xl_ref[...]
    xbuf_ref[8:8 + tt, :] = x
    cw = cw_ref[...]
    xc = (cb_ref[...] + xbuf_ref[5:5 + tt, :] * cw[0:1] + xbuf_ref[6:6 + tt, :] * cw[1:2]
          + xbuf_ref[7:7 + tt, :] * cw[2:3] + x * cw[3:4])
    xbuf_ref[0:8, :] = x[tt - 8:tt, :]

    a, b = _lru_gates(xc, wa_ref, wx_ref, ba_ref[...], bx_ref[...], lam_ref[...])

    rows = lax.broadcasted_iota(jnp.int32, (tt, D_LRU), 0)
    sh = 1
    while sh < tt:
        if sh < 8:
            m = rows >= sh
            b = jnp.where(m, b + a * pltpu.roll(b, sh, axis=0), b)
            a = jnp.where(m, a * pltpu.roll(a, sh, axis=0), a)
        else:
            b = jnp.concatenate([b[:sh], b[sh:] + a[sh:] * b[:tt - sh]], axis=0)
            a = jnp.concatenate([a[:sh], a[sh:] * a[:tt - sh]], axis=0)
        sh *= 2
    h = b + a * carry_ref[0:1, :]
    hlast = h[tt - 1:tt, :]
    carry_ref[...] = jnp.broadcast_to(hlast, (8, D_LRU))
    hl_ref[0] = hlast
    o_ref[...] = h * _gelu(gate_ref[...])


def _lru_prompt(proj, cw, cb, wa, wx, ba, bx, lam, *, nb, seq, tt=256):
    nt = seq // tt
    vec = lambda r: pl.BlockSpec((r, D_LRU), lambda b, t: (0, 0))
    wspec = pl.BlockSpec((D_LRU // 256, 256, 256), lambda b, t: (0, 0, 0))
    return pl.pallas_call(
        functools.partial(_lru_prompt_body, tt=tt),
        grid=(nb, nt),
        in_specs=[
            pl.BlockSpec((tt, D_LRU), lambda b, t: (b * nt + t, 1)),
            pl.BlockSpec((tt, D_LRU), lambda b, t: (b * nt + t, 2)),
            vec(CONV_W), vec(1), wspec, wspec, vec(1), vec(1), vec(1),
        ],
        out_specs=(pl.BlockSpec((tt, D_LRU), lambda b, t: (b * nt + t, 0)),
                   pl.BlockSpec((1, 1, D_LRU), lambda b, t: (b, 0, 0))),
        out_shape=(jax.ShapeDtypeStruct((nb * seq, D_LRU), F32),
                   jax.ShapeDtypeStruct((nb, 1, D_LRU), F32)),
        scratch_shapes=[pltpu.VMEM((tt + 8, D_LRU), F32), pltpu.VMEM((8, D_LRU), F32)],
        compiler_params=pltpu.CompilerParams(
            dimension_semantics=("parallel", "arbitrary"), vmem_limit_bytes=VMEM_LIMIT),
        name="lru_prompt",
    )(proj, proj, cw, cb, wa, wx, ba, bx, lam)


def _lru_decode_body(xl_ref, gate_ref, c0_ref, c1_ref, c2_ref, h0_ref, cw_ref, cb_ref,
                     wa_ref, wx_ref, ba_ref, bx_ref, lam_ref, o_ref, h_ref):
    x = xl_ref[...]
    cw = cw_ref[...]
    xc = (cb_ref[...] + c0_ref[...] * cw[0:1] + c1_ref[...] * cw[1:2]
          + c2_ref[...] * cw[2:3] + x * cw[3:4])
    a, b = _lru_gates(xc, wa_ref, wx_ref, ba_ref[...], bx_ref[...], lam_ref[...])
    h = a * h0_ref[...] + b
    h_ref[...] = h
    o_ref[...] = h * _gelu(gate_ref[...])


def _lru_decode(proj_d, c0, c1, c2, h0, cw, cb, wa, wx, ba, bx, lam):
    nd = proj_d.shape[0]
    full = lambda r: pl.BlockSpec((r, D_LRU), lambda i: (0, 0))
    wspec = pl.BlockSpec((D_LRU // 256, 256, 256), lambda i: (0, 0, 0))
    return pl.pallas_call(
        _lru_decode_body,
        grid=(1,),
        in_specs=[
            pl.BlockSpec((nd, D_LRU), lambda i: (0, 1)),
            pl.BlockSpec((nd, D_LRU), lambda i: (0, 2)),
            full(nd), full(nd), full(nd), full(nd),
            full(CONV_W), full(1), wspec, wspec, full(1), full(1), full(1),
        ],
        out_specs=(full(nd), full(nd)),
        out_shape=(jax.ShapeDtypeStruct((nd, D_LRU), F32),
                   jax.ShapeDtypeStruct((nd, D_LRU), F32)),
        name="lru_decode",
    )(proj_d, proj_d, c0, c1, c2, h0, cw, cb, wa, wx, ba, bx, lam)


def _mix_body(ysp_ref, up_ref, lrup_ref, xp_ref, ysd_ref, ud_ref, lrud_ref, xd_ref,
              dsk_ref, wg_ref, bg_ref, gs_ref, gl_ref, wo_ref, op_ref, od_ref, *, n_tiles):
    i = pl.program_id(0)

    def run(ys_ref, u_ref, lru_ref, x_ref, o_ref):
        yy = ys_ref[...] + dsk_ref[...] * u_ref[...]
        g = _gelu(yy)
        z = jnp.dot(g.astype(BF16), wg_ref[...], preferred_element_type=F32) + bg_ref[...]
        s5o = g * jax.nn.sigmoid(z)
        n1 = _rms(s5o, gs_ref[...]).astype(BF16)
        n2 = _rms(lru_ref[...], gl_ref[...]).astype(BF16)
        o_ref[...] = (x_ref[...]
                      + jnp.dot(n1, wo_ref[0:D_S5, :], preferred_element_type=F32)
                      + jnp.dot(n2, wo_ref[D_S5:, :], preferred_element_type=F32))

    @pl.when(i < n_tiles)
    def _():
        run(ysp_ref, up_ref, lrup_ref, xp_ref, op_ref)

    @pl.when(i == n_tiles)
    def _():
        run(ysd_ref, ud_ref, lrud_ref, xd_ref, od_ref)


def _mix(ys_p, proj_p, lru_p, x1_p, ys_d, proj_d, lru_d, x1_d, dsk, wg, bg, gs, gl, wo, *, tm=512):
    n_p, nd = x1_p.shape[0], x1_d.shape[0]
    n_tiles = n_p // tm
    prow = lambda c: pl.BlockSpec((tm, c), lambda i: (jnp.minimum(i, n_tiles - 1), 0))
    drow = lambda c: pl.BlockSpec((nd, c), lambda i: (0, 0))
    vec = lambda c: pl.BlockSpec((1, c), lambda i: (0, 0))
    return pl.pallas_call(
        functools.partial(_mix_body, n_tiles=n_tiles),
        grid=(n_tiles + 1,),
        in_specs=[prow(D_S5), prow(D_S5), prow(D_LRU), prow(D_MODEL),
                  drow(D_S5), drow(D_S5), drow(D_LRU), drow(D_MODEL), vec(D_S5),
                  pl.BlockSpec((D_S5, D_S5), lambda i: (0, 0)), vec(D_S5), vec(D_S5), vec(D_LRU),
                  pl.BlockSpec((D_MODEL, D_MODEL), lambda i: (0, 0))],
        out_specs=(prow(D_MODEL), drow(D_MODEL)),
        out_shape=(jax.ShapeDtypeStruct((n_p, D_MODEL), F32),
                   jax.ShapeDtypeStruct((nd, D_MODEL), F32)),
        compiler_params=pltpu.CompilerParams(
            dimension_semantics=("arbitrary",), vmem_limit_bytes=VMEM_LIMIT),
        name="mix",
    )(ys_p, proj_p, lru_p, x1_p, ys_d, proj_d, lru_d, x1_d, dsk, wg, bg, gs, gl, wo)


def _unpair(h, nb):
    tiles = S5_GROUPS // GROUPS_PER_TILE
    h5 = h.reshape(tiles, PAIRS_PER_TILE, nb, 2, S5_STATE)
    return jnp.transpose(h5, (2, 0, 3, 1, 4)).reshape(nb, S5_GROUPS, S5_STATE)


def _block_diag4(w):
    w4 = w.reshape(LRU_HEADS // 4, 4, LRU_HEAD_DIM, LRU_HEAD_DIM)
    eye = jnp.eye(4, dtype=w.dtype)
    return jnp.einsum("kaij,ab->kaibj", w4, eye).reshape(LRU_HEADS // 4, 256, 256)


def kernel(x_prompt, x_sample, state_s5_re, state_s5_im, state_lru_h, state_lru_conv, g_ffn1, w1_a, w3_a, w2_a, g_mix, w_in, lam_re, lam_im, log_dt, b_re, b_im, c_re, c_im, d_skip, w_glu, b_glu, conv_w, conv_b, w_a, b_a, w_x, b_x, lam_l, g_out_s5, g_out_lru, w_out, g_ffn2, w1_b, w3_b, w2_b, g_final):
    nb, seq, _ = x_prompt.shape
    nd = x_sample.shape[0]
    n_p = nb * seq
    nk = seq // CHUNK
    g, p = S5_GROUPS, S5_STATE
    row = lambda v: v.reshape(1, -1)

    xp = x_prompt.reshape(n_p, D_MODEL)
    xd = x_sample.reshape(nd, D_MODEL)

    x1_p, x1_d = _ffn(xp, xd, row(g_ffn1[0]), w1_a[0].astype(BF16), w3_a[0].astype(BF16),
                      w2_a[0].astype(BF16))
    proj_p, proj_d = _inproj(x1_p, x1_d, row(g_mix[0]), w_in[0].astype(BF16))

    lmat, cpre, cpim, bd, cd, ar, ai, lr, li = _s5_prep(
        lam_re[0], lam_im[0], log_dt[0], b_re[0], b_im[0], c_re[0], c_im[0])
    ys_p, hf_re, hf_im = _s5_prompt(proj_p, lmat, cpre, cpim, ar, ai, nb=nb, seq=seq)

    ud = jnp.transpose(proj_d[:, :D_S5].reshape(nd, g, S5_GROUP), (1, 0, 2))
    h0d = jnp.concatenate([jnp.transpose(state_s5_re[0], (1, 0, 2)),
                           jnp.transpose(state_s5_im[0], (1, 0, 2))], axis=-1)
    ydg, hd = _s5_decode(ud, h0d, bd, cd, lr, li)
    ys_d = jnp.transpose(ydg, (1, 0, 2)).reshape(nd, D_S5)

    wa_bd = _block_diag4(w_a[0]).astype(BF16)
    wx_bd = _block_diag4(w_x[0]).astype(BF16)
    lru_args = (conv_w[0], row(conv_b[0]), wa_bd, wx_bd, row(b_a[0]), row(b_x[0]), row(lam_l[0]))
    lru_p, hl_p = _lru_prompt(proj_p, *lru_args, nb=nb, seq=seq)
    conv0 = state_lru_conv[0]
    lru_d, hl_d = _lru_decode(proj_d, conv0[:, 0], conv0[:, 1], conv0[:, 2], state_lru_h[0],
                              *lru_args)

    x2_p, x2_d = _mix(ys_p, proj_p, lru_p, x1_p, ys_d, proj_d, lru_d, x1_d,
                      row(d_skip[0]), w_glu[0].astype(BF16), row(b_glu[0]),
                      row(g_out_s5[0]), row(g_out_lru[0]), w_out[0].astype(BF16))
    y_p, y_d = _ffn(x2_p, x2_d, row(g_ffn2[0]), w1_b[0].astype(BF16), w3_b[0].astype(BF16),
                    w2_b[0].astype(BF16), row(g_final))

    tail_p = proj_p.reshape(nb, seq, -1)[:, seq - (CONV_W - 1):, D_S5:D_S5 + D_LRU]
    xl_d = proj_d[:, D_S5:D_S5 + D_LRU]
    return (
        y_p.reshape(nb, seq, D_MODEL),
        y_d.reshape(nd, 1, D_MODEL),
        _unpair(hf_re, nb)[None],
        _unpair(hf_im, nb)[None],
        hl_p.reshape(1, nb, D_LRU),
        tail_p[None],
        jnp.transpose(hd[:, :, :p], (1, 0, 2))[None],
        jnp.transpose(hd[:, :, p:], (1, 0, 2))[None],
        hl_d[None],
        jnp.stack([conv0[:, 1], conv0[:, 2], xl_d], axis=1)[None],
    )
```

```python
import functools

import jax
import jax.numpy as jnp
from jax import lax
from jax.experimental import pallas as pl
from jax.experimental.pallas import tpu as pltpu

F32 = jnp.float32
BF16 = jnp.bfloat16

D_MODEL = 2048
D_S5 = 1024
S5_GROUP = 16
S5_GROUPS = 64
S5_STATE = 64
D_LRU = 1024
LRU_HEADS = 16
LRU_HEAD_DIM = 64
CONV_W = 4
LRU_C = 8.0
D_FF = 5632
EPS = 1e-6

CHUNK = 16
CW = CHUNK * S5_GROUP
N_DOUBLINGS = 7

VMEM_LIMIT = 58 * 1024 * 1024

NN = (((1,), (0,)), ((), ()))
NT = (((1,), (1,)), ((), ()))


def _rms(x, g):
    return x * lax.rsqrt(jnp.mean(x * x, axis=-1, keepdims=True) + EPS) * g


def _split(x):
    hi = x.astype(BF16)
    lo = (x - hi.astype(F32)).astype(BF16)
    return hi, lo


def _dot3(a, b, dims=NN):
    ah, al = _split(a)
    bh, bl = _split(b)
    d = functools.partial(lax.dot_general, dimension_numbers=dims, preferred_element_type=F32)
    return d(ah, bh) + d(al, bh) + d(ah, bl)


def _gelu(x):
    return jax.nn.gelu(x, approximate=True)


def _ffn_body(*refs, final_norm, n_tiles, nd):
    if final_norm:
        xp_ref, xd_ref, g_ref, w1_ref, w3_ref, w2_ref, gf_ref, op_ref, od_ref, xn_ref = refs
    else:
        xp_ref, xd_ref, g_ref, w1_ref, w3_ref, w2_ref, op_ref, od_ref, xn_ref = refs
    i = pl.program_id(0)
    f = pl.program_id(1)

    def run(x_ref, o_ref, rows):
        @pl.when(f == 0)
        def _():
            x = x_ref[...]
            xn_ref[0:rows, :] = _rms(x, g_ref[...]).astype(BF16)
            o_ref[...] = x

        xn = xn_ref[0:rows, :]
        a = jnp.dot(xn, w1_ref[...], preferred_element_type=F32)
        b = jnp.dot(xn, w3_ref[...], preferred_element_type=F32)
        h = (a * jax.nn.sigmoid(a) * b).astype(BF16)
        o_ref[...] += 0.5 * jnp.dot(h, w2_ref[...], preferred_element_type=F32)

        if final_norm:
            @pl.when(f == pl.num_programs(1) - 1)
            def _():
                o_ref[...] = _rms(o_ref[...], gf_ref[...])

    @pl.when(i < n_tiles)
    def _():
        run(xp_ref, op_ref, xp_ref.shape[0])

    @pl.when(i == n_tiles)
    def _():
        run(xd_ref, od_ref, nd)


def _ffn(xp, xd, g, w1, w3, w2, g_final=None, *, tm=1024, tf=512):
    n_p, nd = xp.shape[0], xd.shape[0]
    n_tiles = n_p // tm
    final_norm = g_final is not None
    pspec = pl.BlockSpec((tm, D_MODEL), lambda i, f: (jnp.minimum(i, n_tiles - 1), 0))
    dspec = pl.BlockSpec((nd, D_MODEL), lambda i, f: (0, 0))
    in_specs = [
        pspec, dspec,
        pl.BlockSpec((1, D_MODEL), lambda i, f: (0, 0)),
        pl.BlockSpec((D_MODEL, tf), lambda i, f: (0, f)),
        pl.BlockSpec((D_MODEL, tf), lambda i, f: (0, f)),
        pl.BlockSpec((tf, D_MODEL), lambda i, f: (f, 0)),
    ]
    args = [xp, xd, g, w1, w3, w2]
    if final_norm:
        in_specs.append(pl.BlockSpec((1, D_MODEL), lambda i, f: (0, 0)))
        args.append(g_final)
    return pl.pallas_call(
        functools.partial(_ffn_body, final_norm=final_norm, n_tiles=n_tiles, nd=nd),
        grid=(n_tiles + 1, D_FF // tf),
        in_specs=in_specs,
        out_specs=(pspec, dspec),
        out_shape=(jax.ShapeDtypeStruct((n_p, D_MODEL), F32),
                   jax.ShapeDtypeStruct((nd, D_MODEL), F32)),
        scratch_shapes=[pltpu.VMEM((tm, D_MODEL), BF16)],
        compiler_params=pltpu.CompilerParams(
            dimension_semantics=("arbitrary", "arbitrary"), vmem_limit_bytes=VMEM_LIMIT),
        name="ffn_final" if final_norm else "ffn",
    )(*args)


def _inproj_body(xp_ref, xd_ref, g_ref, w_ref, op_ref, od_ref, xn_ref, *, n_tiles, nd):
    i = pl.program_id(0)
    j = pl.program_id(1)
    nj = pl.num_programs(1)

    def run(x_ref, o_ref, rows):
        @pl.when(j == 0)
        def _():
            xn_ref[0:rows, :] = _rms(x_ref[...], g_ref[...]).astype(BF16)

        @pl.when(j < nj - 1)
        def _():
            o_ref[...] = jnp.dot(xn_ref[0:rows, :], w_ref[...], preferred_element_type=F32)

        @pl.when(j == nj - 1)
        def _():
            o_ref[...] = _gelu(jnp.dot(xn_ref[0:rows, :], w_ref[...],
                                       preferred_element_type=F32))

    @pl.when(i < n_tiles)
    def _():
        run(xp_ref, op_ref, xp_ref.shape[0])

    @pl.when(i == n_tiles)
    def _():
        run(xd_ref, od_ref, nd)


def _inproj(xp, xd, g, w, *, tm=1024, tn=1024):
    n_p, nd = xp.shape[0], xd.shape[0]
    n_tiles = n_p // tm
    d_out = w.shape[1]
    nj = d_out // tn
    return pl.pallas_call(
        functools.partial(_inproj_body, n_tiles=n_tiles, nd=nd),
        grid=(n_tiles + 1, nj),
        in_specs=[
            pl.BlockSpec((tm, D_MODEL), lambda i, j: (jnp.minimum(i, n_tiles - 1), 0)),
            pl.BlockSpec((nd, D_MODEL), lambda i, j: (0, 0)),
            pl.BlockSpec((1, D_MODEL), lambda i, j: (0, 0)),
            pl.BlockSpec((D_MODEL, tn), lambda i, j: (0, j)),
        ],
        out_specs=(
            pl.BlockSpec((tm, tn), lambda i, j: (jnp.minimum(i, n_tiles - 1),
                                                 jnp.where(i < n_tiles, j, nj - 1))),
            pl.BlockSpec((nd, tn), lambda i, j: (0, jnp.where(i < n_tiles, 0, j))),
        ),
        out_shape=(jax.ShapeDtypeStruct((n_p, d_out), F32),
                   jax.ShapeDtypeStruct((nd, d_out), F32)),
        scratch_shapes=[pltpu.VMEM((tm, D_MODEL), BF16)],
        compiler_params=pltpu.CompilerParams(
            dimension_semantics=("arbitrary", "arbitrary"), vmem_limit_bytes=VMEM_LIMIT),
        name="inproj",
    )(xp, xd, g, w)


GROUPS_PER_TILE = 128 // S5_GROUP
PAIRS_PER_TILE = GROUPS_PER_TILE // 2
L_ROWS = CW + 2 * S5_STATE


def _s5_prep_body(lre_ref, lim_ref, ldt_ref, bre_ref, bim_ref, cre_ref, cim_ref,
                  l_ref, cpre_ref, cpim_ref, bd_ref, cd_ref, ar_ref, ai_ref, lr_ref, li_ref,
                  bp_ref):
    gb, half = GROUPS_PER_TILE, PAIRS_PER_TILE
    p = S5_STATE
    lo, hi = slice(0, p), slice(p, 2 * p)
    lam_re = lre_ref[...]
    lam_im = lim_ref[...]
    dt = jnp.exp(ldt_ref[...])
    mag = jnp.exp(lam_re * dt)
    ang = lam_im * dt
    lbr = mag * jnp.cos(ang)
    lbi = mag * jnp.sin(ang)
    lr_ref[:, :, lo] = lbr
    lr_ref[:, :, hi] = lbr
    li_ref[:, :, lo] = -lbi
    li_ref[:, :, hi] = lbi
    nr = lbr - 1.0
    den = lam_re * lam_re + lam_im * lam_im
    cr = (nr * lam_re + lbi * lam_im) / den
    ci = (lbi * lam_re - nr * lam_im) / den
    b_re = bre_ref[...]
    b_im = bim_ref[...]
    bbr = cr * b_re - ci * b_im
    bbi = cr * b_im + ci * b_re
    bd_ref[:, :, lo] = bbr
    bd_ref[:, :, hi] = bbi
    c_re = cre_ref[...]
    c_im = cim_ref[...]
    cd_ref[:, :, lo] = c_re
    cd_ref[:, :, hi] = -c_im

    zeros = jnp.zeros((half, S5_GROUP, p), F32)
    pr = jnp.ones_like(lbr)
    pi = jnp.zeros_like(lbr)
    for d in range(CHUNK):
        rows = slice(d * S5_GROUP, (d + 1) * S5_GROUP)
        back = slice((CHUNK - 1 - d) * S5_GROUP, (CHUNK - d) * S5_GROUP)
        bp_ref[:, back, lo] = bbr * pr - bbi * pi
        bp_ref[:, back, hi] = bbr * pi + bbi * pr
        pr, pi = pr * lbr - pi * lbi, pr * lbi + pi * lbr
        cp_r = c_re * pr - c_im * pi
        cp_i = -(c_re * pi + c_im * pr)
        cpre_ref[0:half, rows, lo] = cp_r[0:half]
        cpre_ref[0:half, rows, hi] = zeros
        cpre_ref[half:gb, rows, lo] = zeros
        cpre_ref[half:gb, rows, hi] = cp_r[half:gb]
        cpim_ref[0:half, rows, lo] = cp_i[0:half]
        cpim_ref[0:half, rows, hi] = zeros
        cpim_ref[half:gb, rows, lo] = zeros
        cpim_ref[half:gb, rows, hi] = cp_i[half:gb]

    qr, qi = pr, pi
    for j in range(N_DOUBLINGS):
        ar_ref[:, j:j + 1, lo] = qr[0:half]
        ar_ref[:, j:j + 1, hi] = qr[half:gb]
        ai_ref[:, j:j + 1, lo] = qi[0:half]
        ai_ref[:, j:j + 1, hi] = qi[half:gb]
        qr, qi = qr * qr - qi * qi, 2.0 * qr * qi
    ar_ref[:, N_DOUBLINGS:, :] = jnp.zeros((half, 8 - N_DOUBLINGS, 2 * p), F32)
    ai_ref[:, N_DOUBLINGS:, :] = jnp.zeros((half, 8 - N_DOUBLINGS, 2 * p), F32)

    lane = lax.broadcasted_iota(jnp.int32, (S5_GROUP, 128), 1)
    for j in range(gb):
        w = _dot3(cd_ref[j], bp_ref[j], NT)
        w0, w1 = w[:, :128], w[:, 128:]
        for t in range(CHUNK):
            rows = slice(t * S5_GROUP, (t + 1) * S5_GROUP)
            shift = (CHUNK - 1 - t) * S5_GROUP
            keep = 128 - shift % 128
            if shift == 0:
                left, right = w0, w1
            elif shift < 128:
                r0 = pltpu.roll(w0, keep, axis=1)
                r1 = pltpu.roll(w1, keep, axis=1)
                left = jnp.where(lane < keep, r0, r1)
                right = jnp.where(lane < keep, r1, 0.0)
            elif shift == 128:
                left, right = w1, jnp.zeros_like(w1)
            else:
                left = jnp.where(lane < keep, pltpu.roll(w1, keep, axis=1), 0.0)
                right = jnp.zeros_like(w1)
            l_ref[j, rows, 0:128] = left
            l_ref[j, rows, 128:256] = right
        l_ref[j, CW:L_ROWS, :] = bp_ref[j].T


def _s5_prep(lam_re, lam_im, log_dt, b_re, b_im, c_re, c_im):
    g, p, gb, half = S5_GROUPS, S5_STATE, GROUPS_PER_TILE, PAIRS_PER_TILE
    lre = lam_re.reshape(g, 1, p)
    lim = lam_im.reshape(g, 1, p)
    ldt = jnp.broadcast_to(log_dt.reshape(g, 1, 1), (g, 1, p))
    bre = jnp.transpose(b_re, (0, 2, 1))
    bim = jnp.transpose(b_im, (0, 2, 1))
    sd = jax.ShapeDtypeStruct
    blk = lambda n, r, c: pl.BlockSpec((n, r, c), lambda i: (i, 0, 0))
    return pl.pallas_call(
        _s5_prep_body,
        grid=(g // gb,),
        in_specs=[blk(gb, 1, p)] * 3 + [blk(gb, S5_GROUP, p)] * 4,
        out_specs=(
            blk(gb, L_ROWS, CW), blk(gb, CW, 2 * p), blk(gb, CW, 2 * p),
            blk(gb, S5_GROUP, 2 * p), blk(gb, S5_GROUP, 2 * p),
            blk(half, 8, 2 * p), blk(half, 8, 2 * p), blk(gb, 1, 2 * p), blk(gb, 1, 2 * p),
        ),
        out_shape=(
            sd((g, L_ROWS, CW), F32),
            sd((g, CW, 2 * p), F32),
            sd((g, CW, 2 * p), F32),
            sd((g, S5_GROUP, 2 * p), F32),
            sd((g, S5_GROUP, 2 * p), F32),
            sd((g // 2, 8, 2 * p), F32),
            sd((g // 2, 8, 2 * p), F32),
            sd((g, 1, 2 * p), F32),
            sd((g, 1, 2 * p), F32),
        ),
        scratch_shapes=[pltpu.VMEM((gb, CW, 2 * p), F32)],
        compiler_params=pltpu.CompilerParams(dimension_semantics=("parallel",)),
        name="s5_prep",
    )(lre, lim, ldt, bre, bim, c_re, c_im)


def _cmul_add(h, hs, ar, ai):
    return h + hs * ar + pltpu.roll(hs, S5_STATE, axis=1) * ai


def _s5p_body(u_ref, l_ref, cpre_ref, cpim_ref, ar_ref, ai_ref, y_ref, hre_ref, him_ref,
              ut_ref, yt_ref, *, nb, nk):
    gb, half = GROUPS_PER_TILE, PAIRS_PER_TILE
    p = S5_STATE
    nrow = nb * nk
    d = functools.partial(jnp.dot, preferred_element_type=F32)

    for t in range(CHUNK):
        xt = u_ref[pl.ds(t, nrow, stride=CHUNK), :].T
        for j in range(gb):
            ut_ref[j, t * S5_GROUP:(t + 1) * S5_GROUP, :] = xt[j * S5_GROUP:(j + 1) * S5_GROUP, :]

    s_re, s_im = [], []
    for j in range(gb):
        uh, ul = _split(ut_ref[j])
        lh, ll = _split(l_ref[j])
        r = d(lh, uh) + d(lh, ul) + d(ll, uh)
        yt_ref[j] = r[0:CW]
        s_re.append(r[CW:CW + p])
        s_im.append(r[CW + p:L_ROWS])

    rows = lax.broadcasted_iota(jnp.int32, (nrow, 2 * p), 0) & (nk - 1)
    for q in range(half):
        re = jnp.concatenate([s_re[q], s_re[q + half]], axis=0).T
        im = jnp.concatenate([s_im[q], s_im[q + half]], axis=0).T
        for k in range(N_DOUBLINGS):
            sh = 1 << k
            keep = rows >= sh
            rs = jnp.where(keep, pltpu.roll(re, sh, axis=0), 0.0)
            js = jnp.where(keep, pltpu.roll(im, sh, axis=0), 0.0)
            ar = ar_ref[q, k:k + 1, :]
            ai = ai_ref[q, k:k + 1, :]
            re, im = re + ar * rs - ai * js, im + ar * js + ai * rs
        for b in range(nb):
            hre_ref[q, b:b + 1, :] = re[(b + 1) * nk - 1:(b + 1) * nk, :]
            him_ref[q, b:b + 1, :] = im[(b + 1) * nk - 1:(b + 1) * nk, :]
        pre = jnp.where(rows >= 1, pltpu.roll(re, 1, axis=0), 0.0)
        pim = jnp.where(rows >= 1, pltpu.roll(im, 1, axis=0), 0.0)
        for j in (q, q + half):
            yt_ref[j] = yt_ref[j] + _dot3(cpre_ref[j], pre, NT) + _dot3(cpim_ref[j], pim, NT)

    for t in range(CHUNK):
        yt = jnp.concatenate(
            [yt_ref[j, t * S5_GROUP:(t + 1) * S5_GROUP, :] for j in range(gb)], axis=0)
        y_ref[pl.ds(t, nrow, stride=CHUNK), :] = yt.T


def _s5_prompt(proj_p, lmat, cpre, cpim, ar, ai, *, nb, seq):
    g, gb, half = S5_GROUPS, GROUPS_PER_TILE, PAIRS_PER_TILE
    n_p = nb * seq
    nk = seq // CHUNK
    blk = lambda n, r, c: pl.BlockSpec((n, r, c), lambda i: (i, 0, 0))
    return pl.pallas_call(
        functools.partial(_s5p_body, nb=nb, nk=nk),
        grid=(g // gb,),
        in_specs=[pl.BlockSpec((n_p, 128), lambda i: (0, i)),
                  blk(gb, L_ROWS, CW), blk(gb, CW, 2 * S5_STATE), blk(gb, CW, 2 * S5_STATE),
                  blk(half, 8, 2 * S5_STATE), blk(half, 8, 2 * S5_STATE)],
        out_specs=(pl.BlockSpec((n_p, 128), lambda i: (0, i)),
                   blk(half, nb, 2 * S5_STATE), blk(half, nb, 2 * S5_STATE)),
        out_shape=(jax.ShapeDtypeStruct((n_p, D_S5), F32),
                   jax.ShapeDtypeStruct((g // 2, nb, 2 * S5_STATE), F32),
                   jax.ShapeDtypeStruct((g // 2, nb, 2 * S5_STATE), F32)),
        scratch_shapes=[pltpu.VMEM((gb, CW, nb * nk), F32), pltpu.VMEM((gb, CW, nb * nk), F32)],
        compiler_params=pltpu.CompilerParams(
            dimension_semantics=("parallel",), vmem_limit_bytes=VMEM_LIMIT),
        name="s5_prompt",
    )(proj_p, lmat, cpre, cpim, ar, ai)


def _s5d_body(u_ref, h0_ref, bd_ref, cd_ref, lr_ref, li_ref, y_ref, h_ref, *, gb):
    for j in range(gb):
        h0 = h0_ref[j]
        bu = _dot3(u_ref[j], bd_ref[j])
        h = _cmul_add(bu, h0, lr_ref[j], li_ref[j])
        h_ref[j] = h
        y_ref[j] = _dot3(h, cd_ref[j], NT)


def _s5_decode(ud, h0, bd, cd, lr, li, *, gb=8):
    g, nbatch = S5_GROUPS, ud.shape[1]
    blk = lambda r, c: pl.BlockSpec((gb, r, c), lambda i: (i, 0, 0))
    return pl.pallas_call(
        functools.partial(_s5d_body, gb=gb),
        grid=(g // gb,),
        in_specs=[blk(nbatch, S5_GROUP), blk(nbatch, 2 * S5_STATE), blk(S5_GROUP, 2 * S5_STATE),
                  blk(S5_GROUP, 2 * S5_STATE), blk(1, 2 * S5_STATE), blk(1, 2 * S5_STATE)],
        out_specs=(blk(nbatch, S5_GROUP), blk(nbatch, 2 * S5_STATE)),
        out_shape=(jax.ShapeDtypeStruct((g, nbatch, S5_GROUP), F32),
                   jax.ShapeDtypeStruct((g, nbatch, 2 * S5_STATE), F32)),
        compiler_params=pltpu.CompilerParams(dimension_semantics=("parallel",)),
        name="s5_decode",
    )(ud, h0, bd, cd, lr, li)


def _lru_gates(xc, wa_ref, wx_ref, ba, bx, lam):
    xcb = xc.astype(BF16)
    nblk = D_LRU // 256
    r_parts, i_parts = [], []
    for k in range(nblk):
        xk = xcb[:, k * 256:(k + 1) * 256]
        r_parts.append(jnp.dot(xk, wa_ref[k], preferred_element_type=F32))
        i_parts.append(jnp.dot(xk, wx_ref[k], preferred_element_type=F32))
    r = jax.nn.sigmoid(jnp.concatenate(r_parts, axis=1) + ba)
    i = jax.nn.sigmoid(jnp.concatenate(i_parts, axis=1) + bx)
    z = -lam
    softplus = jnp.maximum(z, 0.0) + jnp.log1p(jnp.exp(-jnp.abs(z)))
    log_a = (-LRU_C * softplus) * r
    a = jnp.exp(log_a)
    v = -jnp.tanh(log_a) * (a * a + 1.0)
    mult = jnp.where(v > 0.0, v * lax.rsqrt(v), 0.0)
    return a, mult * (i * xc)


def _lru_prompt_body(xl_ref, gate_ref, cw_ref, cb_ref, wa_ref, wx_ref, ba_ref, bx_ref, lam_ref,
                     o_ref, hl_ref, xbuf_ref, carry_ref, *, tt):
    t = pl.program_id(1)

    @pl.when(t == 0)
    def _():
        xbuf_ref[0:8, :] = jnp.zeros((8, D_LRU), F32)
        carry_ref[...] = jnp.zeros((8, D_LRU), F32)

    x = xl_ref[...]
    xbuf_ref[8:8 + tt, :] = x
    cw = cw_ref[...]
    xc = (cb_ref[...] + xbuf_ref[5:5 + tt, :] * cw[0:1] + xbuf_ref[6:6 + tt, :] * cw[1:2]
          + xbuf_ref[7:7 + tt, :] * cw[2:3] + x * cw[3:4])
    xbuf_ref[0:8, :] = x[tt - 8:tt, :]

    a, b = _lru_gates(xc, wa_ref, wx_ref, ba_ref[...], bx_ref[...], lam_ref[...])

    nblk = tt // 8
    a3 = a.reshape(nblk, 8, D_LRU)
    b3 = b.reshape(nblk, 8, D_LRU)
    row = lax.broadcasted_iota(jnp.int32, (nblk, 8, D_LRU), 1)
    for sh in (1, 2, 4):
        keep = row >= sh
        bs = jnp.where(keep, pltpu.roll(b3, sh, axis=1), 0.0)
        sa = jnp.where(keep, pltpu.roll(a3, sh, axis=1), 1.0)
        b3 = b3 + a3 * bs
        a3 = a3 * sa
    carry = carry_ref[0:1, :]
    gate = gate_ref[...]
    for k in range(nblk):
        h = b3[k] + a3[k] * carry
        carry = h[7:8, :]
        o_ref[k * 8:(k + 1) * 8, :] = h * gate[k * 8:(k + 1) * 8, :]
    carry_ref[...] = jnp.broadcast_to(carry, (8, D_LRU))
    hl_ref[0] = carry


def _lru_prompt(proj, cw, cb, wa, wx, ba, bx, lam, *, nb, seq, tt=256):
    nt = seq // tt
    vec = lambda r: pl.BlockSpec((r, D_LRU), lambda b, t: (0, 0))
    wspec = pl.BlockSpec((D_LRU // 256, 256, 256), lambda b, t: (0, 0, 0))
    return pl.pallas_call(
        functools.partial(_lru_prompt_body, tt=tt),
        grid=(nb, nt),
        in_specs=[
            pl.BlockSpec((tt, D_LRU), lambda b, t: (b * nt + t, 1)),
            pl.BlockSpec((tt, D_LRU), lambda b, t: (b * nt + t, 2)),
            vec(CONV_W), vec(1), wspec, wspec, vec(1), vec(1), vec(1),
        ],
        out_specs=(pl.BlockSpec((tt, D_LRU), lambda b, t: (b * nt + t, 0)),
                   pl.BlockSpec((1, 1, D_LRU), lambda b, t: (b, 0, 0))),
        out_shape=(jax.ShapeDtypeStruct((nb * seq, D_LRU), F32),
                   jax.ShapeDtypeStruct((nb, 1, D_LRU), F32)),
        scratch_shapes=[pltpu.VMEM((tt + 8, D_LRU), F32), pltpu.VMEM((8, D_LRU), F32)],
        compiler_params=pltpu.CompilerParams(
            dimension_semantics=("parallel", "arbitrary"), vmem_limit_bytes=VMEM_LIMIT),
        name="lru_prompt",
    )(proj, proj, cw, cb, wa, wx, ba, bx, lam)


def _lru_decode_body(xl_ref, gate_ref, c0_ref, c1_ref, c2_ref, h0_ref, cw_ref, cb_ref,
                     wa_ref, wx_ref, ba_ref, bx_ref, lam_ref, o_ref, h_ref):
    x = xl_ref[...]
    cw = cw_ref[...]
    xc = (cb_ref[...] + c0_ref[...] * cw[0:1] + c1_ref[...] * cw[1:2]
          + c2_ref[...] * cw[2:3] + x * cw[3:4])
    a, b = _lru_gates(xc, wa_ref, wx_ref, ba_ref[...], bx_ref[...], lam_ref[...])
    h = a * h0_ref[...] + b
    h_ref[...] = h
    o_ref[...] = h * gate_ref[...]


def _lru_decode(proj_d, c0, c1, c2, h0, cw, cb, wa, wx, ba, bx, lam):
    nd = proj_d.shape[0]
    full = lambda r: pl.BlockSpec((r, D_LRU), lambda i: (0, 0))
    wspec = pl.BlockSpec((D_LRU // 256, 256, 256), lambda i: (0, 0, 0))
    return pl.pallas_call(
        _lru_decode_body,
        grid=(1,),
        in_specs=[
            pl.BlockSpec((nd, D_LRU), lambda i: (0, 1)),
            pl.BlockSpec((nd, D_LRU), lambda i: (0, 2)),
            full(nd), full(nd), full(nd), full(nd),
            full(CONV_W), full(1), wspec, wspec, full(1), full(1), full(1),
        ],
        out_specs=(full(nd), full(nd)),
        out_shape=(jax.ShapeDtypeStruct((nd, D_LRU), F32),
                   jax.ShapeDtypeStruct((nd, D_LRU), F32)),
        name="lru_decode",
    )(proj_d, proj_d, c0, c1, c2, h0, cw, cb, wa, wx, ba, bx, lam)


def _mix_body(ysp_ref, up_ref, lrup_ref, xp_ref, ysd_ref, ud_ref, lrud_ref, xd_ref,
              dsk_ref, wg_ref, bg_ref, gs_ref, gl_ref, wo_ref, op_ref, od_ref, *, n_tiles):
    i = pl.program_id(0)

    def run(ys_ref, u_ref, lru_ref, x_ref, o_ref):
        yy = ys_ref[...] + dsk_ref[...] * u_ref[...]
        g = _gelu(yy)
        z = jnp.dot(g.astype(BF16), wg_ref[...], preferred_element_type=F32) + bg_ref[...]
        s5o = g * jax.nn.sigmoid(z)
        n1 = _rms(s5o, gs_ref[...]).astype(BF16)
        n2 = _rms(lru_ref[...], gl_ref[...]).astype(BF16)
        o_ref[...] = (x_ref[...]
                      + jnp.dot(n1, wo_ref[0:D_S5, :], preferred_element_type=F32)
                      + jnp.dot(n2, wo_ref[D_S5:, :], preferred_element_type=F32))

    @pl.when(i < n_tiles)
    def _():
        run(ysp_ref, up_ref, lrup_ref, xp_ref, op_ref)

    @pl.when(i == n_tiles)
    def _():
        run(ysd_ref, ud_ref, lrud_ref, xd_ref, od_ref)


def _mix(ys_p, proj_p, lru_p, x1_p, ys_d, proj_d, lru_d, x1_d, dsk, wg, bg, gs, gl, wo, *, tm=512):
    n_p, nd = x1_p.shape[0], x1_d.shape[0]
    n_tiles = n_p // tm
    prow = lambda c: pl.BlockSpec((tm, c), lambda i: (jnp.minimum(i, n_tiles - 1), 0))
    drow = lambda c: pl.BlockSpec((nd, c), lambda i: (0, 0))
    vec = lambda c: pl.BlockSpec((1, c), lambda i: (0, 0))
    return pl.pallas_call(
        functools.partial(_mix_body, n_tiles=n_tiles),
        grid=(n_tiles + 1,),
        in_specs=[prow(D_S5), prow(D_S5), prow(D_LRU), prow(D_MODEL),
                  drow(D_S5), drow(D_S5), drow(D_LRU), drow(D_MODEL), vec(D_S5),
                  pl.BlockSpec((D_S5, D_S5), lambda i: (0, 0)), vec(D_S5), vec(D_S5), vec(D_LRU),
                  pl.BlockSpec((D_MODEL, D_MODEL), lambda i: (0, 0))],
        out_specs=(prow(D_MODEL), drow(D_MODEL)),
        out_shape=(jax.ShapeDtypeStruct((n_p, D_MODEL), F32),
                   jax.ShapeDtypeStruct((nd, D_MODEL), F32)),
        compiler_params=pltpu.CompilerParams(
            dimension_semantics=("arbitrary",), vmem_limit_bytes=VMEM_LIMIT),
        name="mix",
    )(ys_p, proj_p, lru_p, x1_p, ys_d, proj_d, lru_d, x1_d, dsk, wg, bg, gs, gl, wo)


def _unpair(h, nb):
    tiles = S5_GROUPS // GROUPS_PER_TILE
    h5 = h.reshape(tiles, PAIRS_PER_TILE, nb, 2, S5_STATE)
    return jnp.transpose(h5, (2, 0, 3, 1, 4)).reshape(nb, S5_GROUPS, S5_STATE)


def _block_diag4(w):
    w4 = w.reshape(LRU_HEADS // 4, 4, LRU_HEAD_DIM, LRU_HEAD_DIM)
    eye = jnp.eye(4, dtype=w.dtype)
    return jnp.einsum("kaij,ab->kaibj", w4, eye).reshape(LRU_HEADS // 4, 256, 256)


def kernel(x_prompt, x_sample, state_s5_re, state_s5_im, state_lru_h, state_lru_conv, g_ffn1, w1_a, w3_a, w2_a, g_mix, w_in, lam_re, lam_im, log_dt, b_re, b_im, c_re, c_im, d_skip, w_glu, b_glu, conv_w, conv_b, w_a, b_a, w_x, b_x, lam_l, g_out_s5, g_out_lru, w_out, g_ffn2, w1_b, w3_b, w2_b, g_final):
    nb, seq, _ = x_prompt.shape
    nd = x_sample.shape[0]
    n_p = nb * seq
    nk = seq // CHUNK
    g, p = S5_GROUPS, S5_STATE
    row = lambda v: v.reshape(1, -1)

    xp = x_prompt.reshape(n_p, D_MODEL)
    xd = x_sample.reshape(nd, D_MODEL)

    x1_p, x1_d = _ffn(xp, xd, row(g_ffn1[0]), w1_a[0].astype(BF16), w3_a[0].astype(BF16),
                      w2_a[0].astype(BF16))
    proj_p, proj_d = _inproj(x1_p, x1_d, row(g_mix[0]), w_in[0].astype(BF16))

    lmat, cpre, cpim, bd, cd, ar, ai, lr, li = _s5_prep(
        lam_re[0], lam_im[0], log_dt[0], b_re[0], b_im[0], c_re[0], c_im[0])
    ys_p, hf_re, hf_im = _s5_prompt(proj_p, lmat, cpre, cpim, ar, ai, nb=nb, seq=seq)

    ud = jnp.transpose(proj_d[:, :D_S5].reshape(nd, g, S5_GROUP), (1, 0, 2))
    h0d = jnp.concatenate([jnp.transpose(state_s5_re[0], (1, 0, 2)),
                           jnp.transpose(state_s5_im[0], (1, 0, 2))], axis=-1)
    ydg, hd = _s5_decode(ud, h0d, bd, cd, lr, li)
    ys_d = jnp.transpose(ydg, (1, 0, 2)).reshape(nd, D_S5)

    wa_bd = _block_diag4(w_a[0]).astype(BF16)
    wx_bd = _block_diag4(w_x[0]).astype(BF16)
    lru_args = (conv_w[0], row(conv_b[0]), wa_bd, wx_bd, row(b_a[0]), row(b_x[0]), row(lam_l[0]))
    lru_p, hl_p = _lru_prompt(proj_p, *lru_args, nb=nb, seq=seq)
    conv0 = state_lru_conv[0]
    lru_d, hl_d = _lru_decode(proj_d, conv0[:, 0], conv0[:, 1], conv0[:, 2], state_lru_h[0],
                              *lru_args)

    x2_p, x2_d = _mix(ys_p, proj_p, lru_p, x1_p, ys_d, proj_d, lru_d, x1_d,
                      row(d_skip[0]), w_glu[0].astype(BF16), row(b_glu[0]),
                      row(g_out_s5[0]), row(g_out_lru[0]), w_out[0].astype(BF16))
    y_p, y_d = _ffn(x2_p, x2_d, row(g_ffn2[0]), w1_b[0].astype(BF16), w3_b[0].astype(BF16),
                    w2_b[0].astype(BF16), row(g_final))

    tail_p = proj_p.reshape(nb, seq, -1)[:, seq - (CONV_W - 1):, D_S5:D_S5 + D_LRU]
    xl_d = proj_d[:, D_S5:D_S5 + D_LRU]
    return (
        y_p.reshape(nb, seq, D_MODEL),
        y_d.reshape(nd, 1, D_MODEL),
        _unpair(hf_re, nb)[None],
        _unpair(hf_im, nb)[None],
        hl_p.reshape(1, nb, D_LRU),
        tail_p[None],
        jnp.transpose(hd[:, :, :p], (1, 0, 2))[None],
        jnp.transpose(hd[:, :, p:], (1, 0, 2))[None],
        hl_d[None],
        jnp.stack([conv0[:, 1], conv0[:, 2], xl_d], axis=1)[None],
    )
```

```python
import functools

import jax
import jax.numpy as jnp
from jax import lax
from jax.experimental import pallas as pl
from jax.experimental.pallas import tpu as pltpu

F32 = jnp.float32
BF16 = jnp.bfloat16

D_MODEL = 2048
D_S5 = 1024
S5_GROUP = 16
S5_GROUPS = 64
S5_STATE = 64
D_LRU = 1024
LRU_HEADS = 16
LRU_HEAD_DIM = 64
CONV_W = 4
LRU_C = 8.0
D_FF = 5632
EPS = 1e-6

CHUNK = 16
CW = CHUNK * S5_GROUP
N_DOUBLINGS = 7

VMEM_LIMIT = 58 * 1024 * 1024

NN = (((1,), (0,)), ((), ()))
NT = (((1,), (1,)), ((), ()))


def _rms(x, g):
    return x * lax.rsqrt(jnp.mean(x * x, axis=-1, keepdims=True) + EPS) * g


def _split(x):
    hi = x.astype(BF16)
    lo = (x - hi.astype(F32)).astype(BF16)
    return hi, lo


def _dot3(a, b, dims=NN):
    ah, al = _split(a)
    bh, bl = _split(b)
    d = functools.partial(lax.dot_general, dimension_numbers=dims, preferred_element_type=F32)
    return d(ah, bh) + d(al, bh) + d(ah, bl)


def _gelu(x):
    return jax.nn.gelu(x, approximate=True)


def _ffn_body(*refs, final_norm):
    if final_norm:
        xp_ref, xd_ref, g_ref, w1_ref, w3_ref, w2_ref, gf_ref, op_ref, od_ref, xn_ref = refs
    else:
        xp_ref, xd_ref, g_ref, w1_ref, w3_ref, w2_ref, op_ref, od_ref, xn_ref = refs
    f = pl.program_id(1)
    tm = xp_ref.shape[0]

    @pl.when(f == 0)
    def _():
        for x_ref, o_ref, rows in ((xp_ref, op_ref, slice(0, tm)), (xd_ref, od_ref, slice(tm, None))):
            x = x_ref[...]
            xn_ref[rows, :] = _rms(x, g_ref[...]).astype(BF16)
            o_ref[...] = x

    xn = xn_ref[...]
    a = jnp.dot(xn, w1_ref[...], preferred_element_type=F32)
    b = jnp.dot(xn, w3_ref[...], preferred_element_type=F32)
    h = (a * jax.nn.sigmoid(a) * b).astype(BF16)
    upd = 0.5 * jnp.dot(h, w2_ref[...], preferred_element_type=F32)
    op_ref[...] += upd[0:tm]
    od_ref[...] += upd[tm:]

    if final_norm:
        @pl.when(f == pl.num_programs(1) - 1)
        def _():
            op_ref[...] = _rms(op_ref[...], gf_ref[...])
            od_ref[...] = _rms(od_ref[...], gf_ref[...])


def _ffn(xp, xd, g, w1, w3, w2, g_final=None, *, tm=1024, tf=512):
    n_p, nd = xp.shape[0], xd.shape[0]
    n_tiles = n_p // tm
    td = nd // n_tiles
    final_norm = g_final is not None
    pspec = pl.BlockSpec((tm, D_MODEL), lambda i, f: (i, 0))
    dspec = pl.BlockSpec((td, D_MODEL), lambda i, f: (i, 0))
    in_specs = [
        pspec, dspec,
        pl.BlockSpec((1, D_MODEL), lambda i, f: (0, 0)),
        pl.BlockSpec((D_MODEL, tf), lambda i, f: (0, f)),
        pl.BlockSpec((D_MODEL, tf), lambda i, f: (0, f)),
        pl.BlockSpec((tf, D_MODEL), lambda i, f: (f, 0)),
    ]
    args = [xp, xd, g, w1, w3, w2]
    if final_norm:
        in_specs.append(pl.BlockSpec((1, D_MODEL), lambda i, f: (0, 0)))
        args.append(g_final)
    return pl.pallas_call(
        functools.partial(_ffn_body, final_norm=final_norm),
        grid=(n_tiles, D_FF // tf),
        in_specs=in_specs,
        out_specs=(pspec, dspec),
        out_shape=(jax.ShapeDtypeStruct((n_p, D_MODEL), F32),
                   jax.ShapeDtypeStruct((nd, D_MODEL), F32)),
        scratch_shapes=[pltpu.VMEM((tm + td, D_MODEL), BF16)],
        compiler_params=pltpu.CompilerParams(
            dimension_semantics=("parallel", "arbitrary"), vmem_limit_bytes=VMEM_LIMIT),
        name="ffn_final" if final_norm else "ffn",
    )(*args)


def _inproj_body(xp_ref, xd_ref, g_ref, w_ref, op_ref, od_ref, xn_ref, *, n_tiles, nd):
    i = pl.program_id(0)
    j = pl.program_id(1)
    nj = pl.num_programs(1)

    def run(x_ref, o_ref, rows):
        @pl.when(j == 0)
        def _():
            xn_ref[0:rows, :] = _rms(x_ref[...], g_ref[...]).astype(BF16)

        @pl.when(j < nj - 1)
        def _():
            o_ref[...] = jnp.dot(xn_ref[0:rows, :], w_ref[...], preferred_element_type=F32)

        @pl.when(j == nj - 1)
        def _():
            o_ref[...] = _gelu(jnp.dot(xn_ref[0:rows, :], w_ref[...],
                                       preferred_element_type=F32))

    @pl.when(i < n_tiles)
    def _():
        run(xp_ref, op_ref, xp_ref.shape[0])

    @pl.when(i == n_tiles)
    def _():
        run(xd_ref, od_ref, nd)


def _inproj(xp, xd, g, w, *, tm=1024, tn=1024):
    n_p, nd = xp.shape[0], xd.shape[0]
    n_tiles = n_p // tm
    d_out = w.shape[1]
    nj = d_out // tn
    return pl.pallas_call(
        functools.partial(_inproj_body, n_tiles=n_tiles, nd=nd),
        grid=(n_tiles + 1, nj),
        in_specs=[
            pl.BlockSpec((tm, D_MODEL), lambda i, j: (jnp.minimum(i, n_tiles - 1), 0)),
            pl.BlockSpec((nd, D_MODEL), lambda i, j: (0, 0)),
            pl.BlockSpec((1, D_MODEL), lambda i, j: (0, 0)),
            pl.BlockSpec((D_MODEL, tn), lambda i, j: (0, j)),
        ],
        out_specs=(
            pl.BlockSpec((tm, tn), lambda i, j: (jnp.minimum(i, n_tiles - 1),
                                                 jnp.where(i < n_tiles, j, nj - 1))),
            pl.BlockSpec((nd, tn), lambda i, j: (0, jnp.where(i < n_tiles, 0, j))),
        ),
        out_shape=(jax.ShapeDtypeStruct((n_p, d_out), F32),
                   jax.ShapeDtypeStruct((nd, d_out), F32)),
        scratch_shapes=[pltpu.VMEM((tm, D_MODEL), BF16)],
        compiler_params=pltpu.CompilerParams(
            dimension_semantics=("arbitrary", "arbitrary"), vmem_limit_bytes=VMEM_LIMIT),
        name="inproj",
    )(xp, xd, g, w)


GROUPS_PER_TILE = 128 // S5_GROUP
PAIRS_PER_TILE = GROUPS_PER_TILE // 2
L_ROWS = CW + 2 * S5_STATE


def _s5_prep_body(lre_ref, lim_ref, ldt_ref, bre_ref, bim_ref, cre_ref, cim_ref,
                  l_ref, cpre_ref, cpim_ref, bd_ref, cd_ref, ar_ref, ai_ref, lr_ref, li_ref,
                  bp_ref):
    gb, half = GROUPS_PER_TILE, PAIRS_PER_TILE
    p = S5_STATE
    lo, hi = slice(0, p), slice(p, 2 * p)
    lam_re = lre_ref[...]
    lam_im = lim_ref[...]
    dt = jnp.exp(ldt_ref[...])
    mag = jnp.exp(lam_re * dt)
    ang = lam_im * dt
    lbr = mag * jnp.cos(ang)
    lbi = mag * jnp.sin(ang)
    lr_ref[:, :, lo] = lbr
    lr_ref[:, :, hi] = lbr
    li_ref[:, :, lo] = -lbi
    li_ref[:, :, hi] = lbi
    nr = lbr - 1.0
    den = lam_re * lam_re + lam_im * lam_im
    cr = (nr * lam_re + lbi * lam_im) / den
    ci = (lbi * lam_re - nr * lam_im) / den
    b_re = bre_ref[...]
    b_im = bim_ref[...]
    bbr = cr * b_re - ci * b_im
    bbi = cr * b_im + ci * b_re
    bd_ref[:, :, lo] = bbr
    bd_ref[:, :, hi] = bbi
    c_re = cre_ref[...]
    c_im = cim_ref[...]
    cd_ref[:, :, lo] = c_re
    cd_ref[:, :, hi] = -c_im

    zeros = jnp.zeros((half, S5_GROUP, p), F32)
    pr = jnp.ones_like(lbr)
    pi = jnp.zeros_like(lbr)
    for d in range(CHUNK):
        rows = slice(d * S5_GROUP, (d + 1) * S5_GROUP)
        back = slice((CHUNK - 1 - d) * S5_GROUP, (CHUNK - d) * S5_GROUP)
        bp_ref[:, back, lo] = bbr * pr - bbi * pi
        bp_ref[:, back, hi] = bbr * pi + bbi * pr
        pr, pi = pr * lbr - pi * lbi, pr * lbi + pi * lbr
        cp_r = c_re * pr - c_im * pi
        cp_i = -(c_re * pi + c_im * pr)
        cpre_ref[0:half, rows, lo] = cp_r[0:half]
        cpre_ref[0:half, rows, hi] = zeros
        cpre_ref[half:gb, rows, lo] = zeros
        cpre_ref[half:gb, rows, hi] = cp_r[half:gb]
        cpim_ref[0:half, rows, lo] = cp_i[0:half]
        cpim_ref[0:half, rows, hi] = zeros
        cpim_ref[half:gb, rows, lo] = zeros
        cpim_ref[half:gb, rows, hi] = cp_i[half:gb]

    qr, qi = pr, pi
    for j in range(N_DOUBLINGS):
        ar_ref[:, j:j + 1, lo] = qr[0:half]
        ar_ref[:, j:j + 1, hi] = qr[half:gb]
        ai_ref[:, j:j + 1, lo] = qi[0:half]
        ai_ref[:, j:j + 1, hi] = qi[half:gb]
        qr, qi = qr * qr - qi * qi, 2.0 * qr * qi
    ar_ref[:, N_DOUBLINGS:, :] = jnp.zeros((half, 8 - N_DOUBLINGS, 2 * p), F32)
    ai_ref[:, N_DOUBLINGS:, :] = jnp.zeros((half, 8 - N_DOUBLINGS, 2 * p), F32)

    lane = lax.broadcasted_iota(jnp.int32, (S5_GROUP, 128), 1)
    for j in range(gb):
        w = _dot3(cd_ref[j], bp_ref[j], NT)
        w0, w1 = w[:, :128], w[:, 128:]
        for t in range(CHUNK):
            rows = slice(t * S5_GROUP, (t + 1) * S5_GROUP)
            shift = (CHUNK - 1 - t) * S5_GROUP
            keep = 128 - shift % 128
            if shift == 0:
                left, right = w0, w1
            elif shift < 128:
                r0 = pltpu.roll(w0, keep, axis=1)
                r1 = pltpu.roll(w1, keep, axis=1)
                left = jnp.where(lane < keep, r0, r1)
                right = jnp.where(lane < keep, r1, 0.0)
            elif shift == 128:
                left, right = w1, jnp.zeros_like(w1)
            else:
                left = jnp.where(lane < keep, pltpu.roll(w1, keep, axis=1), 0.0)
                right = jnp.zeros_like(w1)
            l_ref[j, rows, 0:128] = left
            l_ref[j, rows, 128:256] = right
        l_ref[j, CW:L_ROWS, :] = bp_ref[j].T


def _s5_prep(lam_re, lam_im, log_dt, b_re, b_im, c_re, c_im):
    g, p, gb, half = S5_GROUPS, S5_STATE, GROUPS_PER_TILE, PAIRS_PER_TILE
    lre = lam_re.reshape(g, 1, p)
    lim = lam_im.reshape(g, 1, p)
    ldt = jnp.broadcast_to(log_dt.reshape(g, 1, 1), (g, 1, p))
    bre = jnp.transpose(b_re, (0, 2, 1))
    bim = jnp.transpose(b_im, (0, 2, 1))
    sd = jax.ShapeDtypeStruct
    blk = lambda n, r, c: pl.BlockSpec((n, r, c), lambda i: (i, 0, 0))
    return pl.pallas_call(
        _s5_prep_body,
        grid=(g // gb,),
        in_specs=[blk(gb, 1, p)] * 3 + [blk(gb, S5_GROUP, p)] * 4,
        out_specs=(
            blk(gb, L_ROWS, CW), blk(gb, CW, 2 * p), blk(gb, CW, 2 * p),
            blk(gb, S5_GROUP, 2 * p), blk(gb, S5_GROUP, 2 * p),
            blk(half, 8, 2 * p), blk(half, 8, 2 * p), blk(gb, 1, 2 * p), blk(gb, 1, 2 * p),
        ),
        out_shape=(
            sd((g, L_ROWS, CW), F32),
            sd((g, CW, 2 * p), F32),
            sd((g, CW, 2 * p), F32),
            sd((g, S5_GROUP, 2 * p), F32),
            sd((g, S5_GROUP, 2 * p), F32),
            sd((g // 2, 8, 2 * p), F32),
            sd((g // 2, 8, 2 * p), F32),
            sd((g, 1, 2 * p), F32),
            sd((g, 1, 2 * p), F32),
        ),
        scratch_shapes=[pltpu.VMEM((gb, CW, 2 * p), F32)],
        compiler_params=pltpu.CompilerParams(dimension_semantics=("parallel",)),
        name="s5_prep",
    )(lre, lim, ldt, bre, bim, c_re, c_im)


def _cmul_add(h, hs, ar, ai):
    return h + hs * ar + pltpu.roll(hs, S5_STATE, axis=1) * ai


def _s5p_body(u_ref, l_ref, cpre_ref, cpim_ref, ar_ref, ai_ref, y_ref, hre_ref, him_ref,
              ut_ref, yt_ref, *, nb, nk):
    gb, half = GROUPS_PER_TILE, PAIRS_PER_TILE
    p = S5_STATE
    nrow = nb * nk
    d = functools.partial(jnp.dot, preferred_element_type=F32)

    for t in range(CHUNK):
        xt = u_ref[pl.ds(t, nrow, stride=CHUNK), :].T
        for j in range(gb):
            ut_ref[j, t * S5_GROUP:(t + 1) * S5_GROUP, :] = xt[j * S5_GROUP:(j + 1) * S5_GROUP, :]

    s_re, s_im = [], []
    for j in range(gb):
        uh, ul = _split(ut_ref[j])
        lh, ll = _split(l_ref[j])
        r = d(lh, uh) + d(lh, ul) + d(ll, uh)
        yt_ref[j] = r[0:CW]
        s_re.append(r[CW:CW + p])
        s_im.append(r[CW + p:L_ROWS])

    rows = lax.broadcasted_iota(jnp.int32, (nrow, 2 * p), 0) & (nk - 1)
    for q in range(half):
        re = jnp.concatenate([s_re[q], s_re[q + half]], axis=0).T
        im = jnp.concatenate([s_im[q], s_im[q + half]], axis=0).T
        for k in range(N_DOUBLINGS):
            sh = 1 << k
            keep = rows >= sh
            rs = jnp.where(keep, pltpu.roll(re, sh, axis=0), 0.0)
            js = jnp.where(keep, pltpu.roll(im, sh, axis=0), 0.0)
            ar = ar_ref[q, k:k + 1, :]
            ai = ai_ref[q, k:k + 1, :]
            re, im = re + ar * rs - ai * js, im + ar * js + ai * rs
        for b in range(nb):
            hre_ref[q, b:b + 1, :] = re[(b + 1) * nk - 1:(b + 1) * nk, :]
            him_ref[q, b:b + 1, :] = im[(b + 1) * nk - 1:(b + 1) * nk, :]
        pre = jnp.where(rows >= 1, pltpu.roll(re, 1, axis=0), 0.0)
        pim = jnp.where(rows >= 1, pltpu.roll(im, 1, axis=0), 0.0)
        for j in (q, q + half):
            yt_ref[j] = yt_ref[j] + _dot3(cpre_ref[j], pre, NT) + _dot3(cpim_ref[j], pim, NT)

    for t in range(CHUNK):
        yt = jnp.concatenate(
            [yt_ref[j, t * S5_GROUP:(t + 1) * S5_GROUP, :] for j in range(gb)], axis=0)
        y_ref[pl.ds(t, nrow, stride=CHUNK), :] = yt.T


def _s5_prompt(proj_p, lmat, cpre, cpim, ar, ai, *, nb, seq):
    g, gb, half = S5_GROUPS, GROUPS_PER_TILE, PAIRS_PER_TILE
    n_p = nb * seq
    nk = seq // CHUNK
    blk = lambda n, r, c: pl.BlockSpec((n, r, c), lambda i: (i, 0, 0))
    return pl.pallas_call(
        functools.partial(_s5p_body, nb=nb, nk=nk),
        grid=(g // gb,),
        in_specs=[pl.BlockSpec((n_p, 128), lambda i: (0, i)),
                  blk(gb, L_ROWS, CW), blk(gb, CW, 2 * S5_STATE), blk(gb, CW, 2 * S5_STATE),
                  blk(half, 8, 2 * S5_STATE), blk(half, 8, 2 * S5_STATE)],
        out_specs=(pl.BlockSpec((n_p, 128), lambda i: (0, i)),
                   blk(half, nb, 2 * S5_STATE), blk(half, nb, 2 * S5_STATE)),
        out_shape=(jax.ShapeDtypeStruct((n_p, D_S5), F32),
                   jax.ShapeDtypeStruct((g // 2, nb, 2 * S5_STATE), F32),
                   jax.ShapeDtypeStruct((g // 2, nb, 2 * S5_STATE), F32)),
        scratch_shapes=[pltpu.VMEM((gb, CW, nb * nk), F32), pltpu.VMEM((gb, CW, nb * nk), F32)],
        compiler_params=pltpu.CompilerParams(
            dimension_semantics=("parallel",), vmem_limit_bytes=VMEM_LIMIT),
        name="s5_prompt",
    )(proj_p, lmat, cpre, cpim, ar, ai)


def _s5d_body(u_ref, h0_ref, bd_ref, cd_ref, lr_ref, li_ref, y_ref, h_ref, *, gb):
    for j in range(gb):
        h0 = h0_ref[j]
        bu = _dot3(u_ref[j], bd_ref[j])
        h = _cmul_add(bu, h0, lr_ref[j], li_ref[j])
        h_ref[j] = h
        y_ref[j] = _dot3(h, cd_ref[j], NT)


def _s5_decode(ud, h0, bd, cd, lr, li, *, gb=8):
    g, nbatch = S5_GROUPS, ud.shape[1]
    blk = lambda r, c: pl.BlockSpec((gb, r, c), lambda i: (i, 0, 0))
    return pl.pallas_call(
        functools.partial(_s5d_body, gb=gb),
        grid=(g // gb,),
        in_specs=[blk(nbatch, S5_GROUP), blk(nbatch, 2 * S5_STATE), blk(S5_GROUP, 2 * S5_STATE),
                  blk(S5_GROUP, 2 * S5_STATE), blk(1, 2 * S5_STATE), blk(1, 2 * S5_STATE)],
        out_specs=(blk(nbatch, S5_GROUP), blk(nbatch, 2 * S5_STATE)),
        out_shape=(jax.ShapeDtypeStruct((g, nbatch, S5_GROUP), F32),
                   jax.ShapeDtypeStruct((g, nbatch, 2 * S5_STATE), F32)),
        compiler_params=pltpu.CompilerParams(dimension_semantics=("parallel",)),
        name="s5_decode",
    )(ud, h0, bd, cd, lr, li)


def _lru_gates(xc, wa_ref, wx_ref, ba, bx, lam):
    xcb = xc.astype(BF16)
    nblk = D_LRU // 256
    r_parts, i_parts = [], []
    for k in range(nblk):
        xk = xcb[:, k * 256:(k + 1) * 256]
        r_parts.append(jnp.dot(xk, wa_ref[k], preferred_element_type=F32))
        i_parts.append(jnp.dot(xk, wx_ref[k], preferred_element_type=F32))
    r = jax.nn.sigmoid(jnp.concatenate(r_parts, axis=1) + ba)
    i = jax.nn.sigmoid(jnp.concatenate(i_parts, axis=1) + bx)
    z = -lam
    softplus = jnp.maximum(z, 0.0) + jnp.log1p(jnp.exp(-jnp.abs(z)))
    log_a = (-LRU_C * softplus) * r
    a = jnp.exp(log_a)
    v = -jnp.tanh(log_a) * (a * a + 1.0)
    mult = jnp.where(v > 0.0, v * lax.rsqrt(v), 0.0)
    return a, mult * (i * xc)


def _lru_prompt_body(xl_ref, gate_ref, cw_ref, cb_ref, wa_ref, wx_ref, ba_ref, bx_ref, lam_ref,
                     o_ref, hl_ref, xbuf_ref, carry_ref, *, tt):
    t = pl.program_id(1)

    @pl.when(t == 0)
    def _():
        xbuf_ref[0:8, :] = jnp.zeros((8, D_LRU), F32)
        carry_ref[...] = jnp.zeros((8, D_LRU), F32)

    x = xl_ref[...]
    xbuf_ref[8:8 + tt, :] = x
    cw = cw_ref[...]
    xc = (cb_ref[...] + xbuf_ref[5:5 + tt, :] * cw[0:1] + xbuf_ref[6:6 + tt, :] * cw[1:2]
          + xbuf_ref[7:7 + tt, :] * cw[2:3] + x * cw[3:4])
    xbuf_ref[0:8, :] = x[tt - 8:tt, :]

    a, b = _lru_gates(xc, wa_ref, wx_ref, ba_ref[...], bx_ref[...], lam_ref[...])

    nblk = tt // 8
    a3 = a.reshape(nblk, 8, D_LRU)
    b3 = b.reshape(nblk, 8, D_LRU)
    row = lax.broadcasted_iota(jnp.int32, (nblk, 8, D_LRU), 1)
    for sh in (1, 2, 4):
        keep = row >= sh
        bs = jnp.where(keep, pltpu.roll(b3, sh, axis=1), 0.0)
        sa = jnp.where(keep, pltpu.roll(a3, sh, axis=1), 1.0)
        b3 = b3 + a3 * bs
        a3 = a3 * sa
    carry = carry_ref[0:1, :]
    gate = gate_ref[...]
    for k in range(nblk):
        h = b3[k] + a3[k] * carry
        carry = h[7:8, :]
        o_ref[k * 8:(k + 1) * 8, :] = h * gate[k * 8:(k + 1) * 8, :]
    carry_ref[...] = jnp.broadcast_to(carry, (8, D_LRU))
    hl_ref[0] = carry


def _lru_prompt(proj, cw, cb, wa, wx, ba, bx, lam, *, nb, seq, tt=256):
    nt = seq // tt
    vec = lambda r: pl.BlockSpec((r, D_LRU), lambda b, t: (0, 0))
    wspec = pl.BlockSpec((D_LRU // 256, 256, 256), lambda b, t: (0, 0, 0))
    return pl.pallas_call(
        functools.partial(_lru_prompt_body, tt=tt),
        grid=(nb, nt),
        in_specs=[
            pl.BlockSpec((tt, D_LRU), lambda b, t: (b * nt + t, 1)),
            pl.BlockSpec((tt, D_LRU), lambda b, t: (b * nt + t, 2)),
            vec(CONV_W), vec(1), wspec, wspec, vec(1), vec(1), vec(1),
        ],
        out_specs=(pl.BlockSpec((tt, D_LRU), lambda b, t: (b * nt + t, 0)),
                   pl.BlockSpec((1, 1, D_LRU), lambda b, t: (b, 0, 0))),
        out_shape=(jax.ShapeDtypeStruct((nb * seq, D_LRU), F32),
                   jax.ShapeDtypeStruct((nb, 1, D_LRU), F32)),
        scratch_shapes=[pltpu.VMEM((tt + 8, D_LRU), F32), pltpu.VMEM((8, D_LRU), F32)],
        compiler_params=pltpu.CompilerParams(
            dimension_semantics=("parallel", "arbitrary"), vmem_limit_bytes=VMEM_LIMIT),
        name="lru_prompt",
    )(proj, proj, cw, cb, wa, wx, ba, bx, lam)


def _lru_decode_body(xl_ref, gate_ref, c0_ref, c1_ref, c2_ref, h0_ref, cw_ref, cb_ref,
                     wa_ref, wx_ref, ba_ref, bx_ref, lam_ref, o_ref, h_ref):
    x = xl_ref[...]
    cw = cw_ref[...]
    xc = (cb_ref[...] + c0_ref[...] * cw[0:1] + c1_ref[...] * cw[1:2]
          + c2_ref[...] * cw[2:3] + x * cw[3:4])
    a, b = _lru_gates(xc, wa_ref, wx_ref, ba_ref[...], bx_ref[...], lam_ref[...])
    h = a * h0_ref[...] + b
    h_ref[...] = h
    o_ref[...] = h * gate_ref[...]


def _lru_decode(proj_d, c0, c1, c2, h0, cw, cb, wa, wx, ba, bx, lam):
    nd = proj_d.shape[0]
    full = lambda r: pl.BlockSpec((r, D_LRU), lambda i: (0, 0))
    wspec = pl.BlockSpec((D_LRU // 256, 256, 256), lambda i: (0, 0, 0))
    return pl.pallas_call(
        _lru_decode_body,
        grid=(1,),
        in_specs=[
            pl.BlockSpec((nd, D_LRU), lambda i: (0, 1)),
            pl.BlockSpec((nd, D_LRU), lambda i: (0, 2)),
            full(nd), full(nd), full(nd), full(nd),
            full(CONV_W), full(1), wspec, wspec, full(1), full(1), full(1),
        ],
        out_specs=(full(nd), full(nd)),
        out_shape=(jax.ShapeDtypeStruct((nd, D_LRU), F32),
                   jax.ShapeDtypeStruct((nd, D_LRU), F32)),
        name="lru_decode",
    )(proj_d, proj_d, c0, c1, c2, h0, cw, cb, wa, wx, ba, bx, lam)


def _mix_body(ysp_ref, up_ref, lrup_ref, xp_ref, ysd_ref, ud_ref, lrud_ref, xd_ref,
              dsk_ref, wg_ref, bg_ref, gs_ref, gl_ref, wo_ref, op_ref, od_ref, *, n_tiles):
    i = pl.program_id(0)

    def run(ys_ref, u_ref, lru_ref, x_ref, o_ref):
        yy = ys_ref[...] + dsk_ref[...] * u_ref[...]
        g = _gelu(yy)
        z = jnp.dot(g.astype(BF16), wg_ref[...], preferred_element_type=F32) + bg_ref[...]
        s5o = g * jax.nn.sigmoid(z)
        n1 = _rms(s5o, gs_ref[...]).astype(BF16)
        n2 = _rms(lru_ref[...], gl_ref[...]).astype(BF16)
        o_ref[...] = (x_ref[...]
                      + jnp.dot(n1, wo_ref[0:D_S5, :], preferred_element_type=F32)
                      + jnp.dot(n2, wo_ref[D_S5:, :], preferred_element_type=F32))

    @pl.when(i < n_tiles)
    def _():
        run(ysp_ref, up_ref, lrup_ref, xp_ref, op_ref)

    @pl.when(i == n_tiles)
    def _():
        run(ysd_ref, ud_ref, lrud_ref, xd_ref, od_ref)


def _mix(ys_p, proj_p, lru_p, x1_p, ys_d, proj_d, lru_d, x1_d, dsk, wg, bg, gs, gl, wo, *, tm=512):
    n_p, nd = x1_p.shape[0], x1_d.shape[0]
    n_tiles = n_p // tm
    prow = lambda c: pl.BlockSpec((tm, c), lambda i: (jnp.minimum(i, n_tiles - 1), 0))
    drow = lambda c: pl.BlockSpec((nd, c), lambda i: (0, 0))
    vec = lambda c: pl.BlockSpec((1, c), lambda i: (0, 0))
    return pl.pallas_call(
        functools.partial(_mix_body, n_tiles=n_tiles),
        grid=(n_tiles + 1,),
        in_specs=[prow(D_S5), prow(D_S5), prow(D_LRU), prow(D_MODEL),
                  drow(D_S5), drow(D_S5), drow(D_LRU), drow(D_MODEL), vec(D_S5),
                  pl.BlockSpec((D_S5, D_S5), lambda i: (0, 0)), vec(D_S5), vec(D_S5), vec(D_LRU),
                  pl.BlockSpec((D_MODEL, D_MODEL), lambda i: (0, 0))],
        out_specs=(prow(D_MODEL), drow(D_MODEL)),
        out_shape=(jax.ShapeDtypeStruct((n_p, D_MODEL), F32),
                   jax.ShapeDtypeStruct((nd, D_MODEL), F32)),
        compiler_params=pltpu.CompilerParams(
            dimension_semantics=("arbitrary",), vmem_limit_bytes=VMEM_LIMIT),
        name="mix",
    )(ys_p, proj_p, lru_p, x1_p, ys_d, proj_d, lru_d, x1_d, dsk, wg, bg, gs, gl, wo)


def _unpair(h, nb):
    tiles = S5_GROUPS // GROUPS_PER_TILE
    h5 = h.reshape(tiles, PAIRS_PER_TILE, nb, 2, S5_STATE)
    return jnp.transpose(h5, (2, 0, 3, 1, 4)).reshape(nb, S5_GROUPS, S5_STATE)


def _block_diag4(w):
    w4 = w.reshape(LRU_HEADS // 4, 4, LRU_HEAD_DIM, LRU_HEAD_DIM)
    eye = jnp.eye(4, dtype=w.dtype)
    return jnp.einsum("kaij,ab->kaibj", w4, eye).reshape(LRU_HEADS // 4, 256, 256)


def kernel(x_prompt, x_sample, state_s5_re, state_s5_im, state_lru_h, state_lru_conv, g_ffn1, w1_a, w3_a, w2_a, g_mix, w_in, lam_re, lam_im, log_dt, b_re, b_im, c_re, c_im, d_skip, w_glu, b_glu, conv_w, conv_b, w_a, b_a, w_x, b_x, lam_l, g_out_s5, g_out_lru, w_out, g_ffn2, w1_b, w3_b, w2_b, g_final):
    nb, seq, _ = x_prompt.shape
    nd = x_sample.shape[0]
    n_p = nb * seq
    nk = seq // CHUNK
    g, p = S5_GROUPS, S5_STATE
    row = lambda v: v.reshape(1, -1)

    xp = x_prompt.reshape(n_p, D_MODEL)
    xd = x_sample.reshape(nd, D_MODEL)

    x1_p, x1_d = _ffn(xp, xd, row(g_ffn1[0]), w1_a[0].astype(BF16), w3_a[0].astype(BF16),
                      w2_a[0].astype(BF16))
    proj_p, proj_d = _inproj(x1_p, x1_d, row(g_mix[0]), w_in[0].astype(BF16))

    lmat, cpre, cpim, bd, cd, ar, ai, lr, li = _s5_prep(
        lam_re[0], lam_im[0], log_dt[0], b_re[0], b_im[0], c_re[0], c_im[0])
    ys_p, hf_re, hf_im = _s5_prompt(proj_p, lmat, cpre, cpim, ar, ai, nb=nb, seq=seq)

    ud = jnp.transpose(proj_d[:, :D_S5].reshape(nd, g, S5_GROUP), (1, 0, 2))
    h0d = jnp.concatenate([jnp.transpose(state_s5_re[0], (1, 0, 2)),
                           jnp.transpose(state_s5_im[0], (1, 0, 2))], axis=-1)
    ydg, hd = _s5_decode(ud, h0d, bd, cd, lr, li)
    ys_d = jnp.transpose(ydg, (1, 0, 2)).reshape(nd, D_S5)

    wa_bd = _block_diag4(w_a[0]).astype(BF16)
    wx_bd = _block_diag4(w_x[0]).astype(BF16)
    lru_args = (conv_w[0], row(conv_b[0]), wa_bd, wx_bd, row(b_a[0]), row(b_x[0]), row(lam_l[0]))
    lru_p, hl_p = _lru_prompt(proj_p, *lru_args, nb=nb, seq=seq)
    conv0 = state_lru_conv[0]
    lru_d, hl_d = _lru_decode(proj_d, conv0[:, 0], conv0[:, 1], conv0[:, 2], state_lru_h[0],
                              *lru_args)

    x2_p, x2_d = _mix(ys_p, proj_p, lru_p, x1_p, ys_d, proj_d, lru_d, x1_d,
                      row(d_skip[0]), w_glu[0].astype(BF16), row(b_glu[0]),
                      row(g_out_s5[0]), row(g_out_lru[0]), w_out[0].astype(BF16))
    y_p, y_d = _ffn(x2_p, x2_d, row(g_ffn2[0]), w1_b[0].astype(BF16), w3_b[0].astype(BF16),
                    w2_b[0].astype(BF16), row(g_final))

    tail_p = proj_p.reshape(nb, seq, -1)[:, seq - (CONV_W - 1):, D_S5:D_S5 + D_LRU]
    xl_d = proj_d[:, D_S5:D_S5 + D_LRU]
    return (
        y_p.reshape(nb, seq, D_MODEL),
        y_d.reshape(nd, 1, D_MODEL),
        _unpair(hf_re, nb)[None],
        _unpair(hf_im, nb)[None],
        hl_p.reshape(1, nb, D_LRU),
        tail_p[None],
        jnp.transpose(hd[:, :, :p], (1, 0, 2))[None],
        jnp.transpose(hd[:, :, p:], (1, 0, 2))[None],
        hl_d[None],
        jnp.stack([conv0[:, 1], conv0[:, 2], xl_d], axis=1)[None],
    )
```

```python
import functools

import jax
import jax.numpy as jnp
from jax import lax
from jax.experimental import pallas as pl
from jax.experimental.pallas import tpu as pltpu

F32 = jnp.float32
BF16 = jnp.bfloat16

D_MODEL = 2048
D_S5 = 1024
S5_GROUP = 16
S5_GROUPS = 64
S5_STATE = 64
D_LRU = 1024
LRU_HEADS = 16
LRU_HEAD_DIM = 64
CONV_W = 4
LRU_C = 8.0
D_FF = 5632
D_IN = D_S5 + 2 * D_LRU
EPS = 1e-6

CHUNK = 16
CW = CHUNK * S5_GROUP
N_DOUBLINGS = 7

VMEM_LIMIT = 58 * 1024 * 1024
FFN_VMEM_LIMIT = 60 * 1024 * 1024

NN = (((1,), (0,)), ((), ()))
NT = (((1,), (1,)), ((), ()))


def _rms(x, g):
    return x * lax.rsqrt(jnp.mean(x * x, axis=-1, keepdims=True) + EPS) * g


def _split(x):
    hi = x.astype(BF16)
    lo = (x - hi.astype(F32)).astype(BF16)
    return hi, lo


def _dot3(a, b, dims=NN):
    ah, al = _split(a)
    bh, bl = _split(b)
    d = functools.partial(lax.dot_general, dimension_numbers=dims, preferred_element_type=F32)
    return d(ah, bh) + d(al, bh) + d(ah, bl)


def _gelu(x):
    return jax.nn.gelu(x, approximate=True)


def _ffn_body(*refs, final_norm, n_casts):
    n_in = 7 if final_norm else 6
    xp_ref, xd_ref, g_ref, w1_ref, w3_ref, w2_ref = refs[:6]
    gf_ref = refs[6] if final_norm else None
    cast_in = refs[n_in:n_in + n_casts]
    op_ref, od_ref = refs[n_in + n_casts:n_in + n_casts + 2]
    cast_out = refs[n_in + n_casts + 2:n_in + 2 * n_casts + 2]
    xn_ref = refs[-1]
    f = pl.program_id(1)
    tm = xp_ref.shape[0]

    for src_ref, dst_ref in zip(cast_in, cast_out):
        dst_ref[...] = src_ref[...].astype(BF16)

    @pl.when(f == 0)
    def _():
        for x_ref, o_ref, rows in ((xp_ref, op_ref, slice(0, tm)), (xd_ref, od_ref, slice(tm, None))):
            x = x_ref[...]
            xn_ref[rows, :] = _rms(x, g_ref[...]).astype(BF16)
            o_ref[...] = x

    xn = xn_ref[...]
    a = jnp.dot(xn, w1_ref[...], preferred_element_type=F32)
    b = jnp.dot(xn, w3_ref[...], preferred_element_type=F32)
    h = (a * jax.nn.sigmoid(a) * b).astype(BF16)
    upd = 0.5 * jnp.dot(h, w2_ref[...], preferred_element_type=F32)
    op_ref[...] += upd[0:tm]
    od_ref[...] += upd[tm:]

    if final_norm:
        @pl.when(f == pl.num_programs(1) - 1)
        def _():
            op_ref[...] = _rms(op_ref[...], gf_ref[...])
            od_ref[...] = _rms(od_ref[...], gf_ref[...])


def _cast_job(w, n_i, n_f, bc=512):
    rows, cols = w.shape
    br = rows // n_i
    n_cb = cols // bc
    assert br * n_i == rows and bc * n_cb == cols and n_cb <= n_f and br % 16 == 0
    return w, pl.BlockSpec(
        (br, bc), lambda i, f: (jnp.minimum(i, n_i - 1),
                                jnp.where(i < n_i, jnp.minimum(f, n_cb - 1), n_cb - 1)))


def _row_cast_job(w, n_i, n_f):
    rows, cols = w.shape
    br = rows // (n_i * n_f)
    assert br * n_i * n_f == rows and br % 16 == 0
    return w, pl.BlockSpec((br, cols), lambda i, f: (i * n_f + f, 0))


def _ffn(xp, xd, g, w1, w3, w2, g_final=None, *, casts=(), tm=1024, tf=512):
    n_p, nd = xp.shape[0], xd.shape[0]
    n_tiles = n_p // tm
    td = nd // n_tiles
    final_norm = g_final is not None
    pspec = pl.BlockSpec((tm, D_MODEL), lambda i, f: (i, 0))
    dspec = pl.BlockSpec((td, D_MODEL), lambda i, f: (i, 0))
    in_specs = [
        pspec, dspec,
        pl.BlockSpec((1, D_MODEL), lambda i, f: (0, 0)),
        pl.BlockSpec((D_MODEL, tf), lambda i, f: (0, f)),
        pl.BlockSpec((D_MODEL, tf), lambda i, f: (0, f)),
        pl.BlockSpec((tf, D_MODEL), lambda i, f: (f, 0)),
    ]
    args = [xp, xd, g, w1, w3, w2]
    if final_norm:
        in_specs.append(pl.BlockSpec((1, D_MODEL), lambda i, f: (0, 0)))
        args.append(g_final)
    in_specs += [spec for _, spec in casts]
    args += [w for w, _ in casts]
    return pl.pallas_call(
        functools.partial(_ffn_body, final_norm=final_norm, n_casts=len(casts)),
        grid=(n_tiles, D_FF // tf),
        in_specs=in_specs,
        out_specs=(pspec, dspec) + tuple(spec for _, spec in casts),
        out_shape=(jax.ShapeDtypeStruct((n_p, D_MODEL), F32),
                   jax.ShapeDtypeStruct((nd, D_MODEL), F32))
        + tuple(jax.ShapeDtypeStruct(w.shape, BF16) for w, _ in casts),
        scratch_shapes=[pltpu.VMEM((tm + td, D_MODEL), BF16)],
        compiler_params=pltpu.CompilerParams(
            dimension_semantics=("arbitrary", "arbitrary"), vmem_limit_bytes=FFN_VMEM_LIMIT),
        name="ffn_final" if final_norm else "ffn",
    )(*args)


def _inproj_body(*refs, n_tiles, nd, n_casts):
    xp_ref, xd_ref, g_ref, w_ref = refs[:4]
    cast_in = refs[4:4 + n_casts]
    op_ref, od_ref = refs[4 + n_casts:6 + n_casts]
    cast_out = refs[6 + n_casts:6 + 2 * n_casts]
    xn_ref = refs[-1]
    i = pl.program_id(0)
    j = pl.program_id(1)
    nj = pl.num_programs(1)

    for src_ref, dst_ref in zip(cast_in, cast_out):
        dst_ref[...] = src_ref[...].astype(BF16)

    def run(x_ref, o_ref, rows):
        @pl.when(j == 0)
        def _():
            xn_ref[0:rows, :] = _rms(x_ref[...], g_ref[...]).astype(BF16)

        @pl.when(j < nj - 1)
        def _():
            o_ref[...] = jnp.dot(xn_ref[0:rows, :], w_ref[...], preferred_element_type=F32)

        @pl.when(j == nj - 1)
        def _():
            o_ref[...] = _gelu(jnp.dot(xn_ref[0:rows, :], w_ref[...],
                                       preferred_element_type=F32))

    @pl.when(i < n_tiles)
    def _():
        run(xp_ref, op_ref, xp_ref.shape[0])

    @pl.when(i == n_tiles)
    def _():
        run(xd_ref, od_ref, nd)


def _inproj(xp, xd, g, w, *, casts=(), tm=1024, tn=1024):
    n_p, nd = xp.shape[0], xd.shape[0]
    n_tiles = n_p // tm
    d_out = w.shape[1]
    nj = d_out // tn
    return pl.pallas_call(
        functools.partial(_inproj_body, n_tiles=n_tiles, nd=nd, n_casts=len(casts)),
        grid=(n_tiles + 1, nj),
        in_specs=[
            pl.BlockSpec((tm, D_MODEL), lambda i, j: (jnp.minimum(i, n_tiles - 1), 0)),
            pl.BlockSpec((nd, D_MODEL), lambda i, j: (0, 0)),
            pl.BlockSpec((1, D_MODEL), lambda i, j: (0, 0)),
            pl.BlockSpec((D_MODEL, tn), lambda i, j: (0, j)),
        ] + [spec for _, spec in casts],
        out_specs=(
            pl.BlockSpec((tm, tn), lambda i, j: (jnp.minimum(i, n_tiles - 1),
                                                 jnp.where(i < n_tiles, j, nj - 1))),
            pl.BlockSpec((nd, tn), lambda i, j: (0, jnp.where(i < n_tiles, 0, j))),
        ) + tuple(spec for _, spec in casts),
        out_shape=(jax.ShapeDtypeStruct((n_p, d_out), F32),
                   jax.ShapeDtypeStruct((nd, d_out), F32))
        + tuple(jax.ShapeDtypeStruct(cw.shape, BF16) for cw, _ in casts),
        scratch_shapes=[pltpu.VMEM((tm, D_MODEL), BF16)],
        compiler_params=pltpu.CompilerParams(
            dimension_semantics=("arbitrary", "arbitrary"), vmem_limit_bytes=VMEM_LIMIT),
        name="inproj",
    )(xp, xd, g, w, *[cw for cw, _ in casts])


GROUPS_PER_TILE = 128 // S5_GROUP
PAIRS_PER_TILE = GROUPS_PER_TILE // 2
L_ROWS = CW + 2 * S5_STATE


def _s5_prep_body(lre_ref, lim_ref, ldt_ref, bre_ref, bim_ref, cre_ref, cim_ref,
                  l_ref, cpre_ref, cpim_ref, bd_ref, cd_ref, ar_ref, ai_ref, lr_ref, li_ref,
                  bp_ref):
    gb, half = GROUPS_PER_TILE, PAIRS_PER_TILE
    p = S5_STATE
    lo, hi = slice(0, p), slice(p, 2 * p)
    lam_re = lre_ref[...]
    lam_im = lim_ref[...]
    dt = jnp.exp(ldt_ref[...])
    mag = jnp.exp(lam_re * dt)
    ang = lam_im * dt
    lbr = mag * jnp.cos(ang)
    lbi = mag * jnp.sin(ang)
    lr_ref[:, :, lo] = lbr
    lr_ref[:, :, hi] = lbr
    li_ref[:, :, lo] = -lbi
    li_ref[:, :, hi] = lbi
    nr = lbr - 1.0
    den = lam_re * lam_re + lam_im * lam_im
    cr = (nr * lam_re + lbi * lam_im) / den
    ci = (lbi * lam_re - nr * lam_im) / den
    b_re = bre_ref[...]
    b_im = bim_ref[...]
    bbr = cr * b_re - ci * b_im
    bbi = cr * b_im + ci * b_re
    bd_ref[:, :, lo] = bbr
    bd_ref[:, :, hi] = bbi
    c_re = cre_ref[...]
    c_im = cim_ref[...]
    cd_ref[:, :, lo] = c_re
    cd_ref[:, :, hi] = -c_im

    zeros = jnp.zeros((half, S5_GROUP, p), F32)
    pr = jnp.ones_like(lbr)
    pi = jnp.zeros_like(lbr)
    for d in range(CHUNK):
        rows = slice(d * S5_GROUP, (d + 1) * S5_GROUP)
        back = slice((CHUNK - 1 - d) * S5_GROUP, (CHUNK - d) * S5_GROUP)
        bp_ref[:, back, lo] = bbr * pr - bbi * pi
        bp_ref[:, back, hi] = bbr * pi + bbi * pr
        pr, pi = pr * lbr - pi * lbi, pr * lbi + pi * lbr
        cp_r = c_re * pr - c_im * pi
        cp_i = -(c_re * pi + c_im * pr)
        cpre_ref[0:half, rows, lo] = cp_r[0:half]
        cpre_ref[0:half, rows, hi] = zeros
        cpre_ref[half:gb, rows, lo] = zeros
        cpre_ref[half:gb, rows, hi] = cp_r[half:gb]
        cpim_ref[0:half, rows, lo] = cp_i[0:half]
        cpim_ref[0:half, rows, hi] = zeros
        cpim_ref[half:gb, rows, lo] = zeros
        cpim_ref[half:gb, rows, hi] = cp_i[half:gb]

    qr, qi = pr, pi
    for j in range(N_DOUBLINGS):
        ar_ref[:, j:j + 1, lo] = qr[0:half]
        ar_ref[:, j:j + 1, hi] = qr[half:gb]
        ai_ref[:, j:j + 1, lo] = qi[0:half]
        ai_ref[:, j:j + 1, hi] = qi[half:gb]
        qr, qi = qr * qr - qi * qi, 2.0 * qr * qi
    ar_ref[:, N_DOUBLINGS:, :] = jnp.zeros((half, 8 - N_DOUBLINGS, 2 * p), F32)
    ai_ref[:, N_DOUBLINGS:, :] = jnp.zeros((half, 8 - N_DOUBLINGS, 2 * p), F32)

    lane = lax.broadcasted_iota(jnp.int32, (S5_GROUP, 128), 1)
    for j in range(gb):
        w = _dot3(cd_ref[j], bp_ref[j], NT)
        w0, w1 = w[:, :128], w[:, 128:]
        for t in range(CHUNK):
            rows = slice(t * S5_GROUP, (t + 1) * S5_GROUP)
            shift = (CHUNK - 1 - t) * S5_GROUP
            keep = 128 - shift % 128
            if shift == 0:
                left, right = w0, w1
            elif shift < 128:
                r0 = pltpu.roll(w0, keep, axis=1)
                r1 = pltpu.roll(w1, keep, axis=1)
                left = jnp.where(lane < keep, r0, r1)
                right = jnp.where(lane < keep, r1, 0.0)
            elif shift == 128:
                left, right = w1, jnp.zeros_like(w1)
            else:
                left = jnp.where(lane < keep, pltpu.roll(w1, keep, axis=1), 0.0)
                right = jnp.zeros_like(w1)
            l_ref[j, rows, 0:128] = left
            l_ref[j, rows, 128:256] = right
        l_ref[j, CW:L_ROWS, :] = bp_ref[j].T


def _s5_prep(lam_re, lam_im, log_dt, b_re, b_im, c_re, c_im):
    g, p, gb, half = S5_GROUPS, S5_STATE, GROUPS_PER_TILE, PAIRS_PER_TILE
    lre = lam_re.reshape(g, 1, p)
    lim = lam_im.reshape(g, 1, p)
    ldt = jnp.broadcast_to(log_dt.reshape(g, 1, 1), (g, 1, p))
    bre = jnp.transpose(b_re, (0, 2, 1))
    bim = jnp.transpose(b_im, (0, 2, 1))
    sd = jax.ShapeDtypeStruct
    blk = lambda n, r, c: pl.BlockSpec((n, r, c), lambda i: (i, 0, 0))
    return pl.pallas_call(
        _s5_prep_body,
        grid=(g // gb,),
        in_specs=[blk(gb, 1, p)] * 3 + [blk(gb, S5_GROUP, p)] * 4,
        out_specs=(
            blk(gb, L_ROWS, CW), blk(gb, CW, 2 * p), blk(gb, CW, 2 * p),
            blk(gb, S5_GROUP, 2 * p), blk(gb, S5_GROUP, 2 * p),
            blk(half, 8, 2 * p), blk(half, 8, 2 * p), blk(gb, 1, 2 * p), blk(gb, 1, 2 * p),
        ),
        out_shape=(
            sd((g, L_ROWS, CW), F32),
            sd((g, CW, 2 * p), F32),
            sd((g, CW, 2 * p), F32),
            sd((g, S5_GROUP, 2 * p), F32),
            sd((g, S5_GROUP, 2 * p), F32),
            sd((g // 2, 8, 2 * p), F32),
            sd((g // 2, 8, 2 * p), F32),
            sd((g, 1, 2 * p), F32),
            sd((g, 1, 2 * p), F32),
        ),
        scratch_shapes=[pltpu.VMEM((gb, CW, 2 * p), F32)],
        compiler_params=pltpu.CompilerParams(dimension_semantics=("parallel",)),
        name="s5_prep",
    )(lre, lim, ldt, bre, bim, c_re, c_im)


def _cmul_add(h, hs, ar, ai):
    return h + hs * ar + pltpu.roll(hs, S5_STATE, axis=1) * ai


def _s5p_body(u_ref, l_ref, cpre_ref, cpim_ref, ar_ref, ai_ref, y_ref, hre_ref, him_ref,
              ut_ref, yt_ref, *, nb, nk):
    gb, half = GROUPS_PER_TILE, PAIRS_PER_TILE
    p = S5_STATE
    nrow = nb * nk
    d = functools.partial(jnp.dot, preferred_element_type=F32)

    for t in range(CHUNK):
        xt = u_ref[pl.ds(t, nrow, stride=CHUNK), :].T
        for j in range(gb):
            ut_ref[j, t * S5_GROUP:(t + 1) * S5_GROUP, :] = xt[j * S5_GROUP:(j + 1) * S5_GROUP, :]

    s_re, s_im = [], []
    for j in range(gb):
        uh, ul = _split(ut_ref[j])
        lh, ll = _split(l_ref[j])
        r = d(lh, uh) + d(lh, ul) + d(ll, uh)
        yt_ref[j] = r[0:CW]
        s_re.append(r[CW:CW + p])
        s_im.append(r[CW + p:L_ROWS])

    rows = lax.broadcasted_iota(jnp.int32, (nrow, 2 * p), 0) & (nk - 1)
    for q in range(half):
        re = jnp.concatenate([s_re[q], s_re[q + half]], axis=0).T
        im = jnp.concatenate([s_im[q], s_im[q + half]], axis=0).T
        for k in range(N_DOUBLINGS):
            sh = 1 << k
            keep = rows >= sh
            rs = jnp.where(keep, pltpu.roll(re, sh, axis=0), 0.0)
            js = jnp.where(keep, pltpu.roll(im, sh, axis=0), 0.0)
            ar = ar_ref[q, k:k + 1, :]
            ai = ai_ref[q, k:k + 1, :]
            re, im = re + ar * rs - ai * js, im + ar * js + ai * rs
        for b in range(nb):
            hre_ref[q, b:b + 1, :] = re[(b + 1) * nk - 1:(b + 1) * nk, :]
            him_ref[q, b:b + 1, :] = im[(b + 1) * nk - 1:(b + 1) * nk, :]
        pre = jnp.where(rows >= 1, pltpu.roll(re, 1, axis=0), 0.0)
        pim = jnp.where(rows >= 1, pltpu.roll(im, 1, axis=0), 0.0)
        for j in (q, q + half):
            yt_ref[j] = yt_ref[j] + _dot3(cpre_ref[j], pre, NT) + _dot3(cpim_ref[j], pim, NT)

    for t in range(CHUNK):
        yt = jnp.concatenate(
            [yt_ref[j, t * S5_GROUP:(t + 1) * S5_GROUP, :] for j in range(gb)], axis=0)
        y_ref[pl.ds(t, nrow, stride=CHUNK), :] = yt.T


def _s5_prompt(proj_p, lmat, cpre, cpim, ar, ai, *, nb, seq):
    g, gb, half = S5_GROUPS, GROUPS_PER_TILE, PAIRS_PER_TILE
    n_p = nb * seq
    nk = seq // CHUNK
    blk = lambda n, r, c: pl.BlockSpec((n, r, c), lambda i: (i, 0, 0))
    return pl.pallas_call(
        functools.partial(_s5p_body, nb=nb, nk=nk),
        grid=(g // gb,),
        in_specs=[pl.BlockSpec((n_p, 128), lambda i: (0, i)),
                  blk(gb, L_ROWS, CW), blk(gb, CW, 2 * S5_STATE), blk(gb, CW, 2 * S5_STATE),
                  blk(half, 8, 2 * S5_STATE), blk(half, 8, 2 * S5_STATE)],
        out_specs=(pl.BlockSpec((n_p, 128), lambda i: (0, i)),
                   blk(half, nb, 2 * S5_STATE), blk(half, nb, 2 * S5_STATE)),
        out_shape=(jax.ShapeDtypeStruct((n_p, D_S5), F32),
                   jax.ShapeDtypeStruct((g // 2, nb, 2 * S5_STATE), F32),
                   jax.ShapeDtypeStruct((g // 2, nb, 2 * S5_STATE), F32)),
        scratch_shapes=[pltpu.VMEM((gb, CW, nb * nk), F32), pltpu.VMEM((gb, CW, nb * nk), F32)],
        compiler_params=pltpu.CompilerParams(
            dimension_semantics=("parallel",), vmem_limit_bytes=VMEM_LIMIT),
        name="s5_prompt",
    )(proj_p, lmat, cpre, cpim, ar, ai)


def _s5d_body(u_ref, h0_ref, bd_ref, cd_ref, lr_ref, li_ref, y_ref, h_ref, *, gb):
    for j in range(gb):
        h0 = h0_ref[j]
        bu = _dot3(u_ref[j], bd_ref[j])
        h = _cmul_add(bu, h0, lr_ref[j], li_ref[j])
        h_ref[j] = h
        y_ref[j] = _dot3(h, cd_ref[j], NT)


def _s5_decode(ud, h0, bd, cd, lr, li, *, gb=8):
    g, nbatch = S5_GROUPS, ud.shape[1]
    blk = lambda r, c: pl.BlockSpec((gb, r, c), lambda i: (i, 0, 0))
    return pl.pallas_call(
        functools.partial(_s5d_body, gb=gb),
        grid=(g // gb,),
        in_specs=[blk(nbatch, S5_GROUP), blk(nbatch, 2 * S5_STATE), blk(S5_GROUP, 2 * S5_STATE),
                  blk(S5_GROUP, 2 * S5_STATE), blk(1, 2 * S5_STATE), blk(1, 2 * S5_STATE)],
        out_specs=(blk(nbatch, S5_GROUP), blk(nbatch, 2 * S5_STATE)),
        out_shape=(jax.ShapeDtypeStruct((g, nbatch, S5_GROUP), F32),
                   jax.ShapeDtypeStruct((g, nbatch, 2 * S5_STATE), F32)),
        compiler_params=pltpu.CompilerParams(dimension_semantics=("parallel",)),
        name="s5_decode",
    )(ud, h0, bd, cd, lr, li)


def _lru_gates(xc, wa_ref, wx_ref, ba, bx, lam):
    xcb = xc.astype(BF16)
    nblk = D_LRU // 256
    r_parts, i_parts = [], []
    for k in range(nblk):
        xk = xcb[:, k * 256:(k + 1) * 256]
        r_parts.append(jnp.dot(xk, wa_ref[k], preferred_element_type=F32))
        i_parts.append(jnp.dot(xk, wx_ref[k], preferred_element_type=F32))
    r = jax.nn.sigmoid(jnp.concatenate(r_parts, axis=1) + ba)
    i = jax.nn.sigmoid(jnp.concatenate(i_parts, axis=1) + bx)
    z = -lam
    softplus = jnp.maximum(z, 0.0) + jnp.log1p(jnp.exp(-jnp.abs(z)))
    log_a = (-LRU_C * softplus) * r
    a = jnp.exp(log_a)
    v = -jnp.tanh(log_a) * (a * a + 1.0)
    mult = jnp.where(v > 0.0, v * lax.rsqrt(v), 0.0)
    return a, mult * (i * xc)


def _lru_prompt_body(xl_ref, gate_ref, cw_ref, cb_ref, wa_ref, wx_ref, ba_ref, bx_ref, lam_ref,
                     o_ref, hl_ref, xbuf_ref, carry_ref, *, tt):
    t = pl.program_id(1)

    @pl.when(t == 0)
    def _():
        xbuf_ref[0:8, :] = jnp.zeros((8, D_LRU), F32)
        carry_ref[...] = jnp.zeros((8, D_LRU), F32)

    x = xl_ref[...]
    xbuf_ref[8:8 + tt, :] = x
    cw = cw_ref[...]
    xc = (cb_ref[...] + xbuf_ref[5:5 + tt, :] * cw[0:1] + xbuf_ref[6:6 + tt, :] * cw[1:2]
          + xbuf_ref[7:7 + tt, :] * cw[2:3] + x * cw[3:4])
    xbuf_ref[0:8, :] = x[tt - 8:tt, :]

    a, b = _lru_gates(xc, wa_ref, wx_ref, ba_ref[...], bx_ref[...], lam_ref[...])

    nblk = tt // 8
    a3 = a.reshape(nblk, 8, D_LRU)
    b3 = b.reshape(nblk, 8, D_LRU)
    row = lax.broadcasted_iota(jnp.int32, (nblk, 8, D_LRU), 1)
    for sh in (1, 2, 4):
        keep = row >= sh
        bs = jnp.where(keep, pltpu.roll(b3, sh, axis=1), 0.0)
        sa = jnp.where(keep, pltpu.roll(a3, sh, axis=1), 1.0)
        b3 = b3 + a3 * bs
        a3 = a3 * sa
    carry = carry_ref[0:1, :]
    gate = gate_ref[...]
    for k in range(nblk):
        h = b3[k] + a3[k] * carry
        carry = h[7:8, :]
        o_ref[k * 8:(k + 1) * 8, :] = h * gate[k * 8:(k + 1) * 8, :]
    carry_ref[...] = jnp.broadcast_to(carry, (8, D_LRU))
    hl_ref[0] = carry


def _lru_prompt(proj, cw, cb, wa, wx, ba, bx, lam, *, nb, seq, tt=256):
    nt = seq // tt
    vec = lambda r: pl.BlockSpec((r, D_LRU), lambda b, t: (0, 0))
    wspec = pl.BlockSpec((D_LRU // 256, 256, 256), lambda b, t: (0, 0, 0))
    return pl.pallas_call(
        functools.partial(_lru_prompt_body, tt=tt),
        grid=(nb, nt),
        in_specs=[
            pl.BlockSpec((tt, D_LRU), lambda b, t: (b * nt + t, 1)),
            pl.BlockSpec((tt, D_LRU), lambda b, t: (b * nt + t, 2)),
            vec(CONV_W), vec(1), wspec, wspec, vec(1), vec(1), vec(1),
        ],
        out_specs=(pl.BlockSpec((tt, D_LRU), lambda b, t: (b * nt + t, 0)),
                   pl.BlockSpec((1, 1, D_LRU), lambda b, t: (b, 0, 0))),
        out_shape=(jax.ShapeDtypeStruct((nb * seq, D_LRU), F32),
                   jax.ShapeDtypeStruct((nb, 1, D_LRU), F32)),
        scratch_shapes=[pltpu.VMEM((tt + 8, D_LRU), F32), pltpu.VMEM((8, D_LRU), F32)],
        compiler_params=pltpu.CompilerParams(
            dimension_semantics=("parallel", "arbitrary"), vmem_limit_bytes=VMEM_LIMIT),
        name="lru_prompt",
    )(proj, proj, cw, cb, wa, wx, ba, bx, lam)


def _lru_decode_body(xl_ref, gate_ref, c0_ref, c1_ref, c2_ref, h0_ref, cw_ref, cb_ref,
                     wa_ref, wx_ref, ba_ref, bx_ref, lam_ref, o_ref, h_ref):
    x = xl_ref[...]
    cw = cw_ref[...]
    xc = (cb_ref[...] + c0_ref[...] * cw[0:1] + c1_ref[...] * cw[1:2]
          + c2_ref[...] * cw[2:3] + x * cw[3:4])
    a, b = _lru_gates(xc, wa_ref, wx_ref, ba_ref[...], bx_ref[...], lam_ref[...])
    h = a * h0_ref[...] + b
    h_ref[...] = h
    o_ref[...] = h * gate_ref[...]


def _lru_decode(proj_d, c0, c1, c2, h0, cw, cb, wa, wx, ba, bx, lam):
    nd = proj_d.shape[0]
    full = lambda r: pl.BlockSpec((r, D_LRU), lambda i: (0, 0))
    wspec = pl.BlockSpec((D_LRU // 256, 256, 256), lambda i: (0, 0, 0))
    return pl.pallas_call(
        _lru_decode_body,
        grid=(1,),
        in_specs=[
            pl.BlockSpec((nd, D_LRU), lambda i: (0, 1)),
            pl.BlockSpec((nd, D_LRU), lambda i: (0, 2)),
            full(nd), full(nd), full(nd), full(nd),
            full(CONV_W), full(1), wspec, wspec, full(1), full(1), full(1),
        ],
        out_specs=(full(nd), full(nd)),
        out_shape=(jax.ShapeDtypeStruct((nd, D_LRU), F32),
                   jax.ShapeDtypeStruct((nd, D_LRU), F32)),
        name="lru_decode",
    )(proj_d, proj_d, c0, c1, c2, h0, cw, cb, wa, wx, ba, bx, lam)


def _mix_body(ysp_ref, up_ref, lrup_ref, xp_ref, ysd_ref, ud_ref, lrud_ref, xd_ref,
              dsk_ref, wg_ref, bg_ref, gs_ref, gl_ref, wo_ref, op_ref, od_ref, *, n_tiles):
    i = pl.program_id(0)

    def run(ys_ref, u_ref, lru_ref, x_ref, o_ref):
        yy = ys_ref[...] + dsk_ref[...] * u_ref[...]
        g = _gelu(yy)
        z = jnp.dot(g.astype(BF16), wg_ref[...], preferred_element_type=F32) + bg_ref[...]
        s5o = g * jax.nn.sigmoid(z)
        n1 = _rms(s5o, gs_ref[...]).astype(BF16)
        n2 = _rms(lru_ref[...], gl_ref[...]).astype(BF16)
        o_ref[...] = (x_ref[...]
                      + jnp.dot(n1, wo_ref[0:D_S5, :], preferred_element_type=F32)
                      + jnp.dot(n2, wo_ref[D_S5:, :], preferred_element_type=F32))

    @pl.when(i < n_tiles)
    def _():
        run(ysp_ref, up_ref, lrup_ref, xp_ref, op_ref)

    @pl.when(i == n_tiles)
    def _():
        run(ysd_ref, ud_ref, lrud_ref, xd_ref, od_ref)


def _mix(ys_p, proj_p, lru_p, x1_p, ys_d, proj_d, lru_d, x1_d, dsk, wg, bg, gs, gl, wo, *, tm=512):
    n_p, nd = x1_p.shape[0], x1_d.shape[0]
    n_tiles = n_p // tm
    prow = lambda c: pl.BlockSpec((tm, c), lambda i: (jnp.minimum(i, n_tiles - 1), 0))
    drow = lambda c: pl.BlockSpec((nd, c), lambda i: (0, 0))
    vec = lambda c: pl.BlockSpec((1, c), lambda i: (0, 0))
    return pl.pallas_call(
        functools.partial(_mix_body, n_tiles=n_tiles),
        grid=(n_tiles + 1,),
        in_specs=[prow(D_S5), prow(D_S5), prow(D_LRU), prow(D_MODEL),
                  drow(D_S5), drow(D_S5), drow(D_LRU), drow(D_MODEL), vec(D_S5),
                  pl.BlockSpec((D_S5, D_S5), lambda i: (0, 0)), vec(D_S5), vec(D_S5), vec(D_LRU),
                  pl.BlockSpec((D_MODEL, D_MODEL), lambda i: (0, 0))],
        out_specs=(prow(D_MODEL), drow(D_MODEL)),
        out_shape=(jax.ShapeDtypeStruct((n_p, D_MODEL), F32),
                   jax.ShapeDtypeStruct((nd, D_MODEL), F32)),
        compiler_params=pltpu.CompilerParams(
            dimension_semantics=("arbitrary",), vmem_limit_bytes=VMEM_LIMIT),
        name="mix",
    )(ys_p, proj_p, lru_p, x1_p, ys_d, proj_d, lru_d, x1_d, dsk, wg, bg, gs, gl, wo)


def _unpair(h, nb):
    tiles = S5_GROUPS // GROUPS_PER_TILE
    h5 = h.reshape(tiles, PAIRS_PER_TILE, nb, 2, S5_STATE)
    return jnp.transpose(h5, (2, 0, 3, 1, 4)).reshape(nb, S5_GROUPS, S5_STATE)


def _block_diag4(w):
    w4 = w.reshape(LRU_HEADS // 4, 4, LRU_HEAD_DIM, LRU_HEAD_DIM)
    eye = jnp.eye(4, dtype=w.dtype)
    return jnp.einsum("kaij,ab->kaibj", w4, eye).reshape(LRU_HEADS // 4, 256, 256)


def kernel(x_prompt, x_sample, state_s5_re, state_s5_im, state_lru_h, state_lru_conv, g_ffn1, w1_a, w3_a, w2_a, g_mix, w_in, lam_re, lam_im, log_dt, b_re, b_im, c_re, c_im, d_skip, w_glu, b_glu, conv_w, conv_b, w_a, b_a, w_x, b_x, lam_l, g_out_s5, g_out_lru, w_out, g_ffn2, w1_b, w3_b, w2_b, g_final):
    nb, seq, _ = x_prompt.shape
    nd = x_sample.shape[0]
    n_p = nb * seq
    nk = seq // CHUNK
    g, p = S5_GROUPS, S5_STATE
    row = lambda v: v.reshape(1, -1)

    xp = x_prompt.reshape(n_p, D_MODEL)
    xd = x_sample.reshape(nd, D_MODEL)

    n_i, n_f = n_p // 1024, D_FF // 512
    x1_p, x1_d, w1_b16, w3_b16, w2_b16 = _ffn(
        xp, xd, row(g_ffn1[0]), w1_a[0].astype(BF16), w3_a[0].astype(BF16), w2_a[0].astype(BF16),
        casts=(_cast_job(w1_b[0], n_i, n_f), _cast_job(w3_b[0], n_i, n_f),
               _row_cast_job(w2_b[0], n_i, n_f)))
    proj_p, proj_d, w_out16, w_glu16 = _inproj(
        x1_p, x1_d, row(g_mix[0]), w_in[0].astype(BF16),
        casts=(_cast_job(w_out[0], n_i, D_IN // 1024, bc=1024),
               _cast_job(w_glu[0], n_i, D_IN // 1024, bc=1024)))

    lmat, cpre, cpim, bd, cd, ar, ai, lr, li = _s5_prep(
        lam_re[0], lam_im[0], log_dt[0], b_re[0], b_im[0], c_re[0], c_im[0])
    ys_p, hf_re, hf_im = _s5_prompt(proj_p, lmat, cpre, cpim, ar, ai, nb=nb, seq=seq)

    ud = jnp.transpose(proj_d[:, :D_S5].reshape(nd, g, S5_GROUP), (1, 0, 2))
    h0d = jnp.concatenate([jnp.transpose(state_s5_re[0], (1, 0, 2)),
                           jnp.transpose(state_s5_im[0], (1, 0, 2))], axis=-1)
    ydg, hd = _s5_decode(ud, h0d, bd, cd, lr, li)
    ys_d = jnp.transpose(ydg, (1, 0, 2)).reshape(nd, D_S5)

    wa_bd = _block_diag4(w_a[0]).astype(BF16)
    wx_bd = _block_diag4(w_x[0]).astype(BF16)
    lru_args = (conv_w[0], row(conv_b[0]), wa_bd, wx_bd, row(b_a[0]), row(b_x[0]), row(lam_l[0]))
    lru_p, hl_p = _lru_prompt(proj_p, *lru_args, nb=nb, seq=seq)
    conv0 = state_lru_conv[0]
    lru_d, hl_d = _lru_decode(proj_d, conv0[:, 0], conv0[:, 1], conv0[:, 2], state_lru_h[0],
                              *lru_args)

    x2_p, x2_d = _mix(ys_p, proj_p, lru_p, x1_p, ys_d, proj_d, lru_d, x1_d,
                      row(d_skip[0]), w_glu16, row(b_glu[0]),
                      row(g_out_s5[0]), row(g_out_lru[0]), w_out16)
    y_p, y_d = _ffn(x2_p, x2_d, row(g_ffn2[0]), w1_b16, w3_b16, w2_b16, row(g_final))

    tail_p = proj_p.reshape(nb, seq, -1)[:, seq - (CONV_W - 1):, D_S5:D_S5 + D_LRU]
    xl_d = proj_d[:, D_S5:D_S5 + D_LRU]
    return (
        y_p.reshape(nb, seq, D_MODEL),
        y_d.reshape(nd, 1, D_MODEL),
        _unpair(hf_re, nb)[None],
        _unpair(hf_im, nb)[None],
        hl_p.reshape(1, nb, D_LRU),
        tail_p[None],
        jnp.transpose(hd[:, :, :p], (1, 0, 2))[None],
        jnp.transpose(hd[:, :, p:], (1, 0, 2))[None],
        hl_d[None],
        jnp.stack([conv0[:, 1], conv0[:, 2], xl_d], axis=1)[None],
    )
```

```python
import functools

import jax
import jax.numpy as jnp
from jax import lax
from jax.experimental import pallas as pl
from jax.experimental.pallas import tpu as pltpu

F32 = jnp.float32
BF16 = jnp.bfloat16

D_MODEL = 2048
D_S5 = 1024
S5_GROUP = 16
S5_GROUPS = 64
S5_STATE = 64
D_LRU = 1024
LRU_HEADS = 16
LRU_HEAD_DIM = 64
CONV_W = 4
LRU_C = 8.0
D_FF = 5632
D_IN = D_S5 + 2 * D_LRU
EPS = 1e-6

CHUNK = 16
CW = CHUNK * S5_GROUP
N_DOUBLINGS = 7

VMEM_LIMIT = 58 * 1024 * 1024
FFN_VMEM_LIMIT = 60 * 1024 * 1024

NN = (((1,), (0,)), ((), ()))
NT = (((1,), (1,)), ((), ()))


def _rms(x, g):
    return x * lax.rsqrt(jnp.mean(x * x, axis=-1, keepdims=True) + EPS) * g


def _split(x):
    hi = x.astype(BF16)
    lo = (x - hi.astype(F32)).astype(BF16)
    return hi, lo


def _dot3(a, b, dims=NN):
    ah, al = _split(a)
    bh, bl = _split(b)
    d = functools.partial(lax.dot_general, dimension_numbers=dims, preferred_element_type=F32)
    return d(ah, bh) + d(al, bh) + d(ah, bl)


def _gelu(x):
    return jax.nn.gelu(x, approximate=True)


def _ffn_body(*refs, final_norm, convert, n_casts, n_prev):
    n_in = 7 if final_norm else 6
    xp_ref, xd_ref, g_ref, w1_ref, w3_ref, w2_ref = refs[:6]
    gf_ref = refs[6] if final_norm else None
    cast_in = refs[n_in:n_in + n_casts]
    outs = refs[n_in + n_casts + n_prev:-1]
    op_ref, od_ref = outs[:2]
    wcopy = outs[2:5] if convert else ()
    cast_out = outs[2 + len(wcopy):]
    xn_ref = refs[-1]
    f = pl.program_id(1)
    tm = xp_ref.shape[0]

    for src_ref, dst_ref in zip(cast_in, cast_out):
        dst_ref[...] = src_ref[...].astype(BF16)

    @pl.when(f == 0)
    def _():
        for x_ref, o_ref, rows in ((xp_ref, op_ref, slice(0, tm)), (xd_ref, od_ref, slice(tm, None))):
            x = x_ref[...]
            xn_ref[rows, :] = _rms(x, g_ref[...]).astype(BF16)
            o_ref[...] = x

    if convert:
        w1, w3, w2 = (w_ref[...].astype(BF16) for w_ref in (w1_ref, w3_ref, w2_ref))
        for dst_ref, w in zip(wcopy, (w1, w3, w2)):
            dst_ref[...] = w
    else:
        w1, w3, w2 = w1_ref[...], w3_ref[...], w2_ref[...]

    xn = xn_ref[...]
    a = jnp.dot(xn, w1, preferred_element_type=F32)
    b = jnp.dot(xn, w3, preferred_element_type=F32)
    h = (a * jax.nn.sigmoid(a) * b).astype(BF16)
    upd = 0.5 * jnp.dot(h, w2, preferred_element_type=F32)
    op_ref[...] += upd[0:tm]
    od_ref[...] += upd[tm:]

    if final_norm:
        @pl.when(f == pl.num_programs(1) - 1)
        def _():
            op_ref[...] = _rms(op_ref[...], gf_ref[...])
            od_ref[...] = _rms(od_ref[...], gf_ref[...])


def _cast_job(w, n_i, n_f, bc=512, i0=0, f_div=1):
    rows, cols = w.shape
    br = rows // n_i
    n_cb = cols // bc
    assert br * n_i == rows and bc * n_cb == cols and n_cb <= n_f and br % 16 == 0
    return w, pl.BlockSpec(
        (br, bc), lambda i, f: (jnp.minimum(i + i0, n_i - 1),
                                jnp.where(i + i0 < n_i, jnp.minimum(f // f_div, n_cb - 1),
                                          n_cb - 1)))


def _row_cast_job(w, n_i, n_f, i0=0, f_div=1):
    rows, cols = w.shape
    br = rows // (n_i * n_f)
    assert br * n_i * n_f == rows and br % 16 == 0
    return w, pl.BlockSpec((br, cols), lambda i, f: ((i + i0) * n_f + f // f_div, 0))


def _flat_cast_job(w, n_blocks):
    rows, cols = w.shape
    br = rows // n_blocks
    assert br * n_blocks == rows and br % 16 == 0
    return w, pl.BlockSpec((br, cols), lambda i, f: (jnp.minimum(f, n_blocks - 1), 0))


def _ffn(xp, xd, g, w1, w3, w2, g_final=None, *, casts=(), first_tile=0, n_tiles=None, prev=(),
         tm=1024):
    n_p, nd = xp.shape[0], xd.shape[0]
    all_tiles = n_p // tm
    n_tiles = all_tiles if n_tiles is None else n_tiles
    td = nd // all_tiles
    final_norm = g_final is not None
    convert = w1.dtype == F32
    tf = 256 if convert else 512
    pspec = pl.BlockSpec((tm, D_MODEL), lambda i, f: (i + first_tile, 0))
    dspec = pl.BlockSpec((td, D_MODEL), lambda i, f: (i + first_tile, 0))
    xspec = pspec if n_tiles > 1 else pl.BlockSpec(
        (tm, D_MODEL), lambda i, f: (i + first_tile, 0), pipeline_mode=pl.Buffered(1))
    w13spec = pl.BlockSpec((D_MODEL, tf), lambda i, f: (0, f))
    w2spec = pl.BlockSpec((tf, D_MODEL), lambda i, f: (f, 0))
    in_specs = [xspec, dspec, pl.BlockSpec((1, D_MODEL), lambda i, f: (0, 0)),
                w13spec, w13spec, w2spec]
    args = [xp, xd, g, w1, w3, w2]
    if final_norm:
        in_specs.append(pl.BlockSpec((1, D_MODEL), lambda i, f: (0, 0)))
        args.append(g_final)
    in_specs += [spec for _, spec in casts]
    args += [w for w, _ in casts]
    n_in = len(args)
    in_specs += [pl.BlockSpec(memory_space=pl.ANY)] * len(prev)
    args += list(prev)
    sd = jax.ShapeDtypeStruct
    out_specs = [pspec, dspec]
    out_shape = [sd((n_p, D_MODEL), F32), sd((nd, D_MODEL), F32)]
    if convert:
        out_specs += [w13spec, w13spec, w2spec]
        out_shape += [sd(w.shape, BF16) for w in (w1, w3, w2)]
    out_specs += [spec for _, spec in casts]
    out_shape += [sd(w.shape, BF16) for w, _ in casts]
    assert len(prev) in (0, len(out_shape))
    return pl.pallas_call(
        functools.partial(_ffn_body, final_norm=final_norm, convert=convert,
                          n_casts=len(casts), n_prev=len(prev)),
        grid=(n_tiles, D_FF // tf),
        in_specs=in_specs,
        out_specs=tuple(out_specs),
        out_shape=tuple(out_shape),
        input_output_aliases={n_in + k: k for k in range(len(prev))},
        scratch_shapes=[pltpu.VMEM((tm + td, D_MODEL), BF16)],
        compiler_params=pltpu.CompilerParams(
            dimension_semantics=("arbitrary", "arbitrary"), vmem_limit_bytes=FFN_VMEM_LIMIT),
        name="ffn_final" if final_norm else ("ffn_first" if convert else "ffn"),
    )(*args)


def _inproj_body(*refs, n_tiles, nd, n_casts):
    xp_ref, xd_ref, g_ref, w_ref = refs[:4]
    cast_in = refs[4:4 + n_casts]
    op_ref, od_ref = refs[4 + n_casts:6 + n_casts]
    cast_out = refs[6 + n_casts:6 + 2 * n_casts]
    xn_ref = refs[-1]
    i = pl.program_id(0)
    j = pl.program_id(1)
    nj = pl.num_programs(1)

    for src_ref, dst_ref in zip(cast_in, cast_out):
        dst_ref[...] = src_ref[...].astype(BF16)

    def run(x_ref, o_ref, rows):
        @pl.when(j == 0)
        def _():
            xn_ref[0:rows, :] = _rms(x_ref[...], g_ref[...]).astype(BF16)

        @pl.when(j < nj - 1)
        def _():
            o_ref[...] = jnp.dot(xn_ref[0:rows, :], w_ref[...], preferred_element_type=F32)

        @pl.when(j == nj - 1)
        def _():
            o_ref[...] = _gelu(jnp.dot(xn_ref[0:rows, :], w_ref[...],
                                       preferred_element_type=F32))

    @pl.when(i < n_tiles)
    def _():
        run(xp_ref, op_ref, xp_ref.shape[0])

    @pl.when(i == n_tiles)
    def _():
        run(xd_ref, od_ref, nd)


def _inproj(xp, xd, g, w, *, casts=(), tm=1024, tn=1024):
    n_p, nd = xp.shape[0], xd.shape[0]
    n_tiles = n_p // tm
    d_out = w.shape[1]
    nj = d_out // tn
    return pl.pallas_call(
        functools.partial(_inproj_body, n_tiles=n_tiles, nd=nd, n_casts=len(casts)),
        grid=(n_tiles + 1, nj),
        in_specs=[
            pl.BlockSpec((tm, D_MODEL), lambda i, j: (jnp.minimum(i, n_tiles - 1), 0)),
            pl.BlockSpec((nd, D_MODEL), lambda i, j: (0, 0)),
            pl.BlockSpec((1, D_MODEL), lambda i, j: (0, 0)),
            pl.BlockSpec((D_MODEL, tn), lambda i, j: (0, j)),
        ] + [spec for _, spec in casts],
        out_specs=(
            pl.BlockSpec((tm, tn), lambda i, j: (jnp.minimum(i, n_tiles - 1),
                                                 jnp.where(i < n_tiles, j, nj - 1))),
            pl.BlockSpec((nd, tn), lambda i, j: (0, jnp.where(i < n_tiles, 0, j))),
        ) + tuple(spec for _, spec in casts),
        out_shape=(jax.ShapeDtypeStruct((n_p, d_out), F32),
                   jax.ShapeDtypeStruct((nd, d_out), F32))
        + tuple(jax.ShapeDtypeStruct(cw.shape, BF16) for cw, _ in casts),
        scratch_shapes=[pltpu.VMEM((tm, D_MODEL), BF16)],
        compiler_params=pltpu.CompilerParams(
            dimension_semantics=("arbitrary", "arbitrary"), vmem_limit_bytes=VMEM_LIMIT),
        name="inproj",
    )(xp, xd, g, w, *[cw for cw, _ in casts])


GROUPS_PER_TILE = 128 // S5_GROUP
PAIRS_PER_TILE = GROUPS_PER_TILE // 2
L_ROWS = CW + 2 * S5_STATE


def _s5_prep_body(lre_ref, lim_ref, ldt_ref, bre_ref, bim_ref, cre_ref, cim_ref,
                  l_ref, cpre_ref, cpim_ref, bd_ref, cd_ref, ar_ref, ai_ref, lr_ref, li_ref,
                  bp_ref):
    gb, half = GROUPS_PER_TILE, PAIRS_PER_TILE
    p = S5_STATE
    lo, hi = slice(0, p), slice(p, 2 * p)
    lam_re = lre_ref[...]
    lam_im = lim_ref[...]
    dt = jnp.exp(ldt_ref[...])
    mag = jnp.exp(lam_re * dt)
    ang = lam_im * dt
    lbr = mag * jnp.cos(ang)
    lbi = mag * jnp.sin(ang)
    lr_ref[:, :, lo] = lbr
    lr_ref[:, :, hi] = lbr
    li_ref[:, :, lo] = -lbi
    li_ref[:, :, hi] = lbi
    nr = lbr - 1.0
    den = lam_re * lam_re + lam_im * lam_im
    cr = (nr * lam_re + lbi * lam_im) / den
    ci = (lbi * lam_re - nr * lam_im) / den
    b_re = bre_ref[...]
    b_im = bim_ref[...]
    bbr = cr * b_re - ci * b_im
    bbi = cr * b_im + ci * b_re
    bd_ref[:, :, lo] = bbr
    bd_ref[:, :, hi] = bbi
    c_re = cre_ref[...]
    c_im = cim_ref[...]
    cd_ref[:, :, lo] = c_re
    cd_ref[:, :, hi] = -c_im

    zeros = jnp.zeros((half, S5_GROUP, p), F32)
    pr = jnp.ones_like(lbr)
    pi = jnp.zeros_like(lbr)
    for d in range(CHUNK):
        rows = slice(d * S5_GROUP, (d + 1) * S5_GROUP)
        back = slice((CHUNK - 1 - d) * S5_GROUP, (CHUNK - d) * S5_GROUP)
        bp_ref[:, back, lo] = bbr * pr - bbi * pi
        bp_ref[:, back, hi] = bbr * pi + bbi * pr
        pr, pi = pr * lbr - pi * lbi, pr * lbi + pi * lbr
        cp_r = c_re * pr - c_im * pi
        cp_i = -(c_re * pi + c_im * pr)
        cpre_ref[0:half, rows, lo] = cp_r[0:half]
        cpre_ref[0:half, rows, hi] = zeros
        cpre_ref[half:gb, rows, lo] = zeros
        cpre_ref[half:gb, rows, hi] = cp_r[half:gb]
        cpim_ref[0:half, rows, lo] = cp_i[0:half]
        cpim_ref[0:half, rows, hi] = zeros
        cpim_ref[half:gb, rows, lo] = zeros
        cpim_ref[half:gb, rows, hi] = cp_i[half:gb]

    qr, qi = pr, pi
    for j in range(N_DOUBLINGS):
        ar_ref[:, j:j + 1, lo] = qr[0:half]
        ar_ref[:, j:j + 1, hi] = qr[half:gb]
        ai_ref[:, j:j + 1, lo] = qi[0:half]
        ai_ref[:, j:j + 1, hi] = qi[half:gb]
        qr, qi = qr * qr - qi * qi, 2.0 * qr * qi
    ar_ref[:, N_DOUBLINGS:, :] = jnp.zeros((half, 8 - N_DOUBLINGS, 2 * p), F32)
    ai_ref[:, N_DOUBLINGS:, :] = jnp.zeros((half, 8 - N_DOUBLINGS, 2 * p), F32)

    lane = lax.broadcasted_iota(jnp.int32, (S5_GROUP, 128), 1)
    for j in range(gb):
        w = _dot3(cd_ref[j], bp_ref[j], NT)
        w0, w1 = w[:, :128], w[:, 128:]
        for t in range(CHUNK):
            rows = slice(t * S5_GROUP, (t + 1) * S5_GROUP)
            shift = (CHUNK - 1 - t) * S5_GROUP
            keep = 128 - shift % 128
            if shift == 0:
                left, right = w0, w1
            elif shift < 128:
                r0 = pltpu.roll(w0, keep, axis=1)
                r1 = pltpu.roll(w1, keep, axis=1)
                left = jnp.where(lane < keep, r0, r1)
                right = jnp.where(lane < keep, r1, 0.0)
            elif shift == 128:
                left, right = w1, jnp.zeros_like(w1)
            else:
                left = jnp.where(lane < keep, pltpu.roll(w1, keep, axis=1), 0.0)
                right = jnp.zeros_like(w1)
            l_ref[j, rows, 0:128] = left
            l_ref[j, rows, 128:256] = right
        l_ref[j, CW:L_ROWS, :] = bp_ref[j].T


def _s5_prep(lam_re, lam_im, log_dt, b_re, b_im, c_re, c_im):
    g, p, gb, half = S5_GROUPS, S5_STATE, GROUPS_PER_TILE, PAIRS_PER_TILE
    lre = lam_re.reshape(g, 1, p)
    lim = lam_im.reshape(g, 1, p)
    ldt = jnp.broadcast_to(log_dt.reshape(g, 1, 1), (g, 1, p))
    bre = jnp.transpose(b_re, (0, 2, 1))
    bim = jnp.transpose(b_im, (0, 2, 1))
    sd = jax.ShapeDtypeStruct
    blk = lambda n, r, c: pl.BlockSpec((n, r, c), lambda i: (i, 0, 0))
    return pl.pallas_call(
        _s5_prep_body,
        grid=(g // gb,),
        in_specs=[blk(gb, 1, p)] * 3 + [blk(gb, S5_GROUP, p)] * 4,
        out_specs=(
            blk(gb, L_ROWS, CW), blk(gb, CW, 2 * p), blk(gb, CW, 2 * p),
            blk(gb, S5_GROUP, 2 * p), blk(gb, S5_GROUP, 2 * p),
            blk(half, 8, 2 * p), blk(half, 8, 2 * p), blk(gb, 1, 2 * p), blk(gb, 1, 2 * p),
        ),
        out_shape=(
            sd((g, L_ROWS, CW), F32),
            sd((g, CW, 2 * p), F32),
            sd((g, CW, 2 * p), F32),
            sd((g, S5_GROUP, 2 * p), F32),
            sd((g, S5_GROUP, 2 * p), F32),
            sd((g // 2, 8, 2 * p), F32),
            sd((g // 2, 8, 2 * p), F32),
            sd((g, 1, 2 * p), F32),
            sd((g, 1, 2 * p), F32),
        ),
        scratch_shapes=[pltpu.VMEM((gb, CW, 2 * p), F32)],
        compiler_params=pltpu.CompilerParams(dimension_semantics=("parallel",)),
        name="s5_prep",
    )(lre, lim, ldt, bre, bim, c_re, c_im)


def _cmul_add(h, hs, ar, ai):
    return h + hs * ar + pltpu.roll(hs, S5_STATE, axis=1) * ai


def _s5p_body(u_ref, l_ref, cpre_ref, cpim_ref, ar_ref, ai_ref, y_ref, hre_ref, him_ref,
              ut_ref, yt_ref, *, nb, nk):
    gb, half = GROUPS_PER_TILE, PAIRS_PER_TILE
    p = S5_STATE
    nrow = nb * nk
    d = functools.partial(jnp.dot, preferred_element_type=F32)

    for t in range(CHUNK):
        xt = u_ref[pl.ds(t, nrow, stride=CHUNK), :].T
        for j in range(gb):
            ut_ref[j, t * S5_GROUP:(t + 1) * S5_GROUP, :] = xt[j * S5_GROUP:(j + 1) * S5_GROUP, :]

    s_re, s_im = [], []
    for j in range(gb):
        uh, ul = _split(ut_ref[j])
        lh, ll = _split(l_ref[j])
        r = d(lh, uh) + d(lh, ul) + d(ll, uh)
        yt_ref[j] = r[0:CW]
        s_re.append(r[CW:CW + p])
        s_im.append(r[CW + p:L_ROWS])

    rows = lax.broadcasted_iota(jnp.int32, (nrow, 2 * p), 0) & (nk - 1)
    for q in range(half):
        re = jnp.concatenate([s_re[q], s_re[q + half]], axis=0).T
        im = jnp.concatenate([s_im[q], s_im[q + half]], axis=0).T
        for k in range(N_DOUBLINGS):
            sh = 1 << k
            keep = rows >= sh
            rs = jnp.where(keep, pltpu.roll(re, sh, axis=0), 0.0)
            js = jnp.where(keep, pltpu.roll(im, sh, axis=0), 0.0)
            ar = ar_ref[q, k:k + 1, :]
            ai = ai_ref[q, k:k + 1, :]
            re, im = re + ar * rs - ai * js, im + ar * js + ai * rs
        for b in range(nb):
            hre_ref[q, b:b + 1, :] = re[(b + 1) * nk - 1:(b + 1) * nk, :]
            him_ref[q, b:b + 1, :] = im[(b + 1) * nk - 1:(b + 1) * nk, :]
        pre = jnp.where(rows >= 1, pltpu.roll(re, 1, axis=0), 0.0)
        pim = jnp.where(rows >= 1, pltpu.roll(im, 1, axis=0), 0.0)
        for j in (q, q + half):
            yt_ref[j] = yt_ref[j] + _dot3(cpre_ref[j], pre, NT) + _dot3(cpim_ref[j], pim, NT)

    for t in range(CHUNK):
        yt = jnp.concatenate(
            [yt_ref[j, t * S5_GROUP:(t + 1) * S5_GROUP, :] for j in range(gb)], axis=0)
        y_ref[pl.ds(t, nrow, stride=CHUNK), :] = yt.T


def _s5_prompt(proj_p, lmat, cpre, cpim, ar, ai, *, nb, seq):
    g, gb, half = S5_GROUPS, GROUPS_PER_TILE, PAIRS_PER_TILE
    n_p = nb * seq
    nk = seq // CHUNK
    blk = lambda n, r, c: pl.BlockSpec((n, r, c), lambda i: (i, 0, 0))
    return pl.pallas_call(
        functools.partial(_s5p_body, nb=nb, nk=nk),
        grid=(g // gb,),
        in_specs=[pl.BlockSpec((n_p, 128), lambda i: (0, i)),
                  blk(gb, L_ROWS, CW), blk(gb, CW, 2 * S5_STATE), blk(gb, CW, 2 * S5_STATE),
                  blk(half, 8, 2 * S5_STATE), blk(half, 8, 2 * S5_STATE)],
        out_specs=(pl.BlockSpec((n_p, 128), lambda i: (0, i)),
                   blk(half, nb, 2 * S5_STATE), blk(half, nb, 2 * S5_STATE)),
        out_shape=(jax.ShapeDtypeStruct((n_p, D_S5), F32),
                   jax.ShapeDtypeStruct((g // 2, nb, 2 * S5_STATE), F32),
                   jax.ShapeDtypeStruct((g // 2, nb, 2 * S5_STATE), F32)),
        scratch_shapes=[pltpu.VMEM((gb, CW, nb * nk), F32), pltpu.VMEM((gb, CW, nb * nk), F32)],
        compiler_params=pltpu.CompilerParams(
            dimension_semantics=("parallel",), vmem_limit_bytes=VMEM_LIMIT),
        name="s5_prompt",
    )(proj_p, lmat, cpre, cpim, ar, ai)


def _s5d_body(u_ref, h0_ref, bd_ref, cd_ref, lr_ref, li_ref, y_ref, h_ref, *, gb):
    for j in range(gb):
        h0 = h0_ref[j]
        bu = _dot3(u_ref[j], bd_ref[j])
        h = _cmul_add(bu, h0, lr_ref[j], li_ref[j])
        h_ref[j] = h
        y_ref[j] = _dot3(h, cd_ref[j], NT)


def _s5_decode(ud, h0, bd, cd, lr, li, *, gb=8):
    g, nbatch = S5_GROUPS, ud.shape[1]
    blk = lambda r, c: pl.BlockSpec((gb, r, c), lambda i: (i, 0, 0))
    return pl.pallas_call(
        functools.partial(_s5d_body, gb=gb),
        grid=(g // gb,),
        in_specs=[blk(nbatch, S5_GROUP), blk(nbatch, 2 * S5_STATE), blk(S5_GROUP, 2 * S5_STATE),
                  blk(S5_GROUP, 2 * S5_STATE), blk(1, 2 * S5_STATE), blk(1, 2 * S5_STATE)],
        out_specs=(blk(nbatch, S5_GROUP), blk(nbatch, 2 * S5_STATE)),
        out_shape=(jax.ShapeDtypeStruct((g, nbatch, S5_GROUP), F32),
                   jax.ShapeDtypeStruct((g, nbatch, 2 * S5_STATE), F32)),
        compiler_params=pltpu.CompilerParams(dimension_semantics=("parallel",)),
        name="s5_decode",
    )(ud, h0, bd, cd, lr, li)


def _lru_gates(xc, wa_ref, wx_ref, ba, bx, lam):
    xcb = xc.astype(BF16)
    nblk = D_LRU // 256
    r_parts, i_parts = [], []
    for k in range(nblk):
        xk = xcb[:, k * 256:(k + 1) * 256]
        r_parts.append(jnp.dot(xk, wa_ref[k], preferred_element_type=F32))
        i_parts.append(jnp.dot(xk, wx_ref[k], preferred_element_type=F32))
    r = jax.nn.sigmoid(jnp.concatenate(r_parts, axis=1) + ba)
    i = jax.nn.sigmoid(jnp.concatenate(i_parts, axis=1) + bx)
    z = -lam
    softplus = jnp.maximum(z, 0.0) + jnp.log1p(jnp.exp(-jnp.abs(z)))
    log_a = (-LRU_C * softplus) * r
    a = jnp.exp(log_a)
    v = -jnp.tanh(log_a) * (a * a + 1.0)
    mult = jnp.where(v > 0.0, v * lax.rsqrt(v), 0.0)
    return a, mult * (i * xc)


def _lru_prompt_body(xl_ref, gate_ref, cw_ref, cb_ref, wa_ref, wx_ref, ba_ref, bx_ref, lam_ref,
                     o_ref, hl_ref, xbuf_ref, carry_ref, *, tt):
    t = pl.program_id(1)

    @pl.when(t == 0)
    def _():
        xbuf_ref[0:8, :] = jnp.zeros((8, D_LRU), F32)
        carry_ref[...] = jnp.zeros((8, D_LRU), F32)

    x = xl_ref[...]
    xbuf_ref[8:8 + tt, :] = x
    cw = cw_ref[...]
    xc = (cb_ref[...] + xbuf_ref[5:5 + tt, :] * cw[0:1] + xbuf_ref[6:6 + tt, :] * cw[1:2]
          + xbuf_ref[7:7 + tt, :] * cw[2:3] + x * cw[3:4])
    xbuf_ref[0:8, :] = x[tt - 8:tt, :]

    a, b = _lru_gates(xc, wa_ref, wx_ref, ba_ref[...], bx_ref[...], lam_ref[...])

    nblk = tt // 8
    a3 = a.reshape(nblk, 8, D_LRU)
    b3 = b.reshape(nblk, 8, D_LRU)
    row = lax.broadcasted_iota(jnp.int32, (nblk, 8, D_LRU), 1)
    for sh in (1, 2, 4):
        keep = row >= sh
        bs = jnp.where(keep, pltpu.roll(b3, sh, axis=1), 0.0)
        sa = jnp.where(keep, pltpu.roll(a3, sh, axis=1), 1.0)
        b3 = b3 + a3 * bs
        a3 = a3 * sa
    carry = carry_ref[0:1, :]
    gate = gate_ref[...]
    for k in range(nblk):
        h = b3[k] + a3[k] * carry
        carry = h[7:8, :]
        o_ref[k * 8:(k + 1) * 8, :] = h * gate[k * 8:(k + 1) * 8, :]
    carry_ref[...] = jnp.broadcast_to(carry, (8, D_LRU))
    hl_ref[0] = carry


def _lru_prompt(proj, cw, cb, wa, wx, ba, bx, lam, *, nb, seq, tt=256):
    nt = seq // tt
    vec = lambda r: pl.BlockSpec((r, D_LRU), lambda b, t: (0, 0))
    wspec = pl.BlockSpec((D_LRU // 256, 256, 256), lambda b, t: (0, 0, 0))
    return pl.pallas_call(
        functools.partial(_lru_prompt_body, tt=tt),
        grid=(nb, nt),
        in_specs=[
            pl.BlockSpec((tt, D_LRU), lambda b, t: (b * nt + t, 1)),
            pl.BlockSpec((tt, D_LRU), lambda b, t: (b * nt + t, 2)),
            vec(CONV_W), vec(1), wspec, wspec, vec(1), vec(1), vec(1),
        ],
        out_specs=(pl.BlockSpec((tt, D_LRU), lambda b, t: (b * nt + t, 0)),
                   pl.BlockSpec((1, 1, D_LRU), lambda b, t: (b, 0, 0))),
        out_shape=(jax.ShapeDtypeStruct((nb * seq, D_LRU), F32),
                   jax.ShapeDtypeStruct((nb, 1, D_LRU), F32)),
        scratch_shapes=[pltpu.VMEM((tt + 8, D_LRU), F32), pltpu.VMEM((8, D_LRU), F32)],
        compiler_params=pltpu.CompilerParams(
            dimension_semantics=("parallel", "arbitrary"), vmem_limit_bytes=VMEM_LIMIT),
        name="lru_prompt",
    )(proj, proj, cw, cb, wa, wx, ba, bx, lam)


def _lru_decode_body(xl_ref, gate_ref, c0_ref, c1_ref, c2_ref, h0_ref, cw_ref, cb_ref,
                     wa_ref, wx_ref, ba_ref, bx_ref, lam_ref, o_ref, h_ref):
    x = xl_ref[...]
    cw = cw_ref[...]
    xc = (cb_ref[...] + c0_ref[...] * cw[0:1] + c1_ref[...] * cw[1:2]
          + c2_ref[...] * cw[2:3] + x * cw[3:4])
    a, b = _lru_gates(xc, wa_ref, wx_ref, ba_ref[...], bx_ref[...], lam_ref[...])
    h = a * h0_ref[...] + b
    h_ref[...] = h
    o_ref[...] = h * gate_ref[...]


def _lru_decode(proj_d, c0, c1, c2, h0, cw, cb, wa, wx, ba, bx, lam):
    nd = proj_d.shape[0]
    full = lambda r: pl.BlockSpec((r, D_LRU), lambda i: (0, 0))
    wspec = pl.BlockSpec((D_LRU // 256, 256, 256), lambda i: (0, 0, 0))
    return pl.pallas_call(
        _lru_decode_body,
        grid=(1,),
        in_specs=[
            pl.BlockSpec((nd, D_LRU), lambda i: (0, 1)),
            pl.BlockSpec((nd, D_LRU), lambda i: (0, 2)),
            full(nd), full(nd), full(nd), full(nd),
            full(CONV_W), full(1), wspec, wspec, full(1), full(1), full(1),
        ],
        out_specs=(full(nd), full(nd)),
        out_shape=(jax.ShapeDtypeStruct((nd, D_LRU), F32),
                   jax.ShapeDtypeStruct((nd, D_LRU), F32)),
        name="lru_decode",
    )(proj_d, proj_d, c0, c1, c2, h0, cw, cb, wa, wx, ba, bx, lam)


def _mix_body(ysp_ref, up_ref, lrup_ref, xp_ref, ysd_ref, ud_ref, lrud_ref, xd_ref,
              dsk_ref, wg_ref, bg_ref, gs_ref, gl_ref, wo_ref, op_ref, od_ref, *, n_tiles):
    i = pl.program_id(0)

    def run(ys_ref, u_ref, lru_ref, x_ref, o_ref):
        yy = ys_ref[...] + dsk_ref[...] * u_ref[...]
        g = _gelu(yy)
        z = jnp.dot(g.astype(BF16), wg_ref[...], preferred_element_type=F32) + bg_ref[...]
        s5o = g * jax.nn.sigmoid(z)
        n1 = _rms(s5o, gs_ref[...]).astype(BF16)
        n2 = _rms(lru_ref[...], gl_ref[...]).astype(BF16)
        o_ref[...] = (x_ref[...]
                      + jnp.dot(n1, wo_ref[0:D_S5, :], preferred_element_type=F32)
                      + jnp.dot(n2, wo_ref[D_S5:, :], preferred_element_type=F32))

    @pl.when(i < n_tiles)
    def _():
        run(ysp_ref, up_ref, lrup_ref, xp_ref, op_ref)

    @pl.when(i == n_tiles)
    def _():
        run(ysd_ref, ud_ref, lrud_ref, xd_ref, od_ref)


def _mix(ys_p, proj_p, lru_p, x1_p, ys_d, proj_d, lru_d, x1_d, dsk, wg, bg, gs, gl, wo, *, tm=512):
    n_p, nd = x1_p.shape[0], x1_d.shape[0]
    n_tiles = n_p // tm
    prow = lambda c: pl.BlockSpec((tm, c), lambda i: (jnp.minimum(i, n_tiles - 1), 0))
    drow = lambda c: pl.BlockSpec((nd, c), lambda i: (0, 0))
    vec = lambda c: pl.BlockSpec((1, c), lambda i: (0, 0))
    return pl.pallas_call(
        functools.partial(_mix_body, n_tiles=n_tiles),
        grid=(n_tiles + 1,),
        in_specs=[prow(D_S5), prow(D_S5), prow(D_LRU), prow(D_MODEL),
                  drow(D_S5), drow(D_S5), drow(D_LRU), drow(D_MODEL), vec(D_S5),
                  pl.BlockSpec((D_S5, D_S5), lambda i: (0, 0)), vec(D_S5), vec(D_S5), vec(D_LRU),
                  pl.BlockSpec((D_MODEL, D_MODEL), lambda i: (0, 0))],
        out_specs=(prow(D_MODEL), drow(D_MODEL)),
        out_shape=(jax.ShapeDtypeStruct((n_p, D_MODEL), F32),
                   jax.ShapeDtypeStruct((nd, D_MODEL), F32)),
        compiler_params=pltpu.CompilerParams(
            dimension_semantics=("arbitrary",), vmem_limit_bytes=VMEM_LIMIT),
        name="mix",
    )(ys_p, proj_p, lru_p, x1_p, ys_d, proj_d, lru_d, x1_d, dsk, wg, bg, gs, gl, wo)


def _unpair(h, nb):
    tiles = S5_GROUPS // GROUPS_PER_TILE
    h5 = h.reshape(tiles, PAIRS_PER_TILE, nb, 2, S5_STATE)
    return jnp.transpose(h5, (2, 0, 3, 1, 4)).reshape(nb, S5_GROUPS, S5_STATE)


def _block_diag4(w):
    w4 = w.reshape(LRU_HEADS // 4, 4, LRU_HEAD_DIM, LRU_HEAD_DIM)
    eye = jnp.eye(4, dtype=w.dtype)
    return jnp.einsum("kaij,ab->kaibj", w4, eye).reshape(LRU_HEADS // 4, 256, 256)


def kernel(x_prompt, x_sample, state_s5_re, state_s5_im, state_lru_h, state_lru_conv, g_ffn1, w1_a, w3_a, w2_a, g_mix, w_in, lam_re, lam_im, log_dt, b_re, b_im, c_re, c_im, d_skip, w_glu, b_glu, conv_w, conv_b, w_a, b_a, w_x, b_x, lam_l, g_out_s5, g_out_lru, w_out, g_ffn2, w1_b, w3_b, w2_b, g_final):
    nb, seq, _ = x_prompt.shape
    nd = x_sample.shape[0]
    n_p = nb * seq
    nk = seq // CHUNK
    g, p = S5_GROUPS, S5_STATE
    row = lambda v: v.reshape(1, -1)

    xp = x_prompt.reshape(n_p, D_MODEL)
    xd = x_sample.reshape(nd, D_MODEL)

    n_i, n_f = n_p // 1024, D_FF // 512
    first = _ffn(
        xp, xd, row(g_ffn1[0]), w1_a[0], w3_a[0], w2_a[0], n_tiles=1,
        casts=(_cast_job(w1_b[0], n_i, n_f, f_div=2), _cast_job(w3_b[0], n_i, n_f, f_div=2),
               _row_cast_job(w2_b[0], n_i, n_f, f_div=2), _flat_cast_job(w_in[0], 16)))
    w1_a16, w3_a16, w2_a16 = first[2:5]
    w_in16 = first[8]
    x1_p, x1_d, w1_b16, w3_b16, w2_b16 = _ffn(
        xp, xd, row(g_ffn1[0]), w1_a16, w3_a16, w2_a16, first_tile=1, n_tiles=n_i - 1,
        casts=(_cast_job(w1_b[0], n_i, n_f, i0=1), _cast_job(w3_b[0], n_i, n_f, i0=1),
               _row_cast_job(w2_b[0], n_i, n_f, i0=1)),
        prev=first[:2] + first[5:8])
    proj_p, proj_d, w_out16, w_glu16 = _inproj(
        x1_p, x1_d, row(g_mix[0]), w_in16,
        casts=(_cast_job(w_out[0], n_i, D_IN // 1024, bc=1024),
               _cast_job(w_glu[0], n_i, D_IN // 1024, bc=1024)))

    lmat, cpre, cpim, bd, cd, ar, ai, lr, li = _s5_prep(
        lam_re[0], lam_im[0], log_dt[0], b_re[0], b_im[0], c_re[0], c_im[0])
    ys_p, hf_re, hf_im = _s5_prompt(proj_p, lmat, cpre, cpim, ar, ai, nb=nb, seq=seq)

    ud = jnp.transpose(proj_d[:, :D_S5].reshape(nd, g, S5_GROUP), (1, 0, 2))
    h0d = jnp.concatenate([jnp.transpose(state_s5_re[0], (1, 0, 2)),
                           jnp.transpose(state_s5_im[0], (1, 0, 2))], axis=-1)
    ydg, hd = _s5_decode(ud, h0d, bd, cd, lr, li)
    ys_d = jnp.transpose(ydg, (1, 0, 2)).reshape(nd, D_S5)

    wa_bd = _block_diag4(w_a[0]).astype(BF16)
    wx_bd = _block_diag4(w_x[0]).astype(BF16)
    lru_args = (conv_w[0], row(conv_b[0]), wa_bd, wx_bd, row(b_a[0]), row(b_x[0]), row(lam_l[0]))
    lru_p, hl_p = _lru_prompt(proj_p, *lru_args, nb=nb, seq=seq)
    conv0 = state_lru_conv[0]
    lru_d, hl_d = _lru_decode(proj_d, conv0[:, 0], conv0[:, 1], conv0[:, 2], state_lru_h[0],
                              *lru_args)

    x2_p, x2_d = _mix(ys_p, proj_p, lru_p, x1_p, ys_d, proj_d, lru_d, x1_d,
                      row(d_skip[0]), w_glu16, row(b_glu[0]),
                      row(g_out_s5[0]), row(g_out_lru[0]), w_out16)
    y_p, y_d = _ffn(x2_p, x2_d, row(g_ffn2[0]), w1_b16, w3_b16, w2_b16, row(g_final))

    tail_p = proj_p.reshape(nb, seq, -1)[:, seq - (CONV_W - 1):, D_S5:D_S5 + D_LRU]
    xl_d = proj_d[:, D_S5:D_S5 + D_LRU]
    return (
        y_p.reshape(nb, seq, D_MODEL),
        y_d.reshape(nd, 1, D_MODEL),
        _unpair(hf_re, nb)[None],
        _unpair(hf_im, nb)[None],
        hl_p.reshape(1, nb, D_LRU),
        tail_p[None],
        jnp.transpose(hd[:, :, :p], (1, 0, 2))[None],
        jnp.transpose(hd[:, :, p:], (1, 0, 2))[None],
        hl_d[None],
        jnp.stack([conv0[:, 1], conv0[:, 2], xl_d], axis=1)[None],
    )
```

```python
import functools

import jax
import jax.numpy as jnp
from jax import lax
from jax.experimental import pallas as pl
from jax.experimental.pallas import tpu as pltpu

F32 = jnp.float32
BF16 = jnp.bfloat16

D_MODEL = 2048
D_S5 = 1024
S5_GROUP = 16
S5_GROUPS = 64
S5_STATE = 64
D_LRU = 1024
LRU_HEADS = 16
LRU_HEAD_DIM = 64
CONV_W = 4
LRU_C = 8.0
D_FF = 5632
D_IN = D_S5 + 2 * D_LRU
EPS = 1e-6

CHUNK = 16
CW = CHUNK * S5_GROUP
N_DOUBLINGS = 7

VMEM_LIMIT = 58 * 1024 * 1024
FFN_VMEM_LIMIT = 60 * 1024 * 1024

NN = (((1,), (0,)), ((), ()))
NT = (((1,), (1,)), ((), ()))


def _rms(x, g):
    return x * lax.rsqrt(jnp.mean(x * x, axis=-1, keepdims=True) + EPS) * g


def _split(x):
    hi = x.astype(BF16)
    lo = (x - hi.astype(F32)).astype(BF16)
    return hi, lo


def _dot3(a, b, dims=NN):
    ah, al = _split(a)
    bh, bl = _split(b)
    d = functools.partial(lax.dot_general, dimension_numbers=dims, preferred_element_type=F32)
    return d(ah, bh) + d(al, bh) + d(ah, bl)


def _gelu(x):
    return jax.nn.gelu(x, approximate=True)


def _ffn_body(*refs, final_norm, convert, n_casts, n_prev):
    n_in = 7 if final_norm else 6
    xp_ref, xd_ref, g_ref, w1_ref, w3_ref, w2_ref = refs[:6]
    gf_ref = refs[6] if final_norm else None
    cast_in = refs[n_in:n_in + n_casts]
    outs = refs[n_in + n_casts + n_prev:-1]
    op_ref, od_ref = outs[:2]
    wcopy = outs[2:5] if convert else ()
    cast_out = outs[2 + len(wcopy):]
    xn_ref = refs[-1]
    f = pl.program_id(1)
    tm = xp_ref.shape[0]

    for src_ref, dst_ref in zip(cast_in, cast_out):
        dst_ref[...] = src_ref[...].astype(BF16)

    @pl.when(f == 0)
    def _():
        for x_ref, o_ref, rows in ((xp_ref, op_ref, slice(0, tm)), (xd_ref, od_ref, slice(tm, None))):
            x = x_ref[...]
            xn_ref[rows, :] = _rms(x, g_ref[...]).astype(BF16)
            o_ref[...] = x

    if convert:
        w1, w3, w2 = (w_ref[...].astype(BF16) for w_ref in (w1_ref, w3_ref, w2_ref))
        for dst_ref, w in zip(wcopy, (w1, w3, w2)):
            dst_ref[...] = w
    else:
        w1, w3, w2 = w1_ref[...], w3_ref[...], w2_ref[...]

    xn = xn_ref[...]
    a = jnp.dot(xn, w1, preferred_element_type=F32)
    b = jnp.dot(xn, w3, preferred_element_type=F32)
    h = (a * jax.nn.sigmoid(a) * b).astype(BF16)
    upd = 0.5 * jnp.dot(h, w2, preferred_element_type=F32)
    op_ref[...] += upd[0:tm]
    od_ref[...] += upd[tm:]

    if final_norm:
        @pl.when(f == pl.num_programs(1) - 1)
        def _():
            op_ref[...] = _rms(op_ref[...], gf_ref[...])
            od_ref[...] = _rms(od_ref[...], gf_ref[...])


def _cast_job(w, n_i, n_f, bc=512, i0=0, f_div=1):
    rows, cols = w.shape
    br = rows // n_i
    n_cb = cols // bc
    assert br * n_i == rows and bc * n_cb == cols and n_cb <= n_f and br % 16 == 0
    return w, pl.BlockSpec(
        (br, bc), lambda i, f: (jnp.minimum(i + i0, n_i - 1),
                                jnp.where(i + i0 < n_i, jnp.minimum(f // f_div, n_cb - 1),
                                          n_cb - 1)))


def _row_cast_job(w, n_i, n_f, i0=0, f_div=1):
    rows, cols = w.shape
    br = rows // (n_i * n_f)
    assert br * n_i * n_f == rows and br % 16 == 0
    return w, pl.BlockSpec((br, cols), lambda i, f: ((i + i0) * n_f + f // f_div, 0))


def _flat_cast_job(w, n_blocks):
    rows, cols = w.shape
    br = rows // n_blocks
    assert br * n_blocks == rows and br % 16 == 0
    return w, pl.BlockSpec((br, cols), lambda i, f: (jnp.minimum(f, n_blocks - 1), 0))


def _ffn(xp, xd, g, w1, w3, w2, g_final=None, *, casts=(), first_tile=0, n_tiles=None, prev=(),
         tm=1024):
    n_p, nd = xp.shape[0], xd.shape[0]
    all_tiles = n_p // tm
    n_tiles = all_tiles if n_tiles is None else n_tiles
    td = nd // all_tiles
    final_norm = g_final is not None
    convert = w1.dtype == F32
    tf = 256 if convert else 512
    pspec = pl.BlockSpec((tm, D_MODEL), lambda i, f: (i + first_tile, 0))
    dspec = pl.BlockSpec((td, D_MODEL), lambda i, f: (i + first_tile, 0))
    xspec = pspec if n_tiles > 1 else pl.BlockSpec(
        (tm, D_MODEL), lambda i, f: (i + first_tile, 0), pipeline_mode=pl.Buffered(1))
    w13spec = pl.BlockSpec((D_MODEL, tf), lambda i, f: (0, f))
    w2spec = pl.BlockSpec((tf, D_MODEL), lambda i, f: (f, 0))
    in_specs = [xspec, dspec, pl.BlockSpec((1, D_MODEL), lambda i, f: (0, 0)),
                w13spec, w13spec, w2spec]
    args = [xp, xd, g, w1, w3, w2]
    if final_norm:
        in_specs.append(pl.BlockSpec((1, D_MODEL), lambda i, f: (0, 0)))
        args.append(g_final)
    in_specs += [spec for _, spec in casts]
    args += [w for w, _ in casts]
    n_in = len(args)
    in_specs += [pl.BlockSpec(memory_space=pl.ANY)] * len(prev)
    args += list(prev)
    sd = jax.ShapeDtypeStruct
    out_specs = [pspec, dspec]
    out_shape = [sd((n_p, D_MODEL), F32), sd((nd, D_MODEL), F32)]
    if convert:
        out_specs += [w13spec, w13spec, w2spec]
        out_shape += [sd(w.shape, BF16) for w in (w1, w3, w2)]
    out_specs += [spec for _, spec in casts]
    out_shape += [sd(w.shape, BF16) for w, _ in casts]
    assert len(prev) in (0, len(out_shape))
    return pl.pallas_call(
        functools.partial(_ffn_body, final_norm=final_norm, convert=convert,
                          n_casts=len(casts), n_prev=len(prev)),
        grid=(n_tiles, D_FF // tf),
        in_specs=in_specs,
        out_specs=tuple(out_specs),
        out_shape=tuple(out_shape),
        input_output_aliases={n_in + k: k for k in range(len(prev))},
        scratch_shapes=[pltpu.VMEM((tm + td, D_MODEL), BF16)],
        compiler_params=pltpu.CompilerParams(
            dimension_semantics=("arbitrary", "arbitrary"), vmem_limit_bytes=FFN_VMEM_LIMIT),
        name="ffn_final" if final_norm else ("ffn_first" if convert else "ffn"),
    )(*args)


def _inproj_body(*refs, n_tiles, nd, n_casts):
    xp_ref, xd_ref, g_ref, w_ref = refs[:4]
    cast_in = refs[4:4 + n_casts]
    op_ref, od_ref = refs[4 + n_casts:6 + n_casts]
    cast_out = refs[6 + n_casts:6 + 2 * n_casts]
    xn_ref = refs[-1]
    i = pl.program_id(0)
    j = pl.program_id(1)
    nj = pl.num_programs(1)

    for src_ref, dst_ref in zip(cast_in, cast_out):
        dst_ref[...] = src_ref[...].astype(BF16)

    def run(x_ref, o_ref, rows):
        @pl.when(j == 0)
        def _():
            xn_ref[0:rows, :] = _rms(x_ref[...], g_ref[...]).astype(BF16)

        @pl.when(j < nj - 1)
        def _():
            o_ref[...] = jnp.dot(xn_ref[0:rows, :], w_ref[...], preferred_element_type=F32)

        @pl.when(j == nj - 1)
        def _():
            o_ref[...] = _gelu(jnp.dot(xn_ref[0:rows, :], w_ref[...],
                                       preferred_element_type=F32))

    @pl.when(i < n_tiles)
    def _():
        run(xp_ref, op_ref, xp_ref.shape[0])

    @pl.when(i == n_tiles)
    def _():
        run(xd_ref, od_ref, nd)


def _inproj(xp, xd, g, w, *, casts=(), tm=1024, tn=1024):
    n_p, nd = xp.shape[0], xd.shape[0]
    n_tiles = n_p // tm
    d_out = w.shape[1]
    nj = d_out // tn
    return pl.pallas_call(
        functools.partial(_inproj_body, n_tiles=n_tiles, nd=nd, n_casts=len(casts)),
        grid=(n_tiles + 1, nj),
        in_specs=[
            pl.BlockSpec((tm, D_MODEL), lambda i, j: (jnp.minimum(i, n_tiles - 1), 0)),
            pl.BlockSpec((nd, D_MODEL), lambda i, j: (0, 0)),
            pl.BlockSpec((1, D_MODEL), lambda i, j: (0, 0)),
            pl.BlockSpec((D_MODEL, tn), lambda i, j: (0, j)),
        ] + [spec for _, spec in casts],
        out_specs=(
            pl.BlockSpec((tm, tn), lambda i, j: (jnp.minimum(i, n_tiles - 1),
                                                 jnp.where(i < n_tiles, j, nj - 1))),
            pl.BlockSpec((nd, tn), lambda i, j: (0, jnp.where(i < n_tiles, 0, j))),
        ) + tuple(spec for _, spec in casts),
        out_shape=(jax.ShapeDtypeStruct((n_p, d_out), F32),
                   jax.ShapeDtypeStruct((nd, d_out), F32))
        + tuple(jax.ShapeDtypeStruct(cw.shape, BF16) for cw, _ in casts),
        scratch_shapes=[pltpu.VMEM((tm, D_MODEL), BF16)],
        compiler_params=pltpu.CompilerParams(
            dimension_semantics=("arbitrary", "arbitrary"), vmem_limit_bytes=VMEM_LIMIT),
        name="inproj",
    )(xp, xd, g, w, *[cw for cw, _ in casts])


GROUPS_PER_TILE = 128 // S5_GROUP
PAIRS_PER_TILE = GROUPS_PER_TILE // 2
L_ROWS = CW + 2 * S5_STATE


def _s5_prep_body(lre_ref, lim_ref, ldt_ref, bre_ref, bim_ref, cre_ref, cim_ref,
                  l_ref, cpre_ref, cpim_ref, bd_ref, cd_ref, ar_ref, ai_ref, lr_ref, li_ref,
                  bp_ref):
    gb, half = GROUPS_PER_TILE, PAIRS_PER_TILE
    p = S5_STATE
    lo, hi = slice(0, p), slice(p, 2 * p)
    lam_re = lre_ref[...]
    lam_im = lim_ref[...]
    dt = jnp.exp(ldt_ref[...])
    mag = jnp.exp(lam_re * dt)
    ang = lam_im * dt
    lbr = mag * jnp.cos(ang)
    lbi = mag * jnp.sin(ang)
    lr_ref[:, :, lo] = lbr
    lr_ref[:, :, hi] = lbr
    li_ref[:, :, lo] = -lbi
    li_ref[:, :, hi] = lbi
    nr = lbr - 1.0
    den = lam_re * lam_re + lam_im * lam_im
    cr = (nr * lam_re + lbi * lam_im) / den
    ci = (lbi * lam_re - nr * lam_im) / den
    b_re = bre_ref[...]
    b_im = bim_ref[...]
    bbr = cr * b_re - ci * b_im
    bbi = cr * b_im + ci * b_re
    bd_ref[:, :, lo] = bbr
    bd_ref[:, :, hi] = bbi
    c_re = cre_ref[...]
    c_im = cim_ref[...]
    cd_ref[:, :, lo] = c_re
    cd_ref[:, :, hi] = -c_im

    zeros = jnp.zeros((half, S5_GROUP, p), F32)
    pr = jnp.ones_like(lbr)
    pi = jnp.zeros_like(lbr)
    for d in range(CHUNK):
        rows = slice(d * S5_GROUP, (d + 1) * S5_GROUP)
        back = slice((CHUNK - 1 - d) * S5_GROUP, (CHUNK - d) * S5_GROUP)
        bp_ref[:, back, lo] = bbr * pr - bbi * pi
        bp_ref[:, back, hi] = bbr * pi + bbi * pr
        pr, pi = pr * lbr - pi * lbi, pr * lbi + pi * lbr
        cp_r = c_re * pr - c_im * pi
        cp_i = -(c_re * pi + c_im * pr)
        cpre_ref[0:half, rows, lo] = cp_r[0:half]
        cpre_ref[0:half, rows, hi] = zeros
        cpre_ref[half:gb, rows, lo] = zeros
        cpre_ref[half:gb, rows, hi] = cp_r[half:gb]
        cpim_ref[0:half, rows, lo] = cp_i[0:half]
        cpim_ref[0:half, rows, hi] = zeros
        cpim_ref[half:gb, rows, lo] = zeros
        cpim_ref[half:gb, rows, hi] = cp_i[half:gb]

    qr, qi = pr, pi
    for j in range(N_DOUBLINGS):
        ar_ref[:, j:j + 1, lo] = qr[0:half]
        ar_ref[:, j:j + 1, hi] = qr[half:gb]
        ai_ref[:, j:j + 1, lo] = qi[0:half]
        ai_ref[:, j:j + 1, hi] = qi[half:gb]
        qr, qi = qr * qr - qi * qi, 2.0 * qr * qi
    ar_ref[:, N_DOUBLINGS:, :] = jnp.zeros((half, 8 - N_DOUBLINGS, 2 * p), F32)
    ai_ref[:, N_DOUBLINGS:, :] = jnp.zeros((half, 8 - N_DOUBLINGS, 2 * p), F32)

    lane = lax.broadcasted_iota(jnp.int32, (S5_GROUP, 128), 1)
    for j in range(gb):
        w = _dot3(cd_ref[j], bp_ref[j], NT)
        w0, w1 = w[:, :128], w[:, 128:]
        for t in range(CHUNK):
            rows = slice(t * S5_GROUP, (t + 1) * S5_GROUP)
            shift = (CHUNK - 1 - t) * S5_GROUP
            keep = 128 - shift % 128
            if shift == 0:
                left, right = w0, w1
            elif shift < 128:
                r0 = pltpu.roll(w0, keep, axis=1)
                r1 = pltpu.roll(w1, keep, axis=1)
                left = jnp.where(lane < keep, r0, r1)
                right = jnp.where(lane < keep, r1, 0.0)
            elif shift == 128:
                left, right = w1, jnp.zeros_like(w1)
            else:
                left = jnp.where(lane < keep, pltpu.roll(w1, keep, axis=1), 0.0)
                right = jnp.zeros_like(w1)
            l_ref[j, rows, 0:128] = left
            l_ref[j, rows, 128:256] = right
        l_ref[j, CW:L_ROWS, :] = bp_ref[j].T


def _s5_prep(lam_re, lam_im, log_dt, b_re, b_im, c_re, c_im):
    g, p, gb, half = S5_GROUPS, S5_STATE, GROUPS_PER_TILE, PAIRS_PER_TILE
    lre = lam_re.reshape(g, 1, p)
    lim = lam_im.reshape(g, 1, p)
    ldt = jnp.broadcast_to(log_dt.reshape(g, 1, 1), (g, 1, p))
    bre = jnp.transpose(b_re, (0, 2, 1))
    bim = jnp.transpose(b_im, (0, 2, 1))
    sd = jax.ShapeDtypeStruct
    blk = lambda n, r, c: pl.BlockSpec((n, r, c), lambda i: (i, 0, 0))
    return pl.pallas_call(
        _s5_prep_body,
        grid=(g // gb,),
        in_specs=[blk(gb, 1, p)] * 3 + [blk(gb, S5_GROUP, p)] * 4,
        out_specs=(
            blk(gb, L_ROWS, CW), blk(gb, CW, 2 * p), blk(gb, CW, 2 * p),
            blk(gb, S5_GROUP, 2 * p), blk(gb, S5_GROUP, 2 * p),
            blk(half, 8, 2 * p), blk(half, 8, 2 * p), blk(gb, 1, 2 * p), blk(gb, 1, 2 * p),
        ),
        out_shape=(
            sd((g, L_ROWS, CW), F32),
            sd((g, CW, 2 * p), F32),
            sd((g, CW, 2 * p), F32),
            sd((g, S5_GROUP, 2 * p), F32),
            sd((g, S5_GROUP, 2 * p), F32),
            sd((g // 2, 8, 2 * p), F32),
            sd((g // 2, 8, 2 * p), F32),
            sd((g, 1, 2 * p), F32),
            sd((g, 1, 2 * p), F32),
        ),
        scratch_shapes=[pltpu.VMEM((gb, CW, 2 * p), F32)],
        compiler_params=pltpu.CompilerParams(dimension_semantics=("parallel",)),
        name="s5_prep",
    )(lre, lim, ldt, bre, bim, c_re, c_im)


def _cmul_add(h, hs, ar, ai):
    return h + hs * ar + pltpu.roll(hs, S5_STATE, axis=1) * ai


def _s5p_body(u_ref, l_ref, cpre_ref, cpim_ref, ar_ref, ai_ref, y_ref, hre_ref, him_ref,
              ut_ref, yt_ref, *, nb, nk):
    gb, half = GROUPS_PER_TILE, PAIRS_PER_TILE
    p = S5_STATE
    nrow = nb * nk
    d = functools.partial(jnp.dot, preferred_element_type=F32)

    for t in range(CHUNK):
        xt = u_ref[pl.ds(t, nrow, stride=CHUNK), :].T
        for j in range(gb):
            ut_ref[j, t * S5_GROUP:(t + 1) * S5_GROUP, :] = xt[j * S5_GROUP:(j + 1) * S5_GROUP, :]

    s_re, s_im = [], []
    for j in range(gb):
        uh, ul = _split(ut_ref[j])
        lh, ll = _split(l_ref[j])
        r = d(lh, uh) + d(lh, ul) + d(ll, uh)
        yt_ref[j] = r[0:CW]
        s_re.append(r[CW:CW + p])
        s_im.append(r[CW + p:L_ROWS])

    rows = lax.broadcasted_iota(jnp.int32, (nrow, 2 * p), 0) & (nk - 1)
    for q in range(half):
        re = jnp.concatenate([s_re[q], s_re[q + half]], axis=0).T
        im = jnp.concatenate([s_im[q], s_im[q + half]], axis=0).T
        for k in range(N_DOUBLINGS):
            sh = 1 << k
            keep = rows >= sh
            rs = jnp.where(keep, pltpu.roll(re, sh, axis=0), 0.0)
            js = jnp.where(keep, pltpu.roll(im, sh, axis=0), 0.0)
            ar = ar_ref[q, k:k + 1, :]
            ai = ai_ref[q, k:k + 1, :]
            re, im = re + ar * rs - ai * js, im + ar * js + ai * rs
        for b in range(nb):
            hre_ref[q, b:b + 1, :] = re[(b + 1) * nk - 1:(b + 1) * nk, :]
            him_ref[q, b:b + 1, :] = im[(b + 1) * nk - 1:(b + 1) * nk, :]
        pre = jnp.where(rows >= 1, pltpu.roll(re, 1, axis=0), 0.0)
        pim = jnp.where(rows >= 1, pltpu.roll(im, 1, axis=0), 0.0)
        for j in (q, q + half):
            yt_ref[j] = yt_ref[j] + _dot3(cpre_ref[j], pre, NT) + _dot3(cpim_ref[j], pim, NT)

    for t in range(CHUNK):
        yt = jnp.concatenate(
            [yt_ref[j, t * S5_GROUP:(t + 1) * S5_GROUP, :] for j in range(gb)], axis=0)
        y_ref[pl.ds(t, nrow, stride=CHUNK), :] = yt.T


def _s5_prompt(proj_p, lmat, cpre, cpim, ar, ai, *, nb, seq):
    g, gb, half = S5_GROUPS, GROUPS_PER_TILE, PAIRS_PER_TILE
    n_p = nb * seq
    nk = seq // CHUNK
    blk = lambda n, r, c: pl.BlockSpec((n, r, c), lambda i: (i, 0, 0))
    return pl.pallas_call(
        functools.partial(_s5p_body, nb=nb, nk=nk),
        grid=(g // gb,),
        in_specs=[pl.BlockSpec((n_p, 128), lambda i: (0, i)),
                  blk(gb, L_ROWS, CW), blk(gb, CW, 2 * S5_STATE), blk(gb, CW, 2 * S5_STATE),
                  blk(half, 8, 2 * S5_STATE), blk(half, 8, 2 * S5_STATE)],
        out_specs=(pl.BlockSpec((n_p, 128), lambda i: (0, i)),
                   blk(half, nb, 2 * S5_STATE), blk(half, nb, 2 * S5_STATE)),
        out_shape=(jax.ShapeDtypeStruct((n_p, D_S5), F32),
                   jax.ShapeDtypeStruct((g // 2, nb, 2 * S5_STATE), F32),
                   jax.ShapeDtypeStruct((g // 2, nb, 2 * S5_STATE), F32)),
        scratch_shapes=[pltpu.VMEM((gb, CW, nb * nk), F32), pltpu.VMEM((gb, CW, nb * nk), F32)],
        compiler_params=pltpu.CompilerParams(
            dimension_semantics=("parallel",), vmem_limit_bytes=VMEM_LIMIT),
        name="s5_prompt",
    )(proj_p, lmat, cpre, cpim, ar, ai)


def _s5d_body(u_ref, h0_ref, bd_ref, cd_ref, lr_ref, li_ref, y_ref, h_ref, *, gb):
    for j in range(gb):
        h0 = h0_ref[j]
        bu = _dot3(u_ref[j], bd_ref[j])
        h = _cmul_add(bu, h0, lr_ref[j], li_ref[j])
        h_ref[j] = h
        y_ref[j] = _dot3(h, cd_ref[j], NT)


def _s5_decode(ud, h0, bd, cd, lr, li, *, gb=8):
    g, nbatch = S5_GROUPS, ud.shape[1]
    blk = lambda r, c: pl.BlockSpec((gb, r, c), lambda i: (i, 0, 0))
    return pl.pallas_call(
        functools.partial(_s5d_body, gb=gb),
        grid=(g // gb,),
        in_specs=[blk(nbatch, S5_GROUP), blk(nbatch, 2 * S5_STATE), blk(S5_GROUP, 2 * S5_STATE),
                  blk(S5_GROUP, 2 * S5_STATE), blk(1, 2 * S5_STATE), blk(1, 2 * S5_STATE)],
        out_specs=(blk(nbatch, S5_GROUP), blk(nbatch, 2 * S5_STATE)),
        out_shape=(jax.ShapeDtypeStruct((g, nbatch, S5_GROUP), F32),
                   jax.ShapeDtypeStruct((g, nbatch, 2 * S5_STATE), F32)),
        compiler_params=pltpu.CompilerParams(dimension_semantics=("parallel",)),
        name="s5_decode",
    )(ud, h0, bd, cd, lr, li)


def _lru_gates(xc, wa_ref, wx_ref, ba, bx, lam):
    xcb = xc.astype(BF16)
    nblk = D_LRU // 256
    r_parts, i_parts = [], []
    for k in range(nblk):
        xk = xcb[:, k * 256:(k + 1) * 256]
        r_parts.append(jnp.dot(xk, wa_ref[k], preferred_element_type=F32))
        i_parts.append(jnp.dot(xk, wx_ref[k], preferred_element_type=F32))
    r = jax.nn.sigmoid(jnp.concatenate(r_parts, axis=1) + ba)
    i = jax.nn.sigmoid(jnp.concatenate(i_parts, axis=1) + bx)
    z = -lam
    softplus = jnp.maximum(z, 0.0) + jnp.log1p(jnp.exp(-jnp.abs(z)))
    log_a = (-LRU_C * softplus) * r
    a = jnp.exp(log_a)
    v = -jnp.tanh(log_a) * (a * a + 1.0)
    mult = jnp.where(v > 0.0, v * lax.rsqrt(v), 0.0)
    return a, mult * (i * xc)


def _lru_tile(xl_ref, gate_ref, cw_ref, cb_ref, wa_ref, wx_ref, ba_ref, bx_ref, lam_ref,
              o_ref, xbuf_ref, carry_ref):
    tt = xl_ref.shape[0]
    x = xl_ref[...]
    xbuf_ref[8:8 + tt, :] = x
    cw = cw_ref[...]
    xc = (cb_ref[...] + xbuf_ref[5:5 + tt, :] * cw[0:1] + xbuf_ref[6:6 + tt, :] * cw[1:2]
          + xbuf_ref[7:7 + tt, :] * cw[2:3] + x * cw[3:4])
    xbuf_ref[0:8, :] = x[tt - 8:tt, :]

    a, b = _lru_gates(xc, wa_ref, wx_ref, ba_ref[...], bx_ref[...], lam_ref[...])

    nblk = tt // 8
    a3 = a.reshape(nblk, 8, D_LRU)
    b3 = b.reshape(nblk, 8, D_LRU)
    row = lax.broadcasted_iota(jnp.int32, (nblk, 8, D_LRU), 1)
    for sh in (1, 2, 4):
        keep = row >= sh
        bs = jnp.where(keep, pltpu.roll(b3, sh, axis=1), 0.0)
        sa = jnp.where(keep, pltpu.roll(a3, sh, axis=1), 1.0)
        b3 = b3 + a3 * bs
        a3 = a3 * sa
    carry = carry_ref[0:1, :]
    gate = gate_ref[...]
    for k in range(nblk):
        h = b3[k] + a3[k] * carry
        carry = h[7:8, :]
        o_ref[k * 8:(k + 1) * 8, :] = h * gate[k * 8:(k + 1) * 8, :]
    carry_ref[...] = jnp.broadcast_to(carry, (8, D_LRU))
    return carry


def _mix_s5_part(ys, u, x, dsk_ref, wg_ref, bg_ref, gs_ref, wo_ref):
    yy = ys + dsk_ref[...] * u
    g = _gelu(yy)
    z = jnp.dot(g.astype(BF16), wg_ref[...], preferred_element_type=F32) + bg_ref[...]
    s5o = g * jax.nn.sigmoid(z)
    n1 = _rms(s5o, gs_ref[...]).astype(BF16)
    return x + jnp.dot(n1, wo_ref[0:D_S5, :], preferred_element_type=F32)


def _mix_lru_part(lru, gl_ref, wo_ref):
    n2 = _rms(lru, gl_ref[...]).astype(BF16)
    return jnp.dot(n2, wo_ref[D_S5:, :], preferred_element_type=F32)


def _lru_mix_body(xl_ref, gate_ref, ys_ref, u_ref, x_ref,
                  cw_ref, cb_ref, wa_ref, wx_ref, ba_ref, bx_ref, lam_ref,
                  dsk_ref, wg_ref, bg_ref, gs_ref, gl_ref, wo_ref,
                  o_ref, hl_ref, xbuf_ref, carry_ref, lru_ref):
    @pl.when(pl.program_id(1) == 0)
    def _():
        xbuf_ref[0:8, :] = jnp.zeros((8, D_LRU), F32)
        carry_ref[...] = jnp.zeros((8, D_LRU), F32)

    o_ref[...] = _mix_s5_part(ys_ref[...], u_ref[...], x_ref[...],
                              dsk_ref, wg_ref, bg_ref, gs_ref, wo_ref)
    hl_ref[0] = _lru_tile(xl_ref, gate_ref, cw_ref, cb_ref, wa_ref, wx_ref, ba_ref, bx_ref, lam_ref,
                          lru_ref, xbuf_ref, carry_ref)
    o_ref[...] += _mix_lru_part(lru_ref[...], gl_ref, wo_ref)


def _lru_mix_prompt(proj_p, ys_p, x1_p, cw, cb, wa, wx, ba, bx, lam, dsk, wg, bg, gs, gl, wo,
                    *, nb, seq, tt=512):
    nt = seq // tt
    rows = lambda c, col: pl.BlockSpec((tt, c), lambda b, t: (b * nt + t, col))
    once = lambda shape: pl.BlockSpec(shape, lambda b, t: (0,) * len(shape),
                                      pipeline_mode=pl.Buffered(1))
    return pl.pallas_call(
        _lru_mix_body,
        grid=(nb, nt),
        in_specs=[
            rows(D_LRU, 1), rows(D_LRU, 2),
            rows(D_S5, 0), rows(D_S5, 0), rows(D_MODEL, 0),
            once((CONV_W, D_LRU)), once((1, D_LRU)),
            once((D_LRU // 256, 256, 256)), once((D_LRU // 256, 256, 256)),
            once((1, D_LRU)), once((1, D_LRU)), once((1, D_LRU)),
            once((1, D_S5)), once((D_S5, D_S5)), once((1, D_S5)), once((1, D_S5)), once((1, D_LRU)),
            once((D_MODEL, D_MODEL)),
        ],
        out_specs=(pl.BlockSpec((tt, D_MODEL), lambda b, t: (b * nt + t, 0)),
                   pl.BlockSpec((1, 1, D_LRU), lambda b, t: (b, 0, 0))),
        out_shape=(jax.ShapeDtypeStruct((nb * seq, D_MODEL), F32),
                   jax.ShapeDtypeStruct((nb, 1, D_LRU), F32)),
        scratch_shapes=[pltpu.VMEM((tt + 8, D_LRU), F32), pltpu.VMEM((8, D_LRU), F32),
                        pltpu.VMEM((tt, D_LRU), F32)],
        compiler_params=pltpu.CompilerParams(
            dimension_semantics=("parallel", "arbitrary"), vmem_limit_bytes=VMEM_LIMIT),
        name="lru_mix_prompt",
    )(proj_p, proj_p, ys_p, proj_p, x1_p, cw, cb, wa, wx, ba, bx, lam, dsk, wg, bg, gs, gl, wo)


def _lru_decode_body(xl_ref, gate_ref, c0_ref, c1_ref, c2_ref, h0_ref, cw_ref, cb_ref,
                     wa_ref, wx_ref, ba_ref, bx_ref, lam_ref, o_ref, h_ref):
    x = xl_ref[...]
    cw = cw_ref[...]
    xc = (cb_ref[...] + c0_ref[...] * cw[0:1] + c1_ref[...] * cw[1:2]
          + c2_ref[...] * cw[2:3] + x * cw[3:4])
    a, b = _lru_gates(xc, wa_ref, wx_ref, ba_ref[...], bx_ref[...], lam_ref[...])
    h = a * h0_ref[...] + b
    h_ref[...] = h
    o_ref[...] = h * gate_ref[...]


def _lru_decode(proj_d, c0, c1, c2, h0, cw, cb, wa, wx, ba, bx, lam):
    nd = proj_d.shape[0]
    full = lambda r: pl.BlockSpec((r, D_LRU), lambda i: (0, 0))
    wspec = pl.BlockSpec((D_LRU // 256, 256, 256), lambda i: (0, 0, 0))
    return pl.pallas_call(
        _lru_decode_body,
        grid=(1,),
        in_specs=[
            pl.BlockSpec((nd, D_LRU), lambda i: (0, 1)),
            pl.BlockSpec((nd, D_LRU), lambda i: (0, 2)),
            full(nd), full(nd), full(nd), full(nd),
            full(CONV_W), full(1), wspec, wspec, full(1), full(1), full(1),
        ],
        out_specs=(full(nd), full(nd)),
        out_shape=(jax.ShapeDtypeStruct((nd, D_LRU), F32),
                   jax.ShapeDtypeStruct((nd, D_LRU), F32)),
        name="lru_decode",
    )(proj_d, proj_d, c0, c1, c2, h0, cw, cb, wa, wx, ba, bx, lam)


def _mix_decode_body(ys_ref, u_ref, lru_ref, x_ref, dsk_ref, wg_ref, bg_ref, gs_ref, gl_ref,
                     wo_ref, o_ref):
    o_ref[...] = (_mix_s5_part(ys_ref[...], u_ref[...], x_ref[...],
                               dsk_ref, wg_ref, bg_ref, gs_ref, wo_ref)
                  + _mix_lru_part(lru_ref[...], gl_ref, wo_ref))


def _mix_decode(ys_d, proj_d, lru_d, x1_d, dsk, wg, bg, gs, gl, wo):
    nd = x1_d.shape[0]
    full = lambda r, c: pl.BlockSpec((r, c), lambda i: (0, 0))
    return pl.pallas_call(
        _mix_decode_body,
        grid=(1,),
        in_specs=[full(nd, D_S5), full(nd, D_S5), full(nd, D_LRU), full(nd, D_MODEL),
                  full(1, D_S5), full(D_S5, D_S5), full(1, D_S5), full(1, D_S5), full(1, D_LRU),
                  full(D_MODEL, D_MODEL)],
        out_specs=full(nd, D_MODEL),
        out_shape=jax.ShapeDtypeStruct((nd, D_MODEL), F32),
        compiler_params=pltpu.CompilerParams(vmem_limit_bytes=VMEM_LIMIT),
        name="mix_decode",
    )(ys_d, proj_d, lru_d, x1_d, dsk, wg, bg, gs, gl, wo)


def _unpair(h, nb):
    tiles = S5_GROUPS // GROUPS_PER_TILE
    h5 = h.reshape(tiles, PAIRS_PER_TILE, nb, 2, S5_STATE)
    return jnp.transpose(h5, (2, 0, 3, 1, 4)).reshape(nb, S5_GROUPS, S5_STATE)


def _block_diag4(w):
    w4 = w.reshape(LRU_HEADS // 4, 4, LRU_HEAD_DIM, LRU_HEAD_DIM)
    eye = jnp.eye(4, dtype=w.dtype)
    return jnp.einsum("kaij,ab->kaibj", w4, eye).reshape(LRU_HEADS // 4, 256, 256)


def kernel(x_prompt, x_sample, state_s5_re, state_s5_im, state_lru_h, state_lru_conv, g_ffn1, w1_a, w3_a, w2_a, g_mix, w_in, lam_re, lam_im, log_dt, b_re, b_im, c_re, c_im, d_skip, w_glu, b_glu, conv_w, conv_b, w_a, b_a, w_x, b_x, lam_l, g_out_s5, g_out_lru, w_out, g_ffn2, w1_b, w3_b, w2_b, g_final):
    nb, seq, _ = x_prompt.shape
    nd = x_sample.shape[0]
    n_p = nb * seq
    nk = seq // CHUNK
    g, p = S5_GROUPS, S5_STATE
    row = lambda v: v.reshape(1, -1)

    xp = x_prompt.reshape(n_p, D_MODEL)
    xd = x_sample.reshape(nd, D_MODEL)

    n_i, n_f = n_p // 1024, D_FF // 512
    first = _ffn(
        xp, xd, row(g_ffn1[0]), w1_a[0], w3_a[0], w2_a[0], n_tiles=1,
        casts=(_cast_job(w1_b[0], n_i, n_f, f_div=2), _cast_job(w3_b[0], n_i, n_f, f_div=2),
               _row_cast_job(w2_b[0], n_i, n_f, f_div=2), _flat_cast_job(w_in[0], 16)))
    w1_a16, w3_a16, w2_a16 = first[2:5]
    w_in16 = first[8]
    x1_p, x1_d, w1_b16, w3_b16, w2_b16 = _ffn(
        xp, xd, row(g_ffn1[0]), w1_a16, w3_a16, w2_a16, first_tile=1, n_tiles=n_i - 1,
        casts=(_cast_job(w1_b[0], n_i, n_f, i0=1), _cast_job(w3_b[0], n_i, n_f, i0=1),
               _row_cast_job(w2_b[0], n_i, n_f, i0=1)),
        prev=first[:2] + first[5:8])
    proj_p, proj_d, w_out16, w_glu16 = _inproj(
        x1_p, x1_d, row(g_mix[0]), w_in16,
        casts=(_cast_job(w_out[0], n_i, D_IN // 1024, bc=1024),
               _cast_job(w_glu[0], n_i, D_IN // 1024, bc=1024)))

    lmat, cpre, cpim, bd, cd, ar, ai, lr, li = _s5_prep(
        lam_re[0], lam_im[0], log_dt[0], b_re[0], b_im[0], c_re[0], c_im[0])
    ys_p, hf_re, hf_im = _s5_prompt(proj_p, lmat, cpre, cpim, ar, ai, nb=nb, seq=seq)

    ud = jnp.transpose(proj_d[:, :D_S5].reshape(nd, g, S5_GROUP), (1, 0, 2))
    h0d = jnp.concatenate([jnp.transpose(state_s5_re[0], (1, 0, 2)),
                           jnp.transpose(state_s5_im[0], (1, 0, 2))], axis=-1)
    ydg, hd = _s5_decode(ud, h0d, bd, cd, lr, li)
    ys_d = jnp.transpose(ydg, (1, 0, 2)).reshape(nd, D_S5)

    wa_bd = _block_diag4(w_a[0]).astype(BF16)
    wx_bd = _block_diag4(w_x[0]).astype(BF16)
    lru_args = (conv_w[0], row(conv_b[0]), wa_bd, wx_bd, row(b_a[0]), row(b_x[0]), row(lam_l[0]))
    conv0 = state_lru_conv[0]
    lru_d, hl_d = _lru_decode(proj_d, conv0[:, 0], conv0[:, 1], conv0[:, 2], state_lru_h[0],
                              *lru_args)

    mix_args = (row(d_skip[0]), w_glu16, row(b_glu[0]), row(g_out_s5[0]), row(g_out_lru[0]), w_out16)
    x2_p, hl_p = _lru_mix_prompt(proj_p, ys_p, x1_p, *lru_args, *mix_args, nb=nb, seq=seq)
    x2_d = _mix_decode(ys_d, proj_d, lru_d, x1_d, *mix_args)
    y_p, y_d = _ffn(x2_p, x2_d, row(g_ffn2[0]), w1_b16, w3_b16, w2_b16, row(g_final))

    tail_p = proj_p.reshape(nb, seq, -1)[:, seq - (CONV_W - 1):, D_S5:D_S5 + D_LRU]
    xl_d = proj_d[:, D_S5:D_S5 + D_LRU]
    return (
        y_p.reshape(nb, seq, D_MODEL),
        y_d.reshape(nd, 1, D_MODEL),
        _unpair(hf_re, nb)[None],
        _unpair(hf_im, nb)[None],
        hl_p.reshape(1, nb, D_LRU),
        tail_p[None],
        jnp.transpose(hd[:, :, :p], (1, 0, 2))[None],
        jnp.transpose(hd[:, :, p:], (1, 0, 2))[None],
        hl_d[None],
        jnp.stack([conv0[:, 1], conv0[:, 2], xl_d], axis=1)[None],
    )
```

```python
import functools

import jax
import jax.numpy as jnp
from jax import lax
from jax.experimental import pallas as pl
from jax.experimental.pallas import tpu as pltpu

F32 = jnp.float32
BF16 = jnp.bfloat16

D_MODEL = 2048
D_S5 = 1024
S5_GROUP = 16
S5_GROUPS = 64
S5_STATE = 64
D_LRU = 1024
LRU_HEADS = 16
LRU_HEAD_DIM = 64
CONV_W = 4
LRU_C = 8.0
D_FF = 5632
D_IN = D_S5 + 2 * D_LRU
EPS = 1e-6

CHUNK = 16
CW = CHUNK * S5_GROUP

VMEM_LIMIT = 58 * 1024 * 1024
FFN_VMEM_LIMIT = 60 * 1024 * 1024

NN = (((1,), (0,)), ((), ()))
NT = (((1,), (1,)), ((), ()))


def _rms(x, g):
    return x * lax.rsqrt(jnp.mean(x * x, axis=-1, keepdims=True) + EPS) * g


def _split(x):
    hi = x.astype(BF16)
    lo = (x - hi.astype(F32)).astype(BF16)
    return hi, lo


def _dot3(a, b, dims=NN):
    ah, al = _split(a)
    bh, bl = _split(b)
    d = functools.partial(lax.dot_general, dimension_numbers=dims, preferred_element_type=F32)
    return d(ah, bh) + d(al, bh) + d(ah, bl)


def _gelu(x):
    return jax.nn.gelu(x, approximate=True)


def _ffn_body(*refs, final_norm, convert, n_casts, n_prev):
    n_in = 7 if final_norm else 6
    xp_ref, xd_ref, g_ref, w1_ref, w3_ref, w2_ref = refs[:6]
    gf_ref = refs[6] if final_norm else None
    cast_in = refs[n_in:n_in + n_casts]
    outs = refs[n_in + n_casts + n_prev:-1]
    op_ref, od_ref = outs[:2]
    wcopy = outs[2:5] if convert else ()
    cast_out = outs[2 + len(wcopy):]
    xn_ref = refs[-1]
    f = pl.program_id(1)
    tm = xp_ref.shape[0]

    for src_ref, dst_ref in zip(cast_in, cast_out):
        dst_ref[...] = src_ref[...].astype(BF16)

    @pl.when(f == 0)
    def _():
        for x_ref, o_ref, rows in ((xp_ref, op_ref, slice(0, tm)), (xd_ref, od_ref, slice(tm, None))):
            x = x_ref[...]
            xn_ref[rows, :] = _rms(x, g_ref[...]).astype(BF16)
            o_ref[...] = x

    if convert:
        w1, w3, w2 = (w_ref[...].astype(BF16) for w_ref in (w1_ref, w3_ref, w2_ref))
        for dst_ref, w in zip(wcopy, (w1, w3, w2)):
            dst_ref[...] = w
    else:
        w1, w3, w2 = w1_ref[...], w3_ref[...], w2_ref[...]

    xn = xn_ref[...]
    a = jnp.dot(xn, w1, preferred_element_type=F32)
    b = jnp.dot(xn, w3, preferred_element_type=F32)
    h = (a * jax.nn.sigmoid(a) * b).astype(BF16)
    upd = 0.5 * jnp.dot(h, w2, preferred_element_type=F32)
    op_ref[...] += upd[0:tm]
    od_ref[...] += upd[tm:]

    if final_norm:
        @pl.when(f == pl.num_programs(1) - 1)
        def _():
            op_ref[...] = _rms(op_ref[...], gf_ref[...])
            od_ref[...] = _rms(od_ref[...], gf_ref[...])


def _cast_job(w, n_i, n_f, bc=512, i0=0, f_div=1):
    rows, cols = w.shape
    br = rows // n_i
    n_cb = cols // bc
    assert br * n_i == rows and bc * n_cb == cols and n_cb <= n_f and br % 16 == 0
    return w, pl.BlockSpec(
        (br, bc), lambda i, f: (jnp.minimum(i + i0, n_i - 1),
                                jnp.where(i + i0 < n_i, jnp.minimum(f // f_div, n_cb - 1),
                                          n_cb - 1)))


def _row_cast_job(w, n_i, n_f, i0=0, f_div=1):
    rows, cols = w.shape
    br = rows // (n_i * n_f)
    assert br * n_i * n_f == rows and br % 16 == 0
    return w, pl.BlockSpec((br, cols), lambda i, f: ((i + i0) * n_f + f // f_div, 0))


def _flat_cast_job(w, n_blocks):
    rows, cols = w.shape
    br = rows // n_blocks
    assert br * n_blocks == rows and br % 16 == 0
    return w, pl.BlockSpec((br, cols), lambda i, f: (jnp.minimum(f, n_blocks - 1), 0))


def _ffn(xp, xd, g, w1, w3, w2, g_final=None, *, casts=(), first_tile=0, n_tiles=None, prev=(),
         tm=1024):
    n_p, nd = xp.shape[0], xd.shape[0]
    all_tiles = n_p // tm
    n_tiles = all_tiles if n_tiles is None else n_tiles
    td = nd // all_tiles
    final_norm = g_final is not None
    convert = w1.dtype == F32
    tf = 256 if convert else 512
    pspec = pl.BlockSpec((tm, D_MODEL), lambda i, f: (i + first_tile, 0))
    dspec = pl.BlockSpec((td, D_MODEL), lambda i, f: (i + first_tile, 0))
    xspec = pspec if n_tiles > 1 else pl.BlockSpec(
        (tm, D_MODEL), lambda i, f: (i + first_tile, 0), pipeline_mode=pl.Buffered(1))
    w13spec = pl.BlockSpec((D_MODEL, tf), lambda i, f: (0, f))
    w2spec = pl.BlockSpec((tf, D_MODEL), lambda i, f: (f, 0))
    in_specs = [xspec, dspec, pl.BlockSpec((1, D_MODEL), lambda i, f: (0, 0)),
                w13spec, w13spec, w2spec]
    args = [xp, xd, g, w1, w3, w2]
    if final_norm:
        in_specs.append(pl.BlockSpec((1, D_MODEL), lambda i, f: (0, 0)))
        args.append(g_final)
    in_specs += [spec for _, spec in casts]
    args += [w for w, _ in casts]
    n_in = len(args)
    in_specs += [pl.BlockSpec(memory_space=pl.ANY)] * len(prev)
    args += list(prev)
    sd = jax.ShapeDtypeStruct
    out_specs = [pspec, dspec]
    out_shape = [sd((n_p, D_MODEL), F32), sd((nd, D_MODEL), F32)]
    if convert:
        out_specs += [w13spec, w13spec, w2spec]
        out_shape += [sd(w.shape, BF16) for w in (w1, w3, w2)]
    out_specs += [spec for _, spec in casts]
    out_shape += [sd(w.shape, BF16) for w, _ in casts]
    assert len(prev) in (0, len(out_shape))
    return pl.pallas_call(
        functools.partial(_ffn_body, final_norm=final_norm, convert=convert,
                          n_casts=len(casts), n_prev=len(prev)),
        grid=(n_tiles, D_FF // tf),
        in_specs=in_specs,
        out_specs=tuple(out_specs),
        out_shape=tuple(out_shape),
        input_output_aliases={n_in + k: k for k in range(len(prev))},
        scratch_shapes=[pltpu.VMEM((tm + td, D_MODEL), BF16)],
        compiler_params=pltpu.CompilerParams(
            dimension_semantics=("arbitrary", "arbitrary"), vmem_limit_bytes=FFN_VMEM_LIMIT),
        name="ffn_final" if final_norm else ("ffn_first" if convert else "ffn"),
    )(*args)


def _inproj_body(*refs, n_tiles, nd, n_casts):
    xp_ref, xd_ref, g_ref, w_ref = refs[:4]
    cast_in = refs[4:4 + n_casts]
    op_ref, od_ref = refs[4 + n_casts:6 + n_casts]
    cast_out = refs[6 + n_casts:6 + 2 * n_casts]
    xn_ref = refs[-1]
    i = pl.program_id(0)
    j = pl.program_id(1)
    nj = pl.num_programs(1)

    for src_ref, dst_ref in zip(cast_in, cast_out):
        dst_ref[...] = src_ref[...].astype(BF16)

    def run(x_ref, o_ref, rows):
        @pl.when(j == 0)
        def _():
            xn_ref[0:rows, :] = _rms(x_ref[...], g_ref[...]).astype(BF16)

        @pl.when(j < nj - 1)
        def _():
            o_ref[...] = jnp.dot(xn_ref[0:rows, :], w_ref[...], preferred_element_type=F32)

        @pl.when(j == nj - 1)
        def _():
            o_ref[...] = _gelu(jnp.dot(xn_ref[0:rows, :], w_ref[...],
                                       preferred_element_type=F32))

    @pl.when(i < n_tiles)
    def _():
        run(xp_ref, op_ref, xp_ref.shape[0])

    @pl.when(i == n_tiles)
    def _():
        run(xd_ref, od_ref, nd)


def _inproj(xp, xd, g, w, *, casts=(), tm=1024, tn=1024):
    n_p, nd = xp.shape[0], xd.shape[0]
    n_tiles = n_p // tm
    d_out = w.shape[1]
    nj = d_out // tn
    return pl.pallas_call(
        functools.partial(_inproj_body, n_tiles=n_tiles, nd=nd, n_casts=len(casts)),
        grid=(n_tiles + 1, nj),
        in_specs=[
            pl.BlockSpec((tm, D_MODEL), lambda i, j: (jnp.minimum(i, n_tiles - 1), 0)),
            pl.BlockSpec((nd, D_MODEL), lambda i, j: (0, 0)),
            pl.BlockSpec((1, D_MODEL), lambda i, j: (0, 0)),
            pl.BlockSpec((D_MODEL, tn), lambda i, j: (0, j)),
        ] + [spec for _, spec in casts],
        out_specs=(
            pl.BlockSpec((tm, tn), lambda i, j: (jnp.minimum(i, n_tiles - 1),
                                                 jnp.where(i < n_tiles, j, nj - 1))),
            pl.BlockSpec((nd, tn), lambda i, j: (0, jnp.where(i < n_tiles, 0, j))),
        ) + tuple(spec for _, spec in casts),
        out_shape=(jax.ShapeDtypeStruct((n_p, d_out), F32),
                   jax.ShapeDtypeStruct((nd, d_out), F32))
        + tuple(jax.ShapeDtypeStruct(cw.shape, BF16) for cw, _ in casts),
        scratch_shapes=[pltpu.VMEM((tm, D_MODEL), BF16)],
        compiler_params=pltpu.CompilerParams(
            dimension_semantics=("arbitrary", "arbitrary"), vmem_limit_bytes=VMEM_LIMIT),
        name="inproj",
    )(xp, xd, g, w, *[cw for cw, _ in casts])


GROUPS_PER_TILE = 128 // S5_GROUP
PAIRS_PER_TILE = GROUPS_PER_TILE // 2
L_ROWS = CW + 2 * S5_STATE


def _s5_prep_body(lre_ref, lim_ref, ldt_ref, bre_ref, bim_ref, cre_ref, cim_ref,
                  l_ref, cpre_ref, cpim_ref, bd_ref, cd_ref, ar_ref, ai_ref, lr_ref, li_ref,
                  bp_ref):
    gb, half = GROUPS_PER_TILE, PAIRS_PER_TILE
    p = S5_STATE
    lo, hi = slice(0, p), slice(p, 2 * p)
    lam_re = lre_ref[...]
    lam_im = lim_ref[...]
    dt = jnp.exp(ldt_ref[...])
    mag = jnp.exp(lam_re * dt)
    ang = lam_im * dt
    lbr = mag * jnp.cos(ang)
    lbi = mag * jnp.sin(ang)
    lr_ref[:, :, lo] = lbr
    lr_ref[:, :, hi] = lbr
    li_ref[:, :, lo] = -lbi
    li_ref[:, :, hi] = lbi
    nr = lbr - 1.0
    den = lam_re * lam_re + lam_im * lam_im
    cr = (nr * lam_re + lbi * lam_im) / den
    ci = (lbi * lam_re - nr * lam_im) / den
    b_re = bre_ref[...]
    b_im = bim_ref[...]
    bbr = cr * b_re - ci * b_im
    bbi = cr * b_im + ci * b_re
    bd_ref[:, :, lo] = bbr
    bd_ref[:, :, hi] = bbi
    c_re = cre_ref[...]
    c_im = cim_ref[...]
    cd_ref[:, :, lo] = c_re
    cd_ref[:, :, hi] = -c_im

    zeros = jnp.zeros((half, S5_GROUP, p), F32)
    pr = jnp.ones_like(lbr)
    pi = jnp.zeros_like(lbr)
    for d in range(CHUNK):
        rows = slice(d * S5_GROUP, (d + 1) * S5_GROUP)
        back = slice((CHUNK - 1 - d) * S5_GROUP, (CHUNK - d) * S5_GROUP)
        bp_ref[:, back, lo] = bbr * pr - bbi * pi
        bp_ref[:, back, hi] = bbr * pi + bbi * pr
        pr, pi = pr * lbr - pi * lbi, pr * lbi + pi * lbr
        cp_r = c_re * pr - c_im * pi
        cp_i = -(c_re * pi + c_im * pr)
        cpre_ref[0:half, rows, lo] = cp_r[0:half]
        cpre_ref[0:half, rows, hi] = zeros
        cpre_ref[half:gb, rows, lo] = zeros
        cpre_ref[half:gb, rows, hi] = cp_r[half:gb]
        cpim_ref[0:half, rows, lo] = cp_i[0:half]
        cpim_ref[0:half, rows, hi] = zeros
        cpim_ref[half:gb, rows, lo] = zeros
        cpim_ref[half:gb, rows, hi] = cp_i[half:gb]

    qr, qi = pr, pi
    for r in range(8):
        ar_ref[:, r:r + 1, lo] = qr[0:half]
        ar_ref[:, r:r + 1, hi] = qr[half:gb]
        ai_ref[:, r:r + 1, lo] = qi[0:half]
        ai_ref[:, r:r + 1, hi] = qi[half:gb]
        qr, qi = qr * pr - qi * pi, qr * pi + qi * pr

    lane = lax.broadcasted_iota(jnp.int32, (S5_GROUP, 128), 1)
    for j in range(gb):
        w = _dot3(cd_ref[j], bp_ref[j], NT)
        w0, w1 = w[:, :128], w[:, 128:]
        for t in range(CHUNK):
            rows = slice(t * S5_GROUP, (t + 1) * S5_GROUP)
            shift = (CHUNK - 1 - t) * S5_GROUP
            keep = 128 - shift % 128
            if shift == 0:
                left, right = w0, w1
            elif shift < 128:
                r0 = pltpu.roll(w0, keep, axis=1)
                r1 = pltpu.roll(w1, keep, axis=1)
                left = jnp.where(lane < keep, r0, r1)
                right = jnp.where(lane < keep, r1, 0.0)
            elif shift == 128:
                left, right = w1, jnp.zeros_like(w1)
            else:
                left = jnp.where(lane < keep, pltpu.roll(w1, keep, axis=1), 0.0)
                right = jnp.zeros_like(w1)
            l_ref[j, rows, 0:128] = left
            l_ref[j, rows, 128:256] = right
        l_ref[j, CW:L_ROWS, :] = bp_ref[j].T


def _s5_prep(lam_re, lam_im, log_dt, b_re, b_im, c_re, c_im):
    g, p, gb, half = S5_GROUPS, S5_STATE, GROUPS_PER_TILE, PAIRS_PER_TILE
    lre = lam_re.reshape(g, 1, p)
    lim = lam_im.reshape(g, 1, p)
    ldt = jnp.broadcast_to(log_dt.reshape(g, 1, 1), (g, 1, p))
    bre = jnp.transpose(b_re, (0, 2, 1))
    bim = jnp.transpose(b_im, (0, 2, 1))
    sd = jax.ShapeDtypeStruct
    blk = lambda n, r, c: pl.BlockSpec((n, r, c), lambda i: (i, 0, 0))
    return pl.pallas_call(
        _s5_prep_body,
        grid=(g // gb,),
        in_specs=[blk(gb, 1, p)] * 3 + [blk(gb, S5_GROUP, p)] * 4,
        out_specs=(
            blk(gb, L_ROWS, CW), blk(gb, CW, 2 * p), blk(gb, CW, 2 * p),
            blk(gb, S5_GROUP, 2 * p), blk(gb, S5_GROUP, 2 * p),
            blk(half, 8, 2 * p), blk(half, 8, 2 * p), blk(gb, 1, 2 * p), blk(gb, 1, 2 * p),
        ),
        out_shape=(
            sd((g, L_ROWS, CW), F32),
            sd((g, CW, 2 * p), F32),
            sd((g, CW, 2 * p), F32),
            sd((g, S5_GROUP, 2 * p), F32),
            sd((g, S5_GROUP, 2 * p), F32),
            sd((g // 2, 8, 2 * p), F32),
            sd((g // 2, 8, 2 * p), F32),
            sd((g, 1, 2 * p), F32),
            sd((g, 1, 2 * p), F32),
        ),
        scratch_shapes=[pltpu.VMEM((gb, CW, 2 * p), F32)],
        compiler_params=pltpu.CompilerParams(dimension_semantics=("parallel",)),
        name="s5_prep",
    )(lre, lim, ldt, bre, bim, c_re, c_im)


def _cmul_add(h, hs, ar, ai):
    return h + hs * ar + pltpu.roll(hs, S5_STATE, axis=1) * ai


def _s5p_body(u_ref, l_ref, cpre_ref, cpim_ref, ar_ref, ai_ref, y_ref, hre_ref, him_ref,
              ut_ref, yt_ref, *, nb, nk):
    gb, half = GROUPS_PER_TILE, PAIRS_PER_TILE
    p = S5_STATE
    nrow = nb * nk
    d = functools.partial(jnp.dot, preferred_element_type=F32)

    for t in range(CHUNK):
        xt = u_ref[pl.ds(t, nrow, stride=CHUNK), :].T
        for j in range(gb):
            ut_ref[j, t * S5_GROUP:(t + 1) * S5_GROUP, :] = xt[j * S5_GROUP:(j + 1) * S5_GROUP, :]

    s_re, s_im = [], []
    for j in range(gb):
        r = d(l_ref[j].astype(BF16), ut_ref[j].astype(BF16))
        yt_ref[j] = r[0:CW]
        s_re.append(r[CW:CW + p])
        s_im.append(r[CW + p:L_ROWS])

    nblk = nrow // 8
    row8 = lax.broadcasted_iota(jnp.int32, (nblk, 8, 2 * p), 1)
    rows = lax.broadcasted_iota(jnp.int32, (nrow, 2 * p), 0) & (nk - 1)
    dnt = functools.partial(lax.dot_general, dimension_numbers=NT, preferred_element_type=F32)
    for q in range(half):
        re = jnp.concatenate([s_re[q], s_re[q + half]], axis=0).T.reshape(nblk, 8, 2 * p)
        im = jnp.concatenate([s_im[q], s_im[q + half]], axis=0).T.reshape(nblk, 8, 2 * p)
        for sh in (1, 2, 4):
            keep = row8 >= sh
            rs = jnp.where(keep, pltpu.roll(re, sh, axis=1), 0.0)
            js = jnp.where(keep, pltpu.roll(im, sh, axis=1), 0.0)
            ar = ar_ref[q, sh - 1:sh, :]
            ai = ai_ref[q, sh - 1:sh, :]
            re, im = re + ar * rs - ai * js, im + ar * js + ai * rs
        pw_r, pw_i = ar_ref[q], ai_ref[q]
        out_r, out_i = [], []
        for k in range(nblk):
            hr, hi = re[k], im[k]
            if k % (nk // 8):
                hr, hi = hr + pw_r * cr - pw_i * ci, hi + pw_r * ci + pw_i * cr
            cr, ci = hr[7:8, :], hi[7:8, :]
            out_r.append(hr)
            out_i.append(hi)
            if (k + 1) % (nk // 8) == 0:
                b = k // (nk // 8)
                hre_ref[q, b:b + 1, :] = cr
                him_ref[q, b:b + 1, :] = ci
        re = jnp.concatenate(out_r, axis=0)
        im = jnp.concatenate(out_i, axis=0)
        pre = jnp.where(rows >= 1, pltpu.roll(re, 1, axis=0), 0.0).astype(BF16)
        pim = jnp.where(rows >= 1, pltpu.roll(im, 1, axis=0), 0.0).astype(BF16)
        for j in (q, q + half):
            yt_ref[j] = (yt_ref[j] + dnt(cpre_ref[j].astype(BF16), pre)
                         + dnt(cpim_ref[j].astype(BF16), pim))

    for t in range(CHUNK):
        yt = jnp.concatenate(
            [yt_ref[j, t * S5_GROUP:(t + 1) * S5_GROUP, :] for j in range(gb)], axis=0)
        y_ref[pl.ds(t, nrow, stride=CHUNK), :] = yt.T


def _s5_prompt(proj_p, lmat, cpre, cpim, ar, ai, *, nb, seq):
    g, gb, half = S5_GROUPS, GROUPS_PER_TILE, PAIRS_PER_TILE
    n_p = nb * seq
    nk = seq // CHUNK
    blk = lambda n, r, c: pl.BlockSpec((n, r, c), lambda i: (i, 0, 0))
    return pl.pallas_call(
        functools.partial(_s5p_body, nb=nb, nk=nk),
        grid=(g // gb,),
        in_specs=[pl.BlockSpec((n_p, 128), lambda i: (0, i)),
                  blk(gb, L_ROWS, CW), blk(gb, CW, 2 * S5_STATE), blk(gb, CW, 2 * S5_STATE),
                  blk(half, 8, 2 * S5_STATE), blk(half, 8, 2 * S5_STATE)],
        out_specs=(pl.BlockSpec((n_p, 128), lambda i: (0, i)),
                   blk(half, nb, 2 * S5_STATE), blk(half, nb, 2 * S5_STATE)),
        out_shape=(jax.ShapeDtypeStruct((n_p, D_S5), F32),
                   jax.ShapeDtypeStruct((g // 2, nb, 2 * S5_STATE), F32),
                   jax.ShapeDtypeStruct((g // 2, nb, 2 * S5_STATE), F32)),
        scratch_shapes=[pltpu.VMEM((gb, CW, nb * nk), F32), pltpu.VMEM((gb, CW, nb * nk), F32)],
        compiler_params=pltpu.CompilerParams(
            dimension_semantics=("parallel",), vmem_limit_bytes=VMEM_LIMIT),
        name="s5_prompt",
    )(proj_p, lmat, cpre, cpim, ar, ai)


def _s5d_body(u_ref, h0_ref, bd_ref, cd_ref, lr_ref, li_ref, y_ref, h_ref, *, gb):
    for j in range(gb):
        h0 = h0_ref[j]
        bu = _dot3(u_ref[j], bd_ref[j])
        h = _cmul_add(bu, h0, lr_ref[j], li_ref[j])
        h_ref[j] = h
        y_ref[j] = _dot3(h, cd_ref[j], NT)


def _s5_decode(ud, h0, bd, cd, lr, li, *, gb=8):
    g, nbatch = S5_GROUPS, ud.shape[1]
    blk = lambda r, c: pl.BlockSpec((gb, r, c), lambda i: (i, 0, 0))
    return pl.pallas_call(
        functools.partial(_s5d_body, gb=gb),
        grid=(g // gb,),
        in_specs=[blk(nbatch, S5_GROUP), blk(nbatch, 2 * S5_STATE), blk(S5_GROUP, 2 * S5_STATE),
                  blk(S5_GROUP, 2 * S5_STATE), blk(1, 2 * S5_STATE), blk(1, 2 * S5_STATE)],
        out_specs=(blk(nbatch, S5_GROUP), blk(nbatch, 2 * S5_STATE)),
        out_shape=(jax.ShapeDtypeStruct((g, nbatch, S5_GROUP), F32),
                   jax.ShapeDtypeStruct((g, nbatch, 2 * S5_STATE), F32)),
        compiler_params=pltpu.CompilerParams(dimension_semantics=("parallel",)),
        name="s5_decode",
    )(ud, h0, bd, cd, lr, li)


def _lru_gates(xc, wa_ref, wx_ref, ba, bx, lam):
    xcb = xc.astype(BF16)
    nblk = D_LRU // 256
    r_parts, i_parts = [], []
    for k in range(nblk):
        xk = xcb[:, k * 256:(k + 1) * 256]
        r_parts.append(jnp.dot(xk, wa_ref[k], preferred_element_type=F32))
        i_parts.append(jnp.dot(xk, wx_ref[k], preferred_element_type=F32))
    r = jax.nn.sigmoid(jnp.concatenate(r_parts, axis=1) + ba)
    i = jax.nn.sigmoid(jnp.concatenate(i_parts, axis=1) + bx)
    z = -lam
    softplus = jnp.maximum(z, 0.0) + jnp.log1p(jnp.exp(-jnp.abs(z)))
    log_a = (-LRU_C * softplus) * r
    a = jnp.exp(log_a)
    v = -jnp.tanh(log_a) * (a * a + 1.0)
    mult = jnp.where(v > 0.0, v * lax.rsqrt(v), 0.0)
    return a, mult * (i * xc)


def _lru_tile(xl_ref, gate_ref, cw_ref, cb_ref, wa_ref, wx_ref, ba_ref, bx_ref, lam_ref,
              o_ref, xbuf_ref, carry_ref):
    tt = xl_ref.shape[0]
    x = xl_ref[...]
    xbuf_ref[8:8 + tt, :] = x
    cw = cw_ref[...]
    xc = (cb_ref[...] + xbuf_ref[5:5 + tt, :] * cw[0:1] + xbuf_ref[6:6 + tt, :] * cw[1:2]
          + xbuf_ref[7:7 + tt, :] * cw[2:3] + x * cw[3:4])
    xbuf_ref[0:8, :] = x[tt - 8:tt, :]

    a, b = _lru_gates(xc, wa_ref, wx_ref, ba_ref[...], bx_ref[...], lam_ref[...])

    nblk = tt // 8
    a3 = a.reshape(nblk, 8, D_LRU)
    b3 = b.reshape(nblk, 8, D_LRU)
    row = lax.broadcasted_iota(jnp.int32, (nblk, 8, D_LRU), 1)
    for sh in (1, 2, 4):
        keep = row >= sh
        bs = jnp.where(keep, pltpu.roll(b3, sh, axis=1), 0.0)
        sa = jnp.where(keep, pltpu.roll(a3, sh, axis=1), 1.0)
        b3 = b3 + a3 * bs
        a3 = a3 * sa
    carry = carry_ref[0:1, :]
    gate = gate_ref[...]
    for k in range(nblk):
        h = b3[k] + a3[k] * carry
        carry = h[7:8, :]
        o_ref[k * 8:(k + 1) * 8, :] = h * gate[k * 8:(k + 1) * 8, :]
    carry_ref[...] = jnp.broadcast_to(carry, (8, D_LRU))
    return carry


def _mix_s5_part(ys, u, x, dsk_ref, wg_ref, bg_ref, gs_ref, wo_ref):
    yy = ys + dsk_ref[...] * u
    g = _gelu(yy)
    z = jnp.dot(g.astype(BF16), wg_ref[...], preferred_element_type=F32) + bg_ref[...]
    s5o = g * jax.nn.sigmoid(z)
    n1 = _rms(s5o, gs_ref[...]).astype(BF16)
    return x + jnp.dot(n1, wo_ref[0:D_S5, :], preferred_element_type=F32)


def _mix_lru_part(lru, gl_ref, wo_ref):
    n2 = _rms(lru, gl_ref[...]).astype(BF16)
    return jnp.dot(n2, wo_ref[D_S5:, :], preferred_element_type=F32)


def _lru_mix_body(xl_ref, gate_ref, ys_ref, u_ref, x_ref,
                  cw_ref, cb_ref, wa_ref, wx_ref, ba_ref, bx_ref, lam_ref,
                  dsk_ref, wg_ref, bg_ref, gs_ref, gl_ref, wo_ref,
                  o_ref, hl_ref, xbuf_ref, carry_ref, lru_ref):
    @pl.when(pl.program_id(1) == 0)
    def _():
        xbuf_ref[0:8, :] = jnp.zeros((8, D_LRU), F32)
        carry_ref[...] = jnp.zeros((8, D_LRU), F32)

    o_ref[...] = _mix_s5_part(ys_ref[...], u_ref[...], x_ref[...],
                              dsk_ref, wg_ref, bg_ref, gs_ref, wo_ref)
    hl_ref[0] = _lru_tile(xl_ref, gate_ref, cw_ref, cb_ref, wa_ref, wx_ref, ba_ref, bx_ref, lam_ref,
                          lru_ref, xbuf_ref, carry_ref)
    o_ref[...] += _mix_lru_part(lru_ref[...], gl_ref, wo_ref)


def _lru_mix_prompt(proj_p, ys_p, x1_p, cw, cb, wa, wx, ba, bx, lam, dsk, wg, bg, gs, gl, wo,
                    *, nb, seq, tt=512):
    nt = seq // tt
    rows = lambda c, col: pl.BlockSpec((tt, c), lambda b, t: (b * nt + t, col))
    once = lambda shape: pl.BlockSpec(shape, lambda b, t: (0,) * len(shape),
                                      pipeline_mode=pl.Buffered(1))
    return pl.pallas_call(
        _lru_mix_body,
        grid=(nb, nt),
        in_specs=[
            rows(D_LRU, 1), rows(D_LRU, 2),
            rows(D_S5, 0), rows(D_S5, 0), rows(D_MODEL, 0),
            once((CONV_W, D_LRU)), once((1, D_LRU)),
            once((D_LRU // 256, 256, 256)), once((D_LRU // 256, 256, 256)),
            once((1, D_LRU)), once((1, D_LRU)), once((1, D_LRU)),
            once((1, D_S5)), once((D_S5, D_S5)), once((1, D_S5)), once((1, D_S5)), once((1, D_LRU)),
            once((D_MODEL, D_MODEL)),
        ],
        out_specs=(pl.BlockSpec((tt, D_MODEL), lambda b, t: (b * nt + t, 0)),
                   pl.BlockSpec((1, 1, D_LRU), lambda b, t: (b, 0, 0))),
        out_shape=(jax.ShapeDtypeStruct((nb * seq, D_MODEL), F32),
                   jax.ShapeDtypeStruct((nb, 1, D_LRU), F32)),
        scratch_shapes=[pltpu.VMEM((tt + 8, D_LRU), F32), pltpu.VMEM((8, D_LRU), F32),
                        pltpu.VMEM((tt, D_LRU), F32)],
        compiler_params=pltpu.CompilerParams(
            dimension_semantics=("parallel", "arbitrary"), vmem_limit_bytes=VMEM_LIMIT),
        name="lru_mix_prompt",
    )(proj_p, proj_p, ys_p, proj_p, x1_p, cw, cb, wa, wx, ba, bx, lam, dsk, wg, bg, gs, gl, wo)


def _lru_decode_body(xl_ref, gate_ref, c0_ref, c1_ref, c2_ref, h0_ref, cw_ref, cb_ref,
                     wa_ref, wx_ref, ba_ref, bx_ref, lam_ref, o_ref, h_ref):
    x = xl_ref[...]
    cw = cw_ref[...]
    xc = (cb_ref[...] + c0_ref[...] * cw[0:1] + c1_ref[...] * cw[1:2]
          + c2_ref[...] * cw[2:3] + x * cw[3:4])
    a, b = _lru_gates(xc, wa_ref, wx_ref, ba_ref[...], bx_ref[...], lam_ref[...])
    h = a * h0_ref[...] + b
    h_ref[...] = h
    o_ref[...] = h * gate_ref[...]


def _lru_decode(proj_d, c0, c1, c2, h0, cw, cb, wa, wx, ba, bx, lam):
    nd = proj_d.shape[0]
    full = lambda r: pl.BlockSpec((r, D_LRU), lambda i: (0, 0))
    wspec = pl.BlockSpec((D_LRU // 256, 256, 256), lambda i: (0, 0, 0))
    return pl.pallas_call(
        _lru_decode_body,
        grid=(1,),
        in_specs=[
            pl.BlockSpec((nd, D_LRU), lambda i: (0, 1)),
            pl.BlockSpec((nd, D_LRU), lambda i: (0, 2)),
            full(nd), full(nd), full(nd), full(nd),
            full(CONV_W), full(1), wspec, wspec, full(1), full(1), full(1),
        ],
        out_specs=(full(nd), full(nd)),
        out_shape=(jax.ShapeDtypeStruct((nd, D_LRU), F32),
                   jax.ShapeDtypeStruct((nd, D_LRU), F32)),
        name="lru_decode",
    )(proj_d, proj_d, c0, c1, c2, h0, cw, cb, wa, wx, ba, bx, lam)


def _mix_decode_body(ys_ref, u_ref, lru_ref, x_ref, dsk_ref, wg_ref, bg_ref, gs_ref, gl_ref,
                     wo_ref, o_ref):
    o_ref[...] = (_mix_s5_part(ys_ref[...], u_ref[...], x_ref[...],
                               dsk_ref, wg_ref, bg_ref, gs_ref, wo_ref)
                  + _mix_lru_part(lru_ref[...], gl_ref, wo_ref))


def _mix_decode(ys_d, proj_d, lru_d, x1_d, dsk, wg, bg, gs, gl, wo):
    nd = x1_d.shape[0]
    full = lambda r, c: pl.BlockSpec((r, c), lambda i: (0, 0))
    return pl.pallas_call(
        _mix_decode_body,
        grid=(1,),
        in_specs=[full(nd, D_S5), full(nd, D_S5), full(nd, D_LRU), full(nd, D_MODEL),
                  full(1, D_S5), full(D_S5, D_S5), full(1, D_S5), full(1, D_S5), full(1, D_LRU),
                  full(D_MODEL, D_MODEL)],
        out_specs=full(nd, D_MODEL),
        out_shape=jax.ShapeDtypeStruct((nd, D_MODEL), F32),
        compiler_params=pltpu.CompilerParams(vmem_limit_bytes=VMEM_LIMIT),
        name="mix_decode",
    )(ys_d, proj_d, lru_d, x1_d, dsk, wg, bg, gs, gl, wo)


def _unpair(h, nb):
    tiles = S5_GROUPS // GROUPS_PER_TILE
    h5 = h.reshape(tiles, PAIRS_PER_TILE, nb, 2, S5_STATE)
    return jnp.transpose(h5, (2, 0, 3, 1, 4)).reshape(nb, S5_GROUPS, S5_STATE)


def _block_diag4(w):
    w4 = w.reshape(LRU_HEADS // 4, 4, LRU_HEAD_DIM, LRU_HEAD_DIM)
    eye = jnp.eye(4, dtype=w.dtype)
    return jnp.einsum("kaij,ab->kaibj", w4, eye).reshape(LRU_HEADS // 4, 256, 256)


def kernel(x_prompt, x_sample, state_s5_re, state_s5_im, state_lru_h, state_lru_conv, g_ffn1, w1_a, w3_a, w2_a, g_mix, w_in, lam_re, lam_im, log_dt, b_re, b_im, c_re, c_im, d_skip, w_glu, b_glu, conv_w, conv_b, w_a, b_a, w_x, b_x, lam_l, g_out_s5, g_out_lru, w_out, g_ffn2, w1_b, w3_b, w2_b, g_final):
    nb, seq, _ = x_prompt.shape
    nd = x_sample.shape[0]
    n_p = nb * seq
    nk = seq // CHUNK
    g, p = S5_GROUPS, S5_STATE
    row = lambda v: v.reshape(1, -1)

    xp = x_prompt.reshape(n_p, D_MODEL)
    xd = x_sample.reshape(nd, D_MODEL)

    n_i, n_f = n_p // 1024, D_FF // 512
    first = _ffn(
        xp, xd, row(g_ffn1[0]), w1_a[0], w3_a[0], w2_a[0], n_tiles=1,
        casts=(_cast_job(w1_b[0], n_i, n_f, f_div=2), _cast_job(w3_b[0], n_i, n_f, f_div=2),
               _row_cast_job(w2_b[0], n_i, n_f, f_div=2), _flat_cast_job(w_in[0], 16)))
    w1_a16, w3_a16, w2_a16 = first[2:5]
    w_in16 = first[8]
    x1_p, x1_d, w1_b16, w3_b16, w2_b16 = _ffn(
        xp, xd, row(g_ffn1[0]), w1_a16, w3_a16, w2_a16, first_tile=1, n_tiles=n_i - 1,
        casts=(_cast_job(w1_b[0], n_i, n_f, i0=1), _cast_job(w3_b[0], n_i, n_f, i0=1),
               _row_cast_job(w2_b[0], n_i, n_f, i0=1)),
        prev=first[:2] + first[5:8])
    proj_p, proj_d, w_out16, w_glu16 = _inproj(
        x1_p, x1_d, row(g_mix[0]), w_in16,
        casts=(_cast_job(w_out[0], n_i, D_IN // 1024, bc=1024),
               _cast_job(w_glu[0], n_i, D_IN // 1024, bc=1024)))

    lmat, cpre, cpim, bd, cd, ar, ai, lr, li = _s5_prep(
        lam_re[0], lam_im[0], log_dt[0], b_re[0], b_im[0], c_re[0], c_im[0])
    ys_p, hf_re, hf_im = _s5_prompt(proj_p, lmat, cpre, cpim, ar, ai, nb=nb, seq=seq)

    ud = jnp.transpose(proj_d[:, :D_S5].reshape(nd, g, S5_GROUP), (1, 0, 2))
    h0d = jnp.concatenate([jnp.transpose(state_s5_re[0], (1, 0, 2)),
                           jnp.transpose(state_s5_im[0], (1, 0, 2))], axis=-1)
    ydg, hd = _s5_decode(ud, h0d, bd, cd, lr, li)
    ys_d = jnp.transpose(ydg, (1, 0, 2)).reshape(nd, D_S5)

    wa_bd = _block_diag4(w_a[0]).astype(BF16)
    wx_bd = _block_diag4(w_x[0]).astype(BF16)
    lru_args = (conv_w[0], row(conv_b[0]), wa_bd, wx_bd, row(b_a[0]), row(b_x[0]), row(lam_l[0]))
    conv0 = state_lru_conv[0]
    lru_d, hl_d = _lru_decode(proj_d, conv0[:, 0], conv0[:, 1], conv0[:, 2], state_lru_h[0],
                              *lru_args)

    mix_args = (row(d_skip[0]), w_glu16, row(b_glu[0]), row(g_out_s5[0]), row(g_out_lru[0]), w_out16)
    x2_p, hl_p = _lru_mix_prompt(proj_p, ys_p, x1_p, *lru_args, *mix_args, nb=nb, seq=seq)
    x2_d = _mix_decode(ys_d, proj_d, lru_d, x1_d, *mix_args)
    y_p, y_d = _ffn(x2_p, x2_d, row(g_ffn2[0]), w1_b16, w3_b16, w2_b16, row(g_final))

    tail_p = proj_p.reshape(nb, seq, -1)[:, seq - (CONV_W - 1):, D_S5:D_S5 + D_LRU]
    xl_d = proj_d[:, D_S5:D_S5 + D_LRU]
    return (
        y_p.reshape(nb, seq, D_MODEL),
        y_d.reshape(nd, 1, D_MODEL),
        _unpair(hf_re, nb)[None],
        _unpair(hf_im, nb)[None],
        hl_p.reshape(1, nb, D_LRU),
        tail_p[None],
        jnp.transpose(hd[:, :, :p], (1, 0, 2))[None],
        jnp.transpose(hd[:, :, p:], (1, 0, 2))[None],
        hl_d[None],
        jnp.stack([conv0[:, 1], conv0[:, 2], xl_d], axis=1)[None],
    )
```

```python
import functools

import jax
import jax.numpy as jnp
from jax import lax
from jax.experimental import pallas as pl
from jax.experimental.pallas import tpu as pltpu

F32 = jnp.float32
BF16 = jnp.bfloat16

D_MODEL = 2048
D_S5 = 1024
S5_GROUP = 16
S5_GROUPS = 64
S5_STATE = 64
D_LRU = 1024
LRU_HEADS = 16
LRU_HEAD_DIM = 64
CONV_W = 4
LRU_C = 8.0
D_FF = 5632
D_IN = D_S5 + 2 * D_LRU
EPS = 1e-6

CHUNK = 16
CW = CHUNK * S5_GROUP

VMEM_LIMIT = 58 * 1024 * 1024
FFN_VMEM_LIMIT = 60 * 1024 * 1024

NN = (((1,), (0,)), ((), ()))
NT = (((1,), (1,)), ((), ()))


def _rms(x, g):
    return x * lax.rsqrt(jnp.mean(x * x, axis=-1, keepdims=True) + EPS) * g


def _split(x):
    hi = x.astype(BF16)
    lo = (x - hi.astype(F32)).astype(BF16)
    return hi, lo


def _dot3(a, b, dims=NN):
    ah, al = _split(a)
    bh, bl = _split(b)
    d = functools.partial(lax.dot_general, dimension_numbers=dims, preferred_element_type=F32)
    return d(ah, bh) + d(al, bh) + d(ah, bl)


def _gelu(x):
    return jax.nn.gelu(x, approximate=True)


def _ffn_body(*refs, final_norm, convert, n_casts, n_prev):
    n_in = 7 if final_norm else 6
    xp_ref, xd_ref, g_ref, w1_ref, w3_ref, w2_ref = refs[:6]
    gf_ref = refs[6] if final_norm else None
    cast_in = refs[n_in:n_in + n_casts]
    outs = refs[n_in + n_casts + n_prev:-1]
    op_ref, od_ref = outs[:2]
    wcopy = outs[2:5] if convert else ()
    cast_out = outs[2 + len(wcopy):]
    xn_ref = refs[-1]
    f = pl.program_id(1)
    tm = xp_ref.shape[0]

    for src_ref, dst_ref in zip(cast_in, cast_out):
        dst_ref[...] = src_ref[...].astype(BF16)

    @pl.when(f == 0)
    def _():
        for x_ref, o_ref, rows in ((xp_ref, op_ref, slice(0, tm)), (xd_ref, od_ref, slice(tm, None))):
            x = x_ref[...]
            xn_ref[rows, :] = _rms(x, g_ref[...]).astype(BF16)
            o_ref[...] = x

    if convert:
        w1, w3, w2 = (w_ref[...].astype(BF16) for w_ref in (w1_ref, w3_ref, w2_ref))
        for dst_ref, w in zip(wcopy, (w1, w3, w2)):
            dst_ref[...] = w
    else:
        w1, w3, w2 = w1_ref[...], w3_ref[...], w2_ref[...]

    xn = xn_ref[...]
    a = jnp.dot(xn, w1, preferred_element_type=F32)
    b = jnp.dot(xn, w3, preferred_element_type=F32)
    h = (a * jax.nn.sigmoid(a) * b).astype(BF16)
    upd = 0.5 * jnp.dot(h, w2, preferred_element_type=F32)
    op_ref[...] += upd[0:tm]
    od_ref[...] += upd[tm:]

    if final_norm:
        @pl.when(f == pl.num_programs(1) - 1)
        def _():
            op_ref[...] = _rms(op_ref[...], gf_ref[...])
            od_ref[...] = _rms(od_ref[...], gf_ref[...])


def _cast_job(w, n_i, n_f, bc=512, i0=0, f_div=1):
    rows, cols = w.shape
    br = rows // n_i
    n_cb = cols // bc
    assert br * n_i == rows and bc * n_cb == cols and n_cb <= n_f and br % 16 == 0
    return w, pl.BlockSpec(
        (br, bc), lambda i, f: (jnp.minimum(i + i0, n_i - 1),
                                jnp.where(i + i0 < n_i, jnp.minimum(f // f_div, n_cb - 1),
                                          n_cb - 1)))


def _row_cast_job(w, n_i, n_f, i0=0, f_div=1):
    rows, cols = w.shape
    br = rows // (n_i * n_f)
    assert br * n_i * n_f == rows and br % 16 == 0
    return w, pl.BlockSpec((br, cols), lambda i, f: ((i + i0) * n_f + f // f_div, 0))


def _flat_cast_job(w, n_blocks):
    rows, cols = w.shape
    br = rows // n_blocks
    assert br * n_blocks == rows and br % 16 == 0
    return w, pl.BlockSpec((br, cols), lambda i, f: (jnp.minimum(f, n_blocks - 1), 0))


def _ffn(xp, xd, g, w1, w3, w2, g_final=None, *, casts=(), first_tile=0, n_tiles=None, prev=(),
         tm=1024):
    n_p, nd = xp.shape[0], xd.shape[0]
    all_tiles = n_p // tm
    n_tiles = all_tiles if n_tiles is None else n_tiles
    td = nd // all_tiles
    final_norm = g_final is not None
    convert = w1.dtype == F32
    tf = 256 if convert else 512
    pspec = pl.BlockSpec((tm, D_MODEL), lambda i, f: (i + first_tile, 0))
    dspec = pl.BlockSpec((td, D_MODEL), lambda i, f: (i + first_tile, 0))
    xspec = pspec if n_tiles > 1 else pl.BlockSpec(
        (tm, D_MODEL), lambda i, f: (i + first_tile, 0), pipeline_mode=pl.Buffered(1))
    w13spec = pl.BlockSpec((D_MODEL, tf), lambda i, f: (0, f))
    w2spec = pl.BlockSpec((tf, D_MODEL), lambda i, f: (f, 0))
    in_specs = [xspec, dspec, pl.BlockSpec((1, D_MODEL), lambda i, f: (0, 0)),
                w13spec, w13spec, w2spec]
    args = [xp, xd, g, w1, w3, w2]
    if final_norm:
        in_specs.append(pl.BlockSpec((1, D_MODEL), lambda i, f: (0, 0)))
        args.append(g_final)
    in_specs += [spec for _, spec in casts]
    args += [w for w, _ in casts]
    n_in = len(args)
    in_specs += [pl.BlockSpec(memory_space=pl.ANY)] * len(prev)
    args += list(prev)
    sd = jax.ShapeDtypeStruct
    out_specs = [pspec, dspec]
    out_shape = [sd((n_p, D_MODEL), F32), sd((nd, D_MODEL), F32)]
    if convert:
        out_specs += [w13spec, w13spec, w2spec]
        out_shape += [sd(w.shape, BF16) for w in (w1, w3, w2)]
    out_specs += [spec for _, spec in casts]
    out_shape += [sd(w.shape, BF16) for w, _ in casts]
    assert len(prev) in (0, len(out_shape))
    return pl.pallas_call(
        functools.partial(_ffn_body, final_norm=final_norm, convert=convert,
                          n_casts=len(casts), n_prev=len(prev)),
        grid=(n_tiles, D_FF // tf),
        in_specs=in_specs,
        out_specs=tuple(out_specs),
        out_shape=tuple(out_shape),
        input_output_aliases={n_in + k: k for k in range(len(prev))},
        scratch_shapes=[pltpu.VMEM((tm + td, D_MODEL), BF16)],
        compiler_params=pltpu.CompilerParams(
            dimension_semantics=("arbitrary", "arbitrary"), vmem_limit_bytes=FFN_VMEM_LIMIT),
        name="ffn_final" if final_norm else ("ffn_first" if convert else "ffn"),
    )(*args)


def _inproj_body(*refs, n_tiles, nd, n_casts):
    xp_ref, xd_ref, g_ref, w_ref = refs[:4]
    cast_in = refs[4:4 + n_casts]
    op_ref, od_ref = refs[4 + n_casts:6 + n_casts]
    cast_out = refs[6 + n_casts:6 + 2 * n_casts]
    xn_ref = refs[-1]
    i = pl.program_id(0)
    j = pl.program_id(1)
    nj = pl.num_programs(1)

    for src_ref, dst_ref in zip(cast_in, cast_out):
        dst_ref[...] = src_ref[...].astype(BF16)

    def run(x_ref, o_ref, rows):
        @pl.when(j == 0)
        def _():
            xn_ref[0:rows, :] = _rms(x_ref[...], g_ref[...]).astype(BF16)

        @pl.when(j < nj - 1)
        def _():
            o_ref[...] = jnp.dot(xn_ref[0:rows, :], w_ref[...], preferred_element_type=F32)

        @pl.when(j == nj - 1)
        def _():
            o_ref[...] = _gelu(jnp.dot(xn_ref[0:rows, :], w_ref[...],
                                       preferred_element_type=F32))

    @pl.when(i < n_tiles)
    def _():
        run(xp_ref, op_ref, xp_ref.shape[0])

    @pl.when(i == n_tiles)
    def _():
        run(xd_ref, od_ref, nd)


def _inproj(xp, xd, g, w, *, casts=(), tm=1024, tn=1024):
    n_p, nd = xp.shape[0], xd.shape[0]
    n_tiles = n_p // tm
    d_out = w.shape[1]
    nj = d_out // tn
    return pl.pallas_call(
        functools.partial(_inproj_body, n_tiles=n_tiles, nd=nd, n_casts=len(casts)),
        grid=(n_tiles + 1, nj),
        in_specs=[
            pl.BlockSpec((tm, D_MODEL), lambda i, j: (jnp.minimum(i, n_tiles - 1), 0)),
            pl.BlockSpec((nd, D_MODEL), lambda i, j: (0, 0)),
            pl.BlockSpec((1, D_MODEL), lambda i, j: (0, 0)),
            pl.BlockSpec((D_MODEL, tn), lambda i, j: (0, j)),
        ] + [spec for _, spec in casts],
        out_specs=(
            pl.BlockSpec((tm, tn), lambda i, j: (jnp.minimum(i, n_tiles - 1),
                                                 jnp.where(i < n_tiles, j, nj - 1))),
            pl.BlockSpec((nd, tn), lambda i, j: (0, jnp.where(i < n_tiles, 0, j))),
        ) + tuple(spec for _, spec in casts),
        out_shape=(jax.ShapeDtypeStruct((n_p, d_out), F32),
                   jax.ShapeDtypeStruct((nd, d_out), F32))
        + tuple(jax.ShapeDtypeStruct(cw.shape, BF16) for cw, _ in casts),
        scratch_shapes=[pltpu.VMEM((tm, D_MODEL), BF16)],
        compiler_params=pltpu.CompilerParams(
            dimension_semantics=("arbitrary", "arbitrary"), vmem_limit_bytes=VMEM_LIMIT),
        name="inproj",
    )(xp, xd, g, w, *[cw for cw, _ in casts])


GROUPS_PER_TILE = 128 // S5_GROUP
PAIRS_PER_TILE = GROUPS_PER_TILE // 2
L_ROWS = CW + 2 * S5_STATE


def _s5_prep_body(lre_ref, lim_ref, ldt_ref, bre_ref, bim_ref, cre_ref, cim_ref,
                  l_ref, cpre_ref, cpim_ref, bd_ref, cd_ref, ar_ref, ai_ref, lr_ref, li_ref,
                  bp_ref):
    gb, half = GROUPS_PER_TILE, PAIRS_PER_TILE
    p = S5_STATE
    lo, hi = slice(0, p), slice(p, 2 * p)
    lam_re = lre_ref[...]
    lam_im = lim_ref[...]
    dt = jnp.exp(ldt_ref[...])
    mag = jnp.exp(lam_re * dt)
    ang = lam_im * dt
    lbr = mag * jnp.cos(ang)
    lbi = mag * jnp.sin(ang)
    lr_ref[:, :, lo] = lbr
    lr_ref[:, :, hi] = lbr
    li_ref[:, :, lo] = -lbi
    li_ref[:, :, hi] = lbi
    nr = lbr - 1.0
    den = lam_re * lam_re + lam_im * lam_im
    cr = (nr * lam_re + lbi * lam_im) / den
    ci = (lbi * lam_re - nr * lam_im) / den
    b_re = bre_ref[...]
    b_im = bim_ref[...]
    bbr = cr * b_re - ci * b_im
    bbi = cr * b_im + ci * b_re
    bd_ref[:, :, lo] = bbr
    bd_ref[:, :, hi] = bbi
    c_re = cre_ref[...]
    c_im = cim_ref[...]
    cd_ref[:, :, lo] = c_re
    cd_ref[:, :, hi] = -c_im

    zeros = jnp.zeros((half, S5_GROUP, p), F32)
    pr = jnp.ones_like(lbr)
    pi = jnp.zeros_like(lbr)
    for d in range(CHUNK):
        rows = slice(d * S5_GROUP, (d + 1) * S5_GROUP)
        back = slice((CHUNK - 1 - d) * S5_GROUP, (CHUNK - d) * S5_GROUP)
        bp_ref[:, back, lo] = bbr * pr - bbi * pi
        bp_ref[:, back, hi] = bbr * pi + bbi * pr
        pr, pi = pr * lbr - pi * lbi, pr * lbi + pi * lbr
        cp_r = c_re * pr - c_im * pi
        cp_i = -(c_re * pi + c_im * pr)
        cpre_ref[0:half, rows, lo] = cp_r[0:half]
        cpre_ref[0:half, rows, hi] = zeros
        cpre_ref[half:gb, rows, lo] = zeros
        cpre_ref[half:gb, rows, hi] = cp_r[half:gb]
        cpim_ref[0:half, rows, lo] = cp_i[0:half]
        cpim_ref[0:half, rows, hi] = zeros
        cpim_ref[half:gb, rows, lo] = zeros
        cpim_ref[half:gb, rows, hi] = cp_i[half:gb]

    qr, qi = pr, pi
    for r in range(8):
        ar_ref[:, r:r + 1, lo] = qr[0:half]
        ar_ref[:, r:r + 1, hi] = qr[half:gb]
        ai_ref[:, r:r + 1, lo] = qi[0:half]
        ai_ref[:, r:r + 1, hi] = qi[half:gb]
        qr, qi = qr * pr - qi * pi, qr * pi + qi * pr

    lane = lax.broadcasted_iota(jnp.int32, (S5_GROUP, 128), 1)
    for j in range(gb):
        w = _dot3(cd_ref[j], bp_ref[j], NT)
        w0, w1 = w[:, :128], w[:, 128:]
        for t in range(CHUNK):
            rows = slice(t * S5_GROUP, (t + 1) * S5_GROUP)
            shift = (CHUNK - 1 - t) * S5_GROUP
            keep = 128 - shift % 128
            if shift == 0:
                left, right = w0, w1
            elif shift < 128:
                r0 = pltpu.roll(w0, keep, axis=1)
                r1 = pltpu.roll(w1, keep, axis=1)
                left = jnp.where(lane < keep, r0, r1)
                right = jnp.where(lane < keep, r1, 0.0)
            elif shift == 128:
                left, right = w1, jnp.zeros_like(w1)
            else:
                left = jnp.where(lane < keep, pltpu.roll(w1, keep, axis=1), 0.0)
                right = jnp.zeros_like(w1)
            l_ref[j, rows, 0:128] = left
            l_ref[j, rows, 128:256] = right
        l_ref[j, CW:L_ROWS, :] = bp_ref[j].T


def _s5_prep(lam_re, lam_im, log_dt, b_re, b_im, c_re, c_im):
    g, p, gb, half = S5_GROUPS, S5_STATE, GROUPS_PER_TILE, PAIRS_PER_TILE
    lre = lam_re.reshape(g, 1, p)
    lim = lam_im.reshape(g, 1, p)
    ldt = jnp.broadcast_to(log_dt.reshape(g, 1, 1), (g, 1, p))
    bre = jnp.transpose(b_re, (0, 2, 1))
    bim = jnp.transpose(b_im, (0, 2, 1))
    sd = jax.ShapeDtypeStruct
    blk = lambda n, r, c: pl.BlockSpec((n, r, c), lambda i: (i, 0, 0))
    return pl.pallas_call(
        _s5_prep_body,
        grid=(g // gb,),
        in_specs=[blk(gb, 1, p)] * 3 + [blk(gb, S5_GROUP, p)] * 4,
        out_specs=(
            blk(gb, L_ROWS, CW), blk(gb, CW, 2 * p), blk(gb, CW, 2 * p),
            blk(gb, S5_GROUP, 2 * p), blk(gb, S5_GROUP, 2 * p),
            blk(half, 8, 2 * p), blk(half, 8, 2 * p), blk(gb, 1, 2 * p), blk(gb, 1, 2 * p),
        ),
        out_shape=(
            sd((g, L_ROWS, CW), F32),
            sd((g, CW, 2 * p), F32),
            sd((g, CW, 2 * p), F32),
            sd((g, S5_GROUP, 2 * p), F32),
            sd((g, S5_GROUP, 2 * p), F32),
            sd((g // 2, 8, 2 * p), F32),
            sd((g // 2, 8, 2 * p), F32),
            sd((g, 1, 2 * p), F32),
            sd((g, 1, 2 * p), F32),
        ),
        scratch_shapes=[pltpu.VMEM((gb, CW, 2 * p), F32)],
        compiler_params=pltpu.CompilerParams(dimension_semantics=("parallel",)),
        name="s5_prep",
    )(lre, lim, ldt, bre, bim, c_re, c_im)


def _cmul_add(h, hs, ar, ai):
    return h + hs * ar + pltpu.roll(hs, S5_STATE, axis=1) * ai


def _s5p_body(u_ref, l_ref, cpre_ref, cpim_ref, ar_ref, ai_ref, y_ref, hre_ref, him_ref,
              ut_ref, yt_ref, *, nb, nk):
    gb, half = GROUPS_PER_TILE, PAIRS_PER_TILE
    p = S5_STATE
    nrow = nb * nk
    d = functools.partial(jnp.dot, preferred_element_type=F32)

    for t in range(CHUNK):
        xt = u_ref[pl.ds(t, nrow, stride=CHUNK), :].T
        for j in range(gb):
            ut_ref[j, t * S5_GROUP:(t + 1) * S5_GROUP, :] = xt[j * S5_GROUP:(j + 1) * S5_GROUP, :]

    s_re, s_im = [], []
    for j in range(gb):
        r = d(l_ref[j].astype(BF16), ut_ref[j].astype(BF16))
        yt_ref[j] = r[0:CW]
        s_re.append(r[CW:CW + p])
        s_im.append(r[CW + p:L_ROWS])

    nblk = nrow // 8
    row8 = lax.broadcasted_iota(jnp.int32, (nblk, 8, 2 * p), 1)
    rows = lax.broadcasted_iota(jnp.int32, (nrow, 2 * p), 0) & (nk - 1)
    dnt = functools.partial(lax.dot_general, dimension_numbers=NT, preferred_element_type=F32)
    for q in range(half):
        re = jnp.concatenate([s_re[q], s_re[q + half]], axis=0).T.reshape(nblk, 8, 2 * p)
        im = jnp.concatenate([s_im[q], s_im[q + half]], axis=0).T.reshape(nblk, 8, 2 * p)
        for sh in (1, 2, 4):
            keep = row8 >= sh
            rs = jnp.where(keep, pltpu.roll(re, sh, axis=1), 0.0)
            js = jnp.where(keep, pltpu.roll(im, sh, axis=1), 0.0)
            ar = ar_ref[q, sh - 1:sh, :]
            ai = ai_ref[q, sh - 1:sh, :]
            re, im = re + ar * rs - ai * js, im + ar * js + ai * rs
        pw_r, pw_i = ar_ref[q], ai_ref[q]
        out_r, out_i = [], []
        for k in range(nblk):
            hr, hi = re[k], im[k]
            if k % (nk // 8):
                hr, hi = hr + pw_r * cr - pw_i * ci, hi + pw_r * ci + pw_i * cr
            cr, ci = hr[7:8, :], hi[7:8, :]
            out_r.append(hr)
            out_i.append(hi)
            if (k + 1) % (nk // 8) == 0:
                b = k // (nk // 8)
                hre_ref[q, b:b + 1, :] = cr
                him_ref[q, b:b + 1, :] = ci
        re = jnp.concatenate(out_r, axis=0)
        im = jnp.concatenate(out_i, axis=0)
        pre = jnp.where(rows >= 1, pltpu.roll(re, 1, axis=0), 0.0).astype(BF16)
        pim = jnp.where(rows >= 1, pltpu.roll(im, 1, axis=0), 0.0).astype(BF16)
        for j in (q, q + half):
            yt_ref[j] = (yt_ref[j] + dnt(cpre_ref[j].astype(BF16), pre)
                         + dnt(cpim_ref[j].astype(BF16), pim))

    for t in range(CHUNK):
        yt = jnp.concatenate(
            [yt_ref[j, t * S5_GROUP:(t + 1) * S5_GROUP, :] for j in range(gb)], axis=0)
        y_ref[pl.ds(t, nrow, stride=CHUNK), :] = yt.T


def _s5_prompt(proj_p, lmat, cpre, cpim, ar, ai, *, nb, seq):
    g, gb, half = S5_GROUPS, GROUPS_PER_TILE, PAIRS_PER_TILE
    n_p = nb * seq
    nk = seq // CHUNK
    blk = lambda n, r, c: pl.BlockSpec((n, r, c), lambda i: (i, 0, 0))
    return pl.pallas_call(
        functools.partial(_s5p_body, nb=nb, nk=nk),
        grid=(g // gb,),
        in_specs=[pl.BlockSpec((n_p, 128), lambda i: (0, i)),
                  blk(gb, L_ROWS, CW), blk(gb, CW, 2 * S5_STATE), blk(gb, CW, 2 * S5_STATE),
                  blk(half, 8, 2 * S5_STATE), blk(half, 8, 2 * S5_STATE)],
        out_specs=(pl.BlockSpec((n_p, 128), lambda i: (0, i)),
                   blk(half, nb, 2 * S5_STATE), blk(half, nb, 2 * S5_STATE)),
        out_shape=(jax.ShapeDtypeStruct((n_p, D_S5), F32),
                   jax.ShapeDtypeStruct((g // 2, nb, 2 * S5_STATE), F32),
                   jax.ShapeDtypeStruct((g // 2, nb, 2 * S5_STATE), F32)),
        scratch_shapes=[pltpu.VMEM((gb, CW, nb * nk), F32), pltpu.VMEM((gb, CW, nb * nk), F32)],
        compiler_params=pltpu.CompilerParams(
            dimension_semantics=("parallel",), vmem_limit_bytes=VMEM_LIMIT),
        name="s5_prompt",
    )(proj_p, lmat, cpre, cpim, ar, ai)


def _s5d_body(u_ref, hre_ref, him_ref, bd_ref, cd_ref, lr_ref, li_ref, y_ref, ore_ref, oim_ref):
    p = S5_STATE
    gb = GROUPS_PER_TILE
    u = u_ref[...]
    ys = []
    for j in range(gb):
        bu = _dot3(u[:, j * S5_GROUP:(j + 1) * S5_GROUP], bd_ref[j])
        h0r = hre_ref[:, j, :]
        h0i = him_ref[:, j, :]
        lbr = lr_ref[j][:, 0:p]
        lbi = li_ref[j][:, p:2 * p]
        hr = lbr * h0r - lbi * h0i + bu[:, 0:p]
        hi = lbr * h0i + lbi * h0r + bu[:, p:2 * p]
        ore_ref[:, j, :] = hr
        oim_ref[:, j, :] = hi
        cd = cd_ref[j]
        ys.append(_dot3(hr, cd[:, 0:p], NT) + _dot3(hi, cd[:, p:2 * p], NT))
    y_ref[...] = jnp.concatenate(ys, axis=1)


def _s5_decode(proj_d, h0_re, h0_im, bd, cd, lr, li):
    g, gb, p = S5_GROUPS, GROUPS_PER_TILE, S5_STATE
    nbatch = proj_d.shape[0]
    blk = lambda r, c: pl.BlockSpec((gb, r, c), lambda i: (i, 0, 0))
    state = pl.BlockSpec((nbatch, gb, p), lambda i: (0, i, 0))
    cols = pl.BlockSpec((nbatch, 128), lambda i: (0, i))
    return pl.pallas_call(
        _s5d_body,
        grid=(g // gb,),
        in_specs=[cols, state, state, blk(S5_GROUP, 2 * p), blk(S5_GROUP, 2 * p),
                  blk(1, 2 * p), blk(1, 2 * p)],
        out_specs=(cols, state, state),
        out_shape=(jax.ShapeDtypeStruct((nbatch, D_S5), F32),
                   jax.ShapeDtypeStruct((nbatch, g, p), F32),
                   jax.ShapeDtypeStruct((nbatch, g, p), F32)),
        compiler_params=pltpu.CompilerParams(dimension_semantics=("parallel",)),
        name="s5_decode",
    )(proj_d, h0_re, h0_im, bd, cd, lr, li)


def _lru_gates(xc, wa_ref, wx_ref, ba, bx, lam):
    xcb = xc.astype(BF16)
    nblk = D_LRU // 256
    r_parts, i_parts = [], []
    for k in range(nblk):
        xk = xcb[:, k * 256:(k + 1) * 256]
        r_parts.append(jnp.dot(xk, wa_ref[k], preferred_element_type=F32))
        i_parts.append(jnp.dot(xk, wx_ref[k], preferred_element_type=F32))
    r = jax.nn.sigmoid(jnp.concatenate(r_parts, axis=1) + ba)
    i = jax.nn.sigmoid(jnp.concatenate(i_parts, axis=1) + bx)
    z = -lam
    softplus = jnp.maximum(z, 0.0) + jnp.log1p(jnp.exp(-jnp.abs(z)))
    log_a = (-LRU_C * softplus) * r
    a = jnp.exp(log_a)
    v = -jnp.tanh(log_a) * (a * a + 1.0)
    mult = jnp.where(v > 0.0, v * lax.rsqrt(v), 0.0)
    return a, mult * (i * xc)


def _lru_tile(xl_ref, gate_ref, cw_ref, cb_ref, wa_ref, wx_ref, ba_ref, bx_ref, lam_ref,
              o_ref, xbuf_ref, carry_ref):
    tt = xl_ref.shape[0]
    x = xl_ref[...]
    xbuf_ref[8:8 + tt, :] = x
    cw = cw_ref[...]
    xc = (cb_ref[...] + xbuf_ref[5:5 + tt, :] * cw[0:1] + xbuf_ref[6:6 + tt, :] * cw[1:2]
          + xbuf_ref[7:7 + tt, :] * cw[2:3] + x * cw[3:4])
    xbuf_ref[0:8, :] = x[tt - 8:tt, :]

    a, b = _lru_gates(xc, wa_ref, wx_ref, ba_ref[...], bx_ref[...], lam_ref[...])

    nblk = tt // 8
    a3 = a.reshape(nblk, 8, D_LRU)
    b3 = b.reshape(nblk, 8, D_LRU)
    row = lax.broadcasted_iota(jnp.int32, (nblk, 8, D_LRU), 1)
    for sh in (1, 2, 4):
        keep = row >= sh
        bs = jnp.where(keep, pltpu.roll(b3, sh, axis=1), 0.0)
        sa = jnp.where(keep, pltpu.roll(a3, sh, axis=1), 1.0)
        b3 = b3 + a3 * bs
        a3 = a3 * sa
    carry = carry_ref[0:1, :]
    gate = gate_ref[...]
    for k in range(nblk):
        h = b3[k] + a3[k] * carry
        carry = h[7:8, :]
        o_ref[k * 8:(k + 1) * 8, :] = h * gate[k * 8:(k + 1) * 8, :]
    carry_ref[...] = jnp.broadcast_to(carry, (8, D_LRU))
    return carry


def _mix_s5_part(ys, u, x, dsk_ref, wg_ref, bg_ref, gs_ref, wo_ref):
    yy = ys + dsk_ref[...] * u
    g = _gelu(yy)
    z = jnp.dot(g.astype(BF16), wg_ref[...], preferred_element_type=F32) + bg_ref[...]
    s5o = g * jax.nn.sigmoid(z)
    n1 = _rms(s5o, gs_ref[...]).astype(BF16)
    return x + jnp.dot(n1, wo_ref[0:D_S5, :], preferred_element_type=F32)


def _mix_lru_part(lru, gl_ref, wo_ref):
    n2 = _rms(lru, gl_ref[...]).astype(BF16)
    return jnp.dot(n2, wo_ref[D_S5:, :], preferred_element_type=F32)


def _lru_mix_body(xl_ref, gate_ref, ys_ref, u_ref, x_ref,
                  cw_ref, cb_ref, wa_ref, wx_ref, ba_ref, bx_ref, lam_ref,
                  dsk_ref, wg_ref, bg_ref, gs_ref, gl_ref, wo_ref,
                  o_ref, hl_ref, xbuf_ref, carry_ref, lru_ref):
    @pl.when(pl.program_id(1) == 0)
    def _():
        xbuf_ref[0:8, :] = jnp.zeros((8, D_LRU), F32)
        carry_ref[...] = jnp.zeros((8, D_LRU), F32)

    o_ref[...] = _mix_s5_part(ys_ref[...], u_ref[...], x_ref[...],
                              dsk_ref, wg_ref, bg_ref, gs_ref, wo_ref)
    hl_ref[0] = _lru_tile(xl_ref, gate_ref, cw_ref, cb_ref, wa_ref, wx_ref, ba_ref, bx_ref, lam_ref,
                          lru_ref, xbuf_ref, carry_ref)
    o_ref[...] += _mix_lru_part(lru_ref[...], gl_ref, wo_ref)


def _lru_mix_prompt(proj_p, ys_p, x1_p, cw, cb, wa, wx, ba, bx, lam, dsk, wg, bg, gs, gl, wo,
                    *, nb, seq, tt=512):
    nt = seq // tt
    rows = lambda c, col: pl.BlockSpec((tt, c), lambda b, t: (b * nt + t, col))
    once = lambda shape: pl.BlockSpec(shape, lambda b, t: (0,) * len(shape),
                                      pipeline_mode=pl.Buffered(1))
    return pl.pallas_call(
        _lru_mix_body,
        grid=(nb, nt),
        in_specs=[
            rows(D_LRU, 1), rows(D_LRU, 2),
            rows(D_S5, 0), rows(D_S5, 0), rows(D_MODEL, 0),
            once((CONV_W, D_LRU)), once((1, D_LRU)),
            once((D_LRU // 256, 256, 256)), once((D_LRU // 256, 256, 256)),
            once((1, D_LRU)), once((1, D_LRU)), once((1, D_LRU)),
            once((1, D_S5)), once((D_S5, D_S5)), once((1, D_S5)), once((1, D_S5)), once((1, D_LRU)),
            once((D_MODEL, D_MODEL)),
        ],
        out_specs=(pl.BlockSpec((tt, D_MODEL), lambda b, t: (b * nt + t, 0)),
                   pl.BlockSpec((1, 1, D_LRU), lambda b, t: (b, 0, 0))),
        out_shape=(jax.ShapeDtypeStruct((nb * seq, D_MODEL), F32),
                   jax.ShapeDtypeStruct((nb, 1, D_LRU), F32)),
        scratch_shapes=[pltpu.VMEM((tt + 8, D_LRU), F32), pltpu.VMEM((8, D_LRU), F32),
                        pltpu.VMEM((tt, D_LRU), F32)],
        compiler_params=pltpu.CompilerParams(
            dimension_semantics=("parallel", "arbitrary"), vmem_limit_bytes=VMEM_LIMIT),
        name="lru_mix_prompt",
    )(proj_p, proj_p, ys_p, proj_p, x1_p, cw, cb, wa, wx, ba, bx, lam, dsk, wg, bg, gs, gl, wo)


def _lru_decode_body(xl_ref, gate_ref, conv_ref, h0_ref, cw_ref, cb_ref,
                     wa_ref, wx_ref, ba_ref, bx_ref, lam_ref, o_ref, h_ref, buf_ref):
    x = xl_ref[...]
    cw = cw_ref[...]
    c0, c1, c2 = (conv_ref[:, k, :] for k in range(CONV_W - 1))
    xc = cb_ref[...] + c0 * cw[0:1] + c1 * cw[1:2] + c2 * cw[2:3] + x * cw[3:4]
    a, b = _lru_gates(xc, wa_ref, wx_ref, ba_ref[...], bx_ref[...], lam_ref[...])
    h = a * h0_ref[...] + b
    h_ref[...] = h
    o_ref[...] = h * gate_ref[...]
    for k, rows in enumerate((c1, c2, x)):
        buf_ref[:, k, :] = rows


def _lru_decode(proj_d, conv0, h0, cw, cb, wa, wx, ba, bx, lam):
    nd = proj_d.shape[0]
    full = lambda r: pl.BlockSpec((r, D_LRU), lambda i: (0, 0))
    conv = pl.BlockSpec((nd, CONV_W - 1, D_LRU), lambda i: (0, 0, 0))
    wspec = pl.BlockSpec((D_LRU // 256, 256, 256), lambda i: (0, 0, 0))
    return pl.pallas_call(
        _lru_decode_body,
        grid=(1,),
        in_specs=[
            pl.BlockSpec((nd, D_LRU), lambda i: (0, 1)),
            pl.BlockSpec((nd, D_LRU), lambda i: (0, 2)),
            conv, full(nd),
            full(CONV_W), full(1), wspec, wspec, full(1), full(1), full(1),
        ],
        out_specs=(full(nd), full(nd), conv),
        out_shape=(jax.ShapeDtypeStruct((nd, D_LRU), F32),
                   jax.ShapeDtypeStruct((nd, D_LRU), F32),
                   jax.ShapeDtypeStruct((nd, CONV_W - 1, D_LRU), F32)),
        name="lru_decode",
    )(proj_d, proj_d, conv0, h0, cw, cb, wa, wx, ba, bx, lam)


def _mix_decode_body(ys_ref, u_ref, lru_ref, x_ref, dsk_ref, wg_ref, bg_ref, gs_ref, gl_ref,
                     wo_ref, o_ref):
    o_ref[...] = (_mix_s5_part(ys_ref[...], u_ref[...], x_ref[...],
                               dsk_ref, wg_ref, bg_ref, gs_ref, wo_ref)
                  + _mix_lru_part(lru_ref[...], gl_ref, wo_ref))


def _mix_decode(ys_d, proj_d, lru_d, x1_d, dsk, wg, bg, gs, gl, wo):
    nd = x1_d.shape[0]
    full = lambda r, c: pl.BlockSpec((r, c), lambda i: (0, 0))
    return pl.pallas_call(
        _mix_decode_body,
        grid=(1,),
        in_specs=[full(nd, D_S5), full(nd, D_S5), full(nd, D_LRU), full(nd, D_MODEL),
                  full(1, D_S5), full(D_S5, D_S5), full(1, D_S5), full(1, D_S5), full(1, D_LRU),
                  full(D_MODEL, D_MODEL)],
        out_specs=full(nd, D_MODEL),
        out_shape=jax.ShapeDtypeStruct((nd, D_MODEL), F32),
        compiler_params=pltpu.CompilerParams(vmem_limit_bytes=VMEM_LIMIT),
        name="mix_decode",
    )(ys_d, proj_d, lru_d, x1_d, dsk, wg, bg, gs, gl, wo)


def _unpair(h, nb):
    tiles = S5_GROUPS // GROUPS_PER_TILE
    h5 = h.reshape(tiles, PAIRS_PER_TILE, nb, 2, S5_STATE)
    return jnp.transpose(h5, (2, 0, 3, 1, 4)).reshape(nb, S5_GROUPS, S5_STATE)


def _block_diag4(w):
    w4 = w.reshape(LRU_HEADS // 4, 4, LRU_HEAD_DIM, LRU_HEAD_DIM)
    eye = jnp.eye(4, dtype=w.dtype)
    return jnp.einsum("kaij,ab->kaibj", w4, eye).reshape(LRU_HEADS // 4, 256, 256)


def kernel(x_prompt, x_sample, state_s5_re, state_s5_im, state_lru_h, state_lru_conv, g_ffn1, w1_a, w3_a, w2_a, g_mix, w_in, lam_re, lam_im, log_dt, b_re, b_im, c_re, c_im, d_skip, w_glu, b_glu, conv_w, conv_b, w_a, b_a, w_x, b_x, lam_l, g_out_s5, g_out_lru, w_out, g_ffn2, w1_b, w3_b, w2_b, g_final):
    nb, seq, _ = x_prompt.shape
    nd = x_sample.shape[0]
    n_p = nb * seq
    nk = seq // CHUNK
    g, p = S5_GROUPS, S5_STATE
    row = lambda v: v.reshape(1, -1)

    xp = x_prompt.reshape(n_p, D_MODEL)
    xd = x_sample.reshape(nd, D_MODEL)

    n_i, n_f = n_p // 1024, D_FF // 512
    first = _ffn(
        xp, xd, row(g_ffn1[0]), w1_a[0], w3_a[0], w2_a[0], n_tiles=1,
        casts=(_cast_job(w1_b[0], n_i, n_f, f_div=2), _cast_job(w3_b[0], n_i, n_f, f_div=2),
               _row_cast_job(w2_b[0], n_i, n_f, f_div=2), _flat_cast_job(w_in[0], 16)))
    w1_a16, w3_a16, w2_a16 = first[2:5]
    w_in16 = first[8]
    x1_p, x1_d, w1_b16, w3_b16, w2_b16 = _ffn(
        xp, xd, row(g_ffn1[0]), w1_a16, w3_a16, w2_a16, first_tile=1, n_tiles=n_i - 1,
        casts=(_cast_job(w1_b[0], n_i, n_f, i0=1), _cast_job(w3_b[0], n_i, n_f, i0=1),
               _row_cast_job(w2_b[0], n_i, n_f, i0=1)),
        prev=first[:2] + first[5:8])
    proj_p, proj_d, w_out16, w_glu16 = _inproj(
        x1_p, x1_d, row(g_mix[0]), w_in16,
        casts=(_cast_job(w_out[0], n_i, D_IN // 1024, bc=1024),
               _cast_job(w_glu[0], n_i, D_IN // 1024, bc=1024)))

    lmat, cpre, cpim, bd, cd, ar, ai, lr, li = _s5_prep(
        lam_re[0], lam_im[0], log_dt[0], b_re[0], b_im[0], c_re[0], c_im[0])
    ys_p, hf_re, hf_im = _s5_prompt(proj_p, lmat, cpre, cpim, ar, ai, nb=nb, seq=seq)

    ys_d, hd_re, hd_im = _s5_decode(proj_d, state_s5_re[0], state_s5_im[0], bd, cd, lr, li)

    wa_bd = _block_diag4(w_a[0]).astype(BF16)
    wx_bd = _block_diag4(w_x[0]).astype(BF16)
    lru_args = (conv_w[0], row(conv_b[0]), wa_bd, wx_bd, row(b_a[0]), row(b_x[0]), row(lam_l[0]))
    lru_d, hl_d, buf_d = _lru_decode(proj_d, state_lru_conv[0], state_lru_h[0], *lru_args)

    mix_args = (row(d_skip[0]), w_glu16, row(b_glu[0]), row(g_out_s5[0]), row(g_out_lru[0]), w_out16)
    x2_p, hl_p = _lru_mix_prompt(proj_p, ys_p, x1_p, *lru_args, *mix_args, nb=nb, seq=seq)
    x2_d = _mix_decode(ys_d, proj_d, lru_d, x1_d, *mix_args)
    y_p, y_d = _ffn(x2_p, x2_d, row(g_ffn2[0]), w1_b16, w3_b16, w2_b16, row(g_final))

    tail_p = proj_p.reshape(nb, seq, -1)[:, seq - (CONV_W - 1):, D_S5:D_S5 + D_LRU]
    return (
        y_p.reshape(nb, seq, D_MODEL),
        y_d.reshape(nd, 1, D_MODEL),
        _unpair(hf_re, nb)[None],
        _unpair(hf_im, nb)[None],
        hl_p.reshape(1, nb, D_LRU),
        tail_p[None],
        hd_re[None],
        hd_im[None],
        hl_d[None],
        buf_d[None],
    )
```

```python
import functools

import jax
import jax.numpy as jnp
from jax import lax
from jax.experimental import pallas as pl
from jax.experimental.pallas import tpu as pltpu

F32 = jnp.float32
BF16 = jnp.bfloat16

D_MODEL = 2048
D_S5 = 1024
S5_GROUP = 16
S5_GROUPS = 64
S5_STATE = 64
D_LRU = 1024
LRU_HEADS = 16
LRU_HEAD_DIM = 64
CONV_W = 4
LRU_C = 8.0
D_FF = 5632
D_IN = D_S5 + 2 * D_LRU
EPS = 1e-6

CHUNK = 16
CW = CHUNK * S5_GROUP

VMEM_LIMIT = 58 * 1024 * 1024
FFN_VMEM_LIMIT = 60 * 1024 * 1024

NN = (((1,), (0,)), ((), ()))
NT = (((1,), (1,)), ((), ()))


def _rms(x, g):
    return x * lax.rsqrt(jnp.mean(x * x, axis=-1, keepdims=True) + EPS) * g


def _split(x):
    hi = x.astype(BF16)
    lo = (x - hi.astype(F32)).astype(BF16)
    return hi, lo


def _dot3(a, b, dims=NN):
    ah, al = _split(a)
    bh, bl = _split(b)
    d = functools.partial(lax.dot_general, dimension_numbers=dims, preferred_element_type=F32)
    return d(ah, bh) + d(al, bh) + d(ah, bl)


def _gelu(x):
    return jax.nn.gelu(x, approximate=True)


def _ffn_body(*refs, final_norm, convert, n_casts, n_prev):
    n_in = 7 if final_norm else 6
    xp_ref, xd_ref, g_ref, w1_ref, w3_ref, w2_ref = refs[:6]
    gf_ref = refs[6] if final_norm else None
    cast_in = refs[n_in:n_in + n_casts]
    outs = refs[n_in + n_casts + n_prev:-1]
    op_ref, od_ref = outs[:2]
    wcopy = outs[2:5] if convert else ()
    cast_out = outs[2 + len(wcopy):]
    xn_ref = refs[-1]
    f = pl.program_id(1)
    tm = xp_ref.shape[0]

    for src_ref, dst_ref in zip(cast_in, cast_out):
        dst_ref[...] = src_ref[...].astype(BF16)

    @pl.when(f == 0)
    def _():
        for x_ref, o_ref, rows in ((xp_ref, op_ref, slice(0, tm)), (xd_ref, od_ref, slice(tm, None))):
            x = x_ref[...]
            xn_ref[rows, :] = _rms(x, g_ref[...]).astype(BF16)
            o_ref[...] = x

    if convert:
        w1, w3, w2 = (w_ref[...].astype(BF16) for w_ref in (w1_ref, w3_ref, w2_ref))
        for dst_ref, w in zip(wcopy, (w1, w3, w2)):
            dst_ref[...] = w
    else:
        w1, w3, w2 = w1_ref[...], w3_ref[...], w2_ref[...]

    xn = xn_ref[...]
    a = jnp.dot(xn, w1, preferred_element_type=F32)
    b = jnp.dot(xn, w3, preferred_element_type=F32)
    h = (a * jax.nn.sigmoid(a) * b).astype(BF16)
    upd = 0.5 * jnp.dot(h, w2, preferred_element_type=F32)
    op_ref[...] += upd[0:tm]
    od_ref[...] += upd[tm:]

    if final_norm:
        @pl.when(f == pl.num_programs(1) - 1)
        def _():
            op_ref[...] = _rms(op_ref[...], gf_ref[...])
            od_ref[...] = _rms(od_ref[...], gf_ref[...])


def _cast_job(w, n_i, n_f, bc=512, i0=0, f_div=1):
    rows, cols = w.shape
    br = rows // n_i
    n_cb = cols // bc
    assert br * n_i == rows and bc * n_cb == cols and n_cb <= n_f and br % 16 == 0
    return w, pl.BlockSpec(
        (br, bc), lambda i, f: (jnp.minimum(i + i0, n_i - 1),
                                jnp.where(i + i0 < n_i, jnp.minimum(f // f_div, n_cb - 1),
                                          n_cb - 1)))


def _row_cast_job(w, n_i, n_f, i0=0, f_div=1):
    rows, cols = w.shape
    br = rows // (n_i * n_f)
    assert br * n_i * n_f == rows and br % 16 == 0
    return w, pl.BlockSpec((br, cols), lambda i, f: ((i + i0) * n_f + f // f_div, 0))


def _flat_cast_job(w, n_blocks):
    rows, cols = w.shape
    br = rows // n_blocks
    assert br * n_blocks == rows and br % 16 == 0
    return w, pl.BlockSpec((br, cols), lambda i, f: (jnp.minimum(f, n_blocks - 1), 0))


def _ffn(xp, xd, g, w1, w3, w2, g_final=None, *, casts=(), first_tile=0, n_tiles=None, prev=(),
         tm=1024):
    n_p, nd = xp.shape[0], xd.shape[0]
    all_tiles = n_p // tm
    n_tiles = all_tiles if n_tiles is None else n_tiles
    td = nd // all_tiles
    final_norm = g_final is not None
    convert = w1.dtype == F32
    tf = 256 if convert else 512
    pspec = pl.BlockSpec((tm, D_MODEL), lambda i, f: (i + first_tile, 0))
    dspec = pl.BlockSpec((td, D_MODEL), lambda i, f: (i + first_tile, 0))
    xspec = pspec if n_tiles > 1 else pl.BlockSpec(
        (tm, D_MODEL), lambda i, f: (i + first_tile, 0), pipeline_mode=pl.Buffered(1))
    w13spec = pl.BlockSpec((D_MODEL, tf), lambda i, f: (0, f))
    w2spec = pl.BlockSpec((tf, D_MODEL), lambda i, f: (f, 0))
    in_specs = [xspec, dspec, pl.BlockSpec((1, D_MODEL), lambda i, f: (0, 0)),
                w13spec, w13spec, w2spec]
    args = [xp, xd, g, w1, w3, w2]
    if final_norm:
        in_specs.append(pl.BlockSpec((1, D_MODEL), lambda i, f: (0, 0)))
        args.append(g_final)
    in_specs += [spec for _, spec in casts]
    args += [w for w, _ in casts]
    n_in = len(args)
    in_specs += [pl.BlockSpec(memory_space=pl.ANY)] * len(prev)
    args += list(prev)
    sd = jax.ShapeDtypeStruct
    out_specs = [pspec, dspec]
    out_shape = [sd((n_p, D_MODEL), F32), sd((nd, D_MODEL), F32)]
    if convert:
        out_specs += [w13spec, w13spec, w2spec]
        out_shape += [sd(w.shape, BF16) for w in (w1, w3, w2)]
    out_specs += [spec for _, spec in casts]
    out_shape += [sd(w.shape, BF16) for w, _ in casts]
    assert len(prev) in (0, len(out_shape))
    return pl.pallas_call(
        functools.partial(_ffn_body, final_norm=final_norm, convert=convert,
                          n_casts=len(casts), n_prev=len(prev)),
        grid=(n_tiles, D_FF // tf),
        in_specs=in_specs,
        out_specs=tuple(out_specs),
        out_shape=tuple(out_shape),
        input_output_aliases={n_in + k: k for k in range(len(prev))},
        scratch_shapes=[pltpu.VMEM((tm + td, D_MODEL), BF16)],
        compiler_params=pltpu.CompilerParams(
            dimension_semantics=("arbitrary", "arbitrary"), vmem_limit_bytes=FFN_VMEM_LIMIT),
        name="ffn_final" if final_norm else ("ffn_first" if convert else "ffn"),
    )(*args)


def _inproj_body(*refs, n_tiles, nd, n_casts):
    xp_ref, xd_ref, g_ref, w_ref = refs[:4]
    cast_in = refs[4:4 + n_casts]
    op_ref, od_ref = refs[4 + n_casts:6 + n_casts]
    cast_out = refs[6 + n_casts:6 + 2 * n_casts]
    xn_ref = refs[-1]
    i = pl.program_id(0)
    j = pl.program_id(1)
    nj = pl.num_programs(1)

    for src_ref, dst_ref in zip(cast_in, cast_out):
        dst_ref[...] = src_ref[...].astype(BF16)

    def run(x_ref, o_ref, rows):
        @pl.when(j == 0)
        def _():
            xn_ref[0:rows, :] = _rms(x_ref[...], g_ref[...]).astype(BF16)

        @pl.when(j < nj - 1)
        def _():
            o_ref[...] = jnp.dot(xn_ref[0:rows, :], w_ref[...], preferred_element_type=F32)

        @pl.when(j == nj - 1)
        def _():
            o_ref[...] = _gelu(jnp.dot(xn_ref[0:rows, :], w_ref[...],
                                       preferred_element_type=F32))

    @pl.when(i < n_tiles)
    def _():
        run(xp_ref, op_ref, xp_ref.shape[0])

    @pl.when(i == n_tiles)
    def _():
        run(xd_ref, od_ref, nd)


def _inproj(xp, xd, g, w, *, casts=(), tm=1024, tn=1024):
    n_p, nd = xp.shape[0], xd.shape[0]
    n_tiles = n_p // tm
    d_out = w.shape[1]
    nj = d_out // tn
    return pl.pallas_call(
        functools.partial(_inproj_body, n_tiles=n_tiles, nd=nd, n_casts=len(casts)),
        grid=(n_tiles + 1, nj),
        in_specs=[
            pl.BlockSpec((tm, D_MODEL), lambda i, j: (jnp.minimum(i, n_tiles - 1), 0)),
            pl.BlockSpec((nd, D_MODEL), lambda i, j: (0, 0)),
            pl.BlockSpec((1, D_MODEL), lambda i, j: (0, 0)),
            pl.BlockSpec((D_MODEL, tn), lambda i, j: (0, j)),
        ] + [spec for _, spec in casts],
        out_specs=(
            pl.BlockSpec((tm, tn), lambda i, j: (jnp.minimum(i, n_tiles - 1),
                                                 jnp.where(i < n_tiles, j, nj - 1))),
            pl.BlockSpec((nd, tn), lambda i, j: (0, jnp.where(i < n_tiles, 0, j))),
        ) + tuple(spec for _, spec in casts),
        out_shape=(jax.ShapeDtypeStruct((n_p, d_out), F32),
                   jax.ShapeDtypeStruct((nd, d_out), F32))
        + tuple(jax.ShapeDtypeStruct(cw.shape, BF16) for cw, _ in casts),
        scratch_shapes=[pltpu.VMEM((tm, D_MODEL), BF16)],
        compiler_params=pltpu.CompilerParams(
            dimension_semantics=("arbitrary", "arbitrary"), vmem_limit_bytes=VMEM_LIMIT),
        name="inproj",
    )(xp, xd, g, w, *[cw for cw, _ in casts])


GROUPS_PER_TILE = 128 // S5_GROUP
PAIRS_PER_TILE = GROUPS_PER_TILE // 2
L_ROWS = CW + 2 * S5_STATE


def _lam_bar(lam_re, lam_im, log_dt):
    dt = jnp.exp(log_dt)
    mag = jnp.exp(lam_re * dt)
    ang = lam_im * dt
    return mag * jnp.cos(ang), mag * jnp.sin(ang)


def _s5_prep_body(lre_ref, lim_ref, ldt_ref, lrec_ref, limc_ref, ldtc_ref,
                  bre_ref, bim_ref, cre_ref, cim_ref,
                  l_ref, cpre_ref, cpim_ref, bd_ref, cd_ref, ar_ref, ai_ref, lr_ref, li_ref,
                  bp_ref):
    gb, half = GROUPS_PER_TILE, PAIRS_PER_TILE
    p = S5_STATE
    lo, hi = slice(0, p), slice(p, 2 * p)
    lam_re = lre_ref[...]
    lam_im = lim_ref[...]
    lbr, lbi = _lam_bar(lam_re, lam_im, ldt_ref[...])
    lbr_c, lbi_c = _lam_bar(lrec_ref[...], limc_ref[...], ldtc_ref[...])
    lr_ref[...] = jnp.broadcast_to(lbr_c, lr_ref.shape)
    li_ref[...] = jnp.broadcast_to(lbi_c, li_ref.shape)
    nr = lbr - 1.0
    den = lam_re * lam_re + lam_im * lam_im
    cr = (nr * lam_re + lbi * lam_im) / den
    ci = (lbi * lam_re - nr * lam_im) / den
    b_re = bre_ref[...]
    b_im = bim_ref[...]
    bbr = cr * b_re - ci * b_im
    bbi = cr * b_im + ci * b_re
    bd_ref[:, :, lo] = bbr
    bd_ref[:, :, hi] = bbi
    c_re = cre_ref[...]
    c_im = cim_ref[...]
    cd_ref[:, :, lo] = c_re
    cd_ref[:, :, hi] = -c_im

    zeros = jnp.zeros((half, S5_GROUP, p), F32)
    pr = jnp.ones_like(lbr)
    pi = jnp.zeros_like(lbr)
    for d in range(CHUNK):
        rows = slice(d * S5_GROUP, (d + 1) * S5_GROUP)
        back = slice((CHUNK - 1 - d) * S5_GROUP, (CHUNK - d) * S5_GROUP)
        bp_ref[:, back, lo] = bbr * pr - bbi * pi
        bp_ref[:, back, hi] = bbr * pi + bbi * pr
        pr, pi = pr * lbr - pi * lbi, pr * lbi + pi * lbr
        cp_r = c_re * pr - c_im * pi
        cp_i = -(c_re * pi + c_im * pr)
        cpre_ref[0:half, rows, lo] = cp_r[0:half]
        cpre_ref[0:half, rows, hi] = zeros
        cpre_ref[half:gb, rows, lo] = zeros
        cpre_ref[half:gb, rows, hi] = cp_r[half:gb]
        cpim_ref[0:half, rows, lo] = cp_i[0:half]
        cpim_ref[0:half, rows, hi] = zeros
        cpim_ref[half:gb, rows, lo] = zeros
        cpim_ref[half:gb, rows, hi] = cp_i[half:gb]

    qr, qi = pr, pi
    for r in range(8):
        ar_ref[:, r:r + 1, lo] = qr[0:half]
        ar_ref[:, r:r + 1, hi] = qr[half:gb]
        ai_ref[:, r:r + 1, lo] = qi[0:half]
        ai_ref[:, r:r + 1, hi] = qi[half:gb]
        qr, qi = qr * pr - qi * pi, qr * pi + qi * pr

    lane = lax.broadcasted_iota(jnp.int32, (S5_GROUP, 128), 1)
    for j in range(gb):
        w = _dot3(cd_ref[j], bp_ref[j], NT)
        w0, w1 = w[:, :128], w[:, 128:]
        for t in range(CHUNK):
            rows = slice(t * S5_GROUP, (t + 1) * S5_GROUP)
            shift = (CHUNK - 1 - t) * S5_GROUP
            keep = 128 - shift % 128
            if shift == 0:
                left, right = w0, w1
            elif shift < 128:
                r0 = pltpu.roll(w0, keep, axis=1)
                r1 = pltpu.roll(w1, keep, axis=1)
                left = jnp.where(lane < keep, r0, r1)
                right = jnp.where(lane < keep, r1, 0.0)
            elif shift == 128:
                left, right = w1, jnp.zeros_like(w1)
            else:
                left = jnp.where(lane < keep, pltpu.roll(w1, keep, axis=1), 0.0)
                right = jnp.zeros_like(w1)
            l_ref[j, rows, 0:128] = left
            l_ref[j, rows, 128:256] = right
        l_ref[j, CW:L_ROWS, :] = bp_ref[j].T


def _s5_prep(lam_re, lam_im, log_dt, b_re, b_im, c_re, c_im):
    g, p, gb, half = S5_GROUPS, S5_STATE, GROUPS_PER_TILE, PAIRS_PER_TILE
    ldt2 = jnp.broadcast_to(log_dt.reshape(g, 1), (g, p))
    rows = lambda v: v.reshape(g, 1, p)
    cols = lambda v: v.reshape(g, p, 1)
    bre = jnp.transpose(b_re, (0, 2, 1))
    bim = jnp.transpose(b_im, (0, 2, 1))
    sd = jax.ShapeDtypeStruct
    blk = lambda n, r, c: pl.BlockSpec((n, r, c), lambda i: (i, 0, 0))
    return pl.pallas_call(
        _s5_prep_body,
        grid=(g // gb,),
        in_specs=[blk(gb, 1, p)] * 3 + [blk(gb, p, 1)] * 3 + [blk(gb, S5_GROUP, p)] * 4,
        out_specs=(
            blk(gb, L_ROWS, CW), blk(gb, CW, 2 * p), blk(gb, CW, 2 * p),
            blk(gb, S5_GROUP, 2 * p), blk(gb, S5_GROUP, 2 * p),
            blk(half, 8, 2 * p), blk(half, 8, 2 * p), blk(gb, p, 128), blk(gb, p, 128),
        ),
        out_shape=(
            sd((g, L_ROWS, CW), F32),
            sd((g, CW, 2 * p), F32),
            sd((g, CW, 2 * p), F32),
            sd((g, S5_GROUP, 2 * p), F32),
            sd((g, S5_GROUP, 2 * p), F32),
            sd((g // 2, 8, 2 * p), F32),
            sd((g // 2, 8, 2 * p), F32),
            sd((g, p, 128), F32),
            sd((g, p, 128), F32),
        ),
        scratch_shapes=[pltpu.VMEM((gb, CW, 2 * p), F32)],
        compiler_params=pltpu.CompilerParams(dimension_semantics=("parallel",)),
        name="s5_prep",
    )(rows(lam_re), rows(lam_im), rows(ldt2), cols(lam_re), cols(lam_im), cols(ldt2),
      bre, bim, c_re, c_im)


def _s5p_body(u_ref, l_ref, cpre_ref, cpim_ref, ar_ref, ai_ref, y_ref, hre_ref, him_ref,
              ut_ref, yt_ref, *, nb, nk):
    gb, half = GROUPS_PER_TILE, PAIRS_PER_TILE
    p = S5_STATE
    nrow = nb * nk
    d = functools.partial(jnp.dot, preferred_element_type=F32)

    for t in range(CHUNK):
        xt = u_ref[pl.ds(t, nrow, stride=CHUNK), :].T
        for j in range(gb):
            ut_ref[j, t * S5_GROUP:(t + 1) * S5_GROUP, :] = xt[j * S5_GROUP:(j + 1) * S5_GROUP, :]

    s_re, s_im = [], []
    for j in range(gb):
        r = d(l_ref[j].astype(BF16), ut_ref[j].astype(BF16))
        yt_ref[j] = r[0:CW]
        s_re.append(r[CW:CW + p])
        s_im.append(r[CW + p:L_ROWS])

    nblk = nrow // 8
    row8 = lax.broadcasted_iota(jnp.int32, (nblk, 8, 2 * p), 1)
    rows = lax.broadcasted_iota(jnp.int32, (nrow, 2 * p), 0) & (nk - 1)
    dnt = functools.partial(lax.dot_general, dimension_numbers=NT, preferred_element_type=F32)
    for q in range(half):
        re = jnp.concatenate([s_re[q], s_re[q + half]], axis=0).T.reshape(nblk, 8, 2 * p)
        im = jnp.concatenate([s_im[q], s_im[q + half]], axis=0).T.reshape(nblk, 8, 2 * p)
        for sh in (1, 2, 4):
            keep = row8 >= sh
            rs = jnp.where(keep, pltpu.roll(re, sh, axis=1), 0.0)
            js = jnp.where(keep, pltpu.roll(im, sh, axis=1), 0.0)
            ar = ar_ref[q, sh - 1:sh, :]
            ai = ai_ref[q, sh - 1:sh, :]
            re, im = re + ar * rs - ai * js, im + ar * js + ai * rs
        pw_r, pw_i = ar_ref[q], ai_ref[q]
        out_r, out_i = [], []
        for k in range(nblk):
            hr, hi = re[k], im[k]
            if k % (nk // 8):
                hr, hi = hr + pw_r * cr - pw_i * ci, hi + pw_r * ci + pw_i * cr
            cr, ci = hr[7:8, :], hi[7:8, :]
            out_r.append(hr)
            out_i.append(hi)
            if (k + 1) % (nk // 8) == 0:
                b = k // (nk // 8)
                hre_ref[q, b:b + 1, :] = cr
                him_ref[q, b:b + 1, :] = ci
        re = jnp.concatenate(out_r, axis=0)
        im = jnp.concatenate(out_i, axis=0)
        pre = jnp.where(rows >= 1, pltpu.roll(re, 1, axis=0), 0.0).astype(BF16)
        pim = jnp.where(rows >= 1, pltpu.roll(im, 1, axis=0), 0.0).astype(BF16)
        for j in (q, q + half):
            yt_ref[j] = (yt_ref[j] + dnt(cpre_ref[j].astype(BF16), pre)
                         + dnt(cpim_ref[j].astype(BF16), pim))

    for t in range(CHUNK):
        yt = jnp.concatenate(
            [yt_ref[j, t * S5_GROUP:(t + 1) * S5_GROUP, :] for j in range(gb)], axis=0)
        y_ref[pl.ds(t, nrow, stride=CHUNK), :] = yt.T


def _s5_prompt(proj_p, lmat, cpre, cpim, ar, ai, *, nb, seq):
    g, gb, half = S5_GROUPS, GROUPS_PER_TILE, PAIRS_PER_TILE
    n_p = nb * seq
    nk = seq // CHUNK
    blk = lambda n, r, c: pl.BlockSpec((n, r, c), lambda i: (i, 0, 0))
    return pl.pallas_call(
        functools.partial(_s5p_body, nb=nb, nk=nk),
        grid=(g // gb,),
        in_specs=[pl.BlockSpec((n_p, 128), lambda i: (0, i)),
                  blk(gb, L_ROWS, CW), blk(gb, CW, 2 * S5_STATE), blk(gb, CW, 2 * S5_STATE),
                  blk(half, 8, 2 * S5_STATE), blk(half, 8, 2 * S5_STATE)],
        out_specs=(pl.BlockSpec((n_p, 128), lambda i: (0, i)),
                   blk(half, nb, 2 * S5_STATE), blk(half, nb, 2 * S5_STATE)),
        out_shape=(jax.ShapeDtypeStruct((n_p, D_S5), F32),
                   jax.ShapeDtypeStruct((g // 2, nb, 2 * S5_STATE), F32),
                   jax.ShapeDtypeStruct((g // 2, nb, 2 * S5_STATE), F32)),
        scratch_shapes=[pltpu.VMEM((gb, CW, nb * nk), F32), pltpu.VMEM((gb, CW, nb * nk), F32)],
        compiler_params=pltpu.CompilerParams(
            dimension_semantics=("parallel",), vmem_limit_bytes=VMEM_LIMIT),
        name="s5_prompt",
    )(proj_p, lmat, cpre, cpim, ar, ai)


def _s5d_body(u_ref, hre_ref, him_ref, bd_ref, cd_ref, lr_ref, li_ref, y_ref, ore_ref, oim_ref):
    p = S5_STATE
    gb = GROUPS_PER_TILE
    tn = (((0,), (0,)), ((), ()))
    ut = u_ref[...].T
    yts = []
    for j in range(gb):
        ug = ut[j * S5_GROUP:(j + 1) * S5_GROUP, :].astype(BF16)
        bu = lax.dot_general(bd_ref[j].astype(BF16), ug, tn,
                             preferred_element_type=F32)
        h0r, h0i = hre_ref[j], him_ref[j]
        lbr, lbi = lr_ref[j], li_ref[j]
        hr = lbr * h0r - lbi * h0i + bu[0:p]
        hi = lbr * h0i + lbi * h0r + bu[p:2 * p]
        ore_ref[j] = hr
        oim_ref[j] = hi
        h = jnp.concatenate([hr, hi], axis=0).astype(BF16)
        yts.append(jnp.dot(cd_ref[j].astype(BF16), h, preferred_element_type=F32))
    y_ref[...] = jnp.concatenate(yts, axis=0).T


def _s5_decode(proj_d, h0_re, h0_im, bd, cd, lr, li):
    g, gb, p = S5_GROUPS, GROUPS_PER_TILE, S5_STATE
    nbatch = proj_d.shape[0]
    blk = lambda r, c: pl.BlockSpec((gb, r, c), lambda i: (i, 0, 0))
    cols = pl.BlockSpec((nbatch, 128), lambda i: (0, i))
    return pl.pallas_call(
        _s5d_body,
        grid=(g // gb,),
        in_specs=[cols, blk(p, nbatch), blk(p, nbatch), blk(S5_GROUP, 2 * p), blk(S5_GROUP, 2 * p),
                  blk(p, nbatch), blk(p, nbatch)],
        out_specs=(cols, blk(p, nbatch), blk(p, nbatch)),
        out_shape=(jax.ShapeDtypeStruct((nbatch, D_S5), F32),
                   jax.ShapeDtypeStruct((g, p, nbatch), F32),
                   jax.ShapeDtypeStruct((g, p, nbatch), F32)),
        compiler_params=pltpu.CompilerParams(dimension_semantics=("parallel",)),
        name="s5_decode",
    )(proj_d, h0_re, h0_im, bd, cd, lr, li)


def _lru_gates(xc, wa_ref, wx_ref, ba, bx, lam):
    xcb = xc.astype(BF16)
    nblk = D_LRU // 256
    r_parts, i_parts = [], []
    for k in range(nblk):
        xk = xcb[:, k * 256:(k + 1) * 256]
        r_parts.append(jnp.dot(xk, wa_ref[k], preferred_element_type=F32))
        i_parts.append(jnp.dot(xk, wx_ref[k], preferred_element_type=F32))
    r = jax.nn.sigmoid(jnp.concatenate(r_parts, axis=1) + ba)
    i = jax.nn.sigmoid(jnp.concatenate(i_parts, axis=1) + bx)
    z = -lam
    softplus = jnp.maximum(z, 0.0) + jnp.log1p(jnp.exp(-jnp.abs(z)))
    log_a = (-LRU_C * softplus) * r
    a = jnp.exp(log_a)
    v = -jnp.tanh(log_a) * (a * a + 1.0)
    mult = jnp.where(v > 0.0, v * lax.rsqrt(v), 0.0)
    return a, mult * (i * xc)


def _lru_tile(xl_ref, gate_ref, cw_ref, cb_ref, wa_ref, wx_ref, ba_ref, bx_ref, lam_ref,
              o_ref, xbuf_ref, carry_ref):
    tt = xl_ref.shape[0]
    x = xl_ref[...]
    xbuf_ref[8:8 + tt, :] = x
    cw = cw_ref[...]
    xc = (cb_ref[...] + xbuf_ref[5:5 + tt, :] * cw[0:1] + xbuf_ref[6:6 + tt, :] * cw[1:2]
          + xbuf_ref[7:7 + tt, :] * cw[2:3] + x * cw[3:4])
    xbuf_ref[0:8, :] = x[tt - 8:tt, :]

    a, b = _lru_gates(xc, wa_ref, wx_ref, ba_ref[...], bx_ref[...], lam_ref[...])

    nblk = tt // 8
    a3 = a.reshape(nblk, 8, D_LRU)
    b3 = b.reshape(nblk, 8, D_LRU)
    row = lax.broadcasted_iota(jnp.int32, (nblk, 8, D_LRU), 1)
    for sh in (1, 2, 4):
        keep = row >= sh
        bs = jnp.where(keep, pltpu.roll(b3, sh, axis=1), 0.0)
        sa = jnp.where(keep, pltpu.roll(a3, sh, axis=1), 1.0)
        b3 = b3 + a3 * bs
        a3 = a3 * sa
    carry = carry_ref[0:1, :]
    gate = gate_ref[...]
    for k in range(nblk):
        h = b3[k] + a3[k] * carry
        carry = h[7:8, :]
        o_ref[k * 8:(k + 1) * 8, :] = h * gate[k * 8:(k + 1) * 8, :]
    carry_ref[...] = jnp.broadcast_to(carry, (8, D_LRU))
    return carry


def _mix_s5_part(ys, u, x, dsk_ref, wg_ref, bg_ref, gs_ref, wo_ref):
    yy = ys + dsk_ref[...] * u
    g = _gelu(yy)
    z = jnp.dot(g.astype(BF16), wg_ref[...], preferred_element_type=F32) + bg_ref[...]
    s5o = g * jax.nn.sigmoid(z)
    n1 = _rms(s5o, gs_ref[...]).astype(BF16)
    return x + jnp.dot(n1, wo_ref[0:D_S5, :], preferred_element_type=F32)


def _mix_lru_part(lru, gl_ref, wo_ref):
    n2 = _rms(lru, gl_ref[...]).astype(BF16)
    return jnp.dot(n2, wo_ref[D_S5:, :], preferred_element_type=F32)


def _lru_mix_body(xl_ref, gate_ref, ys_ref, u_ref, x_ref,
                  cw_ref, cb_ref, wa_ref, wx_ref, ba_ref, bx_ref, lam_ref,
                  dsk_ref, wg_ref, bg_ref, gs_ref, gl_ref, wo_ref,
                  o_ref, hl_ref, xbuf_ref, carry_ref, lru_ref):
    @pl.when(pl.program_id(1) == 0)
    def _():
        xbuf_ref[0:8, :] = jnp.zeros((8, D_LRU), F32)
        carry_ref[...] = jnp.zeros((8, D_LRU), F32)

    o_ref[...] = _mix_s5_part(ys_ref[...], u_ref[...], x_ref[...],
                              dsk_ref, wg_ref, bg_ref, gs_ref, wo_ref)
    hl_ref[0] = _lru_tile(xl_ref, gate_ref, cw_ref, cb_ref, wa_ref, wx_ref, ba_ref, bx_ref, lam_ref,
                          lru_ref, xbuf_ref, carry_ref)
    o_ref[...] += _mix_lru_part(lru_ref[...], gl_ref, wo_ref)


def _lru_mix_prompt(proj_p, ys_p, x1_p, cw, cb, wa, wx, ba, bx, lam, dsk, wg, bg, gs, gl, wo,
                    *, nb, seq, tt=512):
    nt = seq // tt
    rows = lambda c, col: pl.BlockSpec((tt, c), lambda b, t: (b * nt + t, col))
    once = lambda shape: pl.BlockSpec(shape, lambda b, t: (0,) * len(shape),
                                      pipeline_mode=pl.Buffered(1))
    return pl.pallas_call(
        _lru_mix_body,
        grid=(nb, nt),
        in_specs=[
            rows(D_LRU, 1), rows(D_LRU, 2),
            rows(D_S5, 0), rows(D_S5, 0), rows(D_MODEL, 0),
            once((CONV_W, D_LRU)), once((1, D_LRU)),
            once((D_LRU // 256, 256, 256)), once((D_LRU // 256, 256, 256)),
            once((1, D_LRU)), once((1, D_LRU)), once((1, D_LRU)),
            once((1, D_S5)), once((D_S5, D_S5)), once((1, D_S5)), once((1, D_S5)), once((1, D_LRU)),
            once((D_MODEL, D_MODEL)),
        ],
        out_specs=(pl.BlockSpec((tt, D_MODEL), lambda b, t: (b * nt + t, 0)),
                   pl.BlockSpec((1, 1, D_LRU), lambda b, t: (b, 0, 0))),
        out_shape=(jax.ShapeDtypeStruct((nb * seq, D_MODEL), F32),
                   jax.ShapeDtypeStruct((nb, 1, D_LRU), F32)),
        scratch_shapes=[pltpu.VMEM((tt + 8, D_LRU), F32), pltpu.VMEM((8, D_LRU), F32),
                        pltpu.VMEM((tt, D_LRU), F32)],
        compiler_params=pltpu.CompilerParams(
            dimension_semantics=("parallel", "arbitrary"), vmem_limit_bytes=VMEM_LIMIT),
        name="lru_mix_prompt",
    )(proj_p, proj_p, ys_p, proj_p, x1_p, cw, cb, wa, wx, ba, bx, lam, dsk, wg, bg, gs, gl, wo)


def _lru_decode_body(xl_ref, gate_ref, conv_ref, h0_ref, cw_ref, cb_ref,
                     wa_ref, wx_ref, ba_ref, bx_ref, lam_ref, o_ref, h_ref, buf_ref):
    x = xl_ref[...]
    cw = cw_ref[...]
    c0, c1, c2 = (conv_ref[k] for k in range(CONV_W - 1))
    xc = cb_ref[...] + c0 * cw[0:1] + c1 * cw[1:2] + c2 * cw[2:3] + x * cw[3:4]
    a, b = _lru_gates(xc, wa_ref, wx_ref, ba_ref[...], bx_ref[...], lam_ref[...])
    h = a * h0_ref[...] + b
    h_ref[...] = h
    o_ref[...] = h * gate_ref[...]
    for k, rows in enumerate((c1, c2, x)):
        buf_ref[k] = rows


def _lru_decode(proj_d, conv0, h0, cw, cb, wa, wx, ba, bx, lam):
    nd = proj_d.shape[0]
    full = lambda r: pl.BlockSpec((r, D_LRU), lambda i: (0, 0))
    conv = pl.BlockSpec((CONV_W - 1, nd, D_LRU), lambda i: (0, 0, 0))
    wspec = pl.BlockSpec((D_LRU // 256, 256, 256), lambda i: (0, 0, 0))
    return pl.pallas_call(
        _lru_decode_body,
        grid=(1,),
        in_specs=[
            pl.BlockSpec((nd, D_LRU), lambda i: (0, 1)),
            pl.BlockSpec((nd, D_LRU), lambda i: (0, 2)),
            conv, full(nd),
            full(CONV_W), full(1), wspec, wspec, full(1), full(1), full(1),
        ],
        out_specs=(full(nd), full(nd), conv),
        out_shape=(jax.ShapeDtypeStruct((nd, D_LRU), F32),
                   jax.ShapeDtypeStruct((nd, D_LRU), F32),
                   jax.ShapeDtypeStruct((CONV_W - 1, nd, D_LRU), F32)),
        name="lru_decode",
    )(proj_d, proj_d, conv0, h0, cw, cb, wa, wx, ba, bx, lam)


def _mix_decode_body(ys_ref, u_ref, lru_ref, x_ref, dsk_ref, wg_ref, bg_ref, gs_ref, gl_ref,
                     wo_ref, o_ref):
    o_ref[...] = (_mix_s5_part(ys_ref[...], u_ref[...], x_ref[...],
                               dsk_ref, wg_ref, bg_ref, gs_ref, wo_ref)
                  + _mix_lru_part(lru_ref[...], gl_ref, wo_ref))


def _mix_decode(ys_d, proj_d, lru_d, x1_d, dsk, wg, bg, gs, gl, wo):
    nd = x1_d.shape[0]
    full = lambda r, c: pl.BlockSpec((r, c), lambda i: (0, 0))
    return pl.pallas_call(
        _mix_decode_body,
        grid=(1,),
        in_specs=[full(nd, D_S5), full(nd, D_S5), full(nd, D_LRU), full(nd, D_MODEL),
                  full(1, D_S5), full(D_S5, D_S5), full(1, D_S5), full(1, D_S5), full(1, D_LRU),
                  full(D_MODEL, D_MODEL)],
        out_specs=full(nd, D_MODEL),
        out_shape=jax.ShapeDtypeStruct((nd, D_MODEL), F32),
        compiler_params=pltpu.CompilerParams(vmem_limit_bytes=VMEM_LIMIT),
        name="mix_decode",
    )(ys_d, proj_d, lru_d, x1_d, dsk, wg, bg, gs, gl, wo)


def _unpair(h, nb):
    tiles = S5_GROUPS // GROUPS_PER_TILE
    h5 = h.reshape(tiles, PAIRS_PER_TILE, nb, 2, S5_STATE)
    return jnp.transpose(h5, (2, 0, 3, 1, 4)).reshape(nb, S5_GROUPS, S5_STATE)


def _block_diag4(w):
    w4 = w.reshape(LRU_HEADS // 4, 4, LRU_HEAD_DIM, LRU_HEAD_DIM)
    eye = jnp.eye(4, dtype=w.dtype)
    return jnp.einsum("kaij,ab->kaibj", w4, eye).reshape(LRU_HEADS // 4, 256, 256)


def kernel(x_prompt, x_sample, state_s5_re, state_s5_im, state_lru_h, state_lru_conv, g_ffn1, w1_a, w3_a, w2_a, g_mix, w_in, lam_re, lam_im, log_dt, b_re, b_im, c_re, c_im, d_skip, w_glu, b_glu, conv_w, conv_b, w_a, b_a, w_x, b_x, lam_l, g_out_s5, g_out_lru, w_out, g_ffn2, w1_b, w3_b, w2_b, g_final):
    nb, seq, _ = x_prompt.shape
    nd = x_sample.shape[0]
    n_p = nb * seq
    nk = seq // CHUNK
    g, p = S5_GROUPS, S5_STATE
    row = lambda v: v.reshape(1, -1)

    xp = x_prompt.reshape(n_p, D_MODEL)
    xd = x_sample.reshape(nd, D_MODEL)

    n_i, n_f = n_p // 1024, D_FF // 512
    first = _ffn(
        xp, xd, row(g_ffn1[0]), w1_a[0], w3_a[0], w2_a[0], n_tiles=1,
        casts=(_cast_job(w1_b[0], n_i, n_f, f_div=2), _cast_job(w3_b[0], n_i, n_f, f_div=2),
               _row_cast_job(w2_b[0], n_i, n_f, f_div=2), _flat_cast_job(w_in[0], 16)))
    w1_a16, w3_a16, w2_a16 = first[2:5]
    w_in16 = first[8]
    x1_p, x1_d, w1_b16, w3_b16, w2_b16 = _ffn(
        xp, xd, row(g_ffn1[0]), w1_a16, w3_a16, w2_a16, first_tile=1, n_tiles=n_i - 1,
        casts=(_cast_job(w1_b[0], n_i, n_f, i0=1), _cast_job(w3_b[0], n_i, n_f, i0=1),
               _row_cast_job(w2_b[0], n_i, n_f, i0=1)),
        prev=first[:2] + first[5:8])
    proj_p, proj_d, w_out16, w_glu16 = _inproj(
        x1_p, x1_d, row(g_mix[0]), w_in16,
        casts=(_cast_job(w_out[0], n_i, D_IN // 1024, bc=1024),
               _cast_job(w_glu[0], n_i, D_IN // 1024, bc=1024)))

    lmat, cpre, cpim, bd, cd, ar, ai, lr, li = _s5_prep(
        lam_re[0], lam_im[0], log_dt[0], b_re[0], b_im[0], c_re[0], c_im[0])
    ys_p, hf_re, hf_im = _s5_prompt(proj_p, lmat, cpre, cpim, ar, ai, nb=nb, seq=seq)

    to_gpb = lambda s: jnp.transpose(s, (1, 2, 0))
    ys_d, hd_re, hd_im = _s5_decode(proj_d, to_gpb(state_s5_re[0]), to_gpb(state_s5_im[0]),
                                    bd, cd, lr, li)

    wa_bd = _block_diag4(w_a[0]).astype(BF16)
    wx_bd = _block_diag4(w_x[0]).astype(BF16)
    lru_args = (conv_w[0], row(conv_b[0]), wa_bd, wx_bd, row(b_a[0]), row(b_x[0]), row(lam_l[0]))
    lru_d, hl_d, buf_d = _lru_decode(proj_d, jnp.transpose(state_lru_conv[0], (1, 0, 2)),
                                     state_lru_h[0], *lru_args)

    mix_args = (row(d_skip[0]), w_glu16, row(b_glu[0]), row(g_out_s5[0]), row(g_out_lru[0]), w_out16)
    x2_p, hl_p = _lru_mix_prompt(proj_p, ys_p, x1_p, *lru_args, *mix_args, nb=nb, seq=seq)
    x2_d = _mix_decode(ys_d, proj_d, lru_d, x1_d, *mix_args)
    y_p, y_d = _ffn(x2_p, x2_d, row(g_ffn2[0]), w1_b16, w3_b16, w2_b16, row(g_final))

    tail_p = proj_p.reshape(nb, seq, -1)[:, seq - (CONV_W - 1):, D_S5:D_S5 + D_LRU]
    return (
        y_p.reshape(nb, seq, D_MODEL),
        y_d.reshape(nd, 1, D_MODEL),
        _unpair(hf_re, nb)[None],
        _unpair(hf_im, nb)[None],
        hl_p.reshape(1, nb, D_LRU),
        tail_p[None],
        jnp.transpose(hd_re, (2, 0, 1))[None],
        jnp.transpose(hd_im, (2, 0, 1))[None],
        hl_d[None],
        jnp.transpose(buf_d, (1, 0, 2))[None],
    )
```

```python
import functools

import jax
import jax.numpy as jnp
from jax import lax
from jax.experimental import pallas as pl
from jax.experimental.pallas import tpu as pltpu

F32 = jnp.float32
BF16 = jnp.bfloat16

D_MODEL = 2048
D_S5 = 1024
S5_GROUP = 16
S5_GROUPS = 64
S5_STATE = 64
D_LRU = 1024
LRU_HEADS = 16
LRU_HEAD_DIM = 64
CONV_W = 4
LRU_C = 8.0
D_FF = 5632
D_IN = D_S5 + 2 * D_LRU
EPS = 1e-6

MXU_WIDTH_V7X = 256
CHUNK = MXU_WIDTH_V7X // S5_GROUP
CW = CHUNK * S5_GROUP

FFN_TM = 1024
FFN_TF = 512
FFN_TF_F32 = FFN_TF // 2
PROJ_TN = 1024
LRU_TT = 512

VMEM_CAPACITY_V7X = 64 * 1024 * 1024
VMEM_LIMIT = VMEM_CAPACITY_V7X - 6 * 1024 * 1024
FFN_VMEM_LIMIT = VMEM_CAPACITY_V7X - 4 * 1024 * 1024

NN = (((1,), (0,)), ((), ()))
NT = (((1,), (1,)), ((), ()))


def _rms(x, g):
    return x * lax.rsqrt(jnp.mean(x * x, axis=-1, keepdims=True) + EPS) * g


def _split(x):
    hi = x.astype(BF16)
    lo = (x - hi.astype(F32)).astype(BF16)
    return hi, lo


def _dot3(a, b, dims=NN):
    ah, al = _split(a)
    bh, bl = _split(b)
    d = functools.partial(lax.dot_general, dimension_numbers=dims, preferred_element_type=F32)
    return d(ah, bh) + d(al, bh) + d(ah, bl)


def _gelu(x):
    return jax.nn.gelu(x, approximate=True)


def _ffn_body(*refs, final_norm, convert, n_casts, n_prev):
    n_in = 7 if final_norm else 6
    xp_ref, xd_ref, g_ref, w1_ref, w3_ref, w2_ref = refs[:6]
    gf_ref = refs[6] if final_norm else None
    cast_in = refs[n_in:n_in + n_casts]
    outs = refs[n_in + n_casts + n_prev:-1]
    op_ref, od_ref = outs[:2]
    wcopy = outs[2:5] if convert else ()
    cast_out = outs[2 + len(wcopy):]
    xn_ref = refs[-1]
    f = pl.program_id(1)
    tm = xp_ref.shape[0]
    xd_ref, od_ref = (r if len(r.shape) == 2 else r.at[:, 0, :] for r in (xd_ref, od_ref))

    for src_ref, dst_ref in zip(cast_in, cast_out):
        dst_ref[...] = src_ref[...].astype(BF16)

    @pl.when(f == 0)
    def _():
        for x_ref, o_ref, rows in ((xp_ref, op_ref, slice(0, tm)), (xd_ref, od_ref, slice(tm, None))):
            x = x_ref[...]
            xn_ref[rows, :] = _rms(x, g_ref[...]).astype(BF16)
            o_ref[...] = x

    if convert:
        w1, w3, w2 = (w_ref[...].astype(BF16) for w_ref in (w1_ref, w3_ref, w2_ref))
        for dst_ref, w in zip(wcopy, (w1, w3, w2)):
            dst_ref[...] = w
    else:
        w1, w3, w2 = w1_ref[...], w3_ref[...], w2_ref[...]

    xn = xn_ref[...]
    a = jnp.dot(xn, w1, preferred_element_type=F32)
    b = jnp.dot(xn, w3, preferred_element_type=F32)
    h = (a * jax.nn.sigmoid(a) * b).astype(BF16)
    upd = 0.5 * jnp.dot(h, w2, preferred_element_type=F32)
    op_ref[...] += upd[0:tm]
    od_ref[...] += upd[tm:]

    if final_norm:
        @pl.when(f == pl.num_programs(1) - 1)
        def _():
            op_ref[...] = _rms(op_ref[...], gf_ref[...])
            od_ref[...] = _rms(od_ref[...], gf_ref[...])


def _cast_job(w, n_i, n_f, bc=FFN_TF, i0=0, f_div=1):
    rows, cols = w.shape
    br = rows // n_i
    n_cb = cols // bc
    assert br * n_i == rows and bc * n_cb == cols and n_cb <= n_f and br % 16 == 0
    return w, pl.BlockSpec(
        (br, bc), lambda i, f: (jnp.minimum(i + i0, n_i - 1),
                                jnp.where(i + i0 < n_i, jnp.minimum(f // f_div, n_cb - 1),
                                          n_cb - 1)))


def _row_cast_job(w, n_i, n_f, i0=0, f_div=1):
    rows, cols = w.shape
    br = rows // (n_i * n_f)
    assert br * n_i * n_f == rows and br % 16 == 0
    return w, pl.BlockSpec((br, cols), lambda i, f: ((i + i0) * n_f + f // f_div, 0))


def _flat_cast_job(w, n_steps):
    rows, cols = w.shape
    n_blocks = max(n for n in range(1, n_steps + 1) if rows % (16 * n) == 0)
    br = rows // n_blocks
    return w, pl.BlockSpec((br, cols), lambda i, f: (jnp.minimum(f, n_blocks - 1), 0))


def _ffn(xp, xd, g, w1, w3, w2, g_final=None, *, casts=(), first_tile=0, n_tiles=None, prev=(),
         decode_out_3d=False, tm=FFN_TM):
    n_p, nd = xp.shape[0], xd.shape[0]
    all_tiles = n_p // tm
    n_tiles = all_tiles if n_tiles is None else n_tiles
    td = nd // all_tiles
    final_norm = g_final is not None
    convert = w1.dtype == F32
    tf = FFN_TF_F32 if convert else FFN_TF
    pspec = pl.BlockSpec((tm, D_MODEL), lambda i, f: (i + first_tile, 0))
    dspec = pl.BlockSpec((td, D_MODEL), lambda i, f: (i + first_tile, 0))
    dspec3 = pl.BlockSpec((td, 1, D_MODEL), lambda i, f: (i + first_tile, 0, 0))
    yd_shape = (nd, 1, D_MODEL) if decode_out_3d else (nd, D_MODEL)
    xspec = pspec if n_tiles > 1 else pl.BlockSpec(
        (tm, D_MODEL), lambda i, f: (i + first_tile, 0), pipeline_mode=pl.Buffered(1))
    w13spec = pl.BlockSpec((D_MODEL, tf), lambda i, f: (0, f))
    w2spec = pl.BlockSpec((tf, D_MODEL), lambda i, f: (f, 0))
    in_specs = [xspec, dspec if xd.ndim == 2 else dspec3,
                pl.BlockSpec((1, D_MODEL), lambda i, f: (0, 0)), w13spec, w13spec, w2spec]
    args = [xp, xd, g, w1, w3, w2]
    if final_norm:
        in_specs.append(pl.BlockSpec((1, D_MODEL), lambda i, f: (0, 0)))
        args.append(g_final)
    in_specs += [spec for _, spec in casts]
    args += [w for w, _ in casts]
    n_in = len(args)
    in_specs += [pl.BlockSpec(memory_space=pl.ANY)] * len(prev)
    args += list(prev)
    sd = jax.ShapeDtypeStruct
    out_specs = [pspec, dspec3 if decode_out_3d else dspec]
    out_shape = [sd((n_p, D_MODEL), F32), sd(yd_shape, F32)]
    if convert:
        out_specs += [w13spec, w13spec, w2spec]
        out_shape += [sd(w.shape, BF16) for w in (w1, w3, w2)]
    out_specs += [spec for _, spec in casts]
    out_shape += [sd(w.shape, BF16) for w, _ in casts]
    assert len(prev) in (0, len(out_shape))
    return pl.pallas_call(
        functools.partial(_ffn_body, final_norm=final_norm, convert=convert,
                          n_casts=len(casts), n_prev=len(prev)),
        grid=(n_tiles, D_FF // tf),
        in_specs=in_specs,
        out_specs=tuple(out_specs),
        out_shape=tuple(out_shape),
        input_output_aliases={n_in + k: k for k in range(len(prev))},
        scratch_shapes=[pltpu.VMEM((tm + td, D_MODEL), BF16)],
        compiler_params=pltpu.CompilerParams(
            dimension_semantics=("arbitrary", "arbitrary"), vmem_limit_bytes=FFN_VMEM_LIMIT),
        name="ffn_final" if final_norm else ("ffn_first" if convert else "ffn"),
    )(*args)


def _inproj_body(*refs, n_tiles, nd, n_casts):
    xp_ref, xd_ref, g_ref, w_ref = refs[:4]
    cast_in = refs[4:4 + n_casts]
    op_ref, od_ref = refs[4 + n_casts:6 + n_casts]
    cast_out = refs[6 + n_casts:6 + 2 * n_casts]
    xn_ref = refs[-1]
    i = pl.program_id(0)
    j = pl.program_id(1)
    nj = pl.num_programs(1)

    for src_ref, dst_ref in zip(cast_in, cast_out):
        dst_ref[...] = src_ref[...].astype(BF16)

    def run(x_ref, o_ref, rows):
        @pl.when(j == 0)
        def _():
            xn_ref[0:rows, :] = _rms(x_ref[...], g_ref[...]).astype(BF16)

        @pl.when(j < nj - 1)
        def _():
            o_ref[...] = jnp.dot(xn_ref[0:rows, :], w_ref[...], preferred_element_type=F32)

        @pl.when(j == nj - 1)
        def _():
            o_ref[...] = _gelu(jnp.dot(xn_ref[0:rows, :], w_ref[...],
                                       preferred_element_type=F32))

    @pl.when(i < n_tiles)
    def _():
        run(xp_ref, op_ref, xp_ref.shape[0])

    @pl.when(i == n_tiles)
    def _():
        run(xd_ref, od_ref, nd)


def _inproj(xp, xd, g, w, *, casts=(), tm=FFN_TM, tn=PROJ_TN):
    n_p, nd = xp.shape[0], xd.shape[0]
    n_tiles = n_p // tm
    d_out = w.shape[1]
    nj = d_out // tn
    return pl.pallas_call(
        functools.partial(_inproj_body, n_tiles=n_tiles, nd=nd, n_casts=len(casts)),
        grid=(n_tiles + 1, nj),
        in_specs=[
            pl.BlockSpec((tm, D_MODEL), lambda i, j: (jnp.minimum(i, n_tiles - 1), 0)),
            pl.BlockSpec((nd, D_MODEL), lambda i, j: (0, 0)),
            pl.BlockSpec((1, D_MODEL), lambda i, j: (0, 0)),
            pl.BlockSpec((D_MODEL, tn), lambda i, j: (0, j)),
        ] + [spec for _, spec in casts],
        out_specs=(
            pl.BlockSpec((tm, tn), lambda i, j: (jnp.minimum(i, n_tiles - 1),
                                                 jnp.where(i < n_tiles, j, nj - 1))),
            pl.BlockSpec((nd, tn), lambda i, j: (0, jnp.where(i < n_tiles, 0, j))),
        ) + tuple(spec for _, spec in casts),
        out_shape=(jax.ShapeDtypeStruct((n_p, d_out), F32),
                   jax.ShapeDtypeStruct((nd, d_out), F32))
        + tuple(jax.ShapeDtypeStruct(cw.shape, BF16) for cw, _ in casts),
        scratch_shapes=[pltpu.VMEM((tm, D_MODEL), BF16)],
        compiler_params=pltpu.CompilerParams(
            dimension_semantics=("arbitrary", "arbitrary"), vmem_limit_bytes=VMEM_LIMIT),
        name="inproj",
    )(xp, xd, g, w, *[cw for cw, _ in casts])


GROUPS_PER_TILE = 128 // S5_GROUP
PAIRS_PER_TILE = GROUPS_PER_TILE // 2
L_ROWS = CW + 2 * S5_STATE


def _lam_bar(lam_re, lam_im, log_dt):
    dt = jnp.exp(log_dt)
    mag = jnp.exp(lam_re * dt)
    ang = lam_im * dt
    return mag * jnp.cos(ang), mag * jnp.sin(ang)


def _s5_prep_body(lre_ref, lim_ref, ldt_ref, bre_ref, bim_ref, cre_ref, cim_ref,
                  l_ref, cpre_ref, cpim_ref, bd_ref, cd_ref, ar_ref, ai_ref, lr_ref, li_ref,
                  bp_ref, lrow_ref):
    gb, half = GROUPS_PER_TILE, PAIRS_PER_TILE
    p = S5_STATE
    lo, hi = slice(0, p), slice(p, 2 * p)
    lam_re = lre_ref[...]
    lam_im = lim_ref[...]
    lbr, lbi = _lam_bar(lam_re, lam_im, ldt_ref[...])
    lrow_ref[:, :, lo] = lbr
    lrow_ref[:, :, hi] = lbi
    for j in range(gb):
        cols = jnp.broadcast_to(lrow_ref[j], (2 * p, 2 * p)).T
        lr_ref[j] = cols[0:p]
        li_ref[j] = cols[p:2 * p]
    nr = lbr - 1.0
    den = lam_re * lam_re + lam_im * lam_im
    cr = (nr * lam_re + lbi * lam_im) / den
    ci = (lbi * lam_re - nr * lam_im) / den
    b_re = bre_ref[...]
    b_im = bim_ref[...]
    bbr = cr * b_re - ci * b_im
    bbi = cr * b_im + ci * b_re
    bd_ref[:, :, lo] = bbr
    bd_ref[:, :, hi] = bbi
    c_re = cre_ref[...]
    c_im = cim_ref[...]
    cd_ref[:, :, lo] = c_re
    cd_ref[:, :, hi] = -c_im

    zeros = jnp.zeros((half, S5_GROUP, p), F32)
    pr = jnp.ones_like(lbr)
    pi = jnp.zeros_like(lbr)
    for d in range(CHUNK):
        rows = slice(d * S5_GROUP, (d + 1) * S5_GROUP)
        back = slice((CHUNK - 1 - d) * S5_GROUP, (CHUNK - d) * S5_GROUP)
        bp_ref[:, back, lo] = bbr * pr - bbi * pi
        bp_ref[:, back, hi] = bbr * pi + bbi * pr
        pr, pi = pr * lbr - pi * lbi, pr * lbi + pi * lbr
        cp_r = c_re * pr - c_im * pi
        cp_i = -(c_re * pi + c_im * pr)
        cpre_ref[0:half, rows, lo] = cp_r[0:half]
        cpre_ref[0:half, rows, hi] = zeros
        cpre_ref[half:gb, rows, lo] = zeros
        cpre_ref[half:gb, rows, hi] = cp_r[half:gb]
        cpim_ref[0:half, rows, lo] = cp_i[0:half]
        cpim_ref[0:half, rows, hi] = zeros
        cpim_ref[half:gb, rows, lo] = zeros
        cpim_ref[half:gb, rows, hi] = cp_i[half:gb]

    qr, qi = pr, pi
    for r in range(8):
        ar_ref[:, r:r + 1, lo] = qr[0:half]
        ar_ref[:, r:r + 1, hi] = qr[half:gb]
        ai_ref[:, r:r + 1, lo] = qi[0:half]
        ai_ref[:, r:r + 1, hi] = qi[half:gb]
        qr, qi = qr * pr - qi * pi, qr * pi + qi * pr

    lane = lax.broadcasted_iota(jnp.int32, (S5_GROUP, 128), 1)
    for j in range(gb):
        w = _dot3(cd_ref[j], bp_ref[j], NT)
        w0, w1 = w[:, :128], w[:, 128:]
        for t in range(CHUNK):
            rows = slice(t * S5_GROUP, (t + 1) * S5_GROUP)
            shift = (CHUNK - 1 - t) * S5_GROUP
            keep = 128 - shift % 128
            if shift == 0:
                left, right = w0, w1
            elif shift < 128:
                r0 = pltpu.roll(w0, keep, axis=1)
                r1 = pltpu.roll(w1, keep, axis=1)
                left = jnp.where(lane < keep, r0, r1)
                right = jnp.where(lane < keep, r1, 0.0)
            elif shift == 128:
                left, right = w1, jnp.zeros_like(w1)
            else:
                left = jnp.where(lane < keep, pltpu.roll(w1, keep, axis=1), 0.0)
                right = jnp.zeros_like(w1)
            l_ref[j, rows, 0:128] = left
            l_ref[j, rows, 128:256] = right
        l_ref[j, CW:L_ROWS, :] = bp_ref[j].T


def _s5_prep(lam_re, lam_im, log_dt, b_re, b_im, c_re, c_im):
    g, p, gb, half = S5_GROUPS, S5_STATE, GROUPS_PER_TILE, PAIRS_PER_TILE
    lre = lam_re.reshape(g, 1, p)
    lim = lam_im.reshape(g, 1, p)
    ldt = jnp.broadcast_to(log_dt.reshape(g, 1, 1), (g, 1, p))
    bre = jnp.transpose(b_re, (0, 2, 1))
    bim = jnp.transpose(b_im, (0, 2, 1))
    sd = jax.ShapeDtypeStruct
    blk = lambda n, r, c: pl.BlockSpec((n, r, c), lambda i: (i, 0, 0))
    return pl.pallas_call(
        _s5_prep_body,
        grid=(g // gb,),
        in_specs=[blk(gb, 1, p)] * 3 + [blk(gb, S5_GROUP, p)] * 4,
        out_specs=(
            blk(gb, L_ROWS, CW), blk(gb, CW, 2 * p), blk(gb, CW, 2 * p),
            blk(gb, S5_GROUP, 2 * p), blk(gb, S5_GROUP, 2 * p),
            blk(half, 8, 2 * p), blk(half, 8, 2 * p), blk(gb, p, 128), blk(gb, p, 128),
        ),
        out_shape=(
            sd((g, L_ROWS, CW), F32),
            sd((g, CW, 2 * p), F32),
            sd((g, CW, 2 * p), F32),
            sd((g, S5_GROUP, 2 * p), F32),
            sd((g, S5_GROUP, 2 * p), F32),
            sd((g // 2, 8, 2 * p), F32),
            sd((g // 2, 8, 2 * p), F32),
            sd((g, p, 128), F32),
            sd((g, p, 128), F32),
        ),
        scratch_shapes=[pltpu.VMEM((gb, CW, 2 * p), F32), pltpu.VMEM((gb, 1, 2 * p), F32)],
        compiler_params=pltpu.CompilerParams(dimension_semantics=("parallel",)),
        name="s5_prep",
    )(lre, lim, ldt, bre, bim, c_re, c_im)


def _s5p_body(u_ref, l_ref, cpre_ref, cpim_ref, ar_ref, ai_ref, y_ref, hre_ref, him_ref,
              ut_ref, yt_ref, *, nb, nk):
    gb, half = GROUPS_PER_TILE, PAIRS_PER_TILE
    p = S5_STATE
    nrow = nb * nk
    d = functools.partial(jnp.dot, preferred_element_type=F32)

    for t in range(CHUNK):
        xt = u_ref[pl.ds(t, nrow, stride=CHUNK), :].T
        for j in range(gb):
            ut_ref[j, t * S5_GROUP:(t + 1) * S5_GROUP, :] = xt[j * S5_GROUP:(j + 1) * S5_GROUP, :]

    s_re, s_im = [], []
    for j in range(gb):
        r = d(l_ref[j].astype(BF16), ut_ref[j].astype(BF16))
        yt_ref[j] = r[0:CW]
        s_re.append(r[CW:CW + p])
        s_im.append(r[CW + p:L_ROWS])

    nblk = nrow // 8
    row8 = lax.broadcasted_iota(jnp.int32, (nblk, 8, 2 * p), 1)
    rows = lax.broadcasted_iota(jnp.int32, (nrow, 2 * p), 0) & (nk - 1)
    dnt = functools.partial(lax.dot_general, dimension_numbers=NT, preferred_element_type=F32)
    for q in range(half):
        re = jnp.concatenate([s_re[q], s_re[q + half]], axis=0).T.reshape(nblk, 8, 2 * p)
        im = jnp.concatenate([s_im[q], s_im[q + half]], axis=0).T.reshape(nblk, 8, 2 * p)
        for sh in (1, 2, 4):
            keep = row8 >= sh
            rs = jnp.where(keep, pltpu.roll(re, sh, axis=1), 0.0)
            js = jnp.where(keep, pltpu.roll(im, sh, axis=1), 0.0)
            ar = ar_ref[q, sh - 1:sh, :]
            ai = ai_ref[q, sh - 1:sh, :]
            re, im = re + ar * rs - ai * js, im + ar * js + ai * rs
        pw_r, pw_i = ar_ref[q], ai_ref[q]
        out_r, out_i = [], []
        for k in range(nblk):
            hr, hi = re[k], im[k]
            if k % (nk // 8):
                hr, hi = hr + pw_r * cr - pw_i * ci, hi + pw_r * ci + pw_i * cr
            cr, ci = hr[7:8, :], hi[7:8, :]
            out_r.append(hr)
            out_i.append(hi)
            if (k + 1) % (nk // 8) == 0:
                b = k // (nk // 8)
                hre_ref[q, b:b + 1, :] = cr
                him_ref[q, b:b + 1, :] = ci
        re = jnp.concatenate(out_r, axis=0)
        im = jnp.concatenate(out_i, axis=0)
        pre = jnp.where(rows >= 1, pltpu.roll(re, 1, axis=0), 0.0).astype(BF16)
        pim = jnp.where(rows >= 1, pltpu.roll(im, 1, axis=0), 0.0).astype(BF16)
        for j in (q, q + half):
            yt_ref[j] = (yt_ref[j] + dnt(cpre_ref[j].astype(BF16), pre)
                         + dnt(cpim_ref[j].astype(BF16), pim))

    for t in range(CHUNK):
        yt = jnp.concatenate(
            [yt_ref[j, t * S5_GROUP:(t + 1) * S5_GROUP, :] for j in range(gb)], axis=0)
        y_ref[pl.ds(t, nrow, stride=CHUNK), :] = yt.T


def _s5_prompt(proj_p, lmat, cpre, cpim, ar, ai, *, nb, seq):
    g, gb, half = S5_GROUPS, GROUPS_PER_TILE, PAIRS_PER_TILE
    n_p = nb * seq
    nk = seq // CHUNK
    blk = lambda n, r, c: pl.BlockSpec((n, r, c), lambda i: (i, 0, 0))
    return pl.pallas_call(
        functools.partial(_s5p_body, nb=nb, nk=nk),
        grid=(g // gb,),
        in_specs=[pl.BlockSpec((n_p, 128), lambda i: (0, i)),
                  blk(gb, L_ROWS, CW), blk(gb, CW, 2 * S5_STATE), blk(gb, CW, 2 * S5_STATE),
                  blk(half, 8, 2 * S5_STATE), blk(half, 8, 2 * S5_STATE)],
        out_specs=(pl.BlockSpec((n_p, 128), lambda i: (0, i)),
                   blk(half, nb, 2 * S5_STATE), blk(half, nb, 2 * S5_STATE)),
        out_shape=(jax.ShapeDtypeStruct((n_p, D_S5), F32),
                   jax.ShapeDtypeStruct((g // 2, nb, 2 * S5_STATE), F32),
                   jax.ShapeDtypeStruct((g // 2, nb, 2 * S5_STATE), F32)),
        scratch_shapes=[pltpu.VMEM((gb, CW, nb * nk), F32), pltpu.VMEM((gb, CW, nb * nk), F32)],
        compiler_params=pltpu.CompilerParams(
            dimension_semantics=("parallel",), vmem_limit_bytes=VMEM_LIMIT),
        name="s5_prompt",
    )(proj_p, lmat, cpre, cpim, ar, ai)


def _s5d_body(u_ref, hre_ref, him_ref, bd_ref, cd_ref, lr_ref, li_ref, y_ref, ore_ref, oim_ref):
    p = S5_STATE
    gb = GROUPS_PER_TILE
    tn = (((0,), (0,)), ((), ()))
    ut = u_ref[...].T
    yts = []
    for j in range(gb):
        ug = ut[j * S5_GROUP:(j + 1) * S5_GROUP, :].astype(BF16)
        bu = lax.dot_general(bd_ref[j].astype(BF16), ug, tn,
                             preferred_element_type=F32)
        h0r, h0i = hre_ref[j], him_ref[j]
        lbr, lbi = lr_ref[j], li_ref[j]
        hr = lbr * h0r - lbi * h0i + bu[0:p]
        hi = lbr * h0i + lbi * h0r + bu[p:2 * p]
        ore_ref[j] = hr
        oim_ref[j] = hi
        h = jnp.concatenate([hr, hi], axis=0).astype(BF16)
        yts.append(jnp.dot(cd_ref[j].astype(BF16), h, preferred_element_type=F32))
    y_ref[...] = jnp.concatenate(yts, axis=0).T


def _s5_decode(proj_d, h0_re, h0_im, bd, cd, lr, li):
    g, gb, p = S5_GROUPS, GROUPS_PER_TILE, S5_STATE
    nbatch = proj_d.shape[0]
    blk = lambda r, c: pl.BlockSpec((gb, r, c), lambda i: (i, 0, 0))
    cols = pl.BlockSpec((nbatch, 128), lambda i: (0, i))
    return pl.pallas_call(
        _s5d_body,
        grid=(g // gb,),
        in_specs=[cols, blk(p, nbatch), blk(p, nbatch), blk(S5_GROUP, 2 * p), blk(S5_GROUP, 2 * p),
                  blk(p, nbatch), blk(p, nbatch)],
        out_specs=(cols, blk(p, nbatch), blk(p, nbatch)),
        out_shape=(jax.ShapeDtypeStruct((nbatch, D_S5), F32),
                   jax.ShapeDtypeStruct((g, p, nbatch), F32),
                   jax.ShapeDtypeStruct((g, p, nbatch), F32)),
        compiler_params=pltpu.CompilerParams(dimension_semantics=("parallel",)),
        name="s5_decode",
    )(proj_d, h0_re, h0_im, bd, cd, lr, li)


def _lru_gates(xc, wa_ref, wx_ref, ba, bx, lam):
    xcb = xc.astype(BF16)
    nblk = D_LRU // MXU_WIDTH_V7X
    r_parts, i_parts = [], []
    for k in range(nblk):
        xk = xcb[:, k * MXU_WIDTH_V7X:(k + 1) * MXU_WIDTH_V7X]
        r_parts.append(jnp.dot(xk, wa_ref[k], preferred_element_type=F32))
        i_parts.append(jnp.dot(xk, wx_ref[k], preferred_element_type=F32))
    r = jax.nn.sigmoid(jnp.concatenate(r_parts, axis=1) + ba)
    i = jax.nn.sigmoid(jnp.concatenate(i_parts, axis=1) + bx)
    z = -lam
    softplus = jnp.maximum(z, 0.0) + jnp.log1p(jnp.exp(-jnp.abs(z)))
    log_a = (-LRU_C * softplus) * r
    a = jnp.exp(log_a)
    v = -jnp.tanh(log_a) * (a * a + 1.0)
    mult = jnp.where(v > 0.0, v * lax.rsqrt(v), 0.0)
    return a, mult * (i * xc)


def _lru_tile(xl_ref, gate_ref, cw_ref, cb_ref, wa_ref, wx_ref, ba_ref, bx_ref, lam_ref,
              o_ref, xbuf_ref, carry_ref):
    tt = xl_ref.shape[0]
    x = xl_ref[...]
    xbuf_ref[8:8 + tt, :] = x
    cw = cw_ref[...]
    xc = (cb_ref[...] + xbuf_ref[5:5 + tt, :] * cw[0:1] + xbuf_ref[6:6 + tt, :] * cw[1:2]
          + xbuf_ref[7:7 + tt, :] * cw[2:3] + x * cw[3:4])
    xbuf_ref[0:8, :] = x[tt - 8:tt, :]

    a, b = _lru_gates(xc, wa_ref, wx_ref, ba_ref[...], bx_ref[...], lam_ref[...])

    nblk = tt // 8
    a3 = a.reshape(nblk, 8, D_LRU)
    b3 = b.reshape(nblk, 8, D_LRU)
    row = lax.broadcasted_iota(jnp.int32, (nblk, 8, D_LRU), 1)
    for sh in (1, 2, 4):
        keep = row >= sh
        bs = jnp.where(keep, pltpu.roll(b3, sh, axis=1), 0.0)
        sa = jnp.where(keep, pltpu.roll(a3, sh, axis=1), 1.0)
        b3 = b3 + a3 * bs
        a3 = a3 * sa
    carry = carry_ref[0:1, :]
    gate = gate_ref[...]
    for k in range(nblk):
        h = b3[k] + a3[k] * carry
        carry = h[7:8, :]
        o_ref[k * 8:(k + 1) * 8, :] = h * gate[k * 8:(k + 1) * 8, :]
    carry_ref[...] = jnp.broadcast_to(carry, (8, D_LRU))
    return carry


def _mix_s5_part(ys, u, x, dsk_ref, wg_ref, bg_ref, gs_ref, wo_ref):
    yy = ys + dsk_ref[...] * u
    g = _gelu(yy)
    z = jnp.dot(g.astype(BF16), wg_ref[...], preferred_element_type=F32) + bg_ref[...]
    s5o = g * jax.nn.sigmoid(z)
    n1 = _rms(s5o, gs_ref[...]).astype(BF16)
    return x + jnp.dot(n1, wo_ref[0:D_S5, :], preferred_element_type=F32)


def _mix_lru_part(lru, gl_ref, wo_ref):
    n2 = _rms(lru, gl_ref[...]).astype(BF16)
    return jnp.dot(n2, wo_ref[D_S5:, :], preferred_element_type=F32)


def _lru_mix_body(xl_ref, gate_ref, ys_ref, u_ref, x_ref,
                  cw_ref, cb_ref, wa_ref, wx_ref, ba_ref, bx_ref, lam_ref,
                  dsk_ref, wg_ref, bg_ref, gs_ref, gl_ref, wo_ref,
                  o_ref, hl_ref, xbuf_ref, carry_ref, lru_ref):
    @pl.when(pl.program_id(1) == 0)
    def _():
        xbuf_ref[0:8, :] = jnp.zeros((8, D_LRU), F32)
        carry_ref[...] = jnp.zeros((8, D_LRU), F32)

    o_ref[...] = _mix_s5_part(ys_ref[...], u_ref[...], x_ref[...],
                              dsk_ref, wg_ref, bg_ref, gs_ref, wo_ref)
    hl_ref[0] = _lru_tile(xl_ref, gate_ref, cw_ref, cb_ref, wa_ref, wx_ref, ba_ref, bx_ref, lam_ref,
                          lru_ref, xbuf_ref, carry_ref)
    o_ref[...] += _mix_lru_part(lru_ref[...], gl_ref, wo_ref)


def _lru_mix_prompt(proj_p, ys_p, x1_p, cw, cb, wa, wx, ba, bx, lam, dsk, wg, bg, gs, gl, wo,
                    *, nb, seq, tt=LRU_TT):
    nt = seq // tt
    rows = lambda c, col: pl.BlockSpec((tt, c), lambda b, t: (b * nt + t, col))
    once = lambda shape: pl.BlockSpec(shape, lambda b, t: (0,) * len(shape),
                                      pipeline_mode=pl.Buffered(1))
    return pl.pallas_call(
        _lru_mix_body,
        grid=(nb, nt),
        in_specs=[
            rows(D_LRU, 1), rows(D_LRU, 2),
            rows(D_S5, 0), rows(D_S5, 0), rows(D_MODEL, 0),
            once((CONV_W, D_LRU)), once((1, D_LRU)),
            once((D_LRU // 256, 256, 256)), once((D_LRU // 256, 256, 256)),
            once((1, D_LRU)), once((1, D_LRU)), once((1, D_LRU)),
            once((1, D_S5)), once((D_S5, D_S5)), once((1, D_S5)), once((1, D_S5)), once((1, D_LRU)),
            once((D_MODEL, D_MODEL)),
        ],
        out_specs=(pl.BlockSpec((tt, D_MODEL), lambda b, t: (b * nt + t, 0)),
                   pl.BlockSpec((1, 1, D_LRU), lambda b, t: (b, 0, 0))),
        out_shape=(jax.ShapeDtypeStruct((nb * seq, D_MODEL), F32),
                   jax.ShapeDtypeStruct((nb, 1, D_LRU), F32)),
        scratch_shapes=[pltpu.VMEM((tt + 8, D_LRU), F32), pltpu.VMEM((8, D_LRU), F32),
                        pltpu.VMEM((tt, D_LRU), F32)],
        compiler_params=pltpu.CompilerParams(
            dimension_semantics=("parallel", "arbitrary"), vmem_limit_bytes=VMEM_LIMIT),
        name="lru_mix_prompt",
    )(proj_p, proj_p, ys_p, proj_p, x1_p, cw, cb, wa, wx, ba, bx, lam, dsk, wg, bg, gs, gl, wo)


def _lru_decode_body(xl_ref, gate_ref, conv_ref, h0_ref, cw_ref, cb_ref,
                     wa_ref, wx_ref, ba_ref, bx_ref, lam_ref, o_ref, h_ref, buf_ref):
    x = xl_ref[...]
    cw = cw_ref[...]
    c0, c1, c2 = (conv_ref[k] for k in range(CONV_W - 1))
    xc = cb_ref[...] + c0 * cw[0:1] + c1 * cw[1:2] + c2 * cw[2:3] + x * cw[3:4]
    a, b = _lru_gates(xc, wa_ref, wx_ref, ba_ref[...], bx_ref[...], lam_ref[...])
    h = a * h0_ref[...] + b
    h_ref[...] = h
    o_ref[...] = h * gate_ref[...]
    for k, rows in enumerate((c1, c2, x)):
        buf_ref[k] = rows


def _lru_decode(proj_d, conv0, h0, cw, cb, wa, wx, ba, bx, lam):
    nd = proj_d.shape[0]
    full = lambda r: pl.BlockSpec((r, D_LRU), lambda i: (0, 0))
    conv = pl.BlockSpec((CONV_W - 1, nd, D_LRU), lambda i: (0, 0, 0))
    wspec = pl.BlockSpec((D_LRU // 256, 256, 256), lambda i: (0, 0, 0))
    return pl.pallas_call(
        _lru_decode_body,
        grid=(1,),
        in_specs=[
            pl.BlockSpec((nd, D_LRU), lambda i: (0, 1)),
            pl.BlockSpec((nd, D_LRU), lambda i: (0, 2)),
            conv, full(nd),
            full(CONV_W), full(1), wspec, wspec, full(1), full(1), full(1),
        ],
        out_specs=(full(nd), full(nd), conv),
        out_shape=(jax.ShapeDtypeStruct((nd, D_LRU), F32),
                   jax.ShapeDtypeStruct((nd, D_LRU), F32),
                   jax.ShapeDtypeStruct((CONV_W - 1, nd, D_LRU), F32)),
        name="lru_decode",
    )(proj_d, proj_d, conv0, h0, cw, cb, wa, wx, ba, bx, lam)


def _mix_decode_body(ys_ref, u_ref, lru_ref, x_ref, dsk_ref, wg_ref, bg_ref, gs_ref, gl_ref,
                     wo_ref, o_ref):
    o_ref[...] = (_mix_s5_part(ys_ref[...], u_ref[...], x_ref[...],
                               dsk_ref, wg_ref, bg_ref, gs_ref, wo_ref)
                  + _mix_lru_part(lru_ref[...], gl_ref, wo_ref))


def _mix_decode(ys_d, proj_d, lru_d, x1_d, dsk, wg, bg, gs, gl, wo):
    nd = x1_d.shape[0]
    full = lambda r, c: pl.BlockSpec((r, c), lambda i: (0, 0))
    return pl.pallas_call(
        _mix_decode_body,
        grid=(1,),
        in_specs=[full(nd, D_S5), full(nd, D_S5), full(nd, D_LRU), full(nd, D_MODEL),
                  full(1, D_S5), full(D_S5, D_S5), full(1, D_S5), full(1, D_S5), full(1, D_LRU),
                  full(D_MODEL, D_MODEL)],
        out_specs=full(nd, D_MODEL),
        out_shape=jax.ShapeDtypeStruct((nd, D_MODEL), F32),
        compiler_params=pltpu.CompilerParams(vmem_limit_bytes=VMEM_LIMIT),
        name="mix_decode",
    )(ys_d, proj_d, lru_d, x1_d, dsk, wg, bg, gs, gl, wo)


def _unpair(h, nb):
    tiles = S5_GROUPS // GROUPS_PER_TILE
    h5 = h.reshape(tiles, PAIRS_PER_TILE, nb, 2, S5_STATE)
    return jnp.transpose(h5, (2, 0, 3, 1, 4)).reshape(nb, S5_GROUPS, S5_STATE)


def _block_diag4(w):
    w4 = w.reshape(LRU_HEADS // 4, 4, LRU_HEAD_DIM, LRU_HEAD_DIM)
    eye = jnp.eye(4, dtype=w.dtype)
    return jnp.einsum("kaij,ab->kaibj", w4, eye).reshape(LRU_HEADS // 4, 256, 256)


def kernel(x_prompt, x_sample, state_s5_re, state_s5_im, state_lru_h, state_lru_conv, g_ffn1, w1_a, w3_a, w2_a, g_mix, w_in, lam_re, lam_im, log_dt, b_re, b_im, c_re, c_im, d_skip, w_glu, b_glu, conv_w, conv_b, w_a, b_a, w_x, b_x, lam_l, g_out_s5, g_out_lru, w_out, g_ffn2, w1_b, w3_b, w2_b, g_final):
    nb, seq, _ = x_prompt.shape
    nd = x_sample.shape[0]
    n_p = nb * seq
    row = lambda v: v.reshape(1, -1)
    assert w1_a.shape[0] == 1 and x_sample.shape[1] == 1
    assert nd == 128 and n_p % FFN_TM == 0 and (nd * FFN_TM) % (8 * n_p) == 0
    assert seq % LRU_TT == 0 and seq % (8 * CHUNK) == 0

    xp = x_prompt.reshape(n_p, D_MODEL)
    xd = x_sample.reshape(nd, D_MODEL)

    n_i, n_f = n_p // FFN_TM, D_FF // FFN_TF
    f_div = FFN_TF // FFN_TF_F32
    first = _ffn(
        xp, xd, row(g_ffn1[0]), w1_a[0], w3_a[0], w2_a[0], n_tiles=1,
        casts=(_cast_job(w1_b[0], n_i, n_f, f_div=f_div), _cast_job(w3_b[0], n_i, n_f, f_div=f_div),
               _row_cast_job(w2_b[0], n_i, n_f, f_div=f_div), _flat_cast_job(w_in[0], D_FF // FFN_TF_F32)))
    w1_a16, w3_a16, w2_a16 = first[2:5]
    w_in16 = first[8]
    x1_p, x1_d, w1_b16, w3_b16, w2_b16 = _ffn(
        xp, xd, row(g_ffn1[0]), w1_a16, w3_a16, w2_a16, first_tile=1, n_tiles=n_i - 1,
        casts=(_cast_job(w1_b[0], n_i, n_f, i0=1), _cast_job(w3_b[0], n_i, n_f, i0=1),
               _row_cast_job(w2_b[0], n_i, n_f, i0=1)),
        prev=first[:2] + first[5:8])
    proj_p, proj_d, w_out16, w_glu16 = _inproj(
        x1_p, x1_d, row(g_mix[0]), w_in16,
        casts=(_cast_job(w_out[0], n_i, D_IN // PROJ_TN, bc=PROJ_TN),
               _cast_job(w_glu[0], n_i, D_IN // PROJ_TN, bc=PROJ_TN)))

    lmat, cpre, cpim, bd, cd, ar, ai, lr, li = _s5_prep(
        lam_re[0], lam_im[0], log_dt[0], b_re[0], b_im[0], c_re[0], c_im[0])
    ys_p, hf_re, hf_im = _s5_prompt(proj_p, lmat, cpre, cpim, ar, ai, nb=nb, seq=seq)

    to_gpb = lambda s: jnp.transpose(s, (1, 2, 0))
    ys_d, hd_re, hd_im = _s5_decode(proj_d, to_gpb(state_s5_re[0]), to_gpb(state_s5_im[0]),
                                    bd, cd, lr, li)

    wa_bd = _block_diag4(w_a[0]).astype(BF16)
    wx_bd = _block_diag4(w_x[0]).astype(BF16)
    lru_args = (conv_w[0], row(conv_b[0]), wa_bd, wx_bd, row(b_a[0]), row(b_x[0]), row(lam_l[0]))
    lru_d, hl_d, buf_d = _lru_decode(proj_d, jnp.transpose(state_lru_conv[0], (1, 0, 2)),
                                     state_lru_h[0], *lru_args)

    mix_args = (row(d_skip[0]), w_glu16, row(b_glu[0]), row(g_out_s5[0]), row(g_out_lru[0]), w_out16)
    x2_p, hl_p = _lru_mix_prompt(proj_p, ys_p, x1_p, *lru_args, *mix_args, nb=nb, seq=seq)
    x2_d = _mix_decode(ys_d, proj_d, lru_d, x1_d, *mix_args)
    y_p, y_d = _ffn(x2_p, x2_d, row(g_ffn2[0]), w1_b16, w3_b16, w2_b16, row(g_final),
                    decode_out_3d=True)

    tail_p = proj_p.reshape(nb, seq, -1)[:, seq - (CONV_W - 1):, D_S5:D_S5 + D_LRU]
    return (
        y_p.reshape(nb, seq, D_MODEL),
        y_d,
        _unpair(hf_re, nb)[None],
        _unpair(hf_im, nb)[None],
        hl_p.reshape(1, nb, D_LRU),
        tail_p[None],
        jnp.transpose(hd_re, (2, 0, 1))[None],
        jnp.transpose(hd_im, (2, 0, 1))[None],
        hl_d[None],
        jnp.transpose(buf_d, (1, 0, 2))[None],
    )
```

```python
import functools

import jax
import jax.numpy as jnp
from jax import lax
from jax.experimental import pallas as pl
from jax.experimental.pallas import tpu as pltpu

F32 = jnp.float32
BF16 = jnp.bfloat16

D_MODEL = 2048
D_S5 = 1024
S5_GROUP = 16
S5_GROUPS = 64
S5_STATE = 64
D_LRU = 1024
LRU_HEADS = 16
LRU_HEAD_DIM = 64
CONV_W = 4
LRU_C = 8.0
D_FF = 5632
D_IN = D_S5 + 2 * D_LRU
EPS = 1e-6

MXU_WIDTH_V7X = 256
CHUNK = MXU_WIDTH_V7X // S5_GROUP
CW = CHUNK * S5_GROUP

FFN_TM = 1024
FFN_TF = 512
FFN_TF_F32 = FFN_TF // 2
PROJ_TN = 1024
LRU_TT = 512

VMEM_CAPACITY_V7X = 64 * 1024 * 1024
VMEM_LIMIT = VMEM_CAPACITY_V7X - 6 * 1024 * 1024
FFN_VMEM_LIMIT = VMEM_CAPACITY_V7X - 4 * 1024 * 1024

NN = (((1,), (0,)), ((), ()))
NT = (((1,), (1,)), ((), ()))


def _rms(x, g):
    return x * lax.rsqrt(jnp.mean(x * x, axis=-1, keepdims=True) + EPS) * g


def _split(x):
    hi = x.astype(BF16)
    lo = (x - hi.astype(F32)).astype(BF16)
    return hi, lo


def _dot3(a, b, dims=NN):
    ah, al = _split(a)
    bh, bl = _split(b)
    d = functools.partial(lax.dot_general, dimension_numbers=dims, preferred_element_type=F32)
    return d(ah, bh) + d(al, bh) + d(ah, bl)


def _gelu(x):
    return jax.nn.gelu(x, approximate=True)


def _ffn_body(*refs, final_norm, convert, has_decode, n_casts, n_prev):
    refs = list(refs)
    xn_ref = refs.pop()
    xp_ref = refs.pop(0)
    xd_ref = refs.pop(0) if has_decode else None
    g_ref, w1_ref, w3_ref, w2_ref = (refs.pop(0) for _ in range(4))
    gf_ref = refs.pop(0) if final_norm else None
    cast_in = [refs.pop(0) for _ in range(n_casts)]
    del refs[:n_prev]
    op_ref = refs.pop(0)
    od_ref = refs.pop(0) if has_decode else None
    wcopy = [refs.pop(0) for _ in range(3)] if convert else []
    cast_out = refs
    f = pl.program_id(1)
    tm = xp_ref.shape[0]
    parts = [(xp_ref, op_ref, slice(0, tm))]
    if has_decode:
        parts.append((xd_ref, od_ref, slice(tm, None)))

    for src_ref, dst_ref in zip(cast_in, cast_out):
        dst_ref[...] = src_ref[...].astype(BF16)

    @pl.when(f == 0)
    def _():
        for x_ref, o_ref, rows in parts:
            x = x_ref[...]
            xn_ref[rows, :] = _rms(x, g_ref[...]).astype(BF16)
            o_ref[...] = x

    if convert:
        w1, w3, w2 = (w_ref[...].astype(BF16) for w_ref in (w1_ref, w3_ref, w2_ref))
        for dst_ref, w in zip(wcopy, (w1, w3, w2)):
            dst_ref[...] = w
    else:
        w1, w3, w2 = w1_ref[...], w3_ref[...], w2_ref[...]

    xn = xn_ref[...]
    a = jnp.dot(xn, w1, preferred_element_type=F32)
    b = jnp.dot(xn, w3, preferred_element_type=F32)
    h = (a * jax.nn.sigmoid(a) * b).astype(BF16)
    upd = 0.5 * jnp.dot(h, w2, preferred_element_type=F32)
    for _, o_ref, rows in parts:
        o_ref[...] += upd[rows]

    if final_norm:
        @pl.when(f == pl.num_programs(1) - 1)
        def _():
            for _, o_ref, _ in parts:
                o_ref[...] = _rms(o_ref[...], gf_ref[...])


def _cast_job(w, n_i, n_f, bc=FFN_TF, i0=0, f_div=1):
    rows, cols = w.shape
    br = rows // n_i
    n_cb = cols // bc
    assert br * n_i == rows and bc * n_cb == cols and n_cb <= n_f and br % 16 == 0
    return w, pl.BlockSpec(
        (br, bc), lambda i, f: (jnp.minimum(i + i0, n_i - 1),
                                jnp.where(i + i0 < n_i, jnp.minimum(f // f_div, n_cb - 1),
                                          n_cb - 1)))


def _row_cast_job(w, n_i, n_f, i0=0, f_div=1):
    rows, cols = w.shape
    br = rows // (n_i * n_f)
    assert br * n_i * n_f == rows and br % 16 == 0
    return w, pl.BlockSpec((br, cols), lambda i, f: ((i + i0) * n_f + f // f_div, 0))


def _flat_cast_job(w, n_steps):
    rows, cols = w.shape
    n_blocks = max(n for n in range(1, n_steps + 1) if rows % (16 * n) == 0)
    br = rows // n_blocks
    return w, pl.BlockSpec((br, cols), lambda i, f: (jnp.minimum(f, n_blocks - 1), 0))


def _ffn(xp, xd, g, w1, w3, w2, g_final=None, *, casts=(), first_tile=0, n_tiles=None, prev=(),
         tm=FFN_TM):
    n_p = xp.shape[0]
    n_tiles = n_p // tm if n_tiles is None else n_tiles
    has_decode = xd is not None
    nd = xd.shape[0] if has_decode else 0
    assert not has_decode or n_tiles == 1
    final_norm = g_final is not None
    convert = w1.dtype == F32
    tf = FFN_TF_F32 if convert else FFN_TF
    pspec = pl.BlockSpec((tm, D_MODEL), lambda i, f: (i + first_tile, 0))
    dspec = pl.BlockSpec((nd, D_MODEL), lambda i, f: (0, 0))
    vspec = pl.BlockSpec((1, D_MODEL), lambda i, f: (0, 0))
    xspec = pspec if n_tiles > 1 else pl.BlockSpec(
        (tm, D_MODEL), lambda i, f: (i + first_tile, 0), pipeline_mode=pl.Buffered(1))
    w13spec = pl.BlockSpec((D_MODEL, tf), lambda i, f: (0, f))
    w2spec = pl.BlockSpec((tf, D_MODEL), lambda i, f: (f, 0))
    sd = jax.ShapeDtypeStruct
    in_specs, args = [xspec], [xp]
    out_specs, out_shape = [pspec], [sd((n_p, D_MODEL), F32)]
    if has_decode:
        in_specs.append(dspec)
        args.append(xd)
        out_specs.append(dspec)
        out_shape.append(sd((nd, D_MODEL), F32))
    in_specs += [vspec, w13spec, w13spec, w2spec]
    args += [g, w1, w3, w2]
    if final_norm:
        in_specs.append(vspec)
        args.append(g_final)
    in_specs += [spec for _, spec in casts]
    args += [w for w, _ in casts]
    n_in = len(args)
    in_specs += [pl.BlockSpec(memory_space=pl.ANY)] * len(prev)
    args += list(prev)
    if convert:
        out_specs += [w13spec, w13spec, w2spec]
        out_shape += [sd(w.shape, BF16) for w in (w1, w3, w2)]
    out_specs += [spec for _, spec in casts]
    out_shape += [sd(w.shape, BF16) for w, _ in casts]
    assert len(prev) in (0, len(out_shape))
    return pl.pallas_call(
        functools.partial(_ffn_body, final_norm=final_norm, convert=convert,
                          has_decode=has_decode, n_casts=len(casts), n_prev=len(prev)),
        grid=(n_tiles, D_FF // tf),
        in_specs=in_specs,
        out_specs=tuple(out_specs),
        out_shape=tuple(out_shape),
        input_output_aliases={n_in + k: k for k in range(len(prev))},
        scratch_shapes=[pltpu.VMEM((tm + nd, D_MODEL), BF16)],
        compiler_params=pltpu.CompilerParams(
            dimension_semantics=("arbitrary", "arbitrary"), vmem_limit_bytes=FFN_VMEM_LIMIT),
        name=("ffn_final" if final_norm else "ffn") + ("_first" if has_decode else ""),
    )(*args)


def _inproj_body(*refs, n_tiles, nd, n_casts):
    xp_ref, xd_ref, g_ref, w_ref = refs[:4]
    cast_in = refs[4:4 + n_casts]
    op_ref, od_ref = refs[4 + n_casts:6 + n_casts]
    cast_out = refs[6 + n_casts:6 + 2 * n_casts]
    xn_ref = refs[-1]
    i = pl.program_id(0)
    j = pl.program_id(1)
    nj = pl.num_programs(1)

    for src_ref, dst_ref in zip(cast_in, cast_out):
        dst_ref[...] = src_ref[...].astype(BF16)

    def run(x_ref, o_ref, rows):
        @pl.when(j == 0)
        def _():
            xn_ref[0:rows, :] = _rms(x_ref[...], g_ref[...]).astype(BF16)

        @pl.when(j < nj - 1)
        def _():
            o_ref[...] = jnp.dot(xn_ref[0:rows, :], w_ref[...], preferred_element_type=F32)

        @pl.when(j == nj - 1)
        def _():
            o_ref[...] = _gelu(jnp.dot(xn_ref[0:rows, :], w_ref[...],
                                       preferred_element_type=F32))

    @pl.when(i < n_tiles)
    def _():
        run(xp_ref, op_ref, xp_ref.shape[0])

    @pl.when(i == n_tiles)
    def _():
        run(xd_ref, od_ref, nd)


def _inproj(xp, xd, g, w, *, casts=(), tm=FFN_TM, tn=PROJ_TN):
    n_p, nd = xp.shape[0], xd.shape[0]
    n_tiles = n_p // tm
    d_out = w.shape[1]
    nj = d_out // tn
    return pl.pallas_call(
        functools.partial(_inproj_body, n_tiles=n_tiles, nd=nd, n_casts=len(casts)),
        grid=(n_tiles + 1, nj),
        in_specs=[
            pl.BlockSpec((tm, D_MODEL), lambda i, j: (jnp.minimum(i, n_tiles - 1), 0)),
            pl.BlockSpec((nd, D_MODEL), lambda i, j: (0, 0)),
            pl.BlockSpec((1, D_MODEL), lambda i, j: (0, 0)),
            pl.BlockSpec((D_MODEL, tn), lambda i, j: (0, j)),
        ] + [spec for _, spec in casts],
        out_specs=(
            pl.BlockSpec((tm, tn), lambda i, j: (jnp.minimum(i, n_tiles - 1),
                                                 jnp.where(i < n_tiles, j, nj - 1))),
            pl.BlockSpec((nd, tn), lambda i, j: (0, jnp.where(i < n_tiles, 0, j))),
        ) + tuple(spec for _, spec in casts),
        out_shape=(jax.ShapeDtypeStruct((n_p, d_out), F32),
                   jax.ShapeDtypeStruct((nd, d_out), F32))
        + tuple(jax.ShapeDtypeStruct(cw.shape, BF16) for cw, _ in casts),
        scratch_shapes=[pltpu.VMEM((tm, D_MODEL), BF16)],
        compiler_params=pltpu.CompilerParams(
            dimension_semantics=("arbitrary", "arbitrary"), vmem_limit_bytes=VMEM_LIMIT),
        name="inproj",
    )(xp, xd, g, w, *[cw for cw, _ in casts])


GROUPS_PER_TILE = 128 // S5_GROUP
PAIRS_PER_TILE = GROUPS_PER_TILE // 2
L_ROWS = CW + 2 * S5_STATE


def _lam_bar(lam_re, lam_im, log_dt):
    dt = jnp.exp(log_dt)
    mag = jnp.exp(lam_re * dt)
    ang = lam_im * dt
    return mag * jnp.cos(ang), mag * jnp.sin(ang)


def _s5_prep_body(lre_ref, lim_ref, ldt_ref, bre_ref, bim_ref, cre_ref, cim_ref,
                  l_ref, cpre_ref, cpim_ref, bd_ref, cd_ref, ar_ref, ai_ref, lr_ref, li_ref,
                  bp_ref, lrow_ref):
    gb, half = GROUPS_PER_TILE, PAIRS_PER_TILE
    p = S5_STATE
    lo, hi = slice(0, p), slice(p, 2 * p)
    lam_re = lre_ref[...]
    lam_im = lim_ref[...]
    lbr, lbi = _lam_bar(lam_re, lam_im, ldt_ref[...])
    lrow_ref[:, :, lo] = lbr
    lrow_ref[:, :, hi] = lbi
    for j in range(gb):
        cols = jnp.broadcast_to(lrow_ref[j], (2 * p, 2 * p)).T
        lr_ref[j] = cols[0:p]
        li_ref[j] = cols[p:2 * p]
    nr = lbr - 1.0
    den = lam_re * lam_re + lam_im * lam_im
    cr = (nr * lam_re + lbi * lam_im) / den
    ci = (lbi * lam_re - nr * lam_im) / den
    b_re = bre_ref[...]
    b_im = bim_ref[...]
    bbr = cr * b_re - ci * b_im
    bbi = cr * b_im + ci * b_re
    bd_ref[:, :, lo] = bbr
    bd_ref[:, :, hi] = bbi
    c_re = cre_ref[...]
    c_im = cim_ref[...]
    cd_ref[:, :, lo] = c_re
    cd_ref[:, :, hi] = -c_im

    zeros = jnp.zeros((half, S5_GROUP, p), F32)
    pr = jnp.ones_like(lbr)
    pi = jnp.zeros_like(lbr)
    for d in range(CHUNK):
        rows = slice(d * S5_GROUP, (d + 1) * S5_GROUP)
        back = slice((CHUNK - 1 - d) * S5_GROUP, (CHUNK - d) * S5_GROUP)
        bp_ref[:, back, lo] = bbr * pr - bbi * pi
        bp_ref[:, back, hi] = bbr * pi + bbi * pr
        pr, pi = pr * lbr - pi * lbi, pr * lbi + pi * lbr
        cp_r = c_re * pr - c_im * pi
        cp_i = -(c_re * pi + c_im * pr)
        cpre_ref[0:half, rows, lo] = cp_r[0:half]
        cpre_ref[0:half, rows, hi] = zeros
        cpre_ref[half:gb, rows, lo] = zeros
        cpre_ref[half:gb, rows, hi] = cp_r[half:gb]
        cpim_ref[0:half, rows, lo] = cp_i[0:half]
        cpim_ref[0:half, rows, hi] = zeros
        cpim_ref[half:gb, rows, lo] = zeros
        cpim_ref[half:gb, rows, hi] = cp_i[half:gb]

    qr, qi = pr, pi
    for r in range(8):
        ar_ref[:, r:r + 1, lo] = qr[0:half]
        ar_ref[:, r:r + 1, hi] = qr[half:gb]
        ai_ref[:, r:r + 1, lo] = qi[0:half]
        ai_ref[:, r:r + 1, hi] = qi[half:gb]
        qr, qi = qr * pr - qi * pi, qr * pi + qi * pr

    lane = lax.broadcasted_iota(jnp.int32, (S5_GROUP, 128), 1)
    for j in range(gb):
        w = _dot3(cd_ref[j], bp_ref[j], NT)
        w0, w1 = w[:, :128], w[:, 128:]
        for t in range(CHUNK):
            rows = slice(t * S5_GROUP, (t + 1) * S5_GROUP)
            shift = (CHUNK - 1 - t) * S5_GROUP
            keep = 128 - shift % 128
            if shift == 0:
                left, right = w0, w1
            elif shift < 128:
                r0 = pltpu.roll(w0, keep, axis=1)
                r1 = pltpu.roll(w1, keep, axis=1)
                left = jnp.where(lane < keep, r0, r1)
                right = jnp.where(lane < keep, r1, 0.0)
            elif shift == 128:
                left, right = w1, jnp.zeros_like(w1)
            else:
                left = jnp.where(lane < keep, pltpu.roll(w1, keep, axis=1), 0.0)
                right = jnp.zeros_like(w1)
            l_ref[j, rows, 0:128] = left
            l_ref[j, rows, 128:256] = right
        l_ref[j, CW:L_ROWS, :] = bp_ref[j].T


def _s5_prep(lam_re, lam_im, log_dt, b_re, b_im, c_re, c_im):
    g, p, gb, half = S5_GROUPS, S5_STATE, GROUPS_PER_TILE, PAIRS_PER_TILE
    lre = lam_re.reshape(g, 1, p)
    lim = lam_im.reshape(g, 1, p)
    ldt = jnp.broadcast_to(log_dt.reshape(g, 1, 1), (g, 1, p))
    bre = jnp.transpose(b_re, (0, 2, 1))
    bim = jnp.transpose(b_im, (0, 2, 1))
    sd = jax.ShapeDtypeStruct
    blk = lambda n, r, c: pl.BlockSpec((n, r, c), lambda i: (i, 0, 0))
    return pl.pallas_call(
        _s5_prep_body,
        grid=(g // gb,),
        in_specs=[blk(gb, 1, p)] * 3 + [blk(gb, S5_GROUP, p)] * 4,
        out_specs=(
            blk(gb, L_ROWS, CW), blk(gb, CW, 2 * p), blk(gb, CW, 2 * p),
            blk(gb, S5_GROUP, 2 * p), blk(gb, S5_GROUP, 2 * p),
            blk(half, 8, 2 * p), blk(half, 8, 2 * p), blk(gb, p, 128), blk(gb, p, 128),
        ),
        out_shape=(
            sd((g, L_ROWS, CW), F32),
            sd((g, CW, 2 * p), F32),
            sd((g, CW, 2 * p), F32),
            sd((g, S5_GROUP, 2 * p), F32),
            sd((g, S5_GROUP, 2 * p), F32),
            sd((g // 2, 8, 2 * p), F32),
            sd((g // 2, 8, 2 * p), F32),
            sd((g, p, 128), F32),
            sd((g, p, 128), F32),
        ),
        scratch_shapes=[pltpu.VMEM((gb, CW, 2 * p), F32), pltpu.VMEM((gb, 1, 2 * p), F32)],
        compiler_params=pltpu.CompilerParams(dimension_semantics=("parallel",)),
        name="s5_prep",
    )(lre, lim, ldt, bre, bim, c_re, c_im)


def _s5p_body(u_ref, l_ref, cpre_ref, cpim_ref, ar_ref, ai_ref, y_ref, hre_ref, him_ref,
              ut_ref, yt_ref, *, nb, nk):
    gb, half = GROUPS_PER_TILE, PAIRS_PER_TILE
    p = S5_STATE
    nrow = nb * nk
    d = functools.partial(jnp.dot, preferred_element_type=F32)

    for t in range(CHUNK):
        xt = u_ref[pl.ds(t, nrow, stride=CHUNK), :].T
        for j in range(gb):
            ut_ref[j, t * S5_GROUP:(t + 1) * S5_GROUP, :] = xt[j * S5_GROUP:(j + 1) * S5_GROUP, :]

    s_re, s_im = [], []
    for j in range(gb):
        r = d(l_ref[j].astype(BF16), ut_ref[j].astype(BF16))
        yt_ref[j] = r[0:CW]
        s_re.append(r[CW:CW + p])
        s_im.append(r[CW + p:L_ROWS])

    nblk = nrow // 8
    row8 = lax.broadcasted_iota(jnp.int32, (nblk, 8, 2 * p), 1)
    rows = lax.broadcasted_iota(jnp.int32, (nrow, 2 * p), 0) & (nk - 1)
    dnt = functools.partial(lax.dot_general, dimension_numbers=NT, preferred_element_type=F32)
    for q in range(half):
        re = jnp.concatenate([s_re[q], s_re[q + half]], axis=0).T.reshape(nblk, 8, 2 * p)
        im = jnp.concatenate([s_im[q], s_im[q + half]], axis=0).T.reshape(nblk, 8, 2 * p)
        for sh in (1, 2, 4):
            keep = row8 >= sh
            rs = jnp.where(keep, pltpu.roll(re, sh, axis=1), 0.0)
            js = jnp.where(keep, pltpu.roll(im, sh, axis=1), 0.0)
            ar = ar_ref[q, sh - 1:sh, :]
            ai = ai_ref[q, sh - 1:sh, :]
            re, im = re + ar * rs - ai * js, im + ar * js + ai * rs
        pw_r, pw_i = ar_ref[q], ai_ref[q]
        out_r, out_i = [], []
        for k in range(nblk):
            hr, hi = re[k], im[k]
            if k % (nk // 8):
                hr, hi = hr + pw_r * cr - pw_i * ci, hi + pw_r * ci + pw_i * cr
            cr, ci = hr[7:8, :], hi[7:8, :]
            out_r.append(hr)
            out_i.append(hi)
            if (k + 1) % (nk // 8) == 0:
                b = k // (nk // 8)
                hre_ref[q, b:b + 1, :] = cr
                him_ref[q, b:b + 1, :] = ci
        re = jnp.concatenate(out_r, axis=0)
        im = jnp.concatenate(out_i, axis=0)
        pre = jnp.where(rows >= 1, pltpu.roll(re, 1, axis=0), 0.0).astype(BF16)
        pim = jnp.where(rows >= 1, pltpu.roll(im, 1, axis=0), 0.0).astype(BF16)
        for j in (q, q + half):
            yt_ref[j] = (yt_ref[j] + dnt(cpre_ref[j].astype(BF16), pre)
                         + dnt(cpim_ref[j].astype(BF16), pim))

    for t in range(CHUNK):
        yt = jnp.concatenate(
            [yt_ref[j, t * S5_GROUP:(t + 1) * S5_GROUP, :] for j in range(gb)], axis=0)
        y_ref[pl.ds(t, nrow, stride=CHUNK), :] = yt.T


def _s5_prompt(proj_p, lmat, cpre, cpim, ar, ai, *, nb, seq):
    g, gb, half = S5_GROUPS, GROUPS_PER_TILE, PAIRS_PER_TILE
    n_p = nb * seq
    nk = seq // CHUNK
    blk = lambda n, r, c: pl.BlockSpec((n, r, c), lambda i: (i, 0, 0))
    return pl.pallas_call(
        functools.partial(_s5p_body, nb=nb, nk=nk),
        grid=(g // gb,),
        in_specs=[pl.BlockSpec((n_p, 128), lambda i: (0, i)),
                  blk(gb, L_ROWS, CW), blk(gb, CW, 2 * S5_STATE), blk(gb, CW, 2 * S5_STATE),
                  blk(half, 8, 2 * S5_STATE), blk(half, 8, 2 * S5_STATE)],
        out_specs=(pl.BlockSpec((n_p, 128), lambda i: (0, i)),
                   blk(half, nb, 2 * S5_STATE), blk(half, nb, 2 * S5_STATE)),
        out_shape=(jax.ShapeDtypeStruct((n_p, D_S5), F32),
                   jax.ShapeDtypeStruct((g // 2, nb, 2 * S5_STATE), F32),
                   jax.ShapeDtypeStruct((g // 2, nb, 2 * S5_STATE), F32)),
        scratch_shapes=[pltpu.VMEM((gb, CW, nb * nk), F32), pltpu.VMEM((gb, CW, nb * nk), F32)],
        compiler_params=pltpu.CompilerParams(
            dimension_semantics=("parallel",), vmem_limit_bytes=VMEM_LIMIT),
        name="s5_prompt",
    )(proj_p, lmat, cpre, cpim, ar, ai)


def _s5d_body(u_ref, hre_ref, him_ref, bd_ref, cd_ref, lr_ref, li_ref, y_ref, ore_ref, oim_ref):
    p = S5_STATE
    gb = GROUPS_PER_TILE
    tn = (((0,), (0,)), ((), ()))
    ut = u_ref[...].T
    yts = []
    for j in range(gb):
        ug = ut[j * S5_GROUP:(j + 1) * S5_GROUP, :].astype(BF16)
        bu = lax.dot_general(bd_ref[j].astype(BF16), ug, tn,
                             preferred_element_type=F32)
        h0r, h0i = hre_ref[j], him_ref[j]
        lbr, lbi = lr_ref[j], li_ref[j]
        hr = lbr * h0r - lbi * h0i + bu[0:p]
        hi = lbr * h0i + lbi * h0r + bu[p:2 * p]
        ore_ref[j] = hr
        oim_ref[j] = hi
        h = jnp.concatenate([hr, hi], axis=0).astype(BF16)
        yts.append(jnp.dot(cd_ref[j].astype(BF16), h, preferred_element_type=F32))
    y_ref[...] = jnp.concatenate(yts, axis=0).T


def _s5_decode(proj_d, h0_re, h0_im, bd, cd, lr, li):
    g, gb, p = S5_GROUPS, GROUPS_PER_TILE, S5_STATE
    nbatch = proj_d.shape[0]
    blk = lambda r, c: pl.BlockSpec((gb, r, c), lambda i: (i, 0, 0))
    cols = pl.BlockSpec((nbatch, 128), lambda i: (0, i))
    return pl.pallas_call(
        _s5d_body,
        grid=(g // gb,),
        in_specs=[cols, blk(p, nbatch), blk(p, nbatch), blk(S5_GROUP, 2 * p), blk(S5_GROUP, 2 * p),
                  blk(p, nbatch), blk(p, nbatch)],
        out_specs=(cols, blk(p, nbatch), blk(p, nbatch)),
        out_shape=(jax.ShapeDtypeStruct((nbatch, D_S5), F32),
                   jax.ShapeDtypeStruct((g, p, nbatch), F32),
                   jax.ShapeDtypeStruct((g, p, nbatch), F32)),
        compiler_params=pltpu.CompilerParams(dimension_semantics=("parallel",)),
        name="s5_decode",
    )(proj_d, h0_re, h0_im, bd, cd, lr, li)


def _lru_gates(xc, wa_ref, wx_ref, ba, bx, lam):
    xcb = xc.astype(BF16)
    nblk = D_LRU // MXU_WIDTH_V7X
    r_parts, i_parts = [], []
    for k in range(nblk):
        xk = xcb[:, k * MXU_WIDTH_V7X:(k + 1) * MXU_WIDTH_V7X]
        r_parts.append(jnp.dot(xk, wa_ref[k], preferred_element_type=F32))
        i_parts.append(jnp.dot(xk, wx_ref[k], preferred_element_type=F32))
    r = jax.nn.sigmoid(jnp.concatenate(r_parts, axis=1) + ba)
    i = jax.nn.sigmoid(jnp.concatenate(i_parts, axis=1) + bx)
    z = -lam
    softplus = jnp.maximum(z, 0.0) + jnp.log1p(jnp.exp(-jnp.abs(z)))
    log_a = (-LRU_C * softplus) * r
    a = jnp.exp(log_a)
    v = -jnp.tanh(log_a) * (a * a + 1.0)
    mult = jnp.where(v > 0.0, v * lax.rsqrt(v), 0.0)
    return a, mult * (i * xc)


def _lru_tile(xl_ref, gate_ref, cw_ref, cb_ref, wa_ref, wx_ref, ba_ref, bx_ref, lam_ref,
              o_ref, xbuf_ref, carry_ref):
    tt = xl_ref.shape[0]
    x = xl_ref[...]
    xbuf_ref[8:8 + tt, :] = x
    cw = cw_ref[...]
    xc = (cb_ref[...] + xbuf_ref[5:5 + tt, :] * cw[0:1] + xbuf_ref[6:6 + tt, :] * cw[1:2]
          + xbuf_ref[7:7 + tt, :] * cw[2:3] + x * cw[3:4])
    xbuf_ref[0:8, :] = x[tt - 8:tt, :]

    a, b = _lru_gates(xc, wa_ref, wx_ref, ba_ref[...], bx_ref[...], lam_ref[...])

    nblk = tt // 8
    a3 = a.reshape(nblk, 8, D_LRU)
    b3 = b.reshape(nblk, 8, D_LRU)
    row = lax.broadcasted_iota(jnp.int32, (nblk, 8, D_LRU), 1)
    for sh in (1, 2, 4):
        keep = row >= sh
        bs = jnp.where(keep, pltpu.roll(b3, sh, axis=1), 0.0)
        sa = jnp.where(keep, pltpu.roll(a3, sh, axis=1), 1.0)
        b3 = b3 + a3 * bs
        a3 = a3 * sa
    carry = carry_ref[0:1, :]
    gate = gate_ref[...]
    for k in range(nblk):
        h = b3[k] + a3[k] * carry
        carry = h[7:8, :]
        o_ref[k * 8:(k + 1) * 8, :] = h * gate[k * 8:(k + 1) * 8, :]
    carry_ref[...] = jnp.broadcast_to(carry, (8, D_LRU))
    return carry


def _mix_s5_part(ys, u, x, dsk_ref, wg_ref, bg_ref, gs_ref, wo_ref):
    yy = ys + dsk_ref[...] * u
    g = _gelu(yy)
    z = jnp.dot(g.astype(BF16), wg_ref[...], preferred_element_type=F32) + bg_ref[...]
    s5o = g * jax.nn.sigmoid(z)
    n1 = _rms(s5o, gs_ref[...]).astype(BF16)
    return x + jnp.dot(n1, wo_ref[0:D_S5, :], preferred_element_type=F32)


def _mix_lru_part(lru, gl_ref, wo_ref):
    n2 = _rms(lru, gl_ref[...]).astype(BF16)
    return jnp.dot(n2, wo_ref[D_S5:, :], preferred_element_type=F32)


def _lru_mix_body(xl_ref, gate_ref, ys_ref, u_ref, x_ref,
                  cw_ref, cb_ref, wa_ref, wx_ref, ba_ref, bx_ref, lam_ref,
                  dsk_ref, wg_ref, bg_ref, gs_ref, gl_ref, wo_ref,
                  o_ref, hl_ref, xbuf_ref, carry_ref, lru_ref):
    @pl.when(pl.program_id(1) == 0)
    def _():
        xbuf_ref[0:8, :] = jnp.zeros((8, D_LRU), F32)
        carry_ref[...] = jnp.zeros((8, D_LRU), F32)

    o_ref[...] = _mix_s5_part(ys_ref[...], u_ref[...], x_ref[...],
                              dsk_ref, wg_ref, bg_ref, gs_ref, wo_ref)
    hl_ref[0] = _lru_tile(xl_ref, gate_ref, cw_ref, cb_ref, wa_ref, wx_ref, ba_ref, bx_ref, lam_ref,
                          lru_ref, xbuf_ref, carry_ref)
    o_ref[...] += _mix_lru_part(lru_ref[...], gl_ref, wo_ref)


def _lru_mix_prompt(proj_p, ys_p, x1_p, cw, cb, wa, wx, ba, bx, lam, dsk, wg, bg, gs, gl, wo,
                    *, nb, seq, tt=LRU_TT):
    nt = seq // tt
    rows = lambda c, col: pl.BlockSpec((tt, c), lambda b, t: (b * nt + t, col))
    once = lambda shape: pl.BlockSpec(shape, lambda b, t: (0,) * len(shape),
                                      pipeline_mode=pl.Buffered(1))
    return pl.pallas_call(
        _lru_mix_body,
        grid=(nb, nt),
        in_specs=[
            rows(D_LRU, 1), rows(D_LRU, 2),
            rows(D_S5, 0), rows(D_S5, 0), rows(D_MODEL, 0),
            once((CONV_W, D_LRU)), once((1, D_LRU)),
            once((D_LRU // 256, 256, 256)), once((D_LRU // 256, 256, 256)),
            once((1, D_LRU)), once((1, D_LRU)), once((1, D_LRU)),
            once((1, D_S5)), once((D_S5, D_S5)), once((1, D_S5)), once((1, D_S5)), once((1, D_LRU)),
            once((D_MODEL, D_MODEL)),
        ],
        out_specs=(pl.BlockSpec((tt, D_MODEL), lambda b, t: (b * nt + t, 0)),
                   pl.BlockSpec((1, 1, D_LRU), lambda b, t: (b, 0, 0))),
        out_shape=(jax.ShapeDtypeStruct((nb * seq, D_MODEL), F32),
                   jax.ShapeDtypeStruct((nb, 1, D_LRU), F32)),
        scratch_shapes=[pltpu.VMEM((tt + 8, D_LRU), F32), pltpu.VMEM((8, D_LRU), F32),
                        pltpu.VMEM((tt, D_LRU), F32)],
        compiler_params=pltpu.CompilerParams(
            dimension_semantics=("parallel", "arbitrary"), vmem_limit_bytes=VMEM_LIMIT),
        name="lru_mix_prompt",
    )(proj_p, proj_p, ys_p, proj_p, x1_p, cw, cb, wa, wx, ba, bx, lam, dsk, wg, bg, gs, gl, wo)


def _lru_decode_body(xl_ref, gate_ref, conv_ref, h0_ref, cw_ref, cb_ref,
                     wa_ref, wx_ref, ba_ref, bx_ref, lam_ref, o_ref, h_ref, buf_ref):
    x = xl_ref[...]
    cw = cw_ref[...]
    c0, c1, c2 = (conv_ref[k] for k in range(CONV_W - 1))
    xc = cb_ref[...] + c0 * cw[0:1] + c1 * cw[1:2] + c2 * cw[2:3] + x * cw[3:4]
    a, b = _lru_gates(xc, wa_ref, wx_ref, ba_ref[...], bx_ref[...], lam_ref[...])
    h = a * h0_ref[...] + b
    h_ref[...] = h
    o_ref[...] = h * gate_ref[...]
    for k, rows in enumerate((c1, c2, x)):
        buf_ref[k] = rows


def _lru_decode(proj_d, conv0, h0, cw, cb, wa, wx, ba, bx, lam):
    nd = proj_d.shape[0]
    full = lambda r: pl.BlockSpec((r, D_LRU), lambda i: (0, 0))
    conv = pl.BlockSpec((CONV_W - 1, nd, D_LRU), lambda i: (0, 0, 0))
    wspec = pl.BlockSpec((D_LRU // 256, 256, 256), lambda i: (0, 0, 0))
    return pl.pallas_call(
        _lru_decode_body,
        grid=(1,),
        in_specs=[
            pl.BlockSpec((nd, D_LRU), lambda i: (0, 1)),
            pl.BlockSpec((nd, D_LRU), lambda i: (0, 2)),
            conv, full(nd),
            full(CONV_W), full(1), wspec, wspec, full(1), full(1), full(1),
        ],
        out_specs=(full(nd), full(nd), conv),
        out_shape=(jax.ShapeDtypeStruct((nd, D_LRU), F32),
                   jax.ShapeDtypeStruct((nd, D_LRU), F32),
                   jax.ShapeDtypeStruct((CONV_W - 1, nd, D_LRU), F32)),
        name="lru_decode",
    )(proj_d, proj_d, conv0, h0, cw, cb, wa, wx, ba, bx, lam)


def _mix_decode_body(ys_ref, u_ref, lru_ref, x_ref, dsk_ref, wg_ref, bg_ref, gs_ref, gl_ref,
                     wo_ref, o_ref):
    o_ref[...] = (_mix_s5_part(ys_ref[...], u_ref[...], x_ref[...],
                               dsk_ref, wg_ref, bg_ref, gs_ref, wo_ref)
                  + _mix_lru_part(lru_ref[...], gl_ref, wo_ref))


def _mix_decode(ys_d, proj_d, lru_d, x1_d, dsk, wg, bg, gs, gl, wo):
    nd = x1_d.shape[0]
    full = lambda r, c: pl.BlockSpec((r, c), lambda i: (0, 0))
    return pl.pallas_call(
        _mix_decode_body,
        grid=(1,),
        in_specs=[full(nd, D_S5), full(nd, D_S5), full(nd, D_LRU), full(nd, D_MODEL),
                  full(1, D_S5), full(D_S5, D_S5), full(1, D_S5), full(1, D_S5), full(1, D_LRU),
                  full(D_MODEL, D_MODEL)],
        out_specs=full(nd, D_MODEL),
        out_shape=jax.ShapeDtypeStruct((nd, D_MODEL), F32),
        compiler_params=pltpu.CompilerParams(vmem_limit_bytes=VMEM_LIMIT),
        name="mix_decode",
    )(ys_d, proj_d, lru_d, x1_d, dsk, wg, bg, gs, gl, wo)


def _unpair(h, nb):
    tiles = S5_GROUPS // GROUPS_PER_TILE
    h5 = h.reshape(tiles, PAIRS_PER_TILE, nb, 2, S5_STATE)
    return jnp.transpose(h5, (2, 0, 3, 1, 4)).reshape(nb, S5_GROUPS, S5_STATE)


def _block_diag4(w):
    w4 = w.reshape(LRU_HEADS // 4, 4, LRU_HEAD_DIM, LRU_HEAD_DIM)
    eye = jnp.eye(4, dtype=w.dtype)
    return jnp.einsum("kaij,ab->kaibj", w4, eye).reshape(LRU_HEADS // 4, 256, 256)


def kernel(x_prompt, x_sample, state_s5_re, state_s5_im, state_lru_h, state_lru_conv, g_ffn1, w1_a, w3_a, w2_a, g_mix, w_in, lam_re, lam_im, log_dt, b_re, b_im, c_re, c_im, d_skip, w_glu, b_glu, conv_w, conv_b, w_a, b_a, w_x, b_x, lam_l, g_out_s5, g_out_lru, w_out, g_ffn2, w1_b, w3_b, w2_b, g_final):
    nb, seq, _ = x_prompt.shape
    nd = x_sample.shape[0]
    n_p = nb * seq
    row = lambda v: v.reshape(1, -1)
    assert w1_a.shape[0] == 1 and x_sample.shape[1] == 1
    assert nd == 128 and n_p % FFN_TM == 0 and n_p > FFN_TM
    assert seq % LRU_TT == 0 and seq % (8 * CHUNK) == 0

    xp = x_prompt.reshape(n_p, D_MODEL)
    xd = x_sample.reshape(nd, D_MODEL)

    n_i, n_f = n_p // FFN_TM, D_FF // FFN_TF
    f_div = FFN_TF // FFN_TF_F32
    first = _ffn(
        xp, xd, row(g_ffn1[0]), w1_a[0], w3_a[0], w2_a[0], n_tiles=1,
        casts=(_cast_job(w1_b[0], n_i, n_f, f_div=f_div), _cast_job(w3_b[0], n_i, n_f, f_div=f_div),
               _row_cast_job(w2_b[0], n_i, n_f, f_div=f_div), _flat_cast_job(w_in[0], D_FF // FFN_TF_F32)))
    x1_d, (w1_a16, w3_a16, w2_a16), w_in16 = first[1], first[2:5], first[8]
    x1_p, w1_b16, w3_b16, w2_b16 = _ffn(
        xp, None, row(g_ffn1[0]), w1_a16, w3_a16, w2_a16, first_tile=1, n_tiles=n_i - 1,
        casts=(_cast_job(w1_b[0], n_i, n_f, i0=1), _cast_job(w3_b[0], n_i, n_f, i0=1),
               _row_cast_job(w2_b[0], n_i, n_f, i0=1)),
        prev=first[:1] + first[5:8])
    proj_p, proj_d, w_out16, w_glu16 = _inproj(
        x1_p, x1_d, row(g_mix[0]), w_in16,
        casts=(_cast_job(w_out[0], n_i, D_IN // PROJ_TN, bc=PROJ_TN),
               _cast_job(w_glu[0], n_i, D_IN // PROJ_TN, bc=PROJ_TN)))

    lmat, cpre, cpim, bd, cd, ar, ai, lr, li = _s5_prep(
        lam_re[0], lam_im[0], log_dt[0], b_re[0], b_im[0], c_re[0], c_im[0])
    ys_p, hf_re, hf_im = _s5_prompt(proj_p, lmat, cpre, cpim, ar, ai, nb=nb, seq=seq)

    to_gpb = lambda s: jnp.transpose(s, (1, 2, 0))
    ys_d, hd_re, hd_im = _s5_decode(proj_d, to_gpb(state_s5_re[0]), to_gpb(state_s5_im[0]),
                                    bd, cd, lr, li)

    wa_bd = _block_diag4(w_a[0]).astype(BF16)
    wx_bd = _block_diag4(w_x[0]).astype(BF16)
    lru_args = (conv_w[0], row(conv_b[0]), wa_bd, wx_bd, row(b_a[0]), row(b_x[0]), row(lam_l[0]))
    lru_d, hl_d, buf_d = _lru_decode(proj_d, jnp.transpose(state_lru_conv[0], (1, 0, 2)),
                                     state_lru_h[0], *lru_args)

    mix_args = (row(d_skip[0]), w_glu16, row(b_glu[0]), row(g_out_s5[0]), row(g_out_lru[0]), w_out16)
    x2_p, hl_p = _lru_mix_prompt(proj_p, ys_p, x1_p, *lru_args, *mix_args, nb=nb, seq=seq)
    x2_d = _mix_decode(ys_d, proj_d, lru_d, x1_d, *mix_args)
    ffn2 = (row(g_ffn2[0]), w1_b16, w3_b16, w2_b16, row(g_final))
    y_p0, y_d = _ffn(x2_p, x2_d, *ffn2, n_tiles=1)
    y_p, = _ffn(x2_p, None, *ffn2, first_tile=1, n_tiles=n_i - 1, prev=(y_p0,))

    tail_p = proj_p.reshape(nb, seq, -1)[:, seq - (CONV_W - 1):, D_S5:D_S5 + D_LRU]
    return (
        y_p.reshape(nb, seq, D_MODEL),
        y_d.reshape(nd, 1, D_MODEL),
        _unpair(hf_re, nb)[None],
        _unpair(hf_im, nb)[None],
        hl_p.reshape(1, nb, D_LRU),
        tail_p[None],
        jnp.transpose(hd_re, (2, 0, 1))[None],
        jnp.transpose(hd_im, (2, 0, 1))[None],
        hl_d[None],
        jnp.transpose(buf_d, (1, 0, 2))[None],
    )
```

```python
import functools

import jax
import jax.numpy as jnp
from jax import lax
from jax.experimental import pallas as pl
from jax.experimental.pallas import tpu as pltpu

F32 = jnp.float32
BF16 = jnp.bfloat16

D_MODEL = 2048
D_S5 = 1024
S5_GROUP = 16
S5_GROUPS = 64
S5_STATE = 64
D_LRU = 1024
LRU_HEADS = 16
LRU_HEAD_DIM = 64
CONV_W = 4
LRU_C = 8.0
D_FF = 5632
D_IN = D_S5 + 2 * D_LRU
EPS = 1e-6

MXU_WIDTH_V7X = 256
CHUNK = MXU_WIDTH_V7X // S5_GROUP
CW = CHUNK * S5_GROUP

FFN_TM = 1024
FFN_TF = 512
FFN_TF_F32 = FFN_TF // 2
PROJ_TN = 1024
LRU_TT = 512

VMEM_CAPACITY_V7X = 64 * 1024 * 1024
VMEM_LIMIT = VMEM_CAPACITY_V7X - 6 * 1024 * 1024
FFN_VMEM_LIMIT = VMEM_CAPACITY_V7X - 4 * 1024 * 1024

NN = (((1,), (0,)), ((), ()))
NT = (((1,), (1,)), ((), ()))


def _rms(x, g):
    return x * lax.rsqrt(jnp.mean(x * x, axis=-1, keepdims=True) + EPS) * g


def _split(x):
    hi = x.astype(BF16)
    lo = (x - hi.astype(F32)).astype(BF16)
    return hi, lo


def _dot3(a, b, dims=NN):
    ah, al = _split(a)
    bh, bl = _split(b)
    d = functools.partial(lax.dot_general, dimension_numbers=dims, preferred_element_type=F32)
    return d(ah, bh) + d(al, bh) + d(ah, bl)


def _gelu(x):
    return jax.nn.gelu(x, approximate=True)


def _ffn_body(*refs, final_norm, convert, has_decode, n_casts, n_prev):
    refs = list(refs)
    xn_ref = refs.pop()
    xp_ref = refs.pop(0)
    xd_ref = refs.pop(0) if has_decode else None
    g_ref, w1_ref, w3_ref, w2_ref = (refs.pop(0) for _ in range(4))
    gf_ref = refs.pop(0) if final_norm else None
    cast_in = [refs.pop(0) for _ in range(n_casts)]
    del refs[:n_prev]
    op_ref = refs.pop(0)
    od_ref = refs.pop(0) if has_decode else None
    wcopy = [refs.pop(0) for _ in range(3)] if convert else []
    cast_out = refs
    f = pl.program_id(1)
    tm = xp_ref.shape[0]
    parts = [(xp_ref, op_ref, slice(0, tm))]
    if has_decode:
        parts.append((xd_ref, od_ref, slice(tm, None)))

    for src_ref, dst_ref in zip(cast_in, cast_out):
        dst_ref[...] = src_ref[...].astype(BF16)

    @pl.when(f == 0)
    def _():
        for x_ref, o_ref, rows in parts:
            x = x_ref[...]
            xn_ref[rows, :] = _rms(x, g_ref[...]).astype(BF16)
            o_ref[...] = x

    if convert:
        w1, w3, w2 = (w_ref[...].astype(BF16) for w_ref in (w1_ref, w3_ref, w2_ref))
        for dst_ref, w in zip(wcopy, (w1, w3, w2)):
            dst_ref[...] = w
    else:
        w1, w3, w2 = w1_ref[...], w3_ref[...], w2_ref[...]

    xn = xn_ref[...]
    a = jnp.dot(xn, w1, preferred_element_type=F32)
    b = jnp.dot(xn, w3, preferred_element_type=F32)
    h = (a * jax.nn.sigmoid(a) * b).astype(BF16)
    upd = 0.5 * jnp.dot(h, w2, preferred_element_type=F32)
    for _, o_ref, rows in parts:
        o_ref[...] += upd[rows]

    if final_norm:
        @pl.when(f == pl.num_programs(1) - 1)
        def _():
            for _, o_ref, _ in parts:
                o_ref[...] = _rms(o_ref[...], gf_ref[...])


def _cast_job(w, n_i, n_f, bc=FFN_TF, i0=0):
    rows, cols = w.shape
    br = rows // n_i
    n_cb = cols // bc
    assert br * n_i == rows and bc * n_cb == cols and n_cb <= n_f and br % 16 == 0
    return w, pl.BlockSpec(
        (br, bc), lambda i, f: (jnp.minimum(i + i0, n_i - 1),
                                jnp.where(i + i0 < n_i, jnp.minimum(f, n_cb - 1), n_cb - 1)))


def _row_cast_job(w, n_i, n_f, i0=0):
    rows, cols = w.shape
    br = rows // (n_i * n_f)
    assert br * n_i * n_f == rows and br % 16 == 0
    return w, pl.BlockSpec((br, cols), lambda i, f: ((i + i0) * n_f + f, 0))


def _walk_cast_job(w, n_outer, n_inner):
    rows, cols = w.shape
    n_blocks = max(n for n in range(1, n_outer * n_inner + 1) if rows % (16 * n) == 0)
    return w, pl.BlockSpec((rows // n_blocks, cols),
                           lambda i, f: (jnp.minimum(i * n_inner + f, n_blocks - 1), 0))


def _ffn(xp, xd, g, w1, w3, w2, g_final=None, *, casts=(), first_tile=0, n_tiles=None, prev=(),
         tm=FFN_TM):
    n_p = xp.shape[0]
    n_tiles = n_p // tm if n_tiles is None else n_tiles
    has_decode = xd is not None
    nd = xd.shape[0] if has_decode else 0
    assert not has_decode or n_tiles == 1
    final_norm = g_final is not None
    convert = w1.dtype == F32
    tf = FFN_TF_F32 if convert else FFN_TF
    pspec = pl.BlockSpec((tm, D_MODEL), lambda i, f: (i + first_tile, 0))
    dspec = pl.BlockSpec((nd, D_MODEL), lambda i, f: (0, 0))
    vspec = pl.BlockSpec((1, D_MODEL), lambda i, f: (0, 0))
    xspec = pspec if n_tiles > 1 else pl.BlockSpec(
        (tm, D_MODEL), lambda i, f: (i + first_tile, 0), pipeline_mode=pl.Buffered(1))
    w13spec = pl.BlockSpec((D_MODEL, tf), lambda i, f: (0, f))
    w2spec = pl.BlockSpec((tf, D_MODEL), lambda i, f: (f, 0))
    sd = jax.ShapeDtypeStruct
    in_specs, args = [xspec], [xp]
    out_specs, out_shape = [pspec], [sd((n_p, D_MODEL), F32)]
    if has_decode:
        in_specs.append(dspec)
        args.append(xd)
        out_specs.append(dspec)
        out_shape.append(sd((nd, D_MODEL), F32))
    in_specs += [vspec, w13spec, w13spec, w2spec]
    args += [g, w1, w3, w2]
    if final_norm:
        in_specs.append(vspec)
        args.append(g_final)
    in_specs += [spec for _, spec in casts]
    args += [w for w, _ in casts]
    n_in = len(args)
    in_specs += [pl.BlockSpec(memory_space=pl.ANY)] * len(prev)
    args += list(prev)
    if convert:
        out_specs += [w13spec, w13spec, w2spec]
        out_shape += [sd(w.shape, BF16) for w in (w1, w3, w2)]
    out_specs += [spec for _, spec in casts]
    out_shape += [sd(w.shape, BF16) for w, _ in casts]
    assert len(prev) <= len(out_shape)
    return pl.pallas_call(
        functools.partial(_ffn_body, final_norm=final_norm, convert=convert,
                          has_decode=has_decode, n_casts=len(casts), n_prev=len(prev)),
        grid=(n_tiles, D_FF // tf),
        in_specs=in_specs,
        out_specs=tuple(out_specs),
        out_shape=tuple(out_shape),
        input_output_aliases={n_in + k: k for k in range(len(prev))},
        scratch_shapes=[pltpu.VMEM((tm + nd, D_MODEL), BF16)],
        compiler_params=pltpu.CompilerParams(
            dimension_semantics=("arbitrary", "arbitrary"), vmem_limit_bytes=FFN_VMEM_LIMIT),
        name=("ffn_final" if final_norm else "ffn") + ("_first" if has_decode else ""),
    )(*args)


def _inproj_body(*refs, n_tiles, nd, n_casts, n_prev):
    xp_ref, xd_ref, g_ref, w_ref = refs[:4]
    cast_in = refs[4:4 + n_casts]
    outs = refs[4 + n_casts + n_prev:-1]
    op_ref, od_ref = outs[:2]
    cast_out = outs[2:]
    xn_ref = refs[-1]
    i = pl.program_id(0)
    j = pl.program_id(1)
    nj = pl.num_programs(1)

    for src_ref, dst_ref in zip(cast_in, cast_out):
        dst_ref[...] = src_ref[...].astype(BF16)

    def run(x_ref, o_ref, rows):
        @pl.when(j == 0)
        def _():
            xn_ref[0:rows, :] = _rms(x_ref[...], g_ref[...]).astype(BF16)

        @pl.when(j < nj - 1)
        def _():
            o_ref[...] = jnp.dot(xn_ref[0:rows, :], w_ref[...], preferred_element_type=F32)

        @pl.when(j == nj - 1)
        def _():
            o_ref[...] = _gelu(jnp.dot(xn_ref[0:rows, :], w_ref[...],
                                       preferred_element_type=F32))

    @pl.when(i < n_tiles)
    def _():
        run(xp_ref, op_ref, xp_ref.shape[0])

    @pl.when(i == n_tiles)
    def _():
        run(xd_ref, od_ref, nd)


def _inproj(xp, xd, g, w, *, casts=(), prev=(), tm=FFN_TM, tn=PROJ_TN):
    n_p, nd = xp.shape[0], xd.shape[0]
    n_tiles = n_p // tm
    d_out = w.shape[1]
    nj = d_out // tn
    n_in = 4 + len(casts)
    first_aliased = 2 + len(casts) - len(prev)
    return pl.pallas_call(
        functools.partial(_inproj_body, n_tiles=n_tiles, nd=nd, n_casts=len(casts),
                          n_prev=len(prev)),
        grid=(n_tiles + 1, nj),
        in_specs=[
            pl.BlockSpec((tm, D_MODEL), lambda i, j: (jnp.minimum(i, n_tiles - 1), 0)),
            pl.BlockSpec((nd, D_MODEL), lambda i, j: (0, 0)),
            pl.BlockSpec((1, D_MODEL), lambda i, j: (0, 0)),
            pl.BlockSpec((D_MODEL, tn), lambda i, j: (0, j)),
        ] + [spec for _, spec in casts] + [pl.BlockSpec(memory_space=pl.ANY)] * len(prev),
        out_specs=(
            pl.BlockSpec((tm, tn), lambda i, j: (jnp.minimum(i, n_tiles - 1),
                                                 jnp.where(i < n_tiles, j, nj - 1))),
            pl.BlockSpec((nd, tn), lambda i, j: (0, jnp.where(i < n_tiles, 0, j))),
        ) + tuple(spec for _, spec in casts),
        out_shape=(jax.ShapeDtypeStruct((n_p, d_out), F32),
                   jax.ShapeDtypeStruct((nd, d_out), F32))
        + tuple(jax.ShapeDtypeStruct(cw.shape, BF16) for cw, _ in casts),
        input_output_aliases={n_in + k: first_aliased + k for k in range(len(prev))},
        scratch_shapes=[pltpu.VMEM((tm, D_MODEL), BF16)],
        compiler_params=pltpu.CompilerParams(
            dimension_semantics=("arbitrary", "arbitrary"), vmem_limit_bytes=VMEM_LIMIT),
        name="inproj",
    )(xp, xd, g, w, *[cw for cw, _ in casts], *prev)


GROUPS_PER_TILE = 128 // S5_GROUP
PAIRS_PER_TILE = GROUPS_PER_TILE // 2
L_ROWS = CW + 2 * S5_STATE


def _lam_bar(lam_re, lam_im, log_dt):
    dt = jnp.exp(log_dt)
    mag = jnp.exp(lam_re * dt)
    ang = lam_im * dt
    return mag * jnp.cos(ang), mag * jnp.sin(ang)


def _s5_prep_body(lre_ref, lim_ref, ldt_ref, bre_ref, bim_ref, cre_ref, cim_ref,
                  l_ref, cpre_ref, cpim_ref, bd_ref, cd_ref, ar_ref, ai_ref, lr_ref, li_ref,
                  bp_ref, lrow_ref):
    gb, half = GROUPS_PER_TILE, PAIRS_PER_TILE
    p = S5_STATE
    lo, hi = slice(0, p), slice(p, 2 * p)
    lam_re = lre_ref[...]
    lam_im = lim_ref[...]
    lbr, lbi = _lam_bar(lam_re, lam_im, ldt_ref[...])
    lrow_ref[:, :, lo] = lbr
    lrow_ref[:, :, hi] = lbi
    for j in range(gb):
        cols = jnp.broadcast_to(lrow_ref[j], (2 * p, 2 * p)).T
        lr_ref[j] = cols[0:p]
        li_ref[j] = cols[p:2 * p]
    nr = lbr - 1.0
    den = lam_re * lam_re + lam_im * lam_im
    cr = (nr * lam_re + lbi * lam_im) / den
    ci = (lbi * lam_re - nr * lam_im) / den
    b_re = bre_ref[...]
    b_im = bim_ref[...]
    bbr = cr * b_re - ci * b_im
    bbi = cr * b_im + ci * b_re
    bd_ref[:, :, lo] = bbr
    bd_ref[:, :, hi] = bbi
    c_re = cre_ref[...]
    c_im = cim_ref[...]
    cd_ref[:, :, lo] = c_re
    cd_ref[:, :, hi] = -c_im

    zeros = jnp.zeros((half, S5_GROUP, p), F32)
    pr = jnp.ones_like(lbr)
    pi = jnp.zeros_like(lbr)
    for d in range(CHUNK):
        rows = slice(d * S5_GROUP, (d + 1) * S5_GROUP)
        back = slice((CHUNK - 1 - d) * S5_GROUP, (CHUNK - d) * S5_GROUP)
        bp_ref[:, back, lo] = bbr * pr - bbi * pi
        bp_ref[:, back, hi] = bbr * pi + bbi * pr
        pr, pi = pr * lbr - pi * lbi, pr * lbi + pi * lbr
        cp_r = c_re * pr - c_im * pi
        cp_i = -(c_re * pi + c_im * pr)
        cpre_ref[0:half, rows, lo] = cp_r[0:half]
        cpre_ref[0:half, rows, hi] = zeros
        cpre_ref[half:gb, rows, lo] = zeros
        cpre_ref[half:gb, rows, hi] = cp_r[half:gb]
        cpim_ref[0:half, rows, lo] = cp_i[0:half]
        cpim_ref[0:half, rows, hi] = zeros
        cpim_ref[half:gb, rows, lo] = zeros
        cpim_ref[half:gb, rows, hi] = cp_i[half:gb]

    qr, qi = pr, pi
    for r in range(8):
        ar_ref[:, r:r + 1, lo] = qr[0:half]
        ar_ref[:, r:r + 1, hi] = qr[half:gb]
        ai_ref[:, r:r + 1, lo] = qi[0:half]
        ai_ref[:, r:r + 1, hi] = qi[half:gb]
        qr, qi = qr * pr - qi * pi, qr * pi + qi * pr

    lane = lax.broadcasted_iota(jnp.int32, (S5_GROUP, 128), 1)
    for j in range(gb):
        w = _dot3(cd_ref[j], bp_ref[j], NT)
        w0, w1 = w[:, :128], w[:, 128:]
        for t in range(CHUNK):
            rows = slice(t * S5_GROUP, (t + 1) * S5_GROUP)
            shift = (CHUNK - 1 - t) * S5_GROUP
            keep = 128 - shift % 128
            if shift == 0:
                left, right = w0, w1
            elif shift < 128:
                r0 = pltpu.roll(w0, keep, axis=1)
                r1 = pltpu.roll(w1, keep, axis=1)
                left = jnp.where(lane < keep, r0, r1)
                right = jnp.where(lane < keep, r1, 0.0)
            elif shift == 128:
                left, right = w1, jnp.zeros_like(w1)
            else:
                left = jnp.where(lane < keep, pltpu.roll(w1, keep, axis=1), 0.0)
                right = jnp.zeros_like(w1)
            l_ref[j, rows, 0:128] = left.astype(BF16)
            l_ref[j, rows, 128:256] = right.astype(BF16)
        l_ref[j, CW:L_ROWS, :] = bp_ref[j].T.astype(BF16)


def _s5_prep(lam_re, lam_im, log_dt, b_re, b_im, c_re, c_im):
    g, p, gb, half = S5_GROUPS, S5_STATE, GROUPS_PER_TILE, PAIRS_PER_TILE
    lre = lam_re.reshape(g, 1, p)
    lim = lam_im.reshape(g, 1, p)
    ldt = jnp.broadcast_to(log_dt.reshape(g, 1, 1), (g, 1, p))
    bre = jnp.transpose(b_re, (0, 2, 1))
    bim = jnp.transpose(b_im, (0, 2, 1))
    sd = jax.ShapeDtypeStruct
    blk = lambda n, r, c: pl.BlockSpec((n, r, c), lambda i: (i, 0, 0))
    return pl.pallas_call(
        _s5_prep_body,
        grid=(g // gb,),
        in_specs=[blk(gb, 1, p)] * 3 + [blk(gb, S5_GROUP, p)] * 4,
        out_specs=(
            blk(gb, L_ROWS, CW), blk(gb, CW, 2 * p), blk(gb, CW, 2 * p),
            blk(gb, S5_GROUP, 2 * p), blk(gb, S5_GROUP, 2 * p),
            blk(half, 8, 2 * p), blk(half, 8, 2 * p), blk(gb, p, 128), blk(gb, p, 128),
        ),
        out_shape=(
            sd((g, L_ROWS, CW), BF16),
            sd((g, CW, 2 * p), F32),
            sd((g, CW, 2 * p), F32),
            sd((g, S5_GROUP, 2 * p), F32),
            sd((g, S5_GROUP, 2 * p), F32),
            sd((g // 2, 8, 2 * p), F32),
            sd((g // 2, 8, 2 * p), F32),
            sd((g, p, 128), F32),
            sd((g, p, 128), F32),
        ),
        scratch_shapes=[pltpu.VMEM((gb, CW, 2 * p), F32), pltpu.VMEM((gb, 1, 2 * p), F32)],
        compiler_params=pltpu.CompilerParams(dimension_semantics=("parallel",)),
        name="s5_prep",
    )(lre, lim, ldt, bre, bim, c_re, c_im)


def _s5p_body(u_ref, l_ref, cpre_ref, cpim_ref, ar_ref, ai_ref, y_ref, hre_ref, him_ref,
              ut_ref, yt_ref, *, nb, nk):
    gb, half = GROUPS_PER_TILE, PAIRS_PER_TILE
    p = S5_STATE
    nrow = nb * nk
    d = functools.partial(jnp.dot, preferred_element_type=F32)

    for t in range(CHUNK):
        xt = u_ref[pl.ds(t, nrow, stride=CHUNK), :].T
        for j in range(gb):
            ut_ref[j, t * S5_GROUP:(t + 1) * S5_GROUP, :] = xt[j * S5_GROUP:(j + 1) * S5_GROUP, :]

    s_re, s_im = [], []
    for j in range(gb):
        r = d(l_ref[j], ut_ref[j].astype(BF16))
        yt_ref[j] = r[0:CW]
        s_re.append(r[CW:CW + p])
        s_im.append(r[CW + p:L_ROWS])

    nblk = nrow // 8
    row8 = lax.broadcasted_iota(jnp.int32, (nblk, 8, 2 * p), 1)
    rows = lax.broadcasted_iota(jnp.int32, (nrow, 2 * p), 0) & (nk - 1)
    dnt = functools.partial(lax.dot_general, dimension_numbers=NT, preferred_element_type=F32)
    for q in range(half):
        re = jnp.concatenate([s_re[q], s_re[q + half]], axis=0).T.reshape(nblk, 8, 2 * p)
        im = jnp.concatenate([s_im[q], s_im[q + half]], axis=0).T.reshape(nblk, 8, 2 * p)
        for sh in (1, 2, 4):
            keep = row8 >= sh
            rs = jnp.where(keep, pltpu.roll(re, sh, axis=1), 0.0)
            js = jnp.where(keep, pltpu.roll(im, sh, axis=1), 0.0)
            ar = ar_ref[q, sh - 1:sh, :]
            ai = ai_ref[q, sh - 1:sh, :]
            re, im = re + ar * rs - ai * js, im + ar * js + ai * rs
        pw_r, pw_i = ar_ref[q], ai_ref[q]
        out_r, out_i = [], []
        for k in range(nblk):
            hr, hi = re[k], im[k]
            if k % (nk // 8):
                hr, hi = hr + pw_r * cr - pw_i * ci, hi + pw_r * ci + pw_i * cr
            cr, ci = hr[7:8, :], hi[7:8, :]
            out_r.append(hr)
            out_i.append(hi)
            if (k + 1) % (nk // 8) == 0:
                b = k // (nk // 8)
                hre_ref[q, b:b + 1, :] = cr
                him_ref[q, b:b + 1, :] = ci
        re = jnp.concatenate(out_r, axis=0)
        im = jnp.concatenate(out_i, axis=0)
        pre = jnp.where(rows >= 1, pltpu.roll(re, 1, axis=0), 0.0).astype(BF16)
        pim = jnp.where(rows >= 1, pltpu.roll(im, 1, axis=0), 0.0).astype(BF16)
        for j in (q, q + half):
            yt_ref[j] = (yt_ref[j] + dnt(cpre_ref[j].astype(BF16), pre)
                         + dnt(cpim_ref[j].astype(BF16), pim))

    for t in range(CHUNK):
        yt = jnp.concatenate(
            [yt_ref[j, t * S5_GROUP:(t + 1) * S5_GROUP, :] for j in range(gb)], axis=0)
        y_ref[pl.ds(t, nrow, stride=CHUNK), :] = yt.T


def _s5_prompt(proj_p, lmat, cpre, cpim, ar, ai, *, nb, seq):
    g, gb, half = S5_GROUPS, GROUPS_PER_TILE, PAIRS_PER_TILE
    n_p = nb * seq
    nk = seq // CHUNK
    blk = lambda n, r, c: pl.BlockSpec((n, r, c), lambda i: (i, 0, 0))
    return pl.pallas_call(
        functools.partial(_s5p_body, nb=nb, nk=nk),
        grid=(g // gb,),
        in_specs=[pl.BlockSpec((n_p, 128), lambda i: (0, i)),
                  blk(gb, L_ROWS, CW), blk(gb, CW, 2 * S5_STATE), blk(gb, CW, 2 * S5_STATE),
                  blk(half, 8, 2 * S5_STATE), blk(half, 8, 2 * S5_STATE)],
        out_specs=(pl.BlockSpec((n_p, 128), lambda i: (0, i)),
                   blk(half, nb, 2 * S5_STATE), blk(half, nb, 2 * S5_STATE)),
        out_shape=(jax.ShapeDtypeStruct((n_p, D_S5), F32),
                   jax.ShapeDtypeStruct((g // 2, nb, 2 * S5_STATE), F32),
                   jax.ShapeDtypeStruct((g // 2, nb, 2 * S5_STATE), F32)),
        scratch_shapes=[pltpu.VMEM((gb, CW, nb * nk), F32), pltpu.VMEM((gb, CW, nb * nk), F32)],
        compiler_params=pltpu.CompilerParams(
            dimension_semantics=("parallel",), vmem_limit_bytes=VMEM_LIMIT),
        name="s5_prompt",
    )(proj_p, lmat, cpre, cpim, ar, ai)


def _s5d_body(u_ref, hre_ref, him_ref, bd_ref, cd_ref, lr_ref, li_ref, y_ref, ore_ref, oim_ref):
    p = S5_STATE
    gb = GROUPS_PER_TILE
    tn = (((0,), (0,)), ((), ()))
    ut = u_ref[...].T
    yts = []
    for j in range(gb):
        ug = ut[j * S5_GROUP:(j + 1) * S5_GROUP, :].astype(BF16)
        bu = lax.dot_general(bd_ref[j].astype(BF16), ug, tn,
                             preferred_element_type=F32)
        h0r, h0i = hre_ref[j], him_ref[j]
        lbr, lbi = lr_ref[j], li_ref[j]
        hr = lbr * h0r - lbi * h0i + bu[0:p]
        hi = lbr * h0i + lbi * h0r + bu[p:2 * p]
        ore_ref[j] = hr
        oim_ref[j] = hi
        h = jnp.concatenate([hr, hi], axis=0).astype(BF16)
        yts.append(jnp.dot(cd_ref[j].astype(BF16), h, preferred_element_type=F32))
    y_ref[...] = jnp.concatenate(yts, axis=0).T


def _s5_decode(proj_d, h0_re, h0_im, bd, cd, lr, li):
    g, gb, p = S5_GROUPS, GROUPS_PER_TILE, S5_STATE
    nbatch = proj_d.shape[0]
    blk = lambda r, c: pl.BlockSpec((gb, r, c), lambda i: (i, 0, 0))
    cols = pl.BlockSpec((nbatch, 128), lambda i: (0, i))
    return pl.pallas_call(
        _s5d_body,
        grid=(g // gb,),
        in_specs=[cols, blk(p, nbatch), blk(p, nbatch), blk(S5_GROUP, 2 * p), blk(S5_GROUP, 2 * p),
                  blk(p, nbatch), blk(p, nbatch)],
        out_specs=(cols, blk(p, nbatch), blk(p, nbatch)),
        out_shape=(jax.ShapeDtypeStruct((nbatch, D_S5), F32),
                   jax.ShapeDtypeStruct((g, p, nbatch), F32),
                   jax.ShapeDtypeStruct((g, p, nbatch), F32)),
        compiler_params=pltpu.CompilerParams(dimension_semantics=("parallel",)),
        name="s5_decode",
    )(proj_d, h0_re, h0_im, bd, cd, lr, li)


def _lru_gates(xc, wa_ref, wx_ref, ba, bx, lam):
    xcb = xc.astype(BF16)
    nblk = D_LRU // MXU_WIDTH_V7X
    r_parts, i_parts = [], []
    for k in range(nblk):
        xk = xcb[:, k * MXU_WIDTH_V7X:(k + 1) * MXU_WIDTH_V7X]
        r_parts.append(jnp.dot(xk, wa_ref[k], preferred_element_type=F32))
        i_parts.append(jnp.dot(xk, wx_ref[k], preferred_element_type=F32))
    r = jax.nn.sigmoid(jnp.concatenate(r_parts, axis=1) + ba)
    i = jax.nn.sigmoid(jnp.concatenate(i_parts, axis=1) + bx)
    z = -lam
    softplus = jnp.maximum(z, 0.0) + jnp.log1p(jnp.exp(-jnp.abs(z)))
    log_a = (-LRU_C * softplus) * r
    a = jnp.exp(log_a)
    v = -jnp.tanh(log_a) * (a * a + 1.0)
    mult = jnp.where(v > 0.0, v * lax.rsqrt(v), 0.0)
    return a, mult * (i * xc)


def _lru_tile(xl_ref, gate_ref, cw_ref, cb_ref, wa_ref, wx_ref, ba_ref, bx_ref, lam_ref,
              o_ref, xbuf_ref, carry_ref):
    tt = xl_ref.shape[0]
    x = xl_ref[...]
    xbuf_ref[8:8 + tt, :] = x
    cw = cw_ref[...]
    xc = (cb_ref[...] + xbuf_ref[5:5 + tt, :] * cw[0:1] + xbuf_ref[6:6 + tt, :] * cw[1:2]
          + xbuf_ref[7:7 + tt, :] * cw[2:3] + x * cw[3:4])
    xbuf_ref[0:8, :] = x[tt - 8:tt, :]

    a, b = _lru_gates(xc, wa_ref, wx_ref, ba_ref[...], bx_ref[...], lam_ref[...])

    nblk = tt // 8
    a3 = a.reshape(nblk, 8, D_LRU)
    b3 = b.reshape(nblk, 8, D_LRU)
    row = lax.broadcasted_iota(jnp.int32, (nblk, 8, D_LRU), 1)
    for sh in (1, 2, 4):
        keep = row >= sh
        bs = jnp.where(keep, pltpu.roll(b3, sh, axis=1), 0.0)
        sa = jnp.where(keep, pltpu.roll(a3, sh, axis=1), 1.0)
        b3 = b3 + a3 * bs
        a3 = a3 * sa
    carry = carry_ref[0:1, :]
    gate = gate_ref[...]
    for k in range(nblk):
        h = b3[k] + a3[k] * carry
        carry = h[7:8, :]
        o_ref[k * 8:(k + 1) * 8, :] = h * gate[k * 8:(k + 1) * 8, :]
    carry_ref[...] = jnp.broadcast_to(carry, (8, D_LRU))
    return carry


def _mix_s5_part(ys, u, x, dsk_ref, wg_ref, bg_ref, gs_ref, wo_ref):
    yy = ys + dsk_ref[...] * u
    g = _gelu(yy)
    z = jnp.dot(g.astype(BF16), wg_ref[...], preferred_element_type=F32) + bg_ref[...]
    s5o = g * jax.nn.sigmoid(z)
    n1 = _rms(s5o, gs_ref[...]).astype(BF16)
    return x + jnp.dot(n1, wo_ref[0:D_S5, :], preferred_element_type=F32)


def _mix_lru_part(lru, gl_ref, wo_ref):
    n2 = _rms(lru, gl_ref[...]).astype(BF16)
    return jnp.dot(n2, wo_ref[D_S5:, :], preferred_element_type=F32)


def _lru_mix_body(xl_ref, gate_ref, ys_ref, u_ref, x_ref,
                  cw_ref, cb_ref, wa_ref, wx_ref, ba_ref, bx_ref, lam_ref,
                  dsk_ref, wg_ref, bg_ref, gs_ref, gl_ref, wo_ref,
                  o_ref, hl_ref, xbuf_ref, carry_ref, lru_ref):
    @pl.when(pl.program_id(1) == 0)
    def _():
        xbuf_ref[0:8, :] = jnp.zeros((8, D_LRU), F32)
        carry_ref[...] = jnp.zeros((8, D_LRU), F32)

    o_ref[...] = _mix_s5_part(ys_ref[...], u_ref[...], x_ref[...],
                              dsk_ref, wg_ref, bg_ref, gs_ref, wo_ref)
    hl_ref[0] = _lru_tile(xl_ref, gate_ref, cw_ref, cb_ref, wa_ref, wx_ref, ba_ref, bx_ref, lam_ref,
                          lru_ref, xbuf_ref, carry_ref)
    o_ref[...] += _mix_lru_part(lru_ref[...], gl_ref, wo_ref)


def _lru_mix_prompt(proj_p, ys_p, x1_p, cw, cb, wa, wx, ba, bx, lam, dsk, wg, bg, gs, gl, wo,
                    *, nb, seq, tt=LRU_TT):
    nt = seq // tt
    rows = lambda c, col: pl.BlockSpec((tt, c), lambda b, t: (b * nt + t, col))
    once = lambda shape: pl.BlockSpec(shape, lambda b, t: (0,) * len(shape),
                                      pipeline_mode=pl.Buffered(1))
    return pl.pallas_call(
        _lru_mix_body,
        grid=(nb, nt),
        in_specs=[
            rows(D_LRU, 1), rows(D_LRU, 2),
            rows(D_S5, 0), rows(D_S5, 0), rows(D_MODEL, 0),
            once((CONV_W, D_LRU)), once((1, D_LRU)),
            once((D_LRU // 256, 256, 256)), once((D_LRU // 256, 256, 256)),
            once((1, D_LRU)), once((1, D_LRU)), once((1, D_LRU)),
            once((1, D_S5)), once((D_S5, D_S5)), once((1, D_S5)), once((1, D_S5)), once((1, D_LRU)),
            once((D_MODEL, D_MODEL)),
        ],
        out_specs=(pl.BlockSpec((tt, D_MODEL), lambda b, t: (b * nt + t, 0)),
                   pl.BlockSpec((1, 1, D_LRU), lambda b, t: (b, 0, 0))),
        out_shape=(jax.ShapeDtypeStruct((nb * seq, D_MODEL), F32),
                   jax.ShapeDtypeStruct((nb, 1, D_LRU), F32)),
        scratch_shapes=[pltpu.VMEM((tt + 8, D_LRU), F32), pltpu.VMEM((8, D_LRU), F32),
                        pltpu.VMEM((tt, D_LRU), F32)],
        compiler_params=pltpu.CompilerParams(
            dimension_semantics=("parallel", "arbitrary"), vmem_limit_bytes=VMEM_LIMIT),
        name="lru_mix_prompt",
    )(proj_p, proj_p, ys_p, proj_p, x1_p, cw, cb, wa, wx, ba, bx, lam, dsk, wg, bg, gs, gl, wo)


def _lru_decode_body(xl_ref, gate_ref, conv_ref, h0_ref, cw_ref, cb_ref,
                     wa_ref, wx_ref, ba_ref, bx_ref, lam_ref, o_ref, h_ref, buf_ref):
    x = xl_ref[...]
    cw = cw_ref[...]
    c0, c1, c2 = (conv_ref[k] for k in range(CONV_W - 1))
    xc = cb_ref[...] + c0 * cw[0:1] + c1 * cw[1:2] + c2 * cw[2:3] + x * cw[3:4]
    a, b = _lru_gates(xc, wa_ref, wx_ref, ba_ref[...], bx_ref[...], lam_ref[...])
    h = a * h0_ref[...] + b
    h_ref[...] = h
    o_ref[...] = h * gate_ref[...]
    for k, rows in enumerate((c1, c2, x)):
        buf_ref[k] = rows


def _lru_decode(proj_d, conv0, h0, cw, cb, wa, wx, ba, bx, lam):
    nd = proj_d.shape[0]
    full = lambda r: pl.BlockSpec((r, D_LRU), lambda i: (0, 0))
    conv = pl.BlockSpec((CONV_W - 1, nd, D_LRU), lambda i: (0, 0, 0))
    wspec = pl.BlockSpec((D_LRU // 256, 256, 256), lambda i: (0, 0, 0))
    return pl.pallas_call(
        _lru_decode_body,
        grid=(1,),
        in_specs=[
            pl.BlockSpec((nd, D_LRU), lambda i: (0, 1)),
            pl.BlockSpec((nd, D_LRU), lambda i: (0, 2)),
            conv, full(nd),
            full(CONV_W), full(1), wspec, wspec, full(1), full(1), full(1),
        ],
        out_specs=(full(nd), full(nd), conv),
        out_shape=(jax.ShapeDtypeStruct((nd, D_LRU), F32),
                   jax.ShapeDtypeStruct((nd, D_LRU), F32),
                   jax.ShapeDtypeStruct((CONV_W - 1, nd, D_LRU), F32)),
        name="lru_decode",
    )(proj_d, proj_d, conv0, h0, cw, cb, wa, wx, ba, bx, lam)


def _mix_decode_body(ys_ref, u_ref, lru_ref, x_ref, dsk_ref, wg_ref, bg_ref, gs_ref, gl_ref,
                     wo_ref, o_ref):
    o_ref[...] = (_mix_s5_part(ys_ref[...], u_ref[...], x_ref[...],
                               dsk_ref, wg_ref, bg_ref, gs_ref, wo_ref)
                  + _mix_lru_part(lru_ref[...], gl_ref, wo_ref))


def _mix_decode(ys_d, proj_d, lru_d, x1_d, dsk, wg, bg, gs, gl, wo):
    nd = x1_d.shape[0]
    full = lambda r, c: pl.BlockSpec((r, c), lambda i: (0, 0))
    return pl.pallas_call(
        _mix_decode_body,
        grid=(1,),
        in_specs=[full(nd, D_S5), full(nd, D_S5), full(nd, D_LRU), full(nd, D_MODEL),
                  full(1, D_S5), full(D_S5, D_S5), full(1, D_S5), full(1, D_S5), full(1, D_LRU),
                  full(D_MODEL, D_MODEL)],
        out_specs=full(nd, D_MODEL),
        out_shape=jax.ShapeDtypeStruct((nd, D_MODEL), F32),
        compiler_params=pltpu.CompilerParams(vmem_limit_bytes=VMEM_LIMIT),
        name="mix_decode",
    )(ys_d, proj_d, lru_d, x1_d, dsk, wg, bg, gs, gl, wo)


def _unpair(h, nb):
    tiles = S5_GROUPS // GROUPS_PER_TILE
    h5 = h.reshape(tiles, PAIRS_PER_TILE, nb, 2, S5_STATE)
    return jnp.transpose(h5, (2, 0, 3, 1, 4)).reshape(nb, S5_GROUPS, S5_STATE)


def _block_diag4(w):
    w4 = w.reshape(LRU_HEADS // 4, 4, LRU_HEAD_DIM, LRU_HEAD_DIM)
    eye = jnp.eye(4, dtype=w.dtype)
    return jnp.einsum("kaij,ab->kaibj", w4, eye).reshape(LRU_HEADS // 4, 256, 256)


def kernel(x_prompt, x_sample, state_s5_re, state_s5_im, state_lru_h, state_lru_conv, g_ffn1, w1_a, w3_a, w2_a, g_mix, w_in, lam_re, lam_im, log_dt, b_re, b_im, c_re, c_im, d_skip, w_glu, b_glu, conv_w, conv_b, w_a, b_a, w_x, b_x, lam_l, g_out_s5, g_out_lru, w_out, g_ffn2, w1_b, w3_b, w2_b, g_final):
    nb, seq, _ = x_prompt.shape
    nd = x_sample.shape[0]
    n_p = nb * seq
    row = lambda v: v.reshape(1, -1)
    assert w1_a.shape[0] == 1 and x_sample.shape[1] == 1
    assert nd == 128 and n_p % FFN_TM == 0 and n_p > FFN_TM
    assert seq % LRU_TT == 0 and seq % (8 * CHUNK) == 0

    xp = x_prompt.reshape(n_p, D_MODEL)
    xd = x_sample.reshape(nd, D_MODEL)

    n_i, n_f = n_p // FFN_TM, D_FF // FFN_TF
    x1_p0, x1_d, w1_a16, w3_a16, w2_a16 = _ffn(
        xp, xd, row(g_ffn1[0]), w1_a[0], w3_a[0], w2_a[0], n_tiles=1)
    x1_p, w1_b16, w3_b16, w2_b16, w_in16 = _ffn(
        xp, None, row(g_ffn1[0]), w1_a16, w3_a16, w2_a16, first_tile=1, n_tiles=n_i - 1,
        casts=(_cast_job(w1_b[0], n_i, n_f, i0=1), _cast_job(w3_b[0], n_i, n_f, i0=1),
               _row_cast_job(w2_b[0], n_i, n_f, i0=1), _walk_cast_job(w_in[0], n_i - 1, n_f)),
        prev=(x1_p0,))
    nj = D_IN // PROJ_TN
    walk = lambda i, j: jnp.minimum(i * nj + j, n_f - 1)
    blk0 = pl.BlockSpec((D_MODEL // n_i, FFN_TF), lambda i, j: (0, walk(i, j)))
    rows0 = pl.BlockSpec((D_FF // (n_i * n_f), D_MODEL), lambda i, j: (walk(i, j), 0))
    proj_p, proj_d, w_out16, w_glu16, w1_b16, w3_b16, w2_b16 = _inproj(
        x1_p, x1_d, row(g_mix[0]), w_in16,
        casts=(_cast_job(w_out[0], n_i, nj, bc=PROJ_TN), _cast_job(w_glu[0], n_i, nj, bc=PROJ_TN),
               (w1_b[0], blk0), (w3_b[0], blk0), (w2_b[0], rows0)),
        prev=(w1_b16, w3_b16, w2_b16))

    lmat, cpre, cpim, bd, cd, ar, ai, lr, li = _s5_prep(
        lam_re[0], lam_im[0], log_dt[0], b_re[0], b_im[0], c_re[0], c_im[0])
    ys_p, hf_re, hf_im = _s5_prompt(proj_p, lmat, cpre, cpim, ar, ai, nb=nb, seq=seq)

    to_gpb = lambda s: jnp.transpose(s, (1, 2, 0))
    ys_d, hd_re, hd_im = _s5_decode(proj_d, to_gpb(state_s5_re[0]), to_gpb(state_s5_im[0]),
                                    bd, cd, lr, li)

    wa_bd = _block_diag4(w_a[0]).astype(BF16)
    wx_bd = _block_diag4(w_x[0]).astype(BF16)
    lru_args = (conv_w[0], row(conv_b[0]), wa_bd, wx_bd, row(b_a[0]), row(b_x[0]), row(lam_l[0]))
    lru_d, hl_d, buf_d = _lru_decode(proj_d, jnp.transpose(state_lru_conv[0], (1, 0, 2)),
                                     state_lru_h[0], *lru_args)

    mix_args = (row(d_skip[0]), w_glu16, row(b_glu[0]), row(g_out_s5[0]), row(g_out_lru[0]), w_out16)
    x2_p, hl_p = _lru_mix_prompt(proj_p, ys_p, x1_p, *lru_args, *mix_args, nb=nb, seq=seq)
    x2_d = _mix_decode(ys_d, proj_d, lru_d, x1_d, *mix_args)
    ffn2 = (row(g_ffn2[0]), w1_b16, w3_b16, w2_b16, row(g_final))
    y_p0, y_d = _ffn(x2_p, x2_d, *ffn2, n_tiles=1)
    y_p, = _ffn(x2_p, None, *ffn2, first_tile=1, n_tiles=n_i - 1, prev=(y_p0,))

    tail_p = proj_p.reshape(nb, seq, -1)[:, seq - (CONV_W - 1):, D_S5:D_S5 + D_LRU]
    return (
        y_p.reshape(nb, seq, D_MODEL),
        y_d.reshape(nd, 1, D_MODEL),
        _unpair(hf_re, nb)[None],
        _unpair(hf_im, nb)[None],
        hl_p.reshape(1, nb, D_LRU),
        tail_p[None],
        jnp.transpose(hd_re, (2, 0, 1))[None],
        jnp.transpose(hd_im, (2, 0, 1))[None],
        hl_d[None],
        jnp.transpose(buf_d, (1, 0, 2))[None],
    )
```

```python
import functools

import jax
import jax.numpy as jnp
from jax import lax
from jax.experimental import pallas as pl
from jax.experimental.pallas import tpu as pltpu

F32 = jnp.float32
BF16 = jnp.bfloat16

D_MODEL = 2048
D_S5 = 1024
S5_GROUP = 16
S5_GROUPS = 64
S5_STATE = 64
D_LRU = 1024
LRU_HEADS = 16
LRU_HEAD_DIM = 64
CONV_W = 4
LRU_C = 8.0
D_FF = 5632
D_IN = D_S5 + 2 * D_LRU
EPS = 1e-6

MXU_WIDTH_V7X = 256
CHUNK = MXU_WIDTH_V7X // S5_GROUP
CW = CHUNK * S5_GROUP

FFN_TM = 1024
FFN_TF = 512
PROJ_TN = 1024
LRU_TT = 512

VMEM_CAPACITY_V7X = 64 * 1024 * 1024
VMEM_LIMIT = VMEM_CAPACITY_V7X - 6 * 1024 * 1024
FFN_VMEM_LIMIT = VMEM_CAPACITY_V7X - 2 * 1024 * 1024

NN = (((1,), (0,)), ((), ()))
NT = (((1,), (1,)), ((), ()))


def _rms(x, g):
    return x * lax.rsqrt(jnp.mean(x * x, axis=-1, keepdims=True) + EPS) * g


def _split(x):
    hi = x.astype(BF16)
    lo = (x - hi.astype(F32)).astype(BF16)
    return hi, lo


def _dot3(a, b, dims=NN):
    ah, al = _split(a)
    bh, bl = _split(b)
    d = functools.partial(lax.dot_general, dimension_numbers=dims, preferred_element_type=F32)
    return d(ah, bh) + d(al, bh) + d(ah, bl)


def _gelu(x):
    return jax.nn.gelu(x, approximate=True)


def _ffn_body(*refs, final_norm, n_casts):
    refs = list(refs)
    xn_ref = refs.pop()
    xp_ref, xd_ref, g_ref, w1_ref, w3_ref, w2_ref = (refs.pop(0) for _ in range(6))
    gf_ref = refs.pop(0) if final_norm else None
    cast_in = [refs.pop(0) for _ in range(n_casts)]
    op_ref, od_ref = refs.pop(0), refs.pop(0)
    cast_out = refs
    i = pl.program_id(0)
    f = pl.program_id(1)
    tm = xp_ref.shape[0]

    for src_ref, dst_ref in zip(cast_in, cast_out):
        dst_ref[...] = src_ref[...].astype(BF16)

    def step(x_ref, o_ref, rows):
        @pl.when(f == 0)
        def _():
            x = x_ref[...]
            xn_ref[rows, :] = _rms(x, g_ref[...]).astype(BF16)
            o_ref[...] = x

        xn = xn_ref[rows, :]
        a = jnp.dot(xn, w1_ref[...], preferred_element_type=F32)
        b = jnp.dot(xn, w3_ref[...], preferred_element_type=F32)
        h = (a * jax.nn.sigmoid(a) * b).astype(BF16)
        o_ref[...] += 0.5 * jnp.dot(h, w2_ref[...], preferred_element_type=F32)

        if final_norm:
            @pl.when(f == pl.num_programs(1) - 1)
            def _():
                o_ref[...] = _rms(o_ref[...], gf_ref[...])

    step(xp_ref, op_ref, slice(0, tm))

    @pl.when(i == 0)
    def _():
        step(xd_ref, od_ref, slice(tm, tm + xd_ref.shape[0]))


def _cast_job(w, n_i, n_f, bc=FFN_TF):
    rows, cols = w.shape
    br = rows // n_i
    n_cb = cols // bc
    assert br * n_i == rows and bc * n_cb == cols and n_cb <= n_f and br % 16 == 0
    return w, pl.BlockSpec(
        (br, bc), lambda i, f: (jnp.minimum(i, n_i - 1),
                                jnp.where(i < n_i, jnp.minimum(f, n_cb - 1), n_cb - 1)))


def _row_cast_job(w, n_i, n_f):
    rows, cols = w.shape
    br = rows // (n_i * n_f)
    assert br * n_i * n_f == rows and br % 16 == 0
    return w, pl.BlockSpec((br, cols), lambda i, f: (i * n_f + f, 0))


def _ffn(xp, xd, g, w1, w3, w2, g_final=None, *, casts=(), tm=FFN_TM, tf=FFN_TF):
    n_p, nd = xp.shape[0], xd.shape[0]
    final_norm = g_final is not None
    pspec = pl.BlockSpec((tm, D_MODEL), lambda i, f: (i, 0))
    dspec = pl.BlockSpec((nd, D_MODEL), lambda i, f: (0, 0))
    vspec = pl.BlockSpec((1, D_MODEL), lambda i, f: (0, 0))
    in_specs = [pspec, pl.BlockSpec((nd, D_MODEL), lambda i, f: (0, 0), pipeline_mode=pl.Buffered(1)),
                vspec,
                pl.BlockSpec((D_MODEL, tf), lambda i, f: (0, f)),
                pl.BlockSpec((D_MODEL, tf), lambda i, f: (0, f)),
                pl.BlockSpec((tf, D_MODEL), lambda i, f: (f, 0))]
    args = [xp, xd, g, w1, w3, w2]
    if final_norm:
        in_specs.append(vspec)
        args.append(g_final)
    in_specs += [spec for _, spec in casts]
    args += [w for w, _ in casts]
    sd = jax.ShapeDtypeStruct
    return pl.pallas_call(
        functools.partial(_ffn_body, final_norm=final_norm, n_casts=len(casts)),
        grid=(n_p // tm, D_FF // tf),
        in_specs=in_specs,
        out_specs=(pspec, dspec) + tuple(spec for _, spec in casts),
        out_shape=(sd((n_p, D_MODEL), F32), sd((nd, D_MODEL), F32))
        + tuple(sd(w.shape, BF16) for w, _ in casts),
        scratch_shapes=[pltpu.VMEM((tm + nd, D_MODEL), BF16)],
        compiler_params=pltpu.CompilerParams(
            dimension_semantics=("arbitrary", "arbitrary"), vmem_limit_bytes=FFN_VMEM_LIMIT),
        name="ffn_final" if final_norm else "ffn",
    )(*args)


def _inproj_body(*refs, n_tiles, nd, n_casts):
    xp_ref, xd_ref, g_ref, w_ref = refs[:4]
    cast_in = refs[4:4 + n_casts]
    op_ref, od_ref = refs[4 + n_casts:6 + n_casts]
    cast_out = refs[6 + n_casts:-1]
    xn_ref = refs[-1]
    i = pl.program_id(0)
    j = pl.program_id(1)
    nj = pl.num_programs(1)

    for src_ref, dst_ref in zip(cast_in, cast_out):
        dst_ref[...] = src_ref[...].astype(BF16)

    def run(x_ref, o_ref, rows):
        @pl.when(j == 0)
        def _():
            xn_ref[0:rows, :] = _rms(x_ref[...], g_ref[...]).astype(BF16)

        @pl.when(j < nj - 1)
        def _():
            o_ref[...] = jnp.dot(xn_ref[0:rows, :], w_ref[...], preferred_element_type=F32)

        @pl.when(j == nj - 1)
        def _():
            o_ref[...] = _gelu(jnp.dot(xn_ref[0:rows, :], w_ref[...],
                                       preferred_element_type=F32))

    @pl.when(i < n_tiles)
    def _():
        run(xp_ref, op_ref, xp_ref.shape[0])

    @pl.when(i == n_tiles)
    def _():
        run(xd_ref, od_ref, nd)


def _inproj(xp, xd, g, w, *, casts=(), tm=FFN_TM, tn=PROJ_TN):
    n_p, nd = xp.shape[0], xd.shape[0]
    n_tiles = n_p // tm
    d_out = w.shape[1]
    nj = d_out // tn
    return pl.pallas_call(
        functools.partial(_inproj_body, n_tiles=n_tiles, nd=nd, n_casts=len(casts)),
        grid=(n_tiles + 1, nj),
        in_specs=[
            pl.BlockSpec((tm, D_MODEL), lambda i, j: (jnp.minimum(i, n_tiles - 1), 0)),
            pl.BlockSpec((nd, D_MODEL), lambda i, j: (0, 0)),
            pl.BlockSpec((1, D_MODEL), lambda i, j: (0, 0)),
            pl.BlockSpec((D_MODEL, tn), lambda i, j: (0, j)),
        ] + [spec for _, spec in casts],
        out_specs=(
            pl.BlockSpec((tm, tn), lambda i, j: (jnp.minimum(i, n_tiles - 1),
                                                 jnp.where(i < n_tiles, j, nj - 1))),
            pl.BlockSpec((nd, tn), lambda i, j: (0, jnp.where(i < n_tiles, 0, j))),
        ) + tuple(spec for _, spec in casts),
        out_shape=(jax.ShapeDtypeStruct((n_p, d_out), F32),
                   jax.ShapeDtypeStruct((nd, d_out), F32))
        + tuple(jax.ShapeDtypeStruct(cw.shape, BF16) for cw, _ in casts),
        scratch_shapes=[pltpu.VMEM((tm, D_MODEL), BF16)],
        compiler_params=pltpu.CompilerParams(
            dimension_semantics=("arbitrary", "arbitrary"), vmem_limit_bytes=VMEM_LIMIT),
        name="inproj",
    )(xp, xd, g, w, *[cw for cw, _ in casts])


GROUPS_PER_TILE = 128 // S5_GROUP
PAIRS_PER_TILE = GROUPS_PER_TILE // 2
L_ROWS = CW + 2 * S5_STATE


def _lam_bar(lam_re, lam_im, log_dt):
    dt = jnp.exp(log_dt)
    mag = jnp.exp(lam_re * dt)
    ang = lam_im * dt
    return mag * jnp.cos(ang), mag * jnp.sin(ang)


def _s5_prep_body(lre_ref, lim_ref, ldt_ref, bre_ref, bim_ref, cre_ref, cim_ref,
                  l_ref, cpre_ref, cpim_ref, bd_ref, cd_ref, ar_ref, ai_ref, lr_ref, li_ref,
                  bp_ref, lrow_ref):
    gb, half = GROUPS_PER_TILE, PAIRS_PER_TILE
    p = S5_STATE
    lo, hi = slice(0, p), slice(p, 2 * p)
    lam_re = lre_ref[...]
    lam_im = lim_ref[...]
    lbr, lbi = _lam_bar(lam_re, lam_im, ldt_ref[...])
    lrow_ref[:, :, lo] = lbr
    lrow_ref[:, :, hi] = lbi
    for j in range(gb):
        cols = jnp.broadcast_to(lrow_ref[j], (2 * p, 2 * p)).T
        lr_ref[j] = cols[0:p]
        li_ref[j] = cols[p:2 * p]
    nr = lbr - 1.0
    den = lam_re * lam_re + lam_im * lam_im
    cr = (nr * lam_re + lbi * lam_im) / den
    ci = (lbi * lam_re - nr * lam_im) / den
    b_re = bre_ref[...]
    b_im = bim_ref[...]
    bbr = cr * b_re - ci * b_im
    bbi = cr * b_im + ci * b_re
    bd_ref[:, :, lo] = bbr
    bd_ref[:, :, hi] = bbi
    c_re = cre_ref[...]
    c_im = cim_ref[...]
    cd_ref[:, :, lo] = c_re
    cd_ref[:, :, hi] = -c_im

    zeros = jnp.zeros((half, S5_GROUP, p), F32)
    pr = jnp.ones_like(lbr)
    pi = jnp.zeros_like(lbr)
    for d in range(CHUNK):
        rows = slice(d * S5_GROUP, (d + 1) * S5_GROUP)
        back = slice((CHUNK - 1 - d) * S5_GROUP, (CHUNK - d) * S5_GROUP)
        bp_ref[:, back, lo] = bbr * pr - bbi * pi
        bp_ref[:, back, hi] = bbr * pi + bbi * pr
        pr, pi = pr * lbr - pi * lbi, pr * lbi + pi * lbr
        cp_r = c_re * pr - c_im * pi
        cp_i = -(c_re * pi + c_im * pr)
        cpre_ref[0:half, rows, lo] = cp_r[0:half]
        cpre_ref[0:half, rows, hi] = zeros
        cpre_ref[half:gb, rows, lo] = zeros
        cpre_ref[half:gb, rows, hi] = cp_r[half:gb]
        cpim_ref[0:half, rows, lo] = cp_i[0:half]
        cpim_ref[0:half, rows, hi] = zeros
        cpim_ref[half:gb, rows, lo] = zeros
        cpim_ref[half:gb, rows, hi] = cp_i[half:gb]

    qr, qi = pr, pi
    for r in range(8):
        ar_ref[:, r:r + 1, lo] = qr[0:half]
        ar_ref[:, r:r + 1, hi] = qr[half:gb]
        ai_ref[:, r:r + 1, lo] = qi[0:half]
        ai_ref[:, r:r + 1, hi] = qi[half:gb]
        qr, qi = qr * pr - qi * pi, qr * pi + qi * pr

    lane = lax.broadcasted_iota(jnp.int32, (S5_GROUP, 128), 1)
    for j in range(gb):
        w = _dot3(cd_ref[j], bp_ref[j], NT)
        w0, w1 = w[:, :128], w[:, 128:]
        for t in range(CHUNK):
            rows = slice(t * S5_GROUP, (t + 1) * S5_GROUP)
            shift = (CHUNK - 1 - t) * S5_GROUP
            keep = 128 - shift % 128
            if shift == 0:
                left, right = w0, w1
            elif shift < 128:
                r0 = pltpu.roll(w0, keep, axis=1)
                r1 = pltpu.roll(w1, keep, axis=1)
                left = jnp.where(lane < keep, r0, r1)
                right = jnp.where(lane < keep, r1, 0.0)
            elif shift == 128:
                left, right = w1, jnp.zeros_like(w1)
            else:
                left = jnp.where(lane < keep, pltpu.roll(w1, keep, axis=1), 0.0)
                right = jnp.zeros_like(w1)
            l_ref[j, rows, 0:128] = left.astype(BF16)
            l_ref[j, rows, 128:256] = right.astype(BF16)
        l_ref[j, CW:L_ROWS, :] = bp_ref[j].T.astype(BF16)


def _s5_prep(lam_re, lam_im, log_dt, b_re, b_im, c_re, c_im):
    g, p, gb, half = S5_GROUPS, S5_STATE, GROUPS_PER_TILE, PAIRS_PER_TILE
    lre = lam_re.reshape(g, 1, p)
    lim = lam_im.reshape(g, 1, p)
    ldt = jnp.broadcast_to(log_dt.reshape(g, 1, 1), (g, 1, p))
    bre = jnp.transpose(b_re, (0, 2, 1))
    bim = jnp.transpose(b_im, (0, 2, 1))
    sd = jax.ShapeDtypeStruct
    blk = lambda n, r, c: pl.BlockSpec((n, r, c), lambda i: (i, 0, 0))
    return pl.pallas_call(
        _s5_prep_body,
        grid=(g // gb,),
        in_specs=[blk(gb, 1, p)] * 3 + [blk(gb, S5_GROUP, p)] * 4,
        out_specs=(
            blk(gb, L_ROWS, CW), blk(gb, CW, 2 * p), blk(gb, CW, 2 * p),
            blk(gb, S5_GROUP, 2 * p), blk(gb, S5_GROUP, 2 * p),
            blk(half, 8, 2 * p), blk(half, 8, 2 * p), blk(gb, p, 128), blk(gb, p, 128),
        ),
        out_shape=(
            sd((g, L_ROWS, CW), BF16),
            sd((g, CW, 2 * p), F32),
            sd((g, CW, 2 * p), F32),
            sd((g, S5_GROUP, 2 * p), F32),
            sd((g, S5_GROUP, 2 * p), F32),
            sd((g // 2, 8, 2 * p), F32),
            sd((g // 2, 8, 2 * p), F32),
            sd((g, p, 128), F32),
            sd((g, p, 128), F32),
        ),
        scratch_shapes=[pltpu.VMEM((gb, CW, 2 * p), F32), pltpu.VMEM((gb, 1, 2 * p), F32)],
        compiler_params=pltpu.CompilerParams(dimension_semantics=("parallel",)),
        name="s5_prep",
    )(lre, lim, ldt, bre, bim, c_re, c_im)


def _s5p_body(u_ref, l_ref, cpre_ref, cpim_ref, ar_ref, ai_ref, y_ref, hre_ref, him_ref,
              ut_ref, yt_ref, *, nb, nk):
    gb, half = GROUPS_PER_TILE, PAIRS_PER_TILE
    p = S5_STATE
    nrow = nb * nk
    d = functools.partial(jnp.dot, preferred_element_type=F32)

    for t in range(CHUNK):
        xt = u_ref[pl.ds(t, nrow, stride=CHUNK), :].T
        for j in range(gb):
            ut_ref[j, t * S5_GROUP:(t + 1) * S5_GROUP, :] = xt[j * S5_GROUP:(j + 1) * S5_GROUP, :]

    s_re, s_im = [], []
    for j in range(gb):
        r = d(l_ref[j], ut_ref[j].astype(BF16))
        yt_ref[j] = r[0:CW]
        s_re.append(r[CW:CW + p])
        s_im.append(r[CW + p:L_ROWS])

    nblk = nrow // 8
    row8 = lax.broadcasted_iota(jnp.int32, (nblk, 8, 2 * p), 1)
    rows = lax.broadcasted_iota(jnp.int32, (nrow, 2 * p), 0) & (nk - 1)
    dnt = functools.partial(lax.dot_general, dimension_numbers=NT, preferred_element_type=F32)
    for q in range(half):
        re = jnp.concatenate([s_re[q], s_re[q + half]], axis=0).T.reshape(nblk, 8, 2 * p)
        im = jnp.concatenate([s_im[q], s_im[q + half]], axis=0).T.reshape(nblk, 8, 2 * p)
        for sh in (1, 2, 4):
            keep = row8 >= sh
            rs = jnp.where(keep, pltpu.roll(re, sh, axis=1), 0.0)
            js = jnp.where(keep, pltpu.roll(im, sh, axis=1), 0.0)
            ar = ar_ref[q, sh - 1:sh, :]
            ai = ai_ref[q, sh - 1:sh, :]
            re, im = re + ar * rs - ai * js, im + ar * js + ai * rs
        pw_r, pw_i = ar_ref[q], ai_ref[q]
        out_r, out_i = [], []
        for k in range(nblk):
            hr, hi = re[k], im[k]
            if k % (nk // 8):
                hr, hi = hr + pw_r * cr - pw_i * ci, hi + pw_r * ci + pw_i * cr
            cr, ci = hr[7:8, :], hi[7:8, :]
            out_r.append(hr)
            out_i.append(hi)
            if (k + 1) % (nk // 8) == 0:
                b = k // (nk // 8)
                hre_ref[q, b:b + 1, :] = cr
                him_ref[q, b:b + 1, :] = ci
        re = jnp.concatenate(out_r, axis=0)
        im = jnp.concatenate(out_i, axis=0)
        pre = jnp.where(rows >= 1, pltpu.roll(re, 1, axis=0), 0.0).astype(BF16)
        pim = jnp.where(rows >= 1, pltpu.roll(im, 1, axis=0), 0.0).astype(BF16)
        for j in (q, q + half):
            yt_ref[j] = (yt_ref[j] + dnt(cpre_ref[j].astype(BF16), pre)
                         + dnt(cpim_ref[j].astype(BF16), pim))

    for t in range(CHUNK):
        yt = jnp.concatenate(
            [yt_ref[j, t * S5_GROUP:(t + 1) * S5_GROUP, :] for j in range(gb)], axis=0)
        y_ref[pl.ds(t, nrow, stride=CHUNK), :] = yt.T


def _s5_prompt(proj_p, lmat, cpre, cpim, ar, ai, *, nb, seq):
    g, gb, half = S5_GROUPS, GROUPS_PER_TILE, PAIRS_PER_TILE
    n_p = nb * seq
    nk = seq // CHUNK
    blk = lambda n, r, c: pl.BlockSpec((n, r, c), lambda i: (i, 0, 0))
    return pl.pallas_call(
        functools.partial(_s5p_body, nb=nb, nk=nk),
        grid=(g // gb,),
        in_specs=[pl.BlockSpec((n_p, 128), lambda i: (0, i)),
                  blk(gb, L_ROWS, CW), blk(gb, CW, 2 * S5_STATE), blk(gb, CW, 2 * S5_STATE),
                  blk(half, 8, 2 * S5_STATE), blk(half, 8, 2 * S5_STATE)],
        out_specs=(pl.BlockSpec((n_p, 128), lambda i: (0, i)),
                   blk(half, nb, 2 * S5_STATE), blk(half, nb, 2 * S5_STATE)),
        out_shape=(jax.ShapeDtypeStruct((n_p, D_S5), F32),
                   jax.ShapeDtypeStruct((g // 2, nb, 2 * S5_STATE), F32),
                   jax.ShapeDtypeStruct((g // 2, nb, 2 * S5_STATE), F32)),
        scratch_shapes=[pltpu.VMEM((gb, CW, nb * nk), F32), pltpu.VMEM((gb, CW, nb * nk), F32)],
        compiler_params=pltpu.CompilerParams(
            dimension_semantics=("parallel",), vmem_limit_bytes=VMEM_LIMIT),
        name="s5_prompt",
    )(proj_p, lmat, cpre, cpim, ar, ai)


def _s5d_body(u_ref, hre_ref, him_ref, bd_ref, cd_ref, lr_ref, li_ref, y_ref, ore_ref, oim_ref):
    p = S5_STATE
    gb = GROUPS_PER_TILE
    tn = (((0,), (0,)), ((), ()))
    ut = u_ref[...].T
    yts = []
    for j in range(gb):
        ug = ut[j * S5_GROUP:(j + 1) * S5_GROUP, :].astype(BF16)
        bu = lax.dot_general(bd_ref[j].astype(BF16), ug, tn,
                             preferred_element_type=F32)
        h0r, h0i = hre_ref[j], him_ref[j]
        lbr, lbi = lr_ref[j], li_ref[j]
        hr = lbr * h0r - lbi * h0i + bu[0:p]
        hi = lbr * h0i + lbi * h0r + bu[p:2 * p]
        ore_ref[j] = hr
        oim_ref[j] = hi
        h = jnp.concatenate([hr, hi], axis=0).astype(BF16)
        yts.append(jnp.dot(cd_ref[j].astype(BF16), h, preferred_element_type=F32))
    y_ref[...] = jnp.concatenate(yts, axis=0).T


def _s5_decode(proj_d, h0_re, h0_im, bd, cd, lr, li):
    g, gb, p = S5_GROUPS, GROUPS_PER_TILE, S5_STATE
    nbatch = proj_d.shape[0]
    blk = lambda r, c: pl.BlockSpec((gb, r, c), lambda i: (i, 0, 0))
    cols = pl.BlockSpec((nbatch, 128), lambda i: (0, i))
    return pl.pallas_call(
        _s5d_body,
        grid=(g // gb,),
        in_specs=[cols, blk(p, nbatch), blk(p, nbatch), blk(S5_GROUP, 2 * p), blk(S5_GROUP, 2 * p),
                  blk(p, nbatch), blk(p, nbatch)],
        out_specs=(cols, blk(p, nbatch), blk(p, nbatch)),
        out_shape=(jax.ShapeDtypeStruct((nbatch, D_S5), F32),
                   jax.ShapeDtypeStruct((g, p, nbatch), F32),
                   jax.ShapeDtypeStruct((g, p, nbatch), F32)),
        compiler_params=pltpu.CompilerParams(dimension_semantics=("parallel",)),
        name="s5_decode",
    )(proj_d, h0_re, h0_im, bd, cd, lr, li)


def _lru_gates(xc, wa_ref, wx_ref, ba, bx, lam):
    xcb = xc.astype(BF16)
    nblk = D_LRU // MXU_WIDTH_V7X
    r_parts, i_parts = [], []
    for k in range(nblk):
        xk = xcb[:, k * MXU_WIDTH_V7X:(k + 1) * MXU_WIDTH_V7X]
        r_parts.append(jnp.dot(xk, wa_ref[k], preferred_element_type=F32))
        i_parts.append(jnp.dot(xk, wx_ref[k], preferred_element_type=F32))
    r = jax.nn.sigmoid(jnp.concatenate(r_parts, axis=1) + ba)
    i = jax.nn.sigmoid(jnp.concatenate(i_parts, axis=1) + bx)
    z = -lam
    softplus = jnp.maximum(z, 0.0) + jnp.log1p(jnp.exp(-jnp.abs(z)))
    log_a = (-LRU_C * softplus) * r
    a = jnp.exp(log_a)
    v = -jnp.tanh(log_a) * (a * a + 1.0)
    mult = jnp.where(v > 0.0, v * lax.rsqrt(v), 0.0)
    return a, mult * (i * xc)


def _lru_tile(xl_ref, gate_ref, cw_ref, cb_ref, wa_ref, wx_ref, ba_ref, bx_ref, lam_ref,
              o_ref, xbuf_ref, carry_ref):
    tt = xl_ref.shape[0]
    x = xl_ref[...]
    xbuf_ref[8:8 + tt, :] = x
    cw = cw_ref[...]
    xc = (cb_ref[...] + xbuf_ref[5:5 + tt, :] * cw[0:1] + xbuf_ref[6:6 + tt, :] * cw[1:2]
          + xbuf_ref[7:7 + tt, :] * cw[2:3] + x * cw[3:4])
    xbuf_ref[0:8, :] = x[tt - 8:tt, :]

    a, b = _lru_gates(xc, wa_ref, wx_ref, ba_ref[...], bx_ref[...], lam_ref[...])

    nblk = tt // 8
    a3 = a.reshape(nblk, 8, D_LRU)
    b3 = b.reshape(nblk, 8, D_LRU)
    row = lax.broadcasted_iota(jnp.int32, (nblk, 8, D_LRU), 1)
    for sh in (1, 2, 4):
        keep = row >= sh
        bs = jnp.where(keep, pltpu.roll(b3, sh, axis=1), 0.0)
        sa = jnp.where(keep, pltpu.roll(a3, sh, axis=1), 1.0)
        b3 = b3 + a3 * bs
        a3 = a3 * sa
    carry = carry_ref[0:1, :]
    gate = gate_ref[...]
    for k in range(nblk):
        h = b3[k] + a3[k] * carry
        carry = h[7:8, :]
        o_ref[k * 8:(k + 1) * 8, :] = h * gate[k * 8:(k + 1) * 8, :]
    carry_ref[...] = jnp.broadcast_to(carry, (8, D_LRU))
    return carry


def _mix_s5_part(ys, u, x, dsk_ref, wg_ref, bg_ref, gs_ref, wo_ref):
    yy = ys + dsk_ref[...] * u
    g = _gelu(yy)
    z = jnp.dot(g.astype(BF16), wg_ref[...], preferred_element_type=F32) + bg_ref[...]
    s5o = g * jax.nn.sigmoid(z)
    n1 = _rms(s5o, gs_ref[...]).astype(BF16)
    return x + jnp.dot(n1, wo_ref[0:D_S5, :], preferred_element_type=F32)


def _mix_lru_part(lru, gl_ref, wo_ref):
    n2 = _rms(lru, gl_ref[...]).astype(BF16)
    return jnp.dot(n2, wo_ref[D_S5:, :], preferred_element_type=F32)


def _lru_mix_body(xl_ref, gate_ref, ys_ref, u_ref, x_ref,
                  cw_ref, cb_ref, wa_ref, wx_ref, ba_ref, bx_ref, lam_ref,
                  dsk_ref, wg_ref, bg_ref, gs_ref, gl_ref, wo_ref,
                  o_ref, hl_ref, xbuf_ref, carry_ref, lru_ref):
    @pl.when(pl.program_id(1) == 0)
    def _():
        xbuf_ref[0:8, :] = jnp.zeros((8, D_LRU), F32)
        carry_ref[...] = jnp.zeros((8, D_LRU), F32)

    o_ref[...] = _mix_s5_part(ys_ref[...], u_ref[...], x_ref[...],
                              dsk_ref, wg_ref, bg_ref, gs_ref, wo_ref)
    hl_ref[0] = _lru_tile(xl_ref, gate_ref, cw_ref, cb_ref, wa_ref, wx_ref, ba_ref, bx_ref, lam_ref,
                          lru_ref, xbuf_ref, carry_ref)
    o_ref[...] += _mix_lru_part(lru_ref[...], gl_ref, wo_ref)


def _lru_mix_prompt(proj_p, ys_p, x1_p, cw, cb, wa, wx, ba, bx, lam, dsk, wg, bg, gs, gl, wo,
                    *, nb, seq, tt=LRU_TT):
    nt = seq // tt
    rows = lambda c, col: pl.BlockSpec((tt, c), lambda b, t: (b * nt + t, col))
    once = lambda shape: pl.BlockSpec(shape, lambda b, t: (0,) * len(shape),
                                      pipeline_mode=pl.Buffered(1))
    return pl.pallas_call(
        _lru_mix_body,
        grid=(nb, nt),
        in_specs=[
            rows(D_LRU, 1), rows(D_LRU, 2),
            rows(D_S5, 0), rows(D_S5, 0), rows(D_MODEL, 0),
            once((CONV_W, D_LRU)), once((1, D_LRU)),
            once((D_LRU // 256, 256, 256)), once((D_LRU // 256, 256, 256)),
            once((1, D_LRU)), once((1, D_LRU)), once((1, D_LRU)),
            once((1, D_S5)), once((D_S5, D_S5)), once((1, D_S5)), once((1, D_S5)), once((1, D_LRU)),
            once((D_MODEL, D_MODEL)),
        ],
        out_specs=(pl.BlockSpec((tt, D_MODEL), lambda b, t: (b * nt + t, 0)),
                   pl.BlockSpec((1, 1, D_LRU), lambda b, t: (b, 0, 0))),
        out_shape=(jax.ShapeDtypeStruct((nb * seq, D_MODEL), F32),
                   jax.ShapeDtypeStruct((nb, 1, D_LRU), F32)),
        scratch_shapes=[pltpu.VMEM((tt + 8, D_LRU), F32), pltpu.VMEM((8, D_LRU), F32),
                        pltpu.VMEM((tt, D_LRU), F32)],
        compiler_params=pltpu.CompilerParams(
            dimension_semantics=("parallel", "arbitrary"), vmem_limit_bytes=VMEM_LIMIT),
        name="lru_mix_prompt",
    )(proj_p, proj_p, ys_p, proj_p, x1_p, cw, cb, wa, wx, ba, bx, lam, dsk, wg, bg, gs, gl, wo)


def _lru_decode_body(xl_ref, gate_ref, conv_ref, h0_ref, cw_ref, cb_ref,
                     wa_ref, wx_ref, ba_ref, bx_ref, lam_ref, o_ref, h_ref, buf_ref):
    x = xl_ref[...]
    cw = cw_ref[...]
    c0, c1, c2 = (conv_ref[k] for k in range(CONV_W - 1))
    xc = cb_ref[...] + c0 * cw[0:1] + c1 * cw[1:2] + c2 * cw[2:3] + x * cw[3:4]
    a, b = _lru_gates(xc, wa_ref, wx_ref, ba_ref[...], bx_ref[...], lam_ref[...])
    h = a * h0_ref[...] + b
    h_ref[...] = h
    o_ref[...] = h * gate_ref[...]
    for k, rows in enumerate((c1, c2, x)):
        buf_ref[k] = rows


def _lru_decode(proj_d, conv0, h0, cw, cb, wa, wx, ba, bx, lam):
    nd = proj_d.shape[0]
    full = lambda r: pl.BlockSpec((r, D_LRU), lambda i: (0, 0))
    conv = pl.BlockSpec((CONV_W - 1, nd, D_LRU), lambda i: (0, 0, 0))
    wspec = pl.BlockSpec((D_LRU // 256, 256, 256), lambda i: (0, 0, 0))
    return pl.pallas_call(
        _lru_decode_body,
        grid=(1,),
        in_specs=[
            pl.BlockSpec((nd, D_LRU), lambda i: (0, 1)),
            pl.BlockSpec((nd, D_LRU), lambda i: (0, 2)),
            conv, full(nd),
            full(CONV_W), full(1), wspec, wspec, full(1), full(1), full(1),
        ],
        out_specs=(full(nd), full(nd), conv),
        out_shape=(jax.ShapeDtypeStruct((nd, D_LRU), F32),
                   jax.ShapeDtypeStruct((nd, D_LRU), F32),
                   jax.ShapeDtypeStruct((CONV_W - 1, nd, D_LRU), F32)),
        name="lru_decode",
    )(proj_d, proj_d, conv0, h0, cw, cb, wa, wx, ba, bx, lam)


def _mix_decode_body(ys_ref, u_ref, lru_ref, x_ref, dsk_ref, wg_ref, bg_ref, gs_ref, gl_ref,
                     wo_ref, o_ref):
    o_ref[...] = (_mix_s5_part(ys_ref[...], u_ref[...], x_ref[...],
                               dsk_ref, wg_ref, bg_ref, gs_ref, wo_ref)
                  + _mix_lru_part(lru_ref[...], gl_ref, wo_ref))


def _mix_decode(ys_d, proj_d, lru_d, x1_d, dsk, wg, bg, gs, gl, wo):
    nd = x1_d.shape[0]
    full = lambda r, c: pl.BlockSpec((r, c), lambda i: (0, 0))
    return pl.pallas_call(
        _mix_decode_body,
        grid=(1,),
        in_specs=[full(nd, D_S5), full(nd, D_S5), full(nd, D_LRU), full(nd, D_MODEL),
                  full(1, D_S5), full(D_S5, D_S5), full(1, D_S5), full(1, D_S5), full(1, D_LRU),
                  full(D_MODEL, D_MODEL)],
        out_specs=full(nd, D_MODEL),
        out_shape=jax.ShapeDtypeStruct((nd, D_MODEL), F32),
        compiler_params=pltpu.CompilerParams(vmem_limit_bytes=VMEM_LIMIT),
        name="mix_decode",
    )(ys_d, proj_d, lru_d, x1_d, dsk, wg, bg, gs, gl, wo)


def _unpair(h, nb):
    tiles = S5_GROUPS // GROUPS_PER_TILE
    h5 = h.reshape(tiles, PAIRS_PER_TILE, nb, 2, S5_STATE)
    return jnp.transpose(h5, (2, 0, 3, 1, 4)).reshape(nb, S5_GROUPS, S5_STATE)


def _block_diag4(w):
    w4 = w.reshape(LRU_HEADS // 4, 4, LRU_HEAD_DIM, LRU_HEAD_DIM)
    eye = jnp.eye(4, dtype=w.dtype)
    return jnp.einsum("kaij,ab->kaibj", w4, eye).reshape(LRU_HEADS // 4, 256, 256)


def kernel(x_prompt, x_sample, state_s5_re, state_s5_im, state_lru_h, state_lru_conv, g_ffn1, w1_a, w3_a, w2_a, g_mix, w_in, lam_re, lam_im, log_dt, b_re, b_im, c_re, c_im, d_skip, w_glu, b_glu, conv_w, conv_b, w_a, b_a, w_x, b_x, lam_l, g_out_s5, g_out_lru, w_out, g_ffn2, w1_b, w3_b, w2_b, g_final):
    nb, seq, _ = x_prompt.shape
    nd = x_sample.shape[0]
    n_p = nb * seq
    row = lambda v: v.reshape(1, -1)
    assert w1_a.shape[0] == 1 and x_sample.shape[1] == 1
    assert nd == 128 and n_p % FFN_TM == 0 and n_p > FFN_TM
    assert seq % LRU_TT == 0 and seq % (8 * CHUNK) == 0

    xp = x_prompt.reshape(n_p, D_MODEL)
    xd = x_sample.reshape(nd, D_MODEL)

    n_i, n_f = n_p // FFN_TM, D_FF // FFN_TF
    x1_p, x1_d, w1_b16, w3_b16, w2_b16 = _ffn(
        xp, xd, row(g_ffn1[0]), w1_a[0].astype(BF16), w3_a[0].astype(BF16), w2_a[0].astype(BF16),
        casts=(_cast_job(w1_b[0], n_i, n_f), _cast_job(w3_b[0], n_i, n_f),
               _row_cast_job(w2_b[0], n_i, n_f)))
    nj = D_IN // PROJ_TN
    proj_p, proj_d, w_out16, w_glu16 = _inproj(
        x1_p, x1_d, row(g_mix[0]), w_in[0].astype(BF16),
        casts=(_cast_job(w_out[0], n_i, nj, bc=PROJ_TN), _cast_job(w_glu[0], n_i, nj, bc=PROJ_TN)))

    lmat, cpre, cpim, bd, cd, ar, ai, lr, li = _s5_prep(
        lam_re[0], lam_im[0], log_dt[0], b_re[0], b_im[0], c_re[0], c_im[0])
    ys_p, hf_re, hf_im = _s5_prompt(proj_p, lmat, cpre, cpim, ar, ai, nb=nb, seq=seq)

    to_gpb = lambda s: jnp.transpose(s, (1, 2, 0))
    ys_d, hd_re, hd_im = _s5_decode(proj_d, to_gpb(state_s5_re[0]), to_gpb(state_s5_im[0]),
                                    bd, cd, lr, li)

    wa_bd = _block_diag4(w_a[0]).astype(BF16)
    wx_bd = _block_diag4(w_x[0]).astype(BF16)
    lru_args = (conv_w[0], row(conv_b[0]), wa_bd, wx_bd, row(b_a[0]), row(b_x[0]), row(lam_l[0]))
    lru_d, hl_d, buf_d = _lru_decode(proj_d, jnp.transpose(state_lru_conv[0], (1, 0, 2)),
                                     state_lru_h[0], *lru_args)

    mix_args = (row(d_skip[0]), w_glu16, row(b_glu[0]), row(g_out_s5[0]), row(g_out_lru[0]), w_out16)
    x2_p, hl_p = _lru_mix_prompt(proj_p, ys_p, x1_p, *lru_args, *mix_args, nb=nb, seq=seq)
    x2_d = _mix_decode(ys_d, proj_d, lru_d, x1_d, *mix_args)
    y_p, y_d = _ffn(x2_p, x2_d, row(g_ffn2[0]), w1_b16, w3_b16, w2_b16, row(g_final))

    tail_p = proj_p.reshape(nb, seq, -1)[:, seq - (CONV_W - 1):, D_S5:D_S5 + D_LRU]
    return (
        y_p.reshape(nb, seq, D_MODEL),
        y_d.reshape(nd, 1, D_MODEL),
        _unpair(hf_re, nb)[None],
        _unpair(hf_im, nb)[None],
        hl_p.reshape(1, nb, D_LRU),
        tail_p[None],
        jnp.transpose(hd_re, (2, 0, 1))[None],
        jnp.transpose(hd_im, (2, 0, 1))[None],
        hl_d[None],
        jnp.transpose(buf_d, (1, 0, 2))[None],
    )
```

```python
import functools

import jax
import jax.numpy as jnp
from jax import lax
from jax.experimental import pallas as pl
from jax.experimental.pallas import tpu as pltpu

F32 = jnp.float32
BF16 = jnp.bfloat16

D_MODEL = 2048
D_S5 = 1024
S5_GROUP = 16
S5_GROUPS = 64
S5_STATE = 64
D_LRU = 1024
LRU_HEADS = 16
LRU_HEAD_DIM = 64
CONV_W = 4
LRU_C = 8.0
D_FF = 5632
D_IN = D_S5 + 2 * D_LRU
EPS = 1e-6

MXU_WIDTH_V7X = 256
CHUNK = MXU_WIDTH_V7X // S5_GROUP
CW = CHUNK * S5_GROUP

FFN_TM = 1024
FFN_TF = 512
PROJ_TN = 1024
LRU_TT = 512

VMEM_CAPACITY_V7X = 64 * 1024 * 1024
VMEM_LIMIT = VMEM_CAPACITY_V7X - 6 * 1024 * 1024
FFN_VMEM_LIMIT = VMEM_CAPACITY_V7X - 2 * 1024 * 1024

NN = (((1,), (0,)), ((), ()))
NT = (((1,), (1,)), ((), ()))


def _rms(x, g):
    return x * lax.rsqrt(jnp.mean(x * x, axis=-1, keepdims=True) + EPS) * g


def _split(x):
    hi = x.astype(BF16)
    lo = (x - hi.astype(F32)).astype(BF16)
    return hi, lo


def _dot3(a, b, dims=NN):
    ah, al = _split(a)
    bh, bl = _split(b)
    d = functools.partial(lax.dot_general, dimension_numbers=dims, preferred_element_type=F32)
    return d(ah, bh) + d(al, bh) + d(ah, bl)


def _gelu(x):
    return jax.nn.gelu(x, approximate=True)


def _ffn_body(*refs, final_norm, n_casts):
    refs = list(refs)
    xn_ref = refs.pop()
    xp_ref, xd_ref, g_ref, w1_ref, w3_ref, w2_ref = (refs.pop(0) for _ in range(6))
    gf_ref = refs.pop(0) if final_norm else None
    cast_in = [refs.pop(0) for _ in range(n_casts)]
    op_ref, od_ref = refs.pop(0), refs.pop(0)
    cast_out = refs
    i = pl.program_id(0)
    f = pl.program_id(1)
    tm = xp_ref.shape[0]

    for src_ref, dst_ref in zip(cast_in, cast_out):
        dst_ref[...] = src_ref[...].astype(BF16)

    def step(x_ref, o_ref, rows):
        @pl.when(f == 0)
        def _():
            x = x_ref[...]
            xn_ref[rows, :] = _rms(x, g_ref[...]).astype(BF16)
            o_ref[...] = x

        xn = xn_ref[rows, :]
        a = jnp.dot(xn, w1_ref[...], preferred_element_type=F32)
        b = jnp.dot(xn, w3_ref[...], preferred_element_type=F32)
        h = (a * jax.nn.sigmoid(a) * b).astype(BF16)
        o_ref[...] += 0.5 * jnp.dot(h, w2_ref[...], preferred_element_type=F32)

        if final_norm:
            @pl.when(f == pl.num_programs(1) - 1)
            def _():
                o_ref[...] = _rms(o_ref[...], gf_ref[...])

    step(xp_ref, op_ref, slice(0, tm))

    @pl.when(i == 0)
    def _():
        step(xd_ref, od_ref, slice(tm, tm + xd_ref.shape[0]))


def _cast_job(w, n_i, n_f, bc=FFN_TF):
    rows, cols = w.shape
    br = rows // n_i
    n_cb = cols // bc
    assert br * n_i == rows and bc * n_cb == cols and n_cb <= n_f and br % 16 == 0
    return w, pl.BlockSpec(
        (br, bc), lambda i, f: (jnp.minimum(i, n_i - 1),
                                jnp.where(i < n_i, jnp.minimum(f, n_cb - 1), n_cb - 1)))


def _row_cast_job(w, n_i, n_f):
    rows, cols = w.shape
    br = rows // (n_i * n_f)
    assert br * n_i * n_f == rows and br % 16 == 0
    return w, pl.BlockSpec((br, cols), lambda i, f: (i * n_f + f, 0))


def _ffn(xp, xd, g, w1, w3, w2, g_final=None, *, casts=(), tm=FFN_TM, tf=FFN_TF):
    n_p, nd = xp.shape[0], xd.shape[0]
    final_norm = g_final is not None
    pspec = pl.BlockSpec((tm, D_MODEL), lambda i, f: (i, 0))
    dspec = pl.BlockSpec((nd, D_MODEL), lambda i, f: (0, 0))
    vspec = pl.BlockSpec((1, D_MODEL), lambda i, f: (0, 0))
    in_specs = [pspec, pl.BlockSpec((nd, D_MODEL), lambda i, f: (0, 0), pipeline_mode=pl.Buffered(1)),
                vspec,
                pl.BlockSpec((D_MODEL, tf), lambda i, f: (0, f)),
                pl.BlockSpec((D_MODEL, tf), lambda i, f: (0, f)),
                pl.BlockSpec((tf, D_MODEL), lambda i, f: (f, 0))]
    args = [xp, xd, g, w1, w3, w2]
    if final_norm:
        in_specs.append(vspec)
        args.append(g_final)
    in_specs += [spec for _, spec in casts]
    args += [w for w, _ in casts]
    sd = jax.ShapeDtypeStruct
    return pl.pallas_call(
        functools.partial(_ffn_body, final_norm=final_norm, n_casts=len(casts)),
        grid=(n_p // tm, D_FF // tf),
        in_specs=in_specs,
        out_specs=(pspec, dspec) + tuple(spec for _, spec in casts),
        out_shape=(sd((n_p, D_MODEL), F32), sd((nd, D_MODEL), F32))
        + tuple(sd(w.shape, BF16) for w, _ in casts),
        scratch_shapes=[pltpu.VMEM((tm + nd, D_MODEL), BF16)],
        compiler_params=pltpu.CompilerParams(
            dimension_semantics=("arbitrary", "arbitrary"), vmem_limit_bytes=FFN_VMEM_LIMIT),
        name="ffn_final" if final_norm else "ffn",
    )(*args)


def _inproj_body(*refs, n_tiles, nd, n_casts):
    xp_ref, xd_ref, g_ref, w_ref = refs[:4]
    cast_in = refs[4:4 + n_casts]
    op_ref, od_ref = refs[4 + n_casts:6 + n_casts]
    cast_out = refs[6 + n_casts:-1]
    xn_ref = refs[-1]
    i = pl.program_id(0)
    j = pl.program_id(1)
    nj = pl.num_programs(1)

    for src_ref, dst_ref in zip(cast_in, cast_out):
        dst_ref[...] = src_ref[...].astype(BF16)

    def run(x_ref, o_ref, rows):
        @pl.when(j == 0)
        def _():
            xn_ref[0:rows, :] = _rms(x_ref[...], g_ref[...]).astype(BF16)

        @pl.when(j < nj - 1)
        def _():
            o_ref[...] = jnp.dot(xn_ref[0:rows, :], w_ref[j], preferred_element_type=F32)

        @pl.when(j == nj - 1)
        def _():
            o_ref[...] = _gelu(jnp.dot(xn_ref[0:rows, :], w_ref[j], preferred_element_type=F32))

    @pl.when(i < n_tiles)
    def _():
        run(xp_ref, op_ref, xp_ref.shape[0])

    @pl.when(i == n_tiles)
    def _():
        run(xd_ref, od_ref, nd)


def _inproj(xp, xd, g, w, *, casts=(), tm=FFN_TM):
    n_p, nd = xp.shape[0], xd.shape[0]
    n_tiles = n_p // tm
    nj, _, tn = w.shape
    d_out = nj * tn
    return pl.pallas_call(
        functools.partial(_inproj_body, n_tiles=n_tiles, nd=nd, n_casts=len(casts)),
        grid=(n_tiles + 1, nj),
        in_specs=[
            pl.BlockSpec((tm, D_MODEL), lambda i, j: (jnp.minimum(i, n_tiles - 1), 0)),
            pl.BlockSpec((nd, D_MODEL), lambda i, j: (0, 0)),
            pl.BlockSpec((1, D_MODEL), lambda i, j: (0, 0)),
            pl.BlockSpec((nj, D_MODEL, tn), lambda i, j: (0, 0, 0), pipeline_mode=pl.Buffered(1)),
        ] + [spec for _, spec in casts],
        out_specs=(
            pl.BlockSpec((tm, tn), lambda i, j: (jnp.minimum(i, n_tiles - 1),
                                                 jnp.where(i < n_tiles, j, nj - 1))),
            pl.BlockSpec((nd, tn), lambda i, j: (0, jnp.where(i < n_tiles, 0, j))),
        ) + tuple(spec for _, spec in casts),
        out_shape=(jax.ShapeDtypeStruct((n_p, d_out), F32),
                   jax.ShapeDtypeStruct((nd, d_out), F32))
        + tuple(jax.ShapeDtypeStruct(cw.shape, BF16) for cw, _ in casts),
        scratch_shapes=[pltpu.VMEM((tm, D_MODEL), BF16)],
        compiler_params=pltpu.CompilerParams(
            dimension_semantics=("arbitrary", "arbitrary"), vmem_limit_bytes=VMEM_LIMIT),
        name="inproj",
    )(xp, xd, g, w, *[cw for cw, _ in casts])


GROUPS_PER_TILE = 128 // S5_GROUP
PAIRS_PER_TILE = GROUPS_PER_TILE // 2
L_ROWS = CW + 2 * S5_STATE


def _lam_bar(lam_re, lam_im, log_dt):
    dt = jnp.exp(log_dt)
    mag = jnp.exp(lam_re * dt)
    ang = lam_im * dt
    return mag * jnp.cos(ang), mag * jnp.sin(ang)


def _s5_prep_body(lre_ref, lim_ref, ldt_ref, bre_ref, bim_ref, cre_ref, cim_ref,
                  l_ref, cpre_ref, cpim_ref, bd_ref, cd_ref, ar_ref, ai_ref, lr_ref, li_ref,
                  bp_ref, lrow_ref):
    gb, half = GROUPS_PER_TILE, PAIRS_PER_TILE
    p = S5_STATE
    lo, hi = slice(0, p), slice(p, 2 * p)
    lam_re = lre_ref[...]
    lam_im = lim_ref[...]
    lbr, lbi = _lam_bar(lam_re, lam_im, ldt_ref[...])
    lrow_ref[:, :, lo] = lbr
    lrow_ref[:, :, hi] = lbi
    for j in range(gb):
        cols = jnp.broadcast_to(lrow_ref[j], (2 * p, 2 * p)).T
        lr_ref[j] = cols[0:p]
        li_ref[j] = cols[p:2 * p]
    nr = lbr - 1.0
    den = lam_re * lam_re + lam_im * lam_im
    cr = (nr * lam_re + lbi * lam_im) / den
    ci = (lbi * lam_re - nr * lam_im) / den
    b_re = bre_ref[...]
    b_im = bim_ref[...]
    bbr = cr * b_re - ci * b_im
    bbi = cr * b_im + ci * b_re
    bd_ref[:, :, lo] = bbr
    bd_ref[:, :, hi] = bbi
    c_re = cre_ref[...]
    c_im = cim_ref[...]
    cd_ref[:, :, lo] = c_re
    cd_ref[:, :, hi] = -c_im

    zeros = jnp.zeros((half, S5_GROUP, p), F32)
    pr = jnp.ones_like(lbr)
    pi = jnp.zeros_like(lbr)
    for d in range(CHUNK):
        rows = slice(d * S5_GROUP, (d + 1) * S5_GROUP)
        back = slice((CHUNK - 1 - d) * S5_GROUP, (CHUNK - d) * S5_GROUP)
        bp_ref[:, back, lo] = bbr * pr - bbi * pi
        bp_ref[:, back, hi] = bbr * pi + bbi * pr
        pr, pi = pr * lbr - pi * lbi, pr * lbi + pi * lbr
        cp_r = c_re * pr - c_im * pi
        cp_i = -(c_re * pi + c_im * pr)
        cpre_ref[0:half, rows, lo] = cp_r[0:half]
        cpre_ref[0:half, rows, hi] = zeros
        cpre_ref[half:gb, rows, lo] = zeros
        cpre_ref[half:gb, rows, hi] = cp_r[half:gb]
        cpim_ref[0:half, rows, lo] = cp_i[0:half]
        cpim_ref[0:half, rows, hi] = zeros
        cpim_ref[half:gb, rows, lo] = zeros
        cpim_ref[half:gb, rows, hi] = cp_i[half:gb]

    qr, qi = pr, pi
    for r in range(8):
        ar_ref[:, r:r + 1, lo] = qr[0:half]
        ar_ref[:, r:r + 1, hi] = qr[half:gb]
        ai_ref[:, r:r + 1, lo] = qi[0:half]
        ai_ref[:, r:r + 1, hi] = qi[half:gb]
        qr, qi = qr * pr - qi * pi, qr * pi + qi * pr

    lane = lax.broadcasted_iota(jnp.int32, (S5_GROUP, 128), 1)
    for j in range(gb):
        w = _dot3(cd_ref[j], bp_ref[j], NT)
        w0, w1 = w[:, :128], w[:, 128:]
        for t in range(CHUNK):
            rows = slice(t * S5_GROUP, (t + 1) * S5_GROUP)
            shift = (CHUNK - 1 - t) * S5_GROUP
            keep = 128 - shift % 128
            if shift == 0:
                left, right = w0, w1
            elif shift < 128:
                r0 = pltpu.roll(w0, keep, axis=1)
                r1 = pltpu.roll(w1, keep, axis=1)
                left = jnp.where(lane < keep, r0, r1)
                right = jnp.where(lane < keep, r1, 0.0)
            elif shift == 128:
                left, right = w1, jnp.zeros_like(w1)
            else:
                left = jnp.where(lane < keep, pltpu.roll(w1, keep, axis=1), 0.0)
                right = jnp.zeros_like(w1)
            l_ref[j, rows, 0:128] = left.astype(BF16)
            l_ref[j, rows, 128:256] = right.astype(BF16)
        l_ref[j, CW:L_ROWS, :] = bp_ref[j].T.astype(BF16)


def _s5_prep(lam_re, lam_im, log_dt, b_re, b_im, c_re, c_im):
    g, p, gb, half = S5_GROUPS, S5_STATE, GROUPS_PER_TILE, PAIRS_PER_TILE
    lre = lam_re.reshape(g, 1, p)
    lim = lam_im.reshape(g, 1, p)
    ldt = jnp.broadcast_to(log_dt.reshape(g, 1, 1), (g, 1, p))
    bre = jnp.transpose(b_re, (0, 2, 1))
    bim = jnp.transpose(b_im, (0, 2, 1))
    sd = jax.ShapeDtypeStruct
    blk = lambda n, r, c: pl.BlockSpec((n, r, c), lambda i: (i, 0, 0))
    return pl.pallas_call(
        _s5_prep_body,
        grid=(g // gb,),
        in_specs=[blk(gb, 1, p)] * 3 + [blk(gb, S5_GROUP, p)] * 4,
        out_specs=(
            blk(gb, L_ROWS, CW), blk(gb, CW, 2 * p), blk(gb, CW, 2 * p),
            blk(gb, S5_GROUP, 2 * p), blk(gb, S5_GROUP, 2 * p),
            blk(half, 8, 2 * p), blk(half, 8, 2 * p), blk(gb, p, 128), blk(gb, p, 128),
        ),
        out_shape=(
            sd((g, L_ROWS, CW), BF16),
            sd((g, CW, 2 * p), F32),
            sd((g, CW, 2 * p), F32),
            sd((g, S5_GROUP, 2 * p), F32),
            sd((g, S5_GROUP, 2 * p), F32),
            sd((g // 2, 8, 2 * p), F32),
            sd((g // 2, 8, 2 * p), F32),
            sd((g, p, 128), F32),
            sd((g, p, 128), F32),
        ),
        scratch_shapes=[pltpu.VMEM((gb, CW, 2 * p), F32), pltpu.VMEM((gb, 1, 2 * p), F32)],
        compiler_params=pltpu.CompilerParams(dimension_semantics=("parallel",)),
        name="s5_prep",
    )(lre, lim, ldt, bre, bim, c_re, c_im)


def _s5p_body(u_ref, l_ref, cpre_ref, cpim_ref, ar_ref, ai_ref, y_ref, hre_ref, him_ref,
              ut_ref, yt_ref, *, nb, nk):
    gb, half = GROUPS_PER_TILE, PAIRS_PER_TILE
    p = S5_STATE
    nrow = nb * nk
    d = functools.partial(jnp.dot, preferred_element_type=F32)

    for t in range(CHUNK):
        xt = u_ref[pl.ds(t, nrow, stride=CHUNK), :].T
        for j in range(gb):
            ut_ref[j, t * S5_GROUP:(t + 1) * S5_GROUP, :] = xt[j * S5_GROUP:(j + 1) * S5_GROUP, :]

    s_re, s_im = [], []
    for j in range(gb):
        r = d(l_ref[j], ut_ref[j].astype(BF16))
        yt_ref[j] = r[0:CW]
        s_re.append(r[CW:CW + p])
        s_im.append(r[CW + p:L_ROWS])

    nblk = nrow // 8
    row8 = lax.broadcasted_iota(jnp.int32, (nblk, 8, 2 * p), 1)
    rows = lax.broadcasted_iota(jnp.int32, (nrow, 2 * p), 0) & (nk - 1)
    dnt = functools.partial(lax.dot_general, dimension_numbers=NT, preferred_element_type=F32)
    for q in range(half):
        re = jnp.concatenate([s_re[q], s_re[q + half]], axis=0).T.reshape(nblk, 8, 2 * p)
        im = jnp.concatenate([s_im[q], s_im[q + half]], axis=0).T.reshape(nblk, 8, 2 * p)
        for sh in (1, 2, 4):
            keep = row8 >= sh
            rs = jnp.where(keep, pltpu.roll(re, sh, axis=1), 0.0)
            js = jnp.where(keep, pltpu.roll(im, sh, axis=1), 0.0)
            ar = ar_ref[q, sh - 1:sh, :]
            ai = ai_ref[q, sh - 1:sh, :]
            re, im = re + ar * rs - ai * js, im + ar * js + ai * rs
        pw_r, pw_i = ar_ref[q], ai_ref[q]
        out_r, out_i = [], []
        for k in range(nblk):
            hr, hi = re[k], im[k]
            if k % (nk // 8):
                hr, hi = hr + pw_r * cr - pw_i * ci, hi + pw_r * ci + pw_i * cr
            cr, ci = hr[7:8, :], hi[7:8, :]
            out_r.append(hr)
            out_i.append(hi)
            if (k + 1) % (nk // 8) == 0:
                b = k // (nk // 8)
                hre_ref[q, b:b + 1, :] = cr
                him_ref[q, b:b + 1, :] = ci
        re = jnp.concatenate(out_r, axis=0)
        im = jnp.concatenate(out_i, axis=0)
        pre = jnp.where(rows >= 1, pltpu.roll(re, 1, axis=0), 0.0).astype(BF16)
        pim = jnp.where(rows >= 1, pltpu.roll(im, 1, axis=0), 0.0).astype(BF16)
        for j in (q, q + half):
            yt_ref[j] = (yt_ref[j] + dnt(cpre_ref[j].astype(BF16), pre)
                         + dnt(cpim_ref[j].astype(BF16), pim))

    for t in range(CHUNK):
        yt = jnp.concatenate(
            [yt_ref[j, t * S5_GROUP:(t + 1) * S5_GROUP, :] for j in range(gb)], axis=0)
        y_ref[pl.ds(t, nrow, stride=CHUNK), :] = yt.T


def _s5_prompt(proj_p, lmat, cpre, cpim, ar, ai, *, nb, seq):
    g, gb, half = S5_GROUPS, GROUPS_PER_TILE, PAIRS_PER_TILE
    n_p = nb * seq
    nk = seq // CHUNK
    blk = lambda n, r, c: pl.BlockSpec((n, r, c), lambda i: (i, 0, 0))
    return pl.pallas_call(
        functools.partial(_s5p_body, nb=nb, nk=nk),
        grid=(g // gb,),
        in_specs=[pl.BlockSpec((n_p, 128), lambda i: (0, i)),
                  blk(gb, L_ROWS, CW), blk(gb, CW, 2 * S5_STATE), blk(gb, CW, 2 * S5_STATE),
                  blk(half, 8, 2 * S5_STATE), blk(half, 8, 2 * S5_STATE)],
        out_specs=(pl.BlockSpec((n_p, 128), lambda i: (0, i)),
                   blk(half, nb, 2 * S5_STATE), blk(half, nb, 2 * S5_STATE)),
        out_shape=(jax.ShapeDtypeStruct((n_p, D_S5), F32),
                   jax.ShapeDtypeStruct((g // 2, nb, 2 * S5_STATE), F32),
                   jax.ShapeDtypeStruct((g // 2, nb, 2 * S5_STATE), F32)),
        scratch_shapes=[pltpu.VMEM((gb, CW, nb * nk), F32), pltpu.VMEM((gb, CW, nb * nk), F32)],
        compiler_params=pltpu.CompilerParams(
            dimension_semantics=("parallel",), vmem_limit_bytes=VMEM_LIMIT),
        name="s5_prompt",
    )(proj_p, lmat, cpre, cpim, ar, ai)


def _s5d_body(u_ref, hre_ref, him_ref, bd_ref, cd_ref, lr_ref, li_ref, y_ref, ore_ref, oim_ref):
    p = S5_STATE
    gb = GROUPS_PER_TILE
    tn = (((0,), (0,)), ((), ()))
    ut = u_ref[...].T
    yts = []
    for j in range(gb):
        ug = ut[j * S5_GROUP:(j + 1) * S5_GROUP, :].astype(BF16)
        bu = lax.dot_general(bd_ref[j].astype(BF16), ug, tn,
                             preferred_element_type=F32)
        h0r, h0i = hre_ref[j], him_ref[j]
        lbr, lbi = lr_ref[j], li_ref[j]
        hr = lbr * h0r - lbi * h0i + bu[0:p]
        hi = lbr * h0i + lbi * h0r + bu[p:2 * p]
        ore_ref[j] = hr
        oim_ref[j] = hi
        h = jnp.concatenate([hr, hi], axis=0).astype(BF16)
        yts.append(jnp.dot(cd_ref[j].astype(BF16), h, preferred_element_type=F32))
    y_ref[...] = jnp.concatenate(yts, axis=0).T


def _s5_decode(proj_d, h0_re, h0_im, bd, cd, lr, li):
    g, gb, p = S5_GROUPS, GROUPS_PER_TILE, S5_STATE
    nbatch = proj_d.shape[0]
    blk = lambda r, c: pl.BlockSpec((gb, r, c), lambda i: (i, 0, 0))
    cols = pl.BlockSpec((nbatch, 128), lambda i: (0, i))
    return pl.pallas_call(
        _s5d_body,
        grid=(g // gb,),
        in_specs=[cols, blk(p, nbatch), blk(p, nbatch), blk(S5_GROUP, 2 * p), blk(S5_GROUP, 2 * p),
                  blk(p, nbatch), blk(p, nbatch)],
        out_specs=(cols, blk(p, nbatch), blk(p, nbatch)),
        out_shape=(jax.ShapeDtypeStruct((nbatch, D_S5), F32),
                   jax.ShapeDtypeStruct((g, p, nbatch), F32),
                   jax.ShapeDtypeStruct((g, p, nbatch), F32)),
        compiler_params=pltpu.CompilerParams(dimension_semantics=("parallel",)),
        name="s5_decode",
    )(proj_d, h0_re, h0_im, bd, cd, lr, li)


def _lru_gates(xc, wa_ref, wx_ref, ba, bx, lam):
    xcb = xc.astype(BF16)
    nblk = D_LRU // MXU_WIDTH_V7X
    r_parts, i_parts = [], []
    for k in range(nblk):
        xk = xcb[:, k * MXU_WIDTH_V7X:(k + 1) * MXU_WIDTH_V7X]
        r_parts.append(jnp.dot(xk, wa_ref[k], preferred_element_type=F32))
        i_parts.append(jnp.dot(xk, wx_ref[k], preferred_element_type=F32))
    r = jax.nn.sigmoid(jnp.concatenate(r_parts, axis=1) + ba)
    i = jax.nn.sigmoid(jnp.concatenate(i_parts, axis=1) + bx)
    z = -lam
    softplus = jnp.maximum(z, 0.0) + jnp.log1p(jnp.exp(-jnp.abs(z)))
    log_a = (-LRU_C * softplus) * r
    a = jnp.exp(log_a)
    v = -jnp.tanh(log_a) * (a * a + 1.0)
    mult = jnp.where(v > 0.0, v * lax.rsqrt(v), 0.0)
    return a, mult * (i * xc)


def _lru_tile(xl_ref, gate_ref, cw_ref, cb_ref, wa_ref, wx_ref, ba_ref, bx_ref, lam_ref,
              o_ref, xbuf_ref, carry_ref):
    tt = xl_ref.shape[0]
    x = xl_ref[...]
    xbuf_ref[8:8 + tt, :] = x
    cw = cw_ref[...]
    xc = (cb_ref[...] + xbuf_ref[5:5 + tt, :] * cw[0:1] + xbuf_ref[6:6 + tt, :] * cw[1:2]
          + xbuf_ref[7:7 + tt, :] * cw[2:3] + x * cw[3:4])
    xbuf_ref[0:8, :] = x[tt - 8:tt, :]

    a, b = _lru_gates(xc, wa_ref, wx_ref, ba_ref[...], bx_ref[...], lam_ref[...])

    nblk = tt // 8
    a3 = a.reshape(nblk, 8, D_LRU)
    b3 = b.reshape(nblk, 8, D_LRU)
    row = lax.broadcasted_iota(jnp.int32, (nblk, 8, D_LRU), 1)
    for sh in (1, 2, 4):
        keep = row >= sh
        bs = jnp.where(keep, pltpu.roll(b3, sh, axis=1), 0.0)
        sa = jnp.where(keep, pltpu.roll(a3, sh, axis=1), 1.0)
        b3 = b3 + a3 * bs
        a3 = a3 * sa
    carry = carry_ref[0:1, :]
    gate = gate_ref[...]
    for k in range(nblk):
        h = b3[k] + a3[k] * carry
        carry = h[7:8, :]
        o_ref[k * 8:(k + 1) * 8, :] = h * gate[k * 8:(k + 1) * 8, :]
    carry_ref[...] = jnp.broadcast_to(carry, (8, D_LRU))
    return carry


def _mix_s5_part(ys, u, x, dsk_ref, wg_ref, bg_ref, gs_ref, wo_ref):
    yy = ys + dsk_ref[...] * u
    g = _gelu(yy)
    z = jnp.dot(g.astype(BF16), wg_ref[...], preferred_element_type=F32) + bg_ref[...]
    s5o = g * jax.nn.sigmoid(z)
    n1 = _rms(s5o, gs_ref[...]).astype(BF16)
    return x + jnp.dot(n1, wo_ref[0:D_S5, :], preferred_element_type=F32)


def _mix_lru_part(lru, gl_ref, wo_ref):
    n2 = _rms(lru, gl_ref[...]).astype(BF16)
    return jnp.dot(n2, wo_ref[D_S5:, :], preferred_element_type=F32)


def _lru_mix_body(xl_ref, gate_ref, ys_ref, u_ref, x_ref,
                  cw_ref, cb_ref, wa_ref, wx_ref, ba_ref, bx_ref, lam_ref,
                  dsk_ref, wg_ref, bg_ref, gs_ref, gl_ref, wo_ref,
                  o_ref, hl_ref, xbuf_ref, carry_ref, lru_ref):
    @pl.when(pl.program_id(1) == 0)
    def _():
        xbuf_ref[0:8, :] = jnp.zeros((8, D_LRU), F32)
        carry_ref[...] = jnp.zeros((8, D_LRU), F32)

    o_ref[...] = _mix_s5_part(ys_ref[...], u_ref[...], x_ref[...],
                              dsk_ref, wg_ref, bg_ref, gs_ref, wo_ref)
    hl_ref[0] = _lru_tile(xl_ref, gate_ref, cw_ref, cb_ref, wa_ref, wx_ref, ba_ref, bx_ref, lam_ref,
                          lru_ref, xbuf_ref, carry_ref)
    o_ref[...] += _mix_lru_part(lru_ref[...], gl_ref, wo_ref)


def _lru_mix_prompt(proj_p, ys_p, x1_p, cw, cb, wa, wx, ba, bx, lam, dsk, wg, bg, gs, gl, wo,
                    *, nb, seq, tt=LRU_TT):
    nt = seq // tt
    rows = lambda c, col: pl.BlockSpec((tt, c), lambda b, t: (b * nt + t, col))
    once = lambda shape: pl.BlockSpec(shape, lambda b, t: (0,) * len(shape),
                                      pipeline_mode=pl.Buffered(1))
    return pl.pallas_call(
        _lru_mix_body,
        grid=(nb, nt),
        in_specs=[
            rows(D_LRU, 1), rows(D_LRU, 2),
            rows(D_S5, 0), rows(D_S5, 0), rows(D_MODEL, 0),
            once((CONV_W, D_LRU)), once((1, D_LRU)),
            once((D_LRU // 256, 256, 256)), once((D_LRU // 256, 256, 256)),
            once((1, D_LRU)), once((1, D_LRU)), once((1, D_LRU)),
            once((1, D_S5)), once((D_S5, D_S5)), once((1, D_S5)), once((1, D_S5)), once((1, D_LRU)),
            once((D_MODEL, D_MODEL)),
        ],
        out_specs=(pl.BlockSpec((tt, D_MODEL), lambda b, t: (b * nt + t, 0)),
                   pl.BlockSpec((1, 1, D_LRU), lambda b, t: (b, 0, 0))),
        out_shape=(jax.ShapeDtypeStruct((nb * seq, D_MODEL), F32),
                   jax.ShapeDtypeStruct((nb, 1, D_LRU), F32)),
        scratch_shapes=[pltpu.VMEM((tt + 8, D_LRU), F32), pltpu.VMEM((8, D_LRU), F32),
                        pltpu.VMEM((tt, D_LRU), F32)],
        compiler_params=pltpu.CompilerParams(
            dimension_semantics=("parallel", "arbitrary"), vmem_limit_bytes=VMEM_LIMIT),
        name="lru_mix_prompt",
    )(proj_p, proj_p, ys_p, proj_p, x1_p, cw, cb, wa, wx, ba, bx, lam, dsk, wg, bg, gs, gl, wo)


def _lru_decode_body(xl_ref, gate_ref, conv_ref, h0_ref, cw_ref, cb_ref,
                     wa_ref, wx_ref, ba_ref, bx_ref, lam_ref, o_ref, h_ref, buf_ref):
    x = xl_ref[...]
    cw = cw_ref[...]
    c0, c1, c2 = (conv_ref[k] for k in range(CONV_W - 1))
    xc = cb_ref[...] + c0 * cw[0:1] + c1 * cw[1:2] + c2 * cw[2:3] + x * cw[3:4]
    a, b = _lru_gates(xc, wa_ref, wx_ref, ba_ref[...], bx_ref[...], lam_ref[...])
    h = a * h0_ref[...] + b
    h_ref[...] = h
    o_ref[...] = h * gate_ref[...]
    for k, rows in enumerate((c1, c2, x)):
        buf_ref[k] = rows


def _lru_decode(proj_d, conv0, h0, cw, cb, wa, wx, ba, bx, lam):
    nd = proj_d.shape[0]
    full = lambda r: pl.BlockSpec((r, D_LRU), lambda i: (0, 0))
    conv = pl.BlockSpec((CONV_W - 1, nd, D_LRU), lambda i: (0, 0, 0))
    wspec = pl.BlockSpec((D_LRU // 256, 256, 256), lambda i: (0, 0, 0))
    return pl.pallas_call(
        _lru_decode_body,
        grid=(1,),
        in_specs=[
            pl.BlockSpec((nd, D_LRU), lambda i: (0, 1)),
            pl.BlockSpec((nd, D_LRU), lambda i: (0, 2)),
            conv, full(nd),
            full(CONV_W), full(1), wspec, wspec, full(1), full(1), full(1),
        ],
        out_specs=(full(nd), full(nd), conv),
        out_shape=(jax.ShapeDtypeStruct((nd, D_LRU), F32),
                   jax.ShapeDtypeStruct((nd, D_LRU), F32),
                   jax.ShapeDtypeStruct((CONV_W - 1, nd, D_LRU), F32)),
        name="lru_decode",
    )(proj_d, proj_d, conv0, h0, cw, cb, wa, wx, ba, bx, lam)


def _mix_decode_body(ys_ref, u_ref, lru_ref, x_ref, dsk_ref, wg_ref, bg_ref, gs_ref, gl_ref,
                     wo_ref, o_ref):
    o_ref[...] = (_mix_s5_part(ys_ref[...], u_ref[...], x_ref[...],
                               dsk_ref, wg_ref, bg_ref, gs_ref, wo_ref)
                  + _mix_lru_part(lru_ref[...], gl_ref, wo_ref))


def _mix_decode(ys_d, proj_d, lru_d, x1_d, dsk, wg, bg, gs, gl, wo):
    nd = x1_d.shape[0]
    full = lambda r, c: pl.BlockSpec((r, c), lambda i: (0, 0))
    return pl.pallas_call(
        _mix_decode_body,
        grid=(1,),
        in_specs=[full(nd, D_S5), full(nd, D_S5), full(nd, D_LRU), full(nd, D_MODEL),
                  full(1, D_S5), full(D_S5, D_S5), full(1, D_S5), full(1, D_S5), full(1, D_LRU),
                  full(D_MODEL, D_MODEL)],
        out_specs=full(nd, D_MODEL),
        out_shape=jax.ShapeDtypeStruct((nd, D_MODEL), F32),
        compiler_params=pltpu.CompilerParams(vmem_limit_bytes=VMEM_LIMIT),
        name="mix_decode",
    )(ys_d, proj_d, lru_d, x1_d, dsk, wg, bg, gs, gl, wo)


def _unpair(h, nb):
    tiles = S5_GROUPS // GROUPS_PER_TILE
    h5 = h.reshape(tiles, PAIRS_PER_TILE, nb, 2, S5_STATE)
    return jnp.transpose(h5, (2, 0, 3, 1, 4)).reshape(nb, S5_GROUPS, S5_STATE)


def _block_diag4(w):
    w4 = w.reshape(LRU_HEADS // 4, 4, LRU_HEAD_DIM, LRU_HEAD_DIM)
    eye = jnp.eye(4, dtype=w.dtype)
    return jnp.einsum("kaij,ab->kaibj", w4, eye).reshape(LRU_HEADS // 4, 256, 256)


def kernel(x_prompt, x_sample, state_s5_re, state_s5_im, state_lru_h, state_lru_conv, g_ffn1, w1_a, w3_a, w2_a, g_mix, w_in, lam_re, lam_im, log_dt, b_re, b_im, c_re, c_im, d_skip, w_glu, b_glu, conv_w, conv_b, w_a, b_a, w_x, b_x, lam_l, g_out_s5, g_out_lru, w_out, g_ffn2, w1_b, w3_b, w2_b, g_final):
    nb, seq, _ = x_prompt.shape
    nd = x_sample.shape[0]
    n_p = nb * seq
    row = lambda v: v.reshape(1, -1)
    assert w1_a.shape[0] == 1 and x_sample.shape[1] == 1
    assert nd == 128 and n_p % FFN_TM == 0 and n_p > FFN_TM
    assert seq % LRU_TT == 0 and seq % (8 * CHUNK) == 0

    xp = x_prompt.reshape(n_p, D_MODEL)
    xd = x_sample.reshape(nd, D_MODEL)

    n_i, n_f = n_p // FFN_TM, D_FF // FFN_TF
    x1_p, x1_d, w1_b16, w3_b16, w2_b16 = _ffn(
        xp, xd, row(g_ffn1[0]), w1_a[0].astype(BF16), w3_a[0].astype(BF16), w2_a[0].astype(BF16),
        casts=(_cast_job(w1_b[0], n_i, n_f), _cast_job(w3_b[0], n_i, n_f),
               _row_cast_job(w2_b[0], n_i, n_f)))
    nj = D_IN // PROJ_TN
    w_in16 = jnp.transpose(w_in[0].astype(BF16).reshape(D_MODEL, nj, PROJ_TN), (1, 0, 2))
    proj_p, proj_d, w_out16, w_glu16 = _inproj(
        x1_p, x1_d, row(g_mix[0]), w_in16,
        casts=(_cast_job(w_out[0], n_i, nj, bc=PROJ_TN), _cast_job(w_glu[0], n_i, nj, bc=PROJ_TN)))

    lmat, cpre, cpim, bd, cd, ar, ai, lr, li = _s5_prep(
        lam_re[0], lam_im[0], log_dt[0], b_re[0], b_im[0], c_re[0], c_im[0])
    ys_p, hf_re, hf_im = _s5_prompt(proj_p, lmat, cpre, cpim, ar, ai, nb=nb, seq=seq)

    to_gpb = lambda s: jnp.transpose(s, (1, 2, 0))
    ys_d, hd_re, hd_im = _s5_decode(proj_d, to_gpb(state_s5_re[0]), to_gpb(state_s5_im[0]),
                                    bd, cd, lr, li)

    wa_bd = _block_diag4(w_a[0]).astype(BF16)
    wx_bd = _block_diag4(w_x[0]).astype(BF16)
    lru_args = (conv_w[0], row(conv_b[0]), wa_bd, wx_bd, row(b_a[0]), row(b_x[0]), row(lam_l[0]))
    lru_d, hl_d, buf_d = _lru_decode(proj_d, jnp.transpose(state_lru_conv[0], (1, 0, 2)),
                                     state_lru_h[0], *lru_args)

    mix_args = (row(d_skip[0]), w_glu16, row(b_glu[0]), row(g_out_s5[0]), row(g_out_lru[0]), w_out16)
    x2_p, hl_p = _lru_mix_prompt(proj_p, ys_p, x1_p, *lru_args, *mix_args, nb=nb, seq=seq)
    x2_d = _mix_decode(ys_d, proj_d, lru_d, x1_d, *mix_args)
    y_p, y_d = _ffn(x2_p, x2_d, row(g_ffn2[0]), w1_b16, w3_b16, w2_b16, row(g_final))

    tail_p = proj_p.reshape(nb, seq, -1)[:, seq - (CONV_W - 1):, D_S5:D_S5 + D_LRU]
    return (
        y_p.reshape(nb, seq, D_MODEL),
        y_d.reshape(nd, 1, D_MODEL),
        _unpair(hf_re, nb)[None],
        _unpair(hf_im, nb)[None],
        hl_p.reshape(1, nb, D_LRU),
        tail_p[None],
        jnp.transpose(hd_re, (2, 0, 1))[None],
        jnp.transpose(hd_im, (2, 0, 1))[None],
        hl_d[None],
        jnp.transpose(buf_d, (1, 0, 2))[None],
    )
```

```python
import functools

import jax
import jax.numpy as jnp
from jax import lax
from jax.experimental import pallas as pl
from jax.experimental.pallas import tpu as pltpu

F32 = jnp.float32
BF16 = jnp.bfloat16

D_MODEL = 2048
D_S5 = 1024
S5_GROUP = 16
S5_GROUPS = 64
S5_STATE = 64
D_LRU = 1024
LRU_HEADS = 16
LRU_HEAD_DIM = 64
CONV_W = 4
LRU_C = 8.0
D_FF = 5632
D_IN = D_S5 + 2 * D_LRU
EPS = 1e-6

MXU_WIDTH_V7X = 256
CHUNK = MXU_WIDTH_V7X // S5_GROUP
CW = CHUNK * S5_GROUP

FFN_TM = 1024
FFN_TF = 512
PROJ_TN = 1024
LRU_TT = 512

VMEM_CAPACITY_V7X = 64 * 1024 * 1024
VMEM_LIMIT = VMEM_CAPACITY_V7X - 6 * 1024 * 1024
FFN_VMEM_LIMIT = VMEM_CAPACITY_V7X - 2 * 1024 * 1024

NN = (((1,), (0,)), ((), ()))
NT = (((1,), (1,)), ((), ()))


def _rms(x, g):
    return x * lax.rsqrt(jnp.mean(x * x, axis=-1, keepdims=True) + EPS) * g


def _split(x):
    hi = x.astype(BF16)
    lo = (x - hi.astype(F32)).astype(BF16)
    return hi, lo


def _dot3(a, b, dims=NN):
    ah, al = _split(a)
    bh, bl = _split(b)
    d = functools.partial(lax.dot_general, dimension_numbers=dims, preferred_element_type=F32)
    return d(ah, bh) + d(al, bh) + d(ah, bl)


def _gelu(x):
    return jax.nn.gelu(x, approximate=True)


def _ffn_body(*refs, final_norm, n_casts):
    refs = list(refs)
    xn_ref = refs.pop()
    xp_ref, xd_ref, g_ref, w1_ref, w3_ref, w2_ref = (refs.pop(0) for _ in range(6))
    gf_ref = refs.pop(0) if final_norm else None
    cast_in = [refs.pop(0) for _ in range(n_casts)]
    op_ref, od_ref = refs.pop(0), refs.pop(0)
    cast_out = refs
    i = pl.program_id(0)
    f = pl.program_id(1)
    tm = xp_ref.shape[0]

    for src_ref, dst_ref in zip(cast_in, cast_out):
        dst_ref[...] = src_ref[...].astype(BF16)

    def step(x_ref, o_ref, rows):
        @pl.when(f == 0)
        def _():
            x = x_ref[...]
            xn_ref[rows, :] = _rms(x, g_ref[...]).astype(BF16)
            o_ref[...] = x

        xn = xn_ref[rows, :]
        a = jnp.dot(xn, w1_ref[...], preferred_element_type=F32)
        b = jnp.dot(xn, w3_ref[...], preferred_element_type=F32)
        h = (a * jax.nn.sigmoid(a) * b).astype(BF16)
        o_ref[...] += 0.5 * jnp.dot(h, w2_ref[...], preferred_element_type=F32)

        if final_norm:
            @pl.when(f == pl.num_programs(1) - 1)
            def _():
                o_ref[...] = _rms(o_ref[...], gf_ref[...])

    step(xp_ref, op_ref, slice(0, tm))

    @pl.when(i == 0)
    def _():
        step(xd_ref, od_ref, slice(tm, tm + xd_ref.shape[0]))


def _cast_job(w, n_i, n_f, bc=FFN_TF):
    rows, cols = w.shape
    br = rows // n_i
    n_cb = cols // bc
    assert br * n_i == rows and bc * n_cb == cols and n_cb <= n_f and br % 16 == 0
    return w, pl.BlockSpec(
        (br, bc), lambda i, f: (jnp.minimum(i, n_i - 1),
                                jnp.where(i < n_i, jnp.minimum(f, n_cb - 1), n_cb - 1)))


def _row_cast_job(w, n_i, n_f):
    rows, cols = w.shape
    br = rows // (n_i * n_f)
    assert br * n_i * n_f == rows and br % 16 == 0
    return w, pl.BlockSpec((br, cols), lambda i, f: (i * n_f + f, 0))


def _ffn(xp, xd, g, w1, w3, w2, g_final=None, *, casts=(), tm=FFN_TM, tf=FFN_TF):
    n_p, nd = xp.shape[0], xd.shape[0]
    final_norm = g_final is not None
    pspec = pl.BlockSpec((tm, D_MODEL), lambda i, f: (i, 0))
    dspec = pl.BlockSpec((nd, D_MODEL), lambda i, f: (0, 0))
    vspec = pl.BlockSpec((1, D_MODEL), lambda i, f: (0, 0))
    in_specs = [pspec, pl.BlockSpec((nd, D_MODEL), lambda i, f: (0, 0), pipeline_mode=pl.Buffered(1)),
                vspec,
                pl.BlockSpec((D_MODEL, tf), lambda i, f: (0, f)),
                pl.BlockSpec((D_MODEL, tf), lambda i, f: (0, f)),
                pl.BlockSpec((tf, D_MODEL), lambda i, f: (f, 0))]
    args = [xp, xd, g, w1, w3, w2]
    if final_norm:
        in_specs.append(vspec)
        args.append(g_final)
    in_specs += [spec for _, spec in casts]
    args += [w for w, _ in casts]
    sd = jax.ShapeDtypeStruct
    return pl.pallas_call(
        functools.partial(_ffn_body, final_norm=final_norm, n_casts=len(casts)),
        grid=(n_p // tm, D_FF // tf),
        in_specs=in_specs,
        out_specs=(pspec, dspec) + tuple(spec for _, spec in casts),
        out_shape=(sd((n_p, D_MODEL), F32), sd((nd, D_MODEL), F32))
        + tuple(sd(w.shape, BF16) for w, _ in casts),
        scratch_shapes=[pltpu.VMEM((tm + nd, D_MODEL), BF16)],
        compiler_params=pltpu.CompilerParams(
            dimension_semantics=("arbitrary", "arbitrary"), vmem_limit_bytes=FFN_VMEM_LIMIT),
        name="ffn_final" if final_norm else "ffn",
    )(*args)


def _inproj_body(*refs, n_tiles, nd, n_casts):
    xp_ref, xd_ref, g_ref, w_ref = refs[:4]
    cast_in = refs[4:4 + n_casts]
    op_ref, od_ref = refs[4 + n_casts:6 + n_casts]
    cast_out = refs[6 + n_casts:-1]
    xn_ref = refs[-1]
    i = pl.program_id(0)
    j = pl.program_id(1)
    tn = op_ref.shape[1]
    nj = w_ref.shape[1] // tn

    for src_ref, dst_ref in zip(cast_in, cast_out):
        dst_ref[...] = src_ref[...].astype(BF16)

    def run(x_ref, o_ref, rows):
        @pl.when(j == 0)
        def _():
            xn_ref[0:rows, :] = _rms(x_ref[...], g_ref[...]).astype(BF16)

        for col in range(nj):
            @pl.when(j == col)
            def _(col=col):
                y = jnp.dot(xn_ref[0:rows, :], w_ref[:, col * tn:(col + 1) * tn],
                            preferred_element_type=F32)
                o_ref[...] = _gelu(y) if col == nj - 1 else y

    @pl.when(i < n_tiles)
    def _():
        run(xp_ref, op_ref, xp_ref.shape[0])

    @pl.when(i == n_tiles)
    def _():
        run(xd_ref, od_ref, nd)


def _inproj(xp, xd, g, w, *, casts=(), tm=FFN_TM, tn=PROJ_TN):
    n_p, nd = xp.shape[0], xd.shape[0]
    n_tiles = n_p // tm
    d_out = w.shape[1]
    nj = d_out // tn
    return pl.pallas_call(
        functools.partial(_inproj_body, n_tiles=n_tiles, nd=nd, n_casts=len(casts)),
        grid=(n_tiles + 1, nj),
        in_specs=[
            pl.BlockSpec((tm, D_MODEL), lambda i, j: (jnp.minimum(i, n_tiles - 1), 0)),
            pl.BlockSpec((nd, D_MODEL), lambda i, j: (0, 0)),
            pl.BlockSpec((1, D_MODEL), lambda i, j: (0, 0)),
            pl.BlockSpec((D_MODEL, d_out), lambda i, j: (0, 0), pipeline_mode=pl.Buffered(1)),
        ] + [spec for _, spec in casts],
        out_specs=(
            pl.BlockSpec((tm, tn), lambda i, j: (jnp.minimum(i, n_tiles - 1),
                                                 jnp.where(i < n_tiles, j, nj - 1))),
            pl.BlockSpec((nd, tn), lambda i, j: (0, jnp.where(i < n_tiles, 0, j))),
        ) + tuple(spec for _, spec in casts),
        out_shape=(jax.ShapeDtypeStruct((n_p, d_out), F32),
                   jax.ShapeDtypeStruct((nd, d_out), F32))
        + tuple(jax.ShapeDtypeStruct(cw.shape, BF16) for cw, _ in casts),
        scratch_shapes=[pltpu.VMEM((tm, D_MODEL), BF16)],
        compiler_params=pltpu.CompilerParams(
            dimension_semantics=("arbitrary", "arbitrary"), vmem_limit_bytes=VMEM_LIMIT),
        name="inproj",
    )(xp, xd, g, w, *[cw for cw, _ in casts])


GROUPS_PER_TILE = 128 // S5_GROUP
PAIRS_PER_TILE = GROUPS_PER_TILE // 2
L_ROWS = CW + 2 * S5_STATE


def _lam_bar(lam_re, lam_im, log_dt):
    dt = jnp.exp(log_dt)
    mag = jnp.exp(lam_re * dt)
    ang = lam_im * dt
    return mag * jnp.cos(ang), mag * jnp.sin(ang)


def _s5_prep_body(lre_ref, lim_ref, ldt_ref, bre_ref, bim_ref, cre_ref, cim_ref,
                  l_ref, cpre_ref, cpim_ref, bd_ref, cd_ref, ar_ref, ai_ref, lr_ref, li_ref,
                  bp_ref, lrow_ref):
    gb, half = GROUPS_PER_TILE, PAIRS_PER_TILE
    p = S5_STATE
    lo, hi = slice(0, p), slice(p, 2 * p)
    lam_re = lre_ref[...]
    lam_im = lim_ref[...]
    lbr, lbi = _lam_bar(lam_re, lam_im, ldt_ref[...])
    lrow_ref[:, :, lo] = lbr
    lrow_ref[:, :, hi] = lbi
    for j in range(gb):
        cols = jnp.broadcast_to(lrow_ref[j], (2 * p, 2 * p)).T
        lr_ref[j] = cols[0:p]
        li_ref[j] = cols[p:2 * p]
    nr = lbr - 1.0
    den = lam_re * lam_re + lam_im * lam_im
    cr = (nr * lam_re + lbi * lam_im) / den
    ci = (lbi * lam_re - nr * lam_im) / den
    b_re = bre_ref[...]
    b_im = bim_ref[...]
    bbr = cr * b_re - ci * b_im
    bbi = cr * b_im + ci * b_re
    bd_ref[:, :, lo] = bbr
    bd_ref[:, :, hi] = bbi
    c_re = cre_ref[...]
    c_im = cim_ref[...]
    cd_ref[:, :, lo] = c_re
    cd_ref[:, :, hi] = -c_im

    zeros = jnp.zeros((half, S5_GROUP, p), F32)
    pr = jnp.ones_like(lbr)
    pi = jnp.zeros_like(lbr)
    for d in range(CHUNK):
        rows = slice(d * S5_GROUP, (d + 1) * S5_GROUP)
        back = slice((CHUNK - 1 - d) * S5_GROUP, (CHUNK - d) * S5_GROUP)
        bp_ref[:, back, lo] = bbr * pr - bbi * pi
        bp_ref[:, back, hi] = bbr * pi + bbi * pr
        pr, pi = pr * lbr - pi * lbi, pr * lbi + pi * lbr
        cp_r = c_re * pr - c_im * pi
        cp_i = -(c_re * pi + c_im * pr)
        cpre_ref[0:half, rows, lo] = cp_r[0:half]
        cpre_ref[0:half, rows, hi] = zeros
        cpre_ref[half:gb, rows, lo] = zeros
        cpre_ref[half:gb, rows, hi] = cp_r[half:gb]
        cpim_ref[0:half, rows, lo] = cp_i[0:half]
        cpim_ref[0:half, rows, hi] = zeros
        cpim_ref[half:gb, rows, lo] = zeros
        cpim_ref[half:gb, rows, hi] = cp_i[half:gb]

    qr, qi = pr, pi
    for r in range(8):
        ar_ref[:, r:r + 1, lo] = qr[0:half]
        ar_ref[:, r:r + 1, hi] = qr[half:gb]
        ai_ref[:, r:r + 1, lo] = qi[0:half]
        ai_ref[:, r:r + 1, hi] = qi[half:gb]
        qr, qi = qr * pr - qi * pi, qr * pi + qi * pr

    lane = lax.broadcasted_iota(jnp.int32, (S5_GROUP, 128), 1)
    for j in range(gb):
        w = _dot3(cd_ref[j], bp_ref[j], NT)
        w0, w1 = w[:, :128], w[:, 128:]
        for t in range(CHUNK):
            rows = slice(t * S5_GROUP, (t + 1) * S5_GROUP)
            shift = (CHUNK - 1 - t) * S5_GROUP
            keep = 128 - shift % 128
            if shift == 0:
                left, right = w0, w1
            elif shift < 128:
                r0 = pltpu.roll(w0, keep, axis=1)
                r1 = pltpu.roll(w1, keep, axis=1)
                left = jnp.where(lane < keep, r0, r1)
                right = jnp.where(lane < keep, r1, 0.0)
            elif shift == 128:
                left, right = w1, jnp.zeros_like(w1)
            else:
                left = jnp.where(lane < keep, pltpu.roll(w1, keep, axis=1), 0.0)
                right = jnp.zeros_like(w1)
            l_ref[j, rows, 0:128] = left.astype(BF16)
            l_ref[j, rows, 128:256] = right.astype(BF16)
        l_ref[j, CW:L_ROWS, :] = bp_ref[j].T.astype(BF16)


def _s5_prep(lam_re, lam_im, log_dt, b_re, b_im, c_re, c_im):
    g, p, gb, half = S5_GROUPS, S5_STATE, GROUPS_PER_TILE, PAIRS_PER_TILE
    lre = lam_re.reshape(g, 1, p)
    lim = lam_im.reshape(g, 1, p)
    ldt = jnp.broadcast_to(log_dt.reshape(g, 1, 1), (g, 1, p))
    bre = jnp.transpose(b_re, (0, 2, 1))
    bim = jnp.transpose(b_im, (0, 2, 1))
    sd = jax.ShapeDtypeStruct
    blk = lambda n, r, c: pl.BlockSpec((n, r, c), lambda i: (i, 0, 0))
    return pl.pallas_call(
        _s5_prep_body,
        grid=(g // gb,),
        in_specs=[blk(gb, 1, p)] * 3 + [blk(gb, S5_GROUP, p)] * 4,
        out_specs=(
            blk(gb, L_ROWS, CW), blk(gb, CW, 2 * p), blk(gb, CW, 2 * p),
            blk(gb, S5_GROUP, 2 * p), blk(gb, S5_GROUP, 2 * p),
            blk(half, 8, 2 * p), blk(half, 8, 2 * p), blk(gb, p, 128), blk(gb, p, 128),
        ),
        out_shape=(
            sd((g, L_ROWS, CW), BF16),
            sd((g, CW, 2 * p), F32),
            sd((g, CW, 2 * p), F32),
            sd((g, S5_GROUP, 2 * p), F32),
            sd((g, S5_GROUP, 2 * p), F32),
            sd((g // 2, 8, 2 * p), F32),
            sd((g // 2, 8, 2 * p), F32),
            sd((g, p, 128), F32),
            sd((g, p, 128), F32),
        ),
        scratch_shapes=[pltpu.VMEM((gb, CW, 2 * p), F32), pltpu.VMEM((gb, 1, 2 * p), F32)],
        compiler_params=pltpu.CompilerParams(dimension_semantics=("parallel",)),
        name="s5_prep",
    )(lre, lim, ldt, bre, bim, c_re, c_im)


def _s5p_body(u_ref, l_ref, cpre_ref, cpim_ref, ar_ref, ai_ref, y_ref, hre_ref, him_ref,
              ut_ref, yt_ref, *, nb, nk):
    gb, half = GROUPS_PER_TILE, PAIRS_PER_TILE
    p = S5_STATE
    nrow = nb * nk
    d = functools.partial(jnp.dot, preferred_element_type=F32)

    for t in range(CHUNK):
        xt = u_ref[pl.ds(t, nrow, stride=CHUNK), :].T
        for j in range(gb):
            ut_ref[j, t * S5_GROUP:(t + 1) * S5_GROUP, :] = xt[j * S5_GROUP:(j + 1) * S5_GROUP, :]

    s_re, s_im = [], []
    for j in range(gb):
        r = d(l_ref[j], ut_ref[j].astype(BF16))
        yt_ref[j] = r[0:CW]
        s_re.append(r[CW:CW + p])
        s_im.append(r[CW + p:L_ROWS])

    nblk = nrow // 8
    row8 = lax.broadcasted_iota(jnp.int32, (nblk, 8, 2 * p), 1)
    rows = lax.broadcasted_iota(jnp.int32, (nrow, 2 * p), 0) & (nk - 1)
    dnt = functools.partial(lax.dot_general, dimension_numbers=NT, preferred_element_type=F32)
    for q in range(half):
        re = jnp.concatenate([s_re[q], s_re[q + half]], axis=0).T.reshape(nblk, 8, 2 * p)
        im = jnp.concatenate([s_im[q], s_im[q + half]], axis=0).T.reshape(nblk, 8, 2 * p)
        for sh in (1, 2, 4):
            keep = row8 >= sh
            rs = jnp.where(keep, pltpu.roll(re, sh, axis=1), 0.0)
            js = jnp.where(keep, pltpu.roll(im, sh, axis=1), 0.0)
            ar = ar_ref[q, sh - 1:sh, :]
            ai = ai_ref[q, sh - 1:sh, :]
            re, im = re + ar * rs - ai * js, im + ar * js + ai * rs
        pw_r, pw_i = ar_ref[q], ai_ref[q]
        out_r, out_i = [], []
        for k in range(nblk):
            hr, hi = re[k], im[k]
            if k % (nk // 8):
                hr, hi = hr + pw_r * cr - pw_i * ci, hi + pw_r * ci + pw_i * cr
            cr, ci = hr[7:8, :], hi[7:8, :]
            out_r.append(hr)
            out_i.append(hi)
            if (k + 1) % (nk // 8) == 0:
                b = k // (nk // 8)
                hre_ref[q, b:b + 1, :] = cr
                him_ref[q, b:b + 1, :] = ci
        re = jnp.concatenate(out_r, axis=0)
        im = jnp.concatenate(out_i, axis=0)
        pre = jnp.where(rows >= 1, pltpu.roll(re, 1, axis=0), 0.0).astype(BF16)
        pim = jnp.where(rows >= 1, pltpu.roll(im, 1, axis=0), 0.0).astype(BF16)
        for j in (q, q + half):
            yt_ref[j] = (yt_ref[j] + dnt(cpre_ref[j].astype(BF16), pre)
                         + dnt(cpim_ref[j].astype(BF16), pim))

    for t in range(CHUNK):
        yt = jnp.concatenate(
            [yt_ref[j, t * S5_GROUP:(t + 1) * S5_GROUP, :] for j in range(gb)], axis=0)
        y_ref[pl.ds(t, nrow, stride=CHUNK), :] = yt.T


def _s5_prompt(proj_p, lmat, cpre, cpim, ar, ai, *, nb, seq):
    g, gb, half = S5_GROUPS, GROUPS_PER_TILE, PAIRS_PER_TILE
    n_p = nb * seq
    nk = seq // CHUNK
    blk = lambda n, r, c: pl.BlockSpec((n, r, c), lambda i: (i, 0, 0))
    return pl.pallas_call(
        functools.partial(_s5p_body, nb=nb, nk=nk),
        grid=(g // gb,),
        in_specs=[pl.BlockSpec((n_p, 128), lambda i: (0, i)),
                  blk(gb, L_ROWS, CW), blk(gb, CW, 2 * S5_STATE), blk(gb, CW, 2 * S5_STATE),
                  blk(half, 8, 2 * S5_STATE), blk(half, 8, 2 * S5_STATE)],
        out_specs=(pl.BlockSpec((n_p, 128), lambda i: (0, i)),
                   blk(half, nb, 2 * S5_STATE), blk(half, nb, 2 * S5_STATE)),
        out_shape=(jax.ShapeDtypeStruct((n_p, D_S5), F32),
                   jax.ShapeDtypeStruct((g // 2, nb, 2 * S5_STATE), F32),
                   jax.ShapeDtypeStruct((g // 2, nb, 2 * S5_STATE), F32)),
        scratch_shapes=[pltpu.VMEM((gb, CW, nb * nk), F32), pltpu.VMEM((gb, CW, nb * nk), F32)],
        compiler_params=pltpu.CompilerParams(
            dimension_semantics=("parallel",), vmem_limit_bytes=VMEM_LIMIT),
        name="s5_prompt",
    )(proj_p, lmat, cpre, cpim, ar, ai)


def _s5d_body(u_ref, hre_ref, him_ref, bd_ref, cd_ref, lr_ref, li_ref, y_ref, ore_ref, oim_ref):
    p = S5_STATE
    gb = GROUPS_PER_TILE
    tn = (((0,), (0,)), ((), ()))
    ut = u_ref[...].T
    yts = []
    for j in range(gb):
        ug = ut[j * S5_GROUP:(j + 1) * S5_GROUP, :].astype(BF16)
        bu = lax.dot_general(bd_ref[j].astype(BF16), ug, tn,
                             preferred_element_type=F32)
        h0r, h0i = hre_ref[j], him_ref[j]
        lbr, lbi = lr_ref[j], li_ref[j]
        hr = lbr * h0r - lbi * h0i + bu[0:p]
        hi = lbr * h0i + lbi * h0r + bu[p:2 * p]
        ore_ref[j] = hr
        oim_ref[j] = hi
        h = jnp.concatenate([hr, hi], axis=0).astype(BF16)
        yts.append(jnp.dot(cd_ref[j].astype(BF16), h, preferred_element_type=F32))
    y_ref[...] = jnp.concatenate(yts, axis=0).T


def _s5_decode(proj_d, h0_re, h0_im, bd, cd, lr, li):
    g, gb, p = S5_GROUPS, GROUPS_PER_TILE, S5_STATE
    nbatch = proj_d.shape[0]
    blk = lambda r, c: pl.BlockSpec((gb, r, c), lambda i: (i, 0, 0))
    cols = pl.BlockSpec((nbatch, 128), lambda i: (0, i))
    return pl.pallas_call(
        _s5d_body,
        grid=(g // gb,),
        in_specs=[cols, blk(p, nbatch), blk(p, nbatch), blk(S5_GROUP, 2 * p), blk(S5_GROUP, 2 * p),
                  blk(p, nbatch), blk(p, nbatch)],
        out_specs=(cols, blk(p, nbatch), blk(p, nbatch)),
        out_shape=(jax.ShapeDtypeStruct((nbatch, D_S5), F32),
                   jax.ShapeDtypeStruct((g, p, nbatch), F32),
                   jax.ShapeDtypeStruct((g, p, nbatch), F32)),
        compiler_params=pltpu.CompilerParams(dimension_semantics=("parallel",)),
        name="s5_decode",
    )(proj_d, h0_re, h0_im, bd, cd, lr, li)


def _lru_gates(xc, wa_ref, wx_ref, ba, bx, lam):
    xcb = xc.astype(BF16)
    nblk = D_LRU // MXU_WIDTH_V7X
    r_parts, i_parts = [], []
    for k in range(nblk):
        xk = xcb[:, k * MXU_WIDTH_V7X:(k + 1) * MXU_WIDTH_V7X]
        r_parts.append(jnp.dot(xk, wa_ref[k], preferred_element_type=F32))
        i_parts.append(jnp.dot(xk, wx_ref[k], preferred_element_type=F32))
    r = jax.nn.sigmoid(jnp.concatenate(r_parts, axis=1) + ba)
    i = jax.nn.sigmoid(jnp.concatenate(i_parts, axis=1) + bx)
    z = -lam
    softplus = jnp.maximum(z, 0.0) + jnp.log1p(jnp.exp(-jnp.abs(z)))
    log_a = (-LRU_C * softplus) * r
    a = jnp.exp(log_a)
    v = -jnp.tanh(log_a) * (a * a + 1.0)
    mult = jnp.where(v > 0.0, v * lax.rsqrt(v), 0.0)
    return a, mult * (i * xc)


def _lru_tile(xl_ref, gate_ref, cw_ref, cb_ref, wa_ref, wx_ref, ba_ref, bx_ref, lam_ref,
              o_ref, xbuf_ref, carry_ref):
    tt = xl_ref.shape[0]
    x = xl_ref[...]
    xbuf_ref[8:8 + tt, :] = x
    cw = cw_ref[...]
    xc = (cb_ref[...] + xbuf_ref[5:5 + tt, :] * cw[0:1] + xbuf_ref[6:6 + tt, :] * cw[1:2]
          + xbuf_ref[7:7 + tt, :] * cw[2:3] + x * cw[3:4])
    xbuf_ref[0:8, :] = x[tt - 8:tt, :]

    a, b = _lru_gates(xc, wa_ref, wx_ref, ba_ref[...], bx_ref[...], lam_ref[...])

    nblk = tt // 8
    a3 = a.reshape(nblk, 8, D_LRU)
    b3 = b.reshape(nblk, 8, D_LRU)
    row = lax.broadcasted_iota(jnp.int32, (nblk, 8, D_LRU), 1)
    for sh in (1, 2, 4):
        keep = row >= sh
        bs = jnp.where(keep, pltpu.roll(b3, sh, axis=1), 0.0)
        sa = jnp.where(keep, pltpu.roll(a3, sh, axis=1), 1.0)
        b3 = b3 + a3 * bs
        a3 = a3 * sa
    carry = carry_ref[0:1, :]
    gate = gate_ref[...]
    for k in range(nblk):
        h = b3[k] + a3[k] * carry
        carry = h[7:8, :]
        o_ref[k * 8:(k + 1) * 8, :] = h * gate[k * 8:(k + 1) * 8, :]
    carry_ref[...] = jnp.broadcast_to(carry, (8, D_LRU))
    return carry


def _mix_s5_part(ys, u, x, dsk_ref, wg_ref, bg_ref, gs_ref, wo_ref):
    yy = ys + dsk_ref[...] * u
    g = _gelu(yy)
    z = jnp.dot(g.astype(BF16), wg_ref[...], preferred_element_type=F32) + bg_ref[...]
    s5o = g * jax.nn.sigmoid(z)
    n1 = _rms(s5o, gs_ref[...]).astype(BF16)
    return x + jnp.dot(n1, wo_ref[0:D_S5, :], preferred_element_type=F32)


def _mix_lru_part(lru, gl_ref, wo_ref):
    n2 = _rms(lru, gl_ref[...]).astype(BF16)
    return jnp.dot(n2, wo_ref[D_S5:, :], preferred_element_type=F32)


def _lru_mix_body(xl_ref, gate_ref, ys_ref, u_ref, x_ref,
                  cw_ref, cb_ref, wa_ref, wx_ref, ba_ref, bx_ref, lam_ref,
                  dsk_ref, wg_ref, bg_ref, gs_ref, gl_ref, wo_ref,
                  o_ref, hl_ref, xbuf_ref, carry_ref, lru_ref):
    @pl.when(pl.program_id(1) == 0)
    def _():
        xbuf_ref[0:8, :] = jnp.zeros((8, D_LRU), F32)
        carry_ref[...] = jnp.zeros((8, D_LRU), F32)

    o_ref[...] = _mix_s5_part(ys_ref[...], u_ref[...], x_ref[...],
                              dsk_ref, wg_ref, bg_ref, gs_ref, wo_ref)
    hl_ref[0] = _lru_tile(xl_ref, gate_ref, cw_ref, cb_ref, wa_ref, wx_ref, ba_ref, bx_ref, lam_ref,
                          lru_ref, xbuf_ref, carry_ref)
    o_ref[...] += _mix_lru_part(lru_ref[...], gl_ref, wo_ref)


def _lru_mix_prompt(proj_p, ys_p, x1_p, cw, cb, wa, wx, ba, bx, lam, dsk, wg, bg, gs, gl, wo,
                    *, nb, seq, tt=LRU_TT):
    nt = seq // tt
    rows = lambda c, col: pl.BlockSpec((tt, c), lambda b, t: (b * nt + t, col))
    once = lambda shape: pl.BlockSpec(shape, lambda b, t: (0,) * len(shape),
                                      pipeline_mode=pl.Buffered(1))
    return pl.pallas_call(
        _lru_mix_body,
        grid=(nb, nt),
        in_specs=[
            rows(D_LRU, 1), rows(D_LRU, 2),
            rows(D_S5, 0), rows(D_S5, 0), rows(D_MODEL, 0),
            once((CONV_W, D_LRU)), once((1, D_LRU)),
            once((D_LRU // 256, 256, 256)), once((D_LRU // 256, 256, 256)),
            once((1, D_LRU)), once((1, D_LRU)), once((1, D_LRU)),
            once((1, D_S5)), once((D_S5, D_S5)), once((1, D_S5)), once((1, D_S5)), once((1, D_LRU)),
            once((D_MODEL, D_MODEL)),
        ],
        out_specs=(pl.BlockSpec((tt, D_MODEL), lambda b, t: (b * nt + t, 0)),
                   pl.BlockSpec((1, 1, D_LRU), lambda b, t: (b, 0, 0))),
        out_shape=(jax.ShapeDtypeStruct((nb * seq, D_MODEL), F32),
                   jax.ShapeDtypeStruct((nb, 1, D_LRU), F32)),
        scratch_shapes=[pltpu.VMEM((tt + 8, D_LRU), F32), pltpu.VMEM((8, D_LRU), F32),
                        pltpu.VMEM((tt, D_LRU), F32)],
        compiler_params=pltpu.CompilerParams(
            dimension_semantics=("parallel", "arbitrary"), vmem_limit_bytes=VMEM_LIMIT),
        name="lru_mix_prompt",
    )(proj_p, proj_p, ys_p, proj_p, x1_p, cw, cb, wa, wx, ba, bx, lam, dsk, wg, bg, gs, gl, wo)


def _lru_decode_body(xl_ref, gate_ref, conv_ref, h0_ref, cw_ref, cb_ref,
                     wa_ref, wx_ref, ba_ref, bx_ref, lam_ref, o_ref, h_ref, buf_ref):
    x = xl_ref[...]
    cw = cw_ref[...]
    c0, c1, c2 = (conv_ref[k] for k in range(CONV_W - 1))
    xc = cb_ref[...] + c0 * cw[0:1] + c1 * cw[1:2] + c2 * cw[2:3] + x * cw[3:4]
    a, b = _lru_gates(xc, wa_ref, wx_ref, ba_ref[...], bx_ref[...], lam_ref[...])
    h = a * h0_ref[...] + b
    h_ref[...] = h
    o_ref[...] = h * gate_ref[...]
    for k, rows in enumerate((c1, c2, x)):
        buf_ref[k] = rows


def _lru_decode(proj_d, conv0, h0, cw, cb, wa, wx, ba, bx, lam):
    nd = proj_d.shape[0]
    full = lambda r: pl.BlockSpec((r, D_LRU), lambda i: (0, 0))
    conv = pl.BlockSpec((CONV_W - 1, nd, D_LRU), lambda i: (0, 0, 0))
    wspec = pl.BlockSpec((D_LRU // 256, 256, 256), lambda i: (0, 0, 0))
    return pl.pallas_call(
        _lru_decode_body,
        grid=(1,),
        in_specs=[
            pl.BlockSpec((nd, D_LRU), lambda i: (0, 1)),
            pl.BlockSpec((nd, D_LRU), lambda i: (0, 2)),
            conv, full(nd),
            full(CONV_W), full(1), wspec, wspec, full(1), full(1), full(1),
        ],
        out_specs=(full(nd), full(nd), conv),
        out_shape=(jax.ShapeDtypeStruct((nd, D_LRU), F32),
                   jax.ShapeDtypeStruct((nd, D_LRU), F32),
                   jax.ShapeDtypeStruct((CONV_W - 1, nd, D_LRU), F32)),
        name="lru_decode",
    )(proj_d, proj_d, conv0, h0, cw, cb, wa, wx, ba, bx, lam)


def _mix_decode_body(ys_ref, u_ref, lru_ref, x_ref, dsk_ref, wg_ref, bg_ref, gs_ref, gl_ref,
                     wo_ref, o_ref):
    o_ref[...] = (_mix_s5_part(ys_ref[...], u_ref[...], x_ref[...],
                               dsk_ref, wg_ref, bg_ref, gs_ref, wo_ref)
                  + _mix_lru_part(lru_ref[...], gl_ref, wo_ref))


def _mix_decode(ys_d, proj_d, lru_d, x1_d, dsk, wg, bg, gs, gl, wo):
    nd = x1_d.shape[0]
    full = lambda r, c: pl.BlockSpec((r, c), lambda i: (0, 0))
    return pl.pallas_call(
        _mix_decode_body,
        grid=(1,),
        in_specs=[full(nd, D_S5), full(nd, D_S5), full(nd, D_LRU), full(nd, D_MODEL),
                  full(1, D_S5), full(D_S5, D_S5), full(1, D_S5), full(1, D_S5), full(1, D_LRU),
                  full(D_MODEL, D_MODEL)],
        out_specs=full(nd, D_MODEL),
        out_shape=jax.ShapeDtypeStruct((nd, D_MODEL), F32),
        compiler_params=pltpu.CompilerParams(vmem_limit_bytes=VMEM_LIMIT),
        name="mix_decode",
    )(ys_d, proj_d, lru_d, x1_d, dsk, wg, bg, gs, gl, wo)


def _unpair(h, nb):
    tiles = S5_GROUPS // GROUPS_PER_TILE
    h5 = h.reshape(tiles, PAIRS_PER_TILE, nb, 2, S5_STATE)
    return jnp.transpose(h5, (2, 0, 3, 1, 4)).reshape(nb, S5_GROUPS, S5_STATE)


def _block_diag4(w):
    w4 = w.reshape(LRU_HEADS // 4, 4, LRU_HEAD_DIM, LRU_HEAD_DIM)
    eye = jnp.eye(4, dtype=w.dtype)
    return jnp.einsum("kaij,ab->kaibj", w4, eye).reshape(LRU_HEADS // 4, 256, 256)


def kernel(x_prompt, x_sample, state_s5_re, state_s5_im, state_lru_h, state_lru_conv, g_ffn1, w1_a, w3_a, w2_a, g_mix, w_in, lam_re, lam_im, log_dt, b_re, b_im, c_re, c_im, d_skip, w_glu, b_glu, conv_w, conv_b, w_a, b_a, w_x, b_x, lam_l, g_out_s5, g_out_lru, w_out, g_ffn2, w1_b, w3_b, w2_b, g_final):
    nb, seq, _ = x_prompt.shape
    nd = x_sample.shape[0]
    n_p = nb * seq
    row = lambda v: v.reshape(1, -1)
    assert w1_a.shape[0] == 1 and x_sample.shape[1] == 1
    assert nd == 128 and n_p % FFN_TM == 0 and n_p > FFN_TM
    assert seq % LRU_TT == 0 and seq % (8 * CHUNK) == 0

    xp = x_prompt.reshape(n_p, D_MODEL)
    xd = x_sample.reshape(nd, D_MODEL)

    n_i, n_f = n_p // FFN_TM, D_FF // FFN_TF
    x1_p, x1_d, w1_b16, w3_b16, w2_b16 = _ffn(
        xp, xd, row(g_ffn1[0]), w1_a[0].astype(BF16), w3_a[0].astype(BF16), w2_a[0].astype(BF16),
        casts=(_cast_job(w1_b[0], n_i, n_f), _cast_job(w3_b[0], n_i, n_f),
               _row_cast_job(w2_b[0], n_i, n_f)))
    nj = D_IN // PROJ_TN
    proj_p, proj_d, w_out16, w_glu16 = _inproj(
        x1_p, x1_d, row(g_mix[0]), w_in[0].astype(BF16),
        casts=(_cast_job(w_out[0], n_i, nj, bc=PROJ_TN), _cast_job(w_glu[0], n_i, nj, bc=PROJ_TN)))

    lmat, cpre, cpim, bd, cd, ar, ai, lr, li = _s5_prep(
        lam_re[0], lam_im[0], log_dt[0], b_re[0], b_im[0], c_re[0], c_im[0])
    ys_p, hf_re, hf_im = _s5_prompt(proj_p, lmat, cpre, cpim, ar, ai, nb=nb, seq=seq)

    to_gpb = lambda s: jnp.transpose(s, (1, 2, 0))
    ys_d, hd_re, hd_im = _s5_decode(proj_d, to_gpb(state_s5_re[0]), to_gpb(state_s5_im[0]),
                                    bd, cd, lr, li)

    wa_bd = _block_diag4(w_a[0]).astype(BF16)
    wx_bd = _block_diag4(w_x[0]).astype(BF16)
    lru_args = (conv_w[0], row(conv_b[0]), wa_bd, wx_bd, row(b_a[0]), row(b_x[0]), row(lam_l[0]))
    lru_d, hl_d, buf_d = _lru_decode(proj_d, jnp.transpose(state_lru_conv[0], (1, 0, 2)),
                                     state_lru_h[0], *lru_args)

    mix_args = (row(d_skip[0]), w_glu16, row(b_glu[0]), row(g_out_s5[0]), row(g_out_lru[0]), w_out16)
    x2_p, hl_p = _lru_mix_prompt(proj_p, ys_p, x1_p, *lru_args, *mix_args, nb=nb, seq=seq)
    x2_d = _mix_decode(ys_d, proj_d, lru_d, x1_d, *mix_args)
    y_p, y_d = _ffn(x2_p, x2_d, row(g_ffn2[0]), w1_b16, w3_b16, w2_b16, row(g_final))

    tail_p = proj_p.reshape(nb, seq, -1)[:, seq - (CONV_W - 1):, D_S5:D_S5 + D_LRU]
    return (
        y_p.reshape(nb, seq, D_MODEL),
        y_d.reshape(nd, 1, D_MODEL),
        _unpair(hf_re, nb)[None],
        _unpair(hf_im, nb)[None],
        hl_p.reshape(1, nb, D_LRU),
        tail_p[None],
        jnp.transpose(hd_re, (2, 0, 1))[None],
        jnp.transpose(hd_im, (2, 0, 1))[None],
        hl_d[None],
        jnp.transpose(buf_d, (1, 0, 2))[None],
    )
```

```python
import functools

import jax
import jax.numpy as jnp
from jax import lax
from jax.experimental import pallas as pl
from jax.experimental.pallas import tpu as pltpu

F32 = jnp.float32
BF16 = jnp.bfloat16

D_MODEL = 2048
D_S5 = 1024
S5_GROUP = 16
S5_GROUPS = 64
S5_STATE = 64
D_LRU = 1024
LRU_HEADS = 16
LRU_HEAD_DIM = 64
CONV_W = 4
LRU_C = 8.0
D_FF = 5632
D_IN = D_S5 + 2 * D_LRU
EPS = 1e-6

MXU_WIDTH_V7X = 256
CHUNK = MXU_WIDTH_V7X // S5_GROUP
CW = CHUNK * S5_GROUP

FFN_TM = 1024
FFN_TF = 512
PROJ_TN = 1024
LRU_TT = 512

VMEM_CAPACITY_V7X = 64 * 1024 * 1024
VMEM_LIMIT = VMEM_CAPACITY_V7X - 6 * 1024 * 1024
FFN_VMEM_LIMIT = VMEM_CAPACITY_V7X - 2 * 1024 * 1024

NN = (((1,), (0,)), ((), ()))
NT = (((1,), (1,)), ((), ()))


def _rms(x, g):
    return x * lax.rsqrt(jnp.mean(x * x, axis=-1, keepdims=True) + EPS) * g


def _split(x):
    hi = x.astype(BF16)
    lo = (x - hi.astype(F32)).astype(BF16)
    return hi, lo


def _dot3(a, b, dims=NN):
    ah, al = _split(a)
    bh, bl = _split(b)
    d = functools.partial(lax.dot_general, dimension_numbers=dims, preferred_element_type=F32)
    return d(ah, bh) + d(al, bh) + d(ah, bl)


def _gelu(x):
    return jax.nn.gelu(x, approximate=True)


def _ffn_body(*refs, final_norm, convert, has_decode, has_head, n_casts):
    refs = list(refs)
    xn_ref = refs.pop()
    xp_ref = refs.pop(0)
    xd_ref = refs.pop(0) if has_decode else None
    g_ref, w1_ref, w3_ref, w2_ref = (refs.pop(0) for _ in range(4))
    gf_ref = refs.pop(0) if final_norm else None
    head_ref = refs.pop(0) if has_head else None
    cast_in = [refs.pop(0) for _ in range(n_casts)]
    op_ref = refs.pop(0)
    od_ref = refs.pop(0) if has_decode else None
    wcopy = [refs.pop(0) for _ in range(3)] if convert else []
    cast_out = refs
    i = pl.program_id(0)
    f = pl.program_id(1)
    tm = xp_ref.shape[0]

    for src_ref, dst_ref in zip(cast_in, cast_out):
        dst_ref[...] = src_ref[...].astype(BF16)

    if convert:
        w1, w3, w2 = (w_ref[...].astype(BF16) for w_ref in (w1_ref, w3_ref, w2_ref))
        for dst_ref, w in zip(wcopy, (w1, w3, w2)):
            dst_ref[...] = w
        weights = lambda: (w1, w3, w2)
    else:
        weights = lambda: (w1_ref[...], w3_ref[...], w2_ref[...])

    def step(x_ref, o_ref, rows):
        @pl.when(f == 0)
        def _():
            x = x_ref[...]
            xn_ref[rows, :] = _rms(x, g_ref[...]).astype(BF16)
            o_ref[...] = x

        w1, w3, w2 = weights()
        xn = xn_ref[rows, :]
        a = jnp.dot(xn, w1, preferred_element_type=F32)
        b = jnp.dot(xn, w3, preferred_element_type=F32)
        h = (a * jax.nn.sigmoid(a) * b).astype(BF16)
        o_ref[...] += 0.5 * jnp.dot(h, w2, preferred_element_type=F32)

        if final_norm:
            @pl.when(f == pl.num_programs(1) - 1)
            def _():
                o_ref[...] = _rms(o_ref[...], gf_ref[...])

    if has_head:
        piece = head_ref.shape[0]

        @pl.when((i == 0) & (f < tm // piece))
        def _():
            op_ref[pl.ds(pl.multiple_of(f * piece, piece), piece), :] = head_ref[...]

        @pl.when(i > 0)
        def _():
            step(xp_ref, op_ref, slice(0, tm))
    else:
        step(xp_ref, op_ref, slice(0, tm))

    if has_decode:
        @pl.when(i == 0)
        def _():
            step(xd_ref, od_ref, slice(tm, tm + xd_ref.shape[0]))


def _cast_job(w, n_i, n_f, bc=FFN_TF):
    rows, cols = w.shape
    br = rows // n_i
    n_cb = cols // bc
    assert br * n_i == rows and bc * n_cb == cols and n_cb <= n_f and br % 16 == 0
    return w, pl.BlockSpec(
        (br, bc), lambda i, f: (jnp.minimum(i, n_i - 1),
                                jnp.where(i < n_i, jnp.minimum(f, n_cb - 1), n_cb - 1)))


def _row_cast_job(w, n_i, n_f):
    rows, cols = w.shape
    br = rows // (n_i * n_f)
    assert br * n_i * n_f == rows and br % 16 == 0
    return w, pl.BlockSpec((br, cols), lambda i, f: (i * n_f + f, 0))


def _ffn(xp, xd, g, w1, w3, w2, g_final=None, *, head=None, n_tiles=None, casts=(), tm=FFN_TM):
    n_p = xp.shape[0]
    n_tiles = n_p // tm if n_tiles is None else n_tiles
    has_decode, has_head = xd is not None, head is not None
    nd = xd.shape[0] if has_decode else 0
    final_norm = g_final is not None
    convert = w1.dtype == F32
    tf = FFN_TF // 2 if convert else FFN_TF
    n_f = D_FF // tf
    first = 1 if has_head else 0
    hold = lambda i, f: jnp.where(i < first, 0, f)
    pspec = pl.BlockSpec((tm, D_MODEL), lambda i, f: (i, 0))
    dspec = pl.BlockSpec((nd, D_MODEL), lambda i, f: (0, 0))
    vspec = pl.BlockSpec((1, D_MODEL), lambda i, f: (0, 0))
    w13spec = pl.BlockSpec((D_MODEL, tf), lambda i, f: (0, hold(i, f)))
    w2spec = pl.BlockSpec((tf, D_MODEL), lambda i, f: (hold(i, f), 0))
    once = dict(pipeline_mode=pl.Buffered(1))
    sd = jax.ShapeDtypeStruct
    in_specs = [pl.BlockSpec((tm, D_MODEL), lambda i, f: (jnp.maximum(i, first), 0),
                             **(once if n_tiles == 1 else {}))]
    args = [xp]
    out_specs, out_shape = [pspec], [sd((n_tiles * tm, D_MODEL), F32)]
    if has_decode:
        in_specs.append(pl.BlockSpec((nd, D_MODEL), lambda i, f: (0, 0), **once))
        args.append(xd)
        out_specs.append(dspec)
        out_shape.append(sd((nd, D_MODEL), F32))
    in_specs += [vspec, w13spec, w13spec, w2spec]
    args += [g, w1, w3, w2]
    if final_norm:
        in_specs.append(vspec)
        args.append(g_final)
    if has_head:
        piece = tm // 8
        assert head.shape == (tm, D_MODEL) and tm // piece <= n_f
        in_specs.append(pl.BlockSpec(
            (piece, D_MODEL), lambda i, f: (jnp.where(i == 0, jnp.minimum(f, tm // piece - 1),
                                                      tm // piece - 1), 0)))
        args.append(head)
    in_specs += [spec for _, spec in casts]
    args += [w for w, _ in casts]
    if convert:
        out_specs += [w13spec, w13spec, w2spec]
        out_shape += [sd(w.shape, BF16) for w in (w1, w3, w2)]
    out_specs += [spec for _, spec in casts]
    out_shape += [sd(w.shape, BF16) for w, _ in casts]
    return pl.pallas_call(
        functools.partial(_ffn_body, final_norm=final_norm, convert=convert,
                          has_decode=has_decode, has_head=has_head, n_casts=len(casts)),
        grid=(n_tiles, n_f),
        in_specs=in_specs,
        out_specs=tuple(out_specs),
        out_shape=tuple(out_shape),
        scratch_shapes=[pltpu.VMEM((tm + nd, D_MODEL), BF16)],
        compiler_params=pltpu.CompilerParams(
            dimension_semantics=("arbitrary", "arbitrary"), vmem_limit_bytes=FFN_VMEM_LIMIT),
        name=("ffn_final" if final_norm else "ffn") + ("_head" if convert else ""),
    )(*args)


def _inproj_body(*refs, n_tiles, nd, n_casts):
    xp_ref, xd_ref, g_ref, w_ref = refs[:4]
    cast_in = refs[4:4 + n_casts]
    op_ref, od_ref = refs[4 + n_casts:6 + n_casts]
    cast_out = refs[6 + n_casts:-1]
    xn_ref = refs[-1]
    i = pl.program_id(0)
    j = pl.program_id(1)
    tn = op_ref.shape[1]
    nj = w_ref.shape[1] // tn

    for src_ref, dst_ref in zip(cast_in, cast_out):
        dst_ref[...] = src_ref[...].astype(BF16)

    def run(x_ref, o_ref, rows):
        @pl.when(j == 0)
        def _():
            xn_ref[0:rows, :] = _rms(x_ref[...], g_ref[...]).astype(BF16)

        for col in range(nj):
            @pl.when(j == col)
            def _(col=col):
                y = jnp.dot(xn_ref[0:rows, :], w_ref[:, col * tn:(col + 1) * tn],
                            preferred_element_type=F32)
                o_ref[...] = _gelu(y) if col == nj - 1 else y

    @pl.when(i < n_tiles)
    def _():
        run(xp_ref, op_ref, xp_ref.shape[0])

    @pl.when(i == n_tiles)
    def _():
        run(xd_ref, od_ref, nd)


def _inproj(xp, xd, g, w, *, casts=(), tm=FFN_TM, tn=PROJ_TN):
    n_p, nd = xp.shape[0], xd.shape[0]
    n_tiles = n_p // tm
    d_out = w.shape[1]
    nj = d_out // tn
    return pl.pallas_call(
        functools.partial(_inproj_body, n_tiles=n_tiles, nd=nd, n_casts=len(casts)),
        grid=(n_tiles + 1, nj),
        in_specs=[
            pl.BlockSpec((tm, D_MODEL), lambda i, j: (jnp.minimum(i, n_tiles - 1), 0)),
            pl.BlockSpec((nd, D_MODEL), lambda i, j: (0, 0)),
            pl.BlockSpec((1, D_MODEL), lambda i, j: (0, 0)),
            pl.BlockSpec((D_MODEL, d_out), lambda i, j: (0, 0), pipeline_mode=pl.Buffered(1)),
        ] + [spec for _, spec in casts],
        out_specs=(
            pl.BlockSpec((tm, tn), lambda i, j: (jnp.minimum(i, n_tiles - 1),
                                                 jnp.where(i < n_tiles, j, nj - 1))),
            pl.BlockSpec((nd, tn), lambda i, j: (0, jnp.where(i < n_tiles, 0, j))),
        ) + tuple(spec for _, spec in casts),
        out_shape=(jax.ShapeDtypeStruct((n_p, d_out), F32),
                   jax.ShapeDtypeStruct((nd, d_out), F32))
        + tuple(jax.ShapeDtypeStruct(cw.shape, BF16) for cw, _ in casts),
        scratch_shapes=[pltpu.VMEM((tm, D_MODEL), BF16)],
        compiler_params=pltpu.CompilerParams(
            dimension_semantics=("arbitrary", "arbitrary"), vmem_limit_bytes=VMEM_LIMIT),
        name="inproj",
    )(xp, xd, g, w, *[cw for cw, _ in casts])


GROUPS_PER_TILE = 128 // S5_GROUP
PAIRS_PER_TILE = GROUPS_PER_TILE // 2
L_ROWS = CW + 2 * S5_STATE


def _lam_bar(lam_re, lam_im, log_dt):
    dt = jnp.exp(log_dt)
    mag = jnp.exp(lam_re * dt)
    ang = lam_im * dt
    return mag * jnp.cos(ang), mag * jnp.sin(ang)


def _s5_prep_body(lre_ref, lim_ref, ldt_ref, bre_ref, bim_ref, cre_ref, cim_ref,
                  l_ref, cpre_ref, cpim_ref, bd_ref, cd_ref, ar_ref, ai_ref, lr_ref, li_ref,
                  bp_ref, lrow_ref):
    gb, half = GROUPS_PER_TILE, PAIRS_PER_TILE
    p = S5_STATE
    lo, hi = slice(0, p), slice(p, 2 * p)
    lam_re = lre_ref[...]
    lam_im = lim_ref[...]
    lbr, lbi = _lam_bar(lam_re, lam_im, ldt_ref[...])
    lrow_ref[:, :, lo] = lbr
    lrow_ref[:, :, hi] = lbi
    for j in range(gb):
        cols = jnp.broadcast_to(lrow_ref[j], (2 * p, 2 * p)).T
        lr_ref[j] = cols[0:p]
        li_ref[j] = cols[p:2 * p]
    nr = lbr - 1.0
    den = lam_re * lam_re + lam_im * lam_im
    cr = (nr * lam_re + lbi * lam_im) / den
    ci = (lbi * lam_re - nr * lam_im) / den
    b_re = bre_ref[...]
    b_im = bim_ref[...]
    bbr = cr * b_re - ci * b_im
    bbi = cr * b_im + ci * b_re
    bd_ref[:, :, lo] = bbr
    bd_ref[:, :, hi] = bbi
    c_re = cre_ref[...]
    c_im = cim_ref[...]
    cd_ref[:, :, lo] = c_re
    cd_ref[:, :, hi] = -c_im

    zeros = jnp.zeros((half, S5_GROUP, p), F32)
    pr = jnp.ones_like(lbr)
    pi = jnp.zeros_like(lbr)
    for d in range(CHUNK):
        rows = slice(d * S5_GROUP, (d + 1) * S5_GROUP)
        back = slice((CHUNK - 1 - d) * S5_GROUP, (CHUNK - d) * S5_GROUP)
        bp_ref[:, back, lo] = bbr * pr - bbi * pi
        bp_ref[:, back, hi] = bbr * pi + bbi * pr
        pr, pi = pr * lbr - pi * lbi, pr * lbi + pi * lbr
        cp_r = c_re * pr - c_im * pi
        cp_i = -(c_re * pi + c_im * pr)
        cpre_ref[0:half, rows, lo] = cp_r[0:half]
        cpre_ref[0:half, rows, hi] = zeros
        cpre_ref[half:gb, rows, lo] = zeros
        cpre_ref[half:gb, rows, hi] = cp_r[half:gb]
        cpim_ref[0:half, rows, lo] = cp_i[0:half]
        cpim_ref[0:half, rows, hi] = zeros
        cpim_ref[half:gb, rows, lo] = zeros
        cpim_ref[half:gb, rows, hi] = cp_i[half:gb]

    qr, qi = pr, pi
    for r in range(8):
        ar_ref[:, r:r + 1, lo] = qr[0:half]
        ar_ref[:, r:r + 1, hi] = qr[half:gb]
        ai_ref[:, r:r + 1, lo] = qi[0:half]
        ai_ref[:, r:r + 1, hi] = qi[half:gb]
        qr, qi = qr * pr - qi * pi, qr * pi + qi * pr

    lane = lax.broadcasted_iota(jnp.int32, (S5_GROUP, 128), 1)
    for j in range(gb):
        w = _dot3(cd_ref[j], bp_ref[j], NT)
        w0, w1 = w[:, :128], w[:, 128:]
        for t in range(CHUNK):
            rows = slice(t * S5_GROUP, (t + 1) * S5_GROUP)
            shift = (CHUNK - 1 - t) * S5_GROUP
            keep = 128 - shift % 128
            if shift == 0:
                left, right = w0, w1
            elif shift < 128:
                r0 = pltpu.roll(w0, keep, axis=1)
                r1 = pltpu.roll(w1, keep, axis=1)
                left = jnp.where(lane < keep, r0, r1)
                right = jnp.where(lane < keep, r1, 0.0)
            elif shift == 128:
                left, right = w1, jnp.zeros_like(w1)
            else:
                left = jnp.where(lane < keep, pltpu.roll(w1, keep, axis=1), 0.0)
                right = jnp.zeros_like(w1)
            l_ref[j, rows, 0:128] = left.astype(BF16)
            l_ref[j, rows, 128:256] = right.astype(BF16)
        l_ref[j, CW:L_ROWS, :] = bp_ref[j].T.astype(BF16)


def _s5_prep(lam_re, lam_im, log_dt, b_re, b_im, c_re, c_im):
    g, p, gb, half = S5_GROUPS, S5_STATE, GROUPS_PER_TILE, PAIRS_PER_TILE
    lre = lam_re.reshape(g, 1, p)
    lim = lam_im.reshape(g, 1, p)
    ldt = jnp.broadcast_to(log_dt.reshape(g, 1, 1), (g, 1, p))
    bre = jnp.transpose(b_re, (0, 2, 1))
    bim = jnp.transpose(b_im, (0, 2, 1))
    sd = jax.ShapeDtypeStruct
    blk = lambda n, r, c: pl.BlockSpec((n, r, c), lambda i: (i, 0, 0))
    return pl.pallas_call(
        _s5_prep_body,
        grid=(g // gb,),
        in_specs=[blk(gb, 1, p)] * 3 + [blk(gb, S5_GROUP, p)] * 4,
        out_specs=(
            blk(gb, L_ROWS, CW), blk(gb, CW, 2 * p), blk(gb, CW, 2 * p),
            blk(gb, S5_GROUP, 2 * p), blk(gb, S5_GROUP, 2 * p),
            blk(half, 8, 2 * p), blk(half, 8, 2 * p), blk(gb, p, 128), blk(gb, p, 128),
        ),
        out_shape=(
            sd((g, L_ROWS, CW), BF16),
            sd((g, CW, 2 * p), F32),
            sd((g, CW, 2 * p), F32),
            sd((g, S5_GROUP, 2 * p), F32),
            sd((g, S5_GROUP, 2 * p), F32),
            sd((g // 2, 8, 2 * p), F32),
            sd((g // 2, 8, 2 * p), F32),
            sd((g, p, 128), F32),
            sd((g, p, 128), F32),
        ),
        scratch_shapes=[pltpu.VMEM((gb, CW, 2 * p), F32), pltpu.VMEM((gb, 1, 2 * p), F32)],
        compiler_params=pltpu.CompilerParams(dimension_semantics=("parallel",)),
        name="s5_prep",
    )(lre, lim, ldt, bre, bim, c_re, c_im)


def _s5p_body(u_ref, l_ref, cpre_ref, cpim_ref, ar_ref, ai_ref, y_ref, hre_ref, him_ref,
              ut_ref, yt_ref, *, nb, nk):
    gb, half = GROUPS_PER_TILE, PAIRS_PER_TILE
    p = S5_STATE
    nrow = nb * nk
    d = functools.partial(jnp.dot, preferred_element_type=F32)

    for t in range(CHUNK):
        xt = u_ref[pl.ds(t, nrow, stride=CHUNK), :].T
        for j in range(gb):
            ut_ref[j, t * S5_GROUP:(t + 1) * S5_GROUP, :] = xt[j * S5_GROUP:(j + 1) * S5_GROUP, :]

    s_re, s_im = [], []
    for j in range(gb):
        r = d(l_ref[j], ut_ref[j].astype(BF16))
        yt_ref[j] = r[0:CW]
        s_re.append(r[CW:CW + p])
        s_im.append(r[CW + p:L_ROWS])

    nblk = nrow // 8
    row8 = lax.broadcasted_iota(jnp.int32, (nblk, 8, 2 * p), 1)
    rows = lax.broadcasted_iota(jnp.int32, (nrow, 2 * p), 0) & (nk - 1)
    dnt = functools.partial(lax.dot_general, dimension_numbers=NT, preferred_element_type=F32)
    for q in range(half):
        re = jnp.concatenate([s_re[q], s_re[q + half]], axis=0).T.reshape(nblk, 8, 2 * p)
        im = jnp.concatenate([s_im[q], s_im[q + half]], axis=0).T.reshape(nblk, 8, 2 * p)
        for sh in (1, 2, 4):
            keep = row8 >= sh
            rs = jnp.where(keep, pltpu.roll(re, sh, axis=1), 0.0)
            js = jnp.where(keep, pltpu.roll(im, sh, axis=1), 0.0)
            ar = ar_ref[q, sh - 1:sh, :]
            ai = ai_ref[q, sh - 1:sh, :]
            re, im = re + ar * rs - ai * js, im + ar * js + ai * rs
        pw_r, pw_i = ar_ref[q], ai_ref[q]
        out_r, out_i = [], []
        for k in range(nblk):
            hr, hi = re[k], im[k]
            if k % (nk // 8):
                hr, hi = hr + pw_r * cr - pw_i * ci, hi + pw_r * ci + pw_i * cr
            cr, ci = hr[7:8, :], hi[7:8, :]
            out_r.append(hr)
            out_i.append(hi)
            if (k + 1) % (nk // 8) == 0:
                b = k // (nk // 8)
                hre_ref[q, b:b + 1, :] = cr
                him_ref[q, b:b + 1, :] = ci
        re = jnp.concatenate(out_r, axis=0)
        im = jnp.concatenate(out_i, axis=0)
        pre = jnp.where(rows >= 1, pltpu.roll(re, 1, axis=0), 0.0).astype(BF16)
        pim = jnp.where(rows >= 1, pltpu.roll(im, 1, axis=0), 0.0).astype(BF16)
        for j in (q, q + half):
            yt_ref[j] = (yt_ref[j] + dnt(cpre_ref[j].astype(BF16), pre)
                         + dnt(cpim_ref[j].astype(BF16), pim))

    for t in range(CHUNK):
        yt = jnp.concatenate(
            [yt_ref[j, t * S5_GROUP:(t + 1) * S5_GROUP, :] for j in range(gb)], axis=0)
        y_ref[pl.ds(t, nrow, stride=CHUNK), :] = yt.T


def _s5_prompt(proj_p, lmat, cpre, cpim, ar, ai, *, nb, seq):
    g, gb, half = S5_GROUPS, GROUPS_PER_TILE, PAIRS_PER_TILE
    n_p = nb * seq
    nk = seq // CHUNK
    blk = lambda n, r, c: pl.BlockSpec((n, r, c), lambda i: (i, 0, 0))
    return pl.pallas_call(
        functools.partial(_s5p_body, nb=nb, nk=nk),
        grid=(g // gb,),
        in_specs=[pl.BlockSpec((n_p, 128), lambda i: (0, i)),
                  blk(gb, L_ROWS, CW), blk(gb, CW, 2 * S5_STATE), blk(gb, CW, 2 * S5_STATE),
                  blk(half, 8, 2 * S5_STATE), blk(half, 8, 2 * S5_STATE)],
        out_specs=(pl.BlockSpec((n_p, 128), lambda i: (0, i)),
                   blk(half, nb, 2 * S5_STATE), blk(half, nb, 2 * S5_STATE)),
        out_shape=(jax.ShapeDtypeStruct((n_p, D_S5), F32),
                   jax.ShapeDtypeStruct((g // 2, nb, 2 * S5_STATE), F32),
                   jax.ShapeDtypeStruct((g // 2, nb, 2 * S5_STATE), F32)),
        scratch_shapes=[pltpu.VMEM((gb, CW, nb * nk), F32), pltpu.VMEM((gb, CW, nb * nk), F32)],
        compiler_params=pltpu.CompilerParams(
            dimension_semantics=("parallel",), vmem_limit_bytes=VMEM_LIMIT),
        name="s5_prompt",
    )(proj_p, lmat, cpre, cpim, ar, ai)


def _s5d_body(u_ref, hre_ref, him_ref, bd_ref, cd_ref, lr_ref, li_ref, y_ref, ore_ref, oim_ref):
    p = S5_STATE
    gb = GROUPS_PER_TILE
    tn = (((0,), (0,)), ((), ()))
    ut = u_ref[...].T
    yts = []
    for j in range(gb):
        ug = ut[j * S5_GROUP:(j + 1) * S5_GROUP, :].astype(BF16)
        bu = lax.dot_general(bd_ref[j].astype(BF16), ug, tn,
                             preferred_element_type=F32)
        h0r, h0i = hre_ref[j], him_ref[j]
        lbr, lbi = lr_ref[j], li_ref[j]
        hr = lbr * h0r - lbi * h0i + bu[0:p]
        hi = lbr * h0i + lbi * h0r + bu[p:2 * p]
        ore_ref[j] = hr
        oim_ref[j] = hi
        h = jnp.concatenate([hr, hi], axis=0).astype(BF16)
        yts.append(jnp.dot(cd_ref[j].astype(BF16), h, preferred_element_type=F32))
    y_ref[...] = jnp.concatenate(yts, axis=0).T


def _s5_decode(proj_d, h0_re, h0_im, bd, cd, lr, li):
    g, gb, p = S5_GROUPS, GROUPS_PER_TILE, S5_STATE
    nbatch = proj_d.shape[0]
    blk = lambda r, c: pl.BlockSpec((gb, r, c), lambda i: (i, 0, 0))
    cols = pl.BlockSpec((nbatch, 128), lambda i: (0, i))
    return pl.pallas_call(
        _s5d_body,
        grid=(g // gb,),
        in_specs=[cols, blk(p, nbatch), blk(p, nbatch), blk(S5_GROUP, 2 * p), blk(S5_GROUP, 2 * p),
                  blk(p, nbatch), blk(p, nbatch)],
        out_specs=(cols, blk(p, nbatch), blk(p, nbatch)),
        out_shape=(jax.ShapeDtypeStruct((nbatch, D_S5), F32),
                   jax.ShapeDtypeStruct((g, p, nbatch), F32),
                   jax.ShapeDtypeStruct((g, p, nbatch), F32)),
        compiler_params=pltpu.CompilerParams(dimension_semantics=("parallel",)),
        name="s5_decode",
    )(proj_d, h0_re, h0_im, bd, cd, lr, li)


def _lru_gates(xc, wa_ref, wx_ref, ba, bx, lam):
    xcb = xc.astype(BF16)
    nblk = D_LRU // MXU_WIDTH_V7X
    r_parts, i_parts = [], []
    for k in range(nblk):
        xk = xcb[:, k * MXU_WIDTH_V7X:(k + 1) * MXU_WIDTH_V7X]
        r_parts.append(jnp.dot(xk, wa_ref[k], preferred_element_type=F32))
        i_parts.append(jnp.dot(xk, wx_ref[k], preferred_element_type=F32))
    r = jax.nn.sigmoid(jnp.concatenate(r_parts, axis=1) + ba)
    i = jax.nn.sigmoid(jnp.concatenate(i_parts, axis=1) + bx)
    z = -lam
    softplus = jnp.maximum(z, 0.0) + jnp.log1p(jnp.exp(-jnp.abs(z)))
    log_a = (-LRU_C * softplus) * r
    a = jnp.exp(log_a)
    v = -jnp.tanh(log_a) * (a * a + 1.0)
    mult = jnp.where(v > 0.0, v * lax.rsqrt(v), 0.0)
    return a, mult * (i * xc)


def _lru_tile(xl_ref, gate_ref, cw_ref, cb_ref, wa_ref, wx_ref, ba_ref, bx_ref, lam_ref,
              o_ref, xbuf_ref, carry_ref):
    tt = xl_ref.shape[0]
    x = xl_ref[...]
    xbuf_ref[8:8 + tt, :] = x
    cw = cw_ref[...]
    xc = (cb_ref[...] + xbuf_ref[5:5 + tt, :] * cw[0:1] + xbuf_ref[6:6 + tt, :] * cw[1:2]
          + xbuf_ref[7:7 + tt, :] * cw[2:3] + x * cw[3:4])
    xbuf_ref[0:8, :] = x[tt - 8:tt, :]

    a, b = _lru_gates(xc, wa_ref, wx_ref, ba_ref[...], bx_ref[...], lam_ref[...])

    nblk = tt // 8
    a3 = a.reshape(nblk, 8, D_LRU)
    b3 = b.reshape(nblk, 8, D_LRU)
    row = lax.broadcasted_iota(jnp.int32, (nblk, 8, D_LRU), 1)
    for sh in (1, 2, 4):
        keep = row >= sh
        bs = jnp.where(keep, pltpu.roll(b3, sh, axis=1), 0.0)
        sa = jnp.where(keep, pltpu.roll(a3, sh, axis=1), 1.0)
        b3 = b3 + a3 * bs
        a3 = a3 * sa
    carry = carry_ref[0:1, :]
    gate = gate_ref[...]
    for k in range(nblk):
        h = b3[k] + a3[k] * carry
        carry = h[7:8, :]
        o_ref[k * 8:(k + 1) * 8, :] = h * gate[k * 8:(k + 1) * 8, :]
    carry_ref[...] = jnp.broadcast_to(carry, (8, D_LRU))
    return carry


def _mix_s5_part(ys, u, x, dsk_ref, wg_ref, bg_ref, gs_ref, wo_ref):
    yy = ys + dsk_ref[...] * u
    g = _gelu(yy)
    z = jnp.dot(g.astype(BF16), wg_ref[...], preferred_element_type=F32) + bg_ref[...]
    s5o = g * jax.nn.sigmoid(z)
    n1 = _rms(s5o, gs_ref[...]).astype(BF16)
    return x + jnp.dot(n1, wo_ref[0:D_S5, :], preferred_element_type=F32)


def _mix_lru_part(lru, gl_ref, wo_ref):
    n2 = _rms(lru, gl_ref[...]).astype(BF16)
    return jnp.dot(n2, wo_ref[D_S5:, :], preferred_element_type=F32)


def _lru_mix_body(xl_ref, gate_ref, ys_ref, u_ref, x_ref,
                  cw_ref, cb_ref, wa_ref, wx_ref, ba_ref, bx_ref, lam_ref,
                  dsk_ref, wg_ref, bg_ref, gs_ref, gl_ref, wo_ref,
                  o_ref, hl_ref, xbuf_ref, carry_ref, lru_ref):
    @pl.when(pl.program_id(1) == 0)
    def _():
        xbuf_ref[0:8, :] = jnp.zeros((8, D_LRU), F32)
        carry_ref[...] = jnp.zeros((8, D_LRU), F32)

    o_ref[...] = _mix_s5_part(ys_ref[...], u_ref[...], x_ref[...],
                              dsk_ref, wg_ref, bg_ref, gs_ref, wo_ref)
    hl_ref[0] = _lru_tile(xl_ref, gate_ref, cw_ref, cb_ref, wa_ref, wx_ref, ba_ref, bx_ref, lam_ref,
                          lru_ref, xbuf_ref, carry_ref)
    o_ref[...] += _mix_lru_part(lru_ref[...], gl_ref, wo_ref)


def _lru_mix_prompt(proj_p, ys_p, x1_p, cw, cb, wa, wx, ba, bx, lam, dsk, wg, bg, gs, gl, wo,
                    *, nb, seq, tt=LRU_TT):
    nt = seq // tt
    rows = lambda c, col: pl.BlockSpec((tt, c), lambda b, t: (b * nt + t, col))
    once = lambda shape: pl.BlockSpec(shape, lambda b, t: (0,) * len(shape),
                                      pipeline_mode=pl.Buffered(1))
    return pl.pallas_call(
        _lru_mix_body,
        grid=(nb, nt),
        in_specs=[
            rows(D_LRU, 1), rows(D_LRU, 2),
            rows(D_S5, 0), rows(D_S5, 0), rows(D_MODEL, 0),
            once((CONV_W, D_LRU)), once((1, D_LRU)),
            once((D_LRU // 256, 256, 256)), once((D_LRU // 256, 256, 256)),
            once((1, D_LRU)), once((1, D_LRU)), once((1, D_LRU)),
            once((1, D_S5)), once((D_S5, D_S5)), once((1, D_S5)), once((1, D_S5)), once((1, D_LRU)),
            once((D_MODEL, D_MODEL)),
        ],
        out_specs=(pl.BlockSpec((tt, D_MODEL), lambda b, t: (b * nt + t, 0)),
                   pl.BlockSpec((1, 1, D_LRU), lambda b, t: (b, 0, 0))),
        out_shape=(jax.ShapeDtypeStruct((nb * seq, D_MODEL), F32),
                   jax.ShapeDtypeStruct((nb, 1, D_LRU), F32)),
        scratch_shapes=[pltpu.VMEM((tt + 8, D_LRU), F32), pltpu.VMEM((8, D_LRU), F32),
                        pltpu.VMEM((tt, D_LRU), F32)],
        compiler_params=pltpu.CompilerParams(
            dimension_semantics=("parallel", "arbitrary"), vmem_limit_bytes=VMEM_LIMIT),
        name="lru_mix_prompt",
    )(proj_p, proj_p, ys_p, proj_p, x1_p, cw, cb, wa, wx, ba, bx, lam, dsk, wg, bg, gs, gl, wo)


def _lru_decode_body(xl_ref, gate_ref, conv_ref, h0_ref, cw_ref, cb_ref,
                     wa_ref, wx_ref, ba_ref, bx_ref, lam_ref, o_ref, h_ref, buf_ref):
    x = xl_ref[...]
    cw = cw_ref[...]
    c0, c1, c2 = (conv_ref[k] for k in range(CONV_W - 1))
    xc = cb_ref[...] + c0 * cw[0:1] + c1 * cw[1:2] + c2 * cw[2:3] + x * cw[3:4]
    a, b = _lru_gates(xc, wa_ref, wx_ref, ba_ref[...], bx_ref[...], lam_ref[...])
    h = a * h0_ref[...] + b
    h_ref[...] = h
    o_ref[...] = h * gate_ref[...]
    for k, rows in enumerate((c1, c2, x)):
        buf_ref[k] = rows


def _lru_decode(proj_d, conv0, h0, cw, cb, wa, wx, ba, bx, lam):
    nd = proj_d.shape[0]
    full = lambda r: pl.BlockSpec((r, D_LRU), lambda i: (0, 0))
    conv = pl.BlockSpec((CONV_W - 1, nd, D_LRU), lambda i: (0, 0, 0))
    wspec = pl.BlockSpec((D_LRU // 256, 256, 256), lambda i: (0, 0, 0))
    return pl.pallas_call(
        _lru_decode_body,
        grid=(1,),
        in_specs=[
            pl.BlockSpec((nd, D_LRU), lambda i: (0, 1)),
            pl.BlockSpec((nd, D_LRU), lambda i: (0, 2)),
            conv, full(nd),
            full(CONV_W), full(1), wspec, wspec, full(1), full(1), full(1),
        ],
        out_specs=(full(nd), full(nd), conv),
        out_shape=(jax.ShapeDtypeStruct((nd, D_LRU), F32),
                   jax.ShapeDtypeStruct((nd, D_LRU), F32),
                   jax.ShapeDtypeStruct((CONV_W - 1, nd, D_LRU), F32)),
        name="lru_decode",
    )(proj_d, proj_d, conv0, h0, cw, cb, wa, wx, ba, bx, lam)


def _mix_decode_body(ys_ref, u_ref, lru_ref, x_ref, dsk_ref, wg_ref, bg_ref, gs_ref, gl_ref,
                     wo_ref, o_ref):
    o_ref[...] = (_mix_s5_part(ys_ref[...], u_ref[...], x_ref[...],
                               dsk_ref, wg_ref, bg_ref, gs_ref, wo_ref)
                  + _mix_lru_part(lru_ref[...], gl_ref, wo_ref))


def _mix_decode(ys_d, proj_d, lru_d, x1_d, dsk, wg, bg, gs, gl, wo):
    nd = x1_d.shape[0]
    full = lambda r, c: pl.BlockSpec((r, c), lambda i: (0, 0))
    return pl.pallas_call(
        _mix_decode_body,
        grid=(1,),
        in_specs=[full(nd, D_S5), full(nd, D_S5), full(nd, D_LRU), full(nd, D_MODEL),
                  full(1, D_S5), full(D_S5, D_S5), full(1, D_S5), full(1, D_S5), full(1, D_LRU),
                  full(D_MODEL, D_MODEL)],
        out_specs=full(nd, D_MODEL),
        out_shape=jax.ShapeDtypeStruct((nd, D_MODEL), F32),
        compiler_params=pltpu.CompilerParams(vmem_limit_bytes=VMEM_LIMIT),
        name="mix_decode",
    )(ys_d, proj_d, lru_d, x1_d, dsk, wg, bg, gs, gl, wo)


def _unpair(h, nb):
    tiles = S5_GROUPS // GROUPS_PER_TILE
    h5 = h.reshape(tiles, PAIRS_PER_TILE, nb, 2, S5_STATE)
    return jnp.transpose(h5, (2, 0, 3, 1, 4)).reshape(nb, S5_GROUPS, S5_STATE)


def _block_diag4(w):
    w4 = w.reshape(LRU_HEADS // 4, 4, LRU_HEAD_DIM, LRU_HEAD_DIM)
    eye = jnp.eye(4, dtype=w.dtype)
    return jnp.einsum("kaij,ab->kaibj", w4, eye).reshape(LRU_HEADS // 4, 256, 256)


def kernel(x_prompt, x_sample, state_s5_re, state_s5_im, state_lru_h, state_lru_conv, g_ffn1, w1_a, w3_a, w2_a, g_mix, w_in, lam_re, lam_im, log_dt, b_re, b_im, c_re, c_im, d_skip, w_glu, b_glu, conv_w, conv_b, w_a, b_a, w_x, b_x, lam_l, g_out_s5, g_out_lru, w_out, g_ffn2, w1_b, w3_b, w2_b, g_final):
    nb, seq, _ = x_prompt.shape
    nd = x_sample.shape[0]
    n_p = nb * seq
    row = lambda v: v.reshape(1, -1)
    assert w1_a.shape[0] == 1 and x_sample.shape[1] == 1
    assert nd == 128 and n_p % FFN_TM == 0 and n_p > FFN_TM
    assert seq % LRU_TT == 0 and seq % (8 * CHUNK) == 0

    xp = x_prompt.reshape(n_p, D_MODEL)
    xd = x_sample.reshape(nd, D_MODEL)

    n_i, n_f = n_p // FFN_TM, D_FF // FFN_TF
    x1_head, x1_d, w1_a16, w3_a16, w2_a16 = _ffn(
        xp, xd, row(g_ffn1[0]), w1_a[0], w3_a[0], w2_a[0], n_tiles=1)
    x1_p, w1_b16, w3_b16, w2_b16 = _ffn(
        xp, None, row(g_ffn1[0]), w1_a16, w3_a16, w2_a16, head=x1_head,
        casts=(_cast_job(w1_b[0], n_i, n_f), _cast_job(w3_b[0], n_i, n_f),
               _row_cast_job(w2_b[0], n_i, n_f)))
    nj = D_IN // PROJ_TN
    proj_p, proj_d, w_out16, w_glu16 = _inproj(
        x1_p, x1_d, row(g_mix[0]), w_in[0].astype(BF16),
        casts=(_cast_job(w_out[0], n_i, nj, bc=PROJ_TN), _cast_job(w_glu[0], n_i, nj, bc=PROJ_TN)))

    lmat, cpre, cpim, bd, cd, ar, ai, lr, li = _s5_prep(
        lam_re[0], lam_im[0], log_dt[0], b_re[0], b_im[0], c_re[0], c_im[0])
    ys_p, hf_re, hf_im = _s5_prompt(proj_p, lmat, cpre, cpim, ar, ai, nb=nb, seq=seq)

    to_gpb = lambda s: jnp.transpose(s, (1, 2, 0))
    ys_d, hd_re, hd_im = _s5_decode(proj_d, to_gpb(state_s5_re[0]), to_gpb(state_s5_im[0]),
                                    bd, cd, lr, li)

    wa_bd = _block_diag4(w_a[0]).astype(BF16)
    wx_bd = _block_diag4(w_x[0]).astype(BF16)
    lru_args = (conv_w[0], row(conv_b[0]), wa_bd, wx_bd, row(b_a[0]), row(b_x[0]), row(lam_l[0]))
    lru_d, hl_d, buf_d = _lru_decode(proj_d, jnp.transpose(state_lru_conv[0], (1, 0, 2)),
                                     state_lru_h[0], *lru_args)

    mix_args = (row(d_skip[0]), w_glu16, row(b_glu[0]), row(g_out_s5[0]), row(g_out_lru[0]), w_out16)
    x2_p, hl_p = _lru_mix_prompt(proj_p, ys_p, x1_p, *lru_args, *mix_args, nb=nb, seq=seq)
    x2_d = _mix_decode(ys_d, proj_d, lru_d, x1_d, *mix_args)
    y_p, y_d = _ffn(x2_p, x2_d, row(g_ffn2[0]), w1_b16, w3_b16, w2_b16, row(g_final))

    tail_p = proj_p.reshape(nb, seq, -1)[:, seq - (CONV_W - 1):, D_S5:D_S5 + D_LRU]
    return (
        y_p.reshape(nb, seq, D_MODEL),
        y_d.reshape(nd, 1, D_MODEL),
        _unpair(hf_re, nb)[None],
        _unpair(hf_im, nb)[None],
        hl_p.reshape(1, nb, D_LRU),
        tail_p[None],
        jnp.transpose(hd_re, (2, 0, 1))[None],
        jnp.transpose(hd_im, (2, 0, 1))[None],
        hl_d[None],
        jnp.transpose(buf_d, (1, 0, 2))[None],
    )
```

```python
import functools

import jax
import jax.numpy as jnp
from jax import lax
from jax.experimental import pallas as pl
from jax.experimental.pallas import tpu as pltpu

F32 = jnp.float32
BF16 = jnp.bfloat16

D_MODEL = 2048
D_S5 = 1024
S5_GROUP = 16
S5_GROUPS = 64
S5_STATE = 64
D_LRU = 1024
LRU_HEADS = 16
LRU_HEAD_DIM = 64
CONV_W = 4
LRU_C = 8.0
D_FF = 5632
D_IN = D_S5 + 2 * D_LRU
EPS = 1e-6

MXU_WIDTH_V7X = 256
CHUNK = MXU_WIDTH_V7X // S5_GROUP
CW = CHUNK * S5_GROUP

FFN_TM = 1024
FFN_TF = 512
PROJ_TN = 1024
LRU_TT = 512

VMEM_CAPACITY_V7X = 64 * 1024 * 1024
VMEM_LIMIT = VMEM_CAPACITY_V7X - 6 * 1024 * 1024
FFN_VMEM_LIMIT = VMEM_CAPACITY_V7X - 2 * 1024 * 1024

NN = (((1,), (0,)), ((), ()))
NT = (((1,), (1,)), ((), ()))


def _rms(x, g):
    return x * lax.rsqrt(jnp.mean(x * x, axis=-1, keepdims=True) + EPS) * g


def _split(x):
    hi = x.astype(BF16)
    lo = (x - hi.astype(F32)).astype(BF16)
    return hi, lo


def _dot3(a, b, dims=NN):
    ah, al = _split(a)
    bh, bl = _split(b)
    d = functools.partial(lax.dot_general, dimension_numbers=dims, preferred_element_type=F32)
    return d(ah, bh) + d(al, bh) + d(ah, bl)


def _gelu(x):
    return jax.nn.gelu(x, approximate=True)


def _ffn_body(*refs, final_norm, convert, has_decode, joint, has_head, n_casts):
    refs = list(refs)
    xn_ref = refs.pop()
    xp_ref = refs.pop(0)
    xd_ref = refs.pop(0) if has_decode else None
    g_ref, w1_ref, w3_ref, w2_ref = (refs.pop(0) for _ in range(4))
    gf_ref = refs.pop(0) if final_norm else None
    head_ref = refs.pop(0) if has_head else None
    cast_in = [refs.pop(0) for _ in range(n_casts)]
    op_ref = refs.pop(0)
    od_ref = refs.pop(0) if has_decode else None
    wcopy = [refs.pop(0) for _ in range(3)] if convert else []
    cast_out = refs
    i = pl.program_id(0)
    f = pl.program_id(1)
    tm = xp_ref.shape[0]

    for src_ref, dst_ref in zip(cast_in, cast_out):
        dst_ref[...] = src_ref[...].astype(BF16)

    if convert:
        w1, w3, w2 = (w_ref[...].astype(BF16) for w_ref in (w1_ref, w3_ref, w2_ref))
        for dst_ref, w in zip(wcopy, (w1, w3, w2)):
            dst_ref[...] = w
        weights = lambda: (w1, w3, w2)
    else:
        weights = lambda: (w1_ref[...], w3_ref[...], w2_ref[...])

    def step(parts):
        @pl.when(f == 0)
        def _():
            for x_ref, o_ref, rows in parts:
                x = x_ref[...]
                xn_ref[rows, :] = _rms(x, g_ref[...]).astype(BF16)
                o_ref[...] = x

        w1, w3, w2 = weights()
        lo = parts[0][2].start
        xn = xn_ref[lo:parts[-1][2].stop, :]
        for c in range(w1.shape[1] // MXU_WIDTH_V7X):
            cols = slice(c * MXU_WIDTH_V7X, (c + 1) * MXU_WIDTH_V7X)
            a = jnp.dot(xn, w1[:, cols], preferred_element_type=F32)
            b = jnp.dot(xn, w3[:, cols], preferred_element_type=F32)
            h = (a * jax.nn.sigmoid(a) * b).astype(BF16)
            upd = 0.5 * jnp.dot(h, w2[cols, :], preferred_element_type=F32)
            for _, o_ref, rows in parts:
                o_ref[...] += upd[rows.start - lo:rows.stop - lo]

        if final_norm:
            @pl.when(f == pl.num_programs(1) - 1)
            def _():
                for _, o_ref, _ in parts:
                    o_ref[...] = _rms(o_ref[...], gf_ref[...])

    prompt = (xp_ref, op_ref, slice(0, tm))
    decode = (xd_ref, od_ref, slice(tm, tm + xd_ref.shape[0])) if has_decode else None
    if has_head:
        piece = head_ref.shape[0]

        @pl.when((i == 0) & (f < tm // piece))
        def _():
            op_ref[pl.ds(pl.multiple_of(f * piece, piece), piece), :] = head_ref[...]

        @pl.when(i > 0)
        def _():
            step([prompt])
    else:
        step([prompt, decode] if joint else [prompt])

    if has_decode and not joint:
        @pl.when(i == 0)
        def _():
            step([decode])


def _cast_job(w, n_i, n_f, bc=FFN_TF):
    rows, cols = w.shape
    br = rows // n_i
    n_cb = cols // bc
    assert br * n_i == rows and bc * n_cb == cols and n_cb <= n_f and br % 16 == 0
    return w, pl.BlockSpec(
        (br, bc), lambda i, f: (jnp.minimum(i, n_i - 1),
                                jnp.where(i < n_i, jnp.minimum(f, n_cb - 1), n_cb - 1)))


def _row_cast_job(w, n_i, n_f):
    rows, cols = w.shape
    br = rows // (n_i * n_f)
    assert br * n_i * n_f == rows and br % 16 == 0
    return w, pl.BlockSpec((br, cols), lambda i, f: (i * n_f + f, 0))


def _walk_cast_job(w, n_i, n_f):
    rows, cols = w.shape
    n_blocks = max(n for n in range(1, n_i * n_f + 1) if rows % (16 * n) == 0)
    return w, pl.BlockSpec((rows // n_blocks, cols),
                           lambda i, f: (jnp.minimum(i * n_f + f, n_blocks - 1), 0))


def _ffn(xp, xd, g, w1, w3, w2, g_final=None, *, head=None, n_tiles=None, casts=(), tm=FFN_TM):
    n_p = xp.shape[0]
    n_tiles = n_p // tm if n_tiles is None else n_tiles
    has_decode, has_head = xd is not None, head is not None
    nd = xd.shape[0] if has_decode else 0
    final_norm = g_final is not None
    convert = w1.dtype == F32
    tf = FFN_TF // 2 if convert else FFN_TF
    n_f = D_FF // tf
    first = 1 if has_head else 0
    hold = lambda i, f: jnp.where(i < first, 0, f)
    pspec = pl.BlockSpec((tm, D_MODEL), lambda i, f: (i, 0))
    dspec = pl.BlockSpec((nd, D_MODEL), lambda i, f: (0, 0))
    vspec = pl.BlockSpec((1, D_MODEL), lambda i, f: (0, 0))
    w13spec = pl.BlockSpec((D_MODEL, tf), lambda i, f: (0, hold(i, f)))
    w2spec = pl.BlockSpec((tf, D_MODEL), lambda i, f: (hold(i, f), 0))
    once = dict(pipeline_mode=pl.Buffered(1))
    sd = jax.ShapeDtypeStruct
    in_specs = [pl.BlockSpec((tm, D_MODEL), lambda i, f: (jnp.maximum(i, first), 0),
                             **(once if n_tiles == 1 else {}))]
    args = [xp]
    out_specs, out_shape = [pspec], [sd((n_tiles * tm, D_MODEL), F32)]
    if has_decode:
        in_specs.append(pl.BlockSpec((nd, D_MODEL), lambda i, f: (0, 0), **once))
        args.append(xd)
        out_specs.append(dspec)
        out_shape.append(sd((nd, D_MODEL), F32))
    in_specs += [vspec, w13spec, w13spec, w2spec]
    args += [g, w1, w3, w2]
    if final_norm:
        in_specs.append(vspec)
        args.append(g_final)
    if has_head:
        piece = tm // 8
        assert head.shape == (tm, D_MODEL) and tm // piece <= n_f
        in_specs.append(pl.BlockSpec(
            (piece, D_MODEL), lambda i, f: (jnp.where(i == 0, jnp.minimum(f, tm // piece - 1),
                                                      tm // piece - 1), 0)))
        args.append(head)
    in_specs += [spec for _, spec in casts]
    args += [w for w, _ in casts]
    if convert:
        out_specs += [w13spec, w13spec, w2spec]
        out_shape += [sd(w.shape, BF16) for w in (w1, w3, w2)]
    out_specs += [spec for _, spec in casts]
    out_shape += [sd(w.shape, BF16) for w, _ in casts]
    return pl.pallas_call(
        functools.partial(_ffn_body, final_norm=final_norm, convert=convert,
                          has_decode=has_decode, joint=has_decode and n_tiles == 1,
                          has_head=has_head, n_casts=len(casts)),
        grid=(n_tiles, n_f),
        in_specs=in_specs,
        out_specs=tuple(out_specs),
        out_shape=tuple(out_shape),
        scratch_shapes=[pltpu.VMEM((tm + nd, D_MODEL), BF16)],
        compiler_params=pltpu.CompilerParams(
            dimension_semantics=("arbitrary", "arbitrary"), vmem_limit_bytes=FFN_VMEM_LIMIT),
        name=("ffn_final" if final_norm else "ffn") + ("_head" if convert else ""),
    )(*args)


def _inproj_body(*refs, n_tiles, nd, n_casts):
    xp_ref, xd_ref, g_ref, w_ref = refs[:4]
    cast_in = refs[4:4 + n_casts]
    op_ref, od_ref = refs[4 + n_casts:6 + n_casts]
    cast_out = refs[6 + n_casts:-1]
    xn_ref = refs[-1]
    i = pl.program_id(0)
    j = pl.program_id(1)
    tn = op_ref.shape[1]
    nj = w_ref.shape[1] // tn

    for src_ref, dst_ref in zip(cast_in, cast_out):
        dst_ref[...] = src_ref[...].astype(BF16)

    def run(x_ref, o_ref, rows):
        @pl.when(j == 0)
        def _():
            xn_ref[0:rows, :] = _rms(x_ref[...], g_ref[...]).astype(BF16)

        @pl.when(j < nj - 1)
        def _():
            w_cols = pl.ds(pl.multiple_of(j * tn, tn), tn)
            o_ref[...] = jnp.dot(xn_ref[0:rows, :], w_ref[:, w_cols], preferred_element_type=F32)

        @pl.when(j == nj - 1)
        def _():
            o_ref[...] = _gelu(jnp.dot(xn_ref[0:rows, :], w_ref[:, (nj - 1) * tn:],
                                       preferred_element_type=F32))

    @pl.when(i < n_tiles)
    def _():
        run(xp_ref, op_ref, xp_ref.shape[0])

    @pl.when(i == n_tiles)
    def _():
        run(xd_ref, od_ref, nd)


def _inproj(xp, xd, g, w, *, casts=(), tm=FFN_TM, tn=PROJ_TN):
    n_p, nd = xp.shape[0], xd.shape[0]
    n_tiles = n_p // tm
    d_out = w.shape[1]
    nj = d_out // tn
    return pl.pallas_call(
        functools.partial(_inproj_body, n_tiles=n_tiles, nd=nd, n_casts=len(casts)),
        grid=(n_tiles + 1, nj),
        in_specs=[
            pl.BlockSpec((tm, D_MODEL), lambda i, j: (jnp.minimum(i, n_tiles - 1), 0)),
            pl.BlockSpec((nd, D_MODEL), lambda i, j: (0, 0)),
            pl.BlockSpec((1, D_MODEL), lambda i, j: (0, 0)),
            pl.BlockSpec((D_MODEL, d_out), lambda i, j: (0, 0), pipeline_mode=pl.Buffered(1)),
        ] + [spec for _, spec in casts],
        out_specs=(
            pl.BlockSpec((tm, tn), lambda i, j: (jnp.minimum(i, n_tiles - 1),
                                                 jnp.where(i < n_tiles, j, nj - 1))),
            pl.BlockSpec((nd, tn), lambda i, j: (0, jnp.where(i < n_tiles, 0, j))),
        ) + tuple(spec for _, spec in casts),
        out_shape=(jax.ShapeDtypeStruct((n_p, d_out), F32),
                   jax.ShapeDtypeStruct((nd, d_out), F32))
        + tuple(jax.ShapeDtypeStruct(cw.shape, BF16) for cw, _ in casts),
        scratch_shapes=[pltpu.VMEM((tm, D_MODEL), BF16)],
        compiler_params=pltpu.CompilerParams(
            dimension_semantics=("arbitrary", "arbitrary"), vmem_limit_bytes=VMEM_LIMIT),
        name="inproj",
    )(xp, xd, g, w, *[cw for cw, _ in casts])


GROUPS_PER_TILE = 128 // S5_GROUP
PAIRS_PER_TILE = GROUPS_PER_TILE // 2
L_ROWS = CW + 2 * S5_STATE


def _lam_bar(lam_re, lam_im, log_dt):
    dt = jnp.exp(log_dt)
    mag = jnp.exp(lam_re * dt)
    ang = lam_im * dt
    return mag * jnp.cos(ang), mag * jnp.sin(ang)


def _s5_prep_body(lre_ref, lim_ref, ldt_ref, bre_ref, bim_ref, cre_ref, cim_ref,
                  l_ref, cpre_ref, cpim_ref, bd_ref, cd_ref, ar_ref, ai_ref, lr_ref, li_ref,
                  bp_ref, lrow_ref):
    gb, half = GROUPS_PER_TILE, PAIRS_PER_TILE
    p = S5_STATE
    lo, hi = slice(0, p), slice(p, 2 * p)
    lam_re = lre_ref[...]
    lam_im = lim_ref[...]
    lbr, lbi = _lam_bar(lam_re, lam_im, ldt_ref[...])
    lrow_ref[:, :, lo] = lbr
    lrow_ref[:, :, hi] = lbi
    for j in range(gb):
        cols = jnp.broadcast_to(lrow_ref[j], (2 * p, 2 * p)).T
        lr_ref[j] = cols[0:p]
        li_ref[j] = cols[p:2 * p]
    nr = lbr - 1.0
    den = lam_re * lam_re + lam_im * lam_im
    cr = (nr * lam_re + lbi * lam_im) / den
    ci = (lbi * lam_re - nr * lam_im) / den
    b_re = bre_ref[...]
    b_im = bim_ref[...]
    bbr = cr * b_re - ci * b_im
    bbi = cr * b_im + ci * b_re
    bd_ref[:, :, lo] = bbr
    bd_ref[:, :, hi] = bbi
    c_re = cre_ref[...]
    c_im = cim_ref[...]
    cd_ref[:, :, lo] = c_re
    cd_ref[:, :, hi] = -c_im

    zeros = jnp.zeros((half, S5_GROUP, p), F32)
    pr = jnp.ones_like(lbr)
    pi = jnp.zeros_like(lbr)
    for d in range(CHUNK):
        rows = slice(d * S5_GROUP, (d + 1) * S5_GROUP)
        back = slice((CHUNK - 1 - d) * S5_GROUP, (CHUNK - d) * S5_GROUP)
        bp_ref[:, back, lo] = bbr * pr - bbi * pi
        bp_ref[:, back, hi] = bbr * pi + bbi * pr
        pr, pi = pr * lbr - pi * lbi, pr * lbi + pi * lbr
        cp_r = c_re * pr - c_im * pi
        cp_i = -(c_re * pi + c_im * pr)
        cpre_ref[0:half, rows, lo] = cp_r[0:half]
        cpre_ref[0:half, rows, hi] = zeros
        cpre_ref[half:gb, rows, lo] = zeros
        cpre_ref[half:gb, rows, hi] = cp_r[half:gb]
        cpim_ref[0:half, rows, lo] = cp_i[0:half]
        cpim_ref[0:half, rows, hi] = zeros
        cpim_ref[half:gb, rows, lo] = zeros
        cpim_ref[half:gb, rows, hi] = cp_i[half:gb]

    qr, qi = pr, pi
    for r in range(8):
        ar_ref[:, r:r + 1, lo] = qr[0:half]
        ar_ref[:, r:r + 1, hi] = qr[half:gb]
        ai_ref[:, r:r + 1, lo] = qi[0:half]
        ai_ref[:, r:r + 1, hi] = qi[half:gb]
        qr, qi = qr * pr - qi * pi, qr * pi + qi * pr

    lane = lax.broadcasted_iota(jnp.int32, (S5_GROUP, 128), 1)
    for j in range(gb):
        w = _dot3(cd_ref[j], bp_ref[j], NT)
        w0, w1 = w[:, :128], w[:, 128:]
        for t in range(CHUNK):
            rows = slice(t * S5_GROUP, (t + 1) * S5_GROUP)
            shift = (CHUNK - 1 - t) * S5_GROUP
            keep = 128 - shift % 128
            if shift == 0:
                left, right = w0, w1
            elif shift < 128:
                r0 = pltpu.roll(w0, keep, axis=1)
                r1 = pltpu.roll(w1, keep, axis=1)
                left = jnp.where(lane < keep, r0, r1)
                right = jnp.where(lane < keep, r1, 0.0)
            elif shift == 128:
                left, right = w1, jnp.zeros_like(w1)
            else:
                left = jnp.where(lane < keep, pltpu.roll(w1, keep, axis=1), 0.0)
                right = jnp.zeros_like(w1)
            l_ref[j, rows, 0:128] = left.astype(BF16)
            l_ref[j, rows, 128:256] = right.astype(BF16)
        l_ref[j, CW:L_ROWS, :] = bp_ref[j].T.astype(BF16)


def _s5_prep(lam_re, lam_im, log_dt, b_re, b_im, c_re, c_im):
    g, p, gb, half = S5_GROUPS, S5_STATE, GROUPS_PER_TILE, PAIRS_PER_TILE
    lre = lam_re.reshape(g, 1, p)
    lim = lam_im.reshape(g, 1, p)
    ldt = jnp.broadcast_to(log_dt.reshape(g, 1, 1), (g, 1, p))
    bre = jnp.transpose(b_re, (0, 2, 1))
    bim = jnp.transpose(b_im, (0, 2, 1))
    sd = jax.ShapeDtypeStruct
    blk = lambda n, r, c: pl.BlockSpec((n, r, c), lambda i: (i, 0, 0))
    return pl.pallas_call(
        _s5_prep_body,
        grid=(g // gb,),
        in_specs=[blk(gb, 1, p)] * 3 + [blk(gb, S5_GROUP, p)] * 4,
        out_specs=(
            blk(gb, L_ROWS, CW), blk(gb, CW, 2 * p), blk(gb, CW, 2 * p),
            blk(gb, S5_GROUP, 2 * p), blk(gb, S5_GROUP, 2 * p),
            blk(half, 8, 2 * p), blk(half, 8, 2 * p), blk(gb, p, 128), blk(gb, p, 128),
        ),
        out_shape=(
            sd((g, L_ROWS, CW), BF16),
            sd((g, CW, 2 * p), F32),
            sd((g, CW, 2 * p), F32),
            sd((g, S5_GROUP, 2 * p), F32),
            sd((g, S5_GROUP, 2 * p), F32),
            sd((g // 2, 8, 2 * p), F32),
            sd((g // 2, 8, 2 * p), F32),
            sd((g, p, 128), F32),
            sd((g, p, 128), F32),
        ),
        scratch_shapes=[pltpu.VMEM((gb, CW, 2 * p), F32), pltpu.VMEM((gb, 1, 2 * p), F32)],
        compiler_params=pltpu.CompilerParams(dimension_semantics=("parallel",)),
        name="s5_prep",
    )(lre, lim, ldt, bre, bim, c_re, c_im)


def _s5p_body(u_ref, l_ref, cpre_ref, cpim_ref, ar_ref, ai_ref, y_ref, hre_ref, him_ref,
              ut_ref, yt_ref, *, nb, nk):
    gb, half = GROUPS_PER_TILE, PAIRS_PER_TILE
    p = S5_STATE
    nrow = nb * nk
    d = functools.partial(jnp.dot, preferred_element_type=F32)

    for t in range(CHUNK):
        xt = u_ref[pl.ds(t, nrow, stride=CHUNK), :].T
        for j in range(gb):
            ut_ref[j, t * S5_GROUP:(t + 1) * S5_GROUP, :] = xt[j * S5_GROUP:(j + 1) * S5_GROUP, :]

    s_re, s_im = [], []
    for j in range(gb):
        r = d(l_ref[j], ut_ref[j].astype(BF16))
        yt_ref[j] = r[0:CW]
        s_re.append(r[CW:CW + p])
        s_im.append(r[CW + p:L_ROWS])

    nblk = nrow // 8
    row8 = lax.broadcasted_iota(jnp.int32, (nblk, 8, 2 * p), 1)
    rows = lax.broadcasted_iota(jnp.int32, (nrow, 2 * p), 0) & (nk - 1)
    dnt = functools.partial(lax.dot_general, dimension_numbers=NT, preferred_element_type=F32)
    for q in range(half):
        re = jnp.concatenate([s_re[q], s_re[q + half]], axis=0).T.reshape(nblk, 8, 2 * p)
        im = jnp.concatenate([s_im[q], s_im[q + half]], axis=0).T.reshape(nblk, 8, 2 * p)
        for sh in (1, 2, 4):
            keep = row8 >= sh
            rs = jnp.where(keep, pltpu.roll(re, sh, axis=1), 0.0)
            js = jnp.where(keep, pltpu.roll(im, sh, axis=1), 0.0)
            ar = ar_ref[q, sh - 1:sh, :]
            ai = ai_ref[q, sh - 1:sh, :]
            re, im = re + ar * rs - ai * js, im + ar * js + ai * rs
        pw_r, pw_i = ar_ref[q], ai_ref[q]
        out_r, out_i = [], []
        for k in range(nblk):
            hr, hi = re[k], im[k]
            if k % (nk // 8):
                hr, hi = hr + pw_r * cr - pw_i * ci, hi + pw_r * ci + pw_i * cr
            cr, ci = hr[7:8, :], hi[7:8, :]
            out_r.append(hr)
            out_i.append(hi)
            if (k + 1) % (nk // 8) == 0:
                b = k // (nk // 8)
                hre_ref[q, b:b + 1, :] = cr
                him_ref[q, b:b + 1, :] = ci
        re = jnp.concatenate(out_r, axis=0)
        im = jnp.concatenate(out_i, axis=0)
        pre = jnp.where(rows >= 1, pltpu.roll(re, 1, axis=0), 0.0).astype(BF16)
        pim = jnp.where(rows >= 1, pltpu.roll(im, 1, axis=0), 0.0).astype(BF16)
        for j in (q, q + half):
            yt_ref[j] = (yt_ref[j] + dnt(cpre_ref[j].astype(BF16), pre)
                         + dnt(cpim_ref[j].astype(BF16), pim))

    for t in range(CHUNK):
        yt = jnp.concatenate(
            [yt_ref[j, t * S5_GROUP:(t + 1) * S5_GROUP, :] for j in range(gb)], axis=0)
        y_ref[pl.ds(t, nrow, stride=CHUNK), :] = yt.T


def _s5_prompt(proj_p, lmat, cpre, cpim, ar, ai, *, nb, seq):
    g, gb, half = S5_GROUPS, GROUPS_PER_TILE, PAIRS_PER_TILE
    n_p = nb * seq
    nk = seq // CHUNK
    blk = lambda n, r, c: pl.BlockSpec((n, r, c), lambda i: (i, 0, 0))
    return pl.pallas_call(
        functools.partial(_s5p_body, nb=nb, nk=nk),
        grid=(g // gb,),
        in_specs=[pl.BlockSpec((n_p, 128), lambda i: (0, i)),
                  blk(gb, L_ROWS, CW), blk(gb, CW, 2 * S5_STATE), blk(gb, CW, 2 * S5_STATE),
                  blk(half, 8, 2 * S5_STATE), blk(half, 8, 2 * S5_STATE)],
        out_specs=(pl.BlockSpec((n_p, 128), lambda i: (0, i)),
                   blk(half, nb, 2 * S5_STATE), blk(half, nb, 2 * S5_STATE)),
        out_shape=(jax.ShapeDtypeStruct((n_p, D_S5), F32),
                   jax.ShapeDtypeStruct((g // 2, nb, 2 * S5_STATE), F32),
                   jax.ShapeDtypeStruct((g // 2, nb, 2 * S5_STATE), F32)),
        scratch_shapes=[pltpu.VMEM((gb, CW, nb * nk), F32), pltpu.VMEM((gb, CW, nb * nk), F32)],
        compiler_params=pltpu.CompilerParams(
            dimension_semantics=("parallel",), vmem_limit_bytes=VMEM_LIMIT),
        name="s5_prompt",
    )(proj_p, lmat, cpre, cpim, ar, ai)


def _s5d_body(u_ref, hre_ref, him_ref, bd_ref, cd_ref, lr_ref, li_ref, y_ref, ore_ref, oim_ref):
    p = S5_STATE
    gb = GROUPS_PER_TILE
    tn = (((0,), (0,)), ((), ()))
    ut = u_ref[...].T
    yts = []
    for j in range(gb):
        ug = ut[j * S5_GROUP:(j + 1) * S5_GROUP, :].astype(BF16)
        bu = lax.dot_general(bd_ref[j].astype(BF16), ug, tn,
                             preferred_element_type=F32)
        h0r, h0i = hre_ref[j], him_ref[j]
        lbr, lbi = lr_ref[j], li_ref[j]
        hr = lbr * h0r - lbi * h0i + bu[0:p]
        hi = lbr * h0i + lbi * h0r + bu[p:2 * p]
        ore_ref[j] = hr
        oim_ref[j] = hi
        h = jnp.concatenate([hr, hi], axis=0).astype(BF16)
        yts.append(jnp.dot(cd_ref[j].astype(BF16), h, preferred_element_type=F32))
    y_ref[...] = jnp.concatenate(yts, axis=0).T


def _s5_decode(proj_d, h0_re, h0_im, bd, cd, lr, li):
    g, gb, p = S5_GROUPS, GROUPS_PER_TILE, S5_STATE
    nbatch = proj_d.shape[0]
    blk = lambda r, c: pl.BlockSpec((gb, r, c), lambda i: (i, 0, 0))
    cols = pl.BlockSpec((nbatch, 128), lambda i: (0, i))
    return pl.pallas_call(
        _s5d_body,
        grid=(g // gb,),
        in_specs=[cols, blk(p, nbatch), blk(p, nbatch), blk(S5_GROUP, 2 * p), blk(S5_GROUP, 2 * p),
                  blk(p, nbatch), blk(p, nbatch)],
        out_specs=(cols, blk(p, nbatch), blk(p, nbatch)),
        out_shape=(jax.ShapeDtypeStruct((nbatch, D_S5), F32),
                   jax.ShapeDtypeStruct((g, p, nbatch), F32),
                   jax.ShapeDtypeStruct((g, p, nbatch), F32)),
        compiler_params=pltpu.CompilerParams(dimension_semantics=("parallel",)),
        name="s5_decode",
    )(proj_d, h0_re, h0_im, bd, cd, lr, li)


def _lru_gates(xc, wa_ref, wx_ref, ba, bx, lam):
    xcb = xc.astype(BF16)
    nblk = D_LRU // MXU_WIDTH_V7X
    r_parts, i_parts = [], []
    for k in range(nblk):
        xk = xcb[:, k * MXU_WIDTH_V7X:(k + 1) * MXU_WIDTH_V7X]
        r_parts.append(jnp.dot(xk, wa_ref[k], preferred_element_type=F32))
        i_parts.append(jnp.dot(xk, wx_ref[k], preferred_element_type=F32))
    r = jax.nn.sigmoid(jnp.concatenate(r_parts, axis=1) + ba)
    i = jax.nn.sigmoid(jnp.concatenate(i_parts, axis=1) + bx)
    z = -lam
    softplus = jnp.maximum(z, 0.0) + jnp.log1p(jnp.exp(-jnp.abs(z)))
    log_a = (-LRU_C * softplus) * r
    a = jnp.exp(log_a)
    v = -jnp.tanh(log_a) * (a * a + 1.0)
    mult = jnp.where(v > 0.0, v * lax.rsqrt(v), 0.0)
    return a, mult * (i * xc)


def _lru_tile(xl_ref, gate_ref, cw_ref, cb_ref, wa_ref, wx_ref, ba_ref, bx_ref, lam_ref,
              o_ref, xbuf_ref, carry_ref):
    tt = xl_ref.shape[0]
    x = xl_ref[...]
    xbuf_ref[8:8 + tt, :] = x
    cw = cw_ref[...]
    xc = (cb_ref[...] + xbuf_ref[5:5 + tt, :] * cw[0:1] + xbuf_ref[6:6 + tt, :] * cw[1:2]
          + xbuf_ref[7:7 + tt, :] * cw[2:3] + x * cw[3:4])
    xbuf_ref[0:8, :] = x[tt - 8:tt, :]

    a, b = _lru_gates(xc, wa_ref, wx_ref, ba_ref[...], bx_ref[...], lam_ref[...])

    nblk = tt // 8
    a3 = a.reshape(nblk, 8, D_LRU)
    b3 = b.reshape(nblk, 8, D_LRU)
    row = lax.broadcasted_iota(jnp.int32, (nblk, 8, D_LRU), 1)
    for sh in (1, 2, 4):
        keep = row >= sh
        bs = jnp.where(keep, pltpu.roll(b3, sh, axis=1), 0.0)
        sa = jnp.where(keep, pltpu.roll(a3, sh, axis=1), 1.0)
        b3 = b3 + a3 * bs
        a3 = a3 * sa
    carry = carry_ref[0:1, :]
    gate = gate_ref[...]
    for k in range(nblk):
        h = b3[k] + a3[k] * carry
        carry = h[7:8, :]
        o_ref[k * 8:(k + 1) * 8, :] = h * gate[k * 8:(k + 1) * 8, :]
    carry_ref[...] = jnp.broadcast_to(carry, (8, D_LRU))
    return carry


def _mix_s5_part(ys, u, x, dsk_ref, wg_ref, bg_ref, gs_ref, wo_ref):
    yy = ys + dsk_ref[...] * u
    g = _gelu(yy)
    z = jnp.dot(g.astype(BF16), wg_ref[...], preferred_element_type=F32) + bg_ref[...]
    s5o = g * jax.nn.sigmoid(z)
    n1 = _rms(s5o, gs_ref[...]).astype(BF16)
    return x + jnp.dot(n1, wo_ref[0:D_S5, :], preferred_element_type=F32)


def _mix_lru_part(lru, gl_ref, wo_ref):
    n2 = _rms(lru, gl_ref[...]).astype(BF16)
    return jnp.dot(n2, wo_ref[D_S5:, :], preferred_element_type=F32)


def _lru_mix_body(xl_ref, gate_ref, ys_ref, u_ref, x_ref,
                  cw_ref, cb_ref, wa_ref, wx_ref, ba_ref, bx_ref, lam_ref,
                  dsk_ref, wg_ref, bg_ref, gs_ref, gl_ref, wo_ref,
                  o_ref, hl_ref, xbuf_ref, carry_ref, lru_ref):
    @pl.when(pl.program_id(1) == 0)
    def _():
        xbuf_ref[0:8, :] = jnp.zeros((8, D_LRU), F32)
        carry_ref[...] = jnp.zeros((8, D_LRU), F32)

    o_ref[...] = _mix_s5_part(ys_ref[...], u_ref[...], x_ref[...],
                              dsk_ref, wg_ref, bg_ref, gs_ref, wo_ref)
    hl_ref[0] = _lru_tile(xl_ref, gate_ref, cw_ref, cb_ref, wa_ref, wx_ref, ba_ref, bx_ref, lam_ref,
                          lru_ref, xbuf_ref, carry_ref)
    o_ref[...] += _mix_lru_part(lru_ref[...], gl_ref, wo_ref)


def _lru_mix_prompt(proj_p, ys_p, x1_p, cw, cb, wa, wx, ba, bx, lam, dsk, wg, bg, gs, gl, wo,
                    *, nb, seq, tt=LRU_TT):
    nt = seq // tt
    rows = lambda c, col: pl.BlockSpec((tt, c), lambda b, t: (b * nt + t, col))
    once = lambda shape: pl.BlockSpec(shape, lambda b, t: (0,) * len(shape),
                                      pipeline_mode=pl.Buffered(1))
    return pl.pallas_call(
        _lru_mix_body,
        grid=(nb, nt),
        in_specs=[
            rows(D_LRU, 1), rows(D_LRU, 2),
            rows(D_S5, 0), rows(D_S5, 0), rows(D_MODEL, 0),
            once((CONV_W, D_LRU)), once((1, D_LRU)),
            once((D_LRU // 256, 256, 256)), once((D_LRU // 256, 256, 256)),
            once((1, D_LRU)), once((1, D_LRU)), once((1, D_LRU)),
            once((1, D_S5)), once((D_S5, D_S5)), once((1, D_S5)), once((1, D_S5)), once((1, D_LRU)),
            once((D_MODEL, D_MODEL)),
        ],
        out_specs=(pl.BlockSpec((tt, D_MODEL), lambda b, t: (b * nt + t, 0)),
                   pl.BlockSpec((1, 1, D_LRU), lambda b, t: (b, 0, 0))),
        out_shape=(jax.ShapeDtypeStruct((nb * seq, D_MODEL), F32),
                   jax.ShapeDtypeStruct((nb, 1, D_LRU), F32)),
        scratch_shapes=[pltpu.VMEM((tt + 8, D_LRU), F32), pltpu.VMEM((8, D_LRU), F32),
                        pltpu.VMEM((tt, D_LRU), F32)],
        compiler_params=pltpu.CompilerParams(
            dimension_semantics=("parallel", "arbitrary"), vmem_limit_bytes=VMEM_LIMIT),
        name="lru_mix_prompt",
    )(proj_p, proj_p, ys_p, proj_p, x1_p, cw, cb, wa, wx, ba, bx, lam, dsk, wg, bg, gs, gl, wo)


def _lru_decode_body(xl_ref, gate_ref, conv_ref, h0_ref, cw_ref, cb_ref,
                     wa_ref, wx_ref, ba_ref, bx_ref, lam_ref, o_ref, h_ref, buf_ref):
    x = xl_ref[...]
    cw = cw_ref[...]
    c0, c1, c2 = (conv_ref[k] for k in range(CONV_W - 1))
    xc = cb_ref[...] + c0 * cw[0:1] + c1 * cw[1:2] + c2 * cw[2:3] + x * cw[3:4]
    a, b = _lru_gates(xc, wa_ref, wx_ref, ba_ref[...], bx_ref[...], lam_ref[...])
    h = a * h0_ref[...] + b
    h_ref[...] = h
    o_ref[...] = h * gate_ref[...]
    for k, rows in enumerate((c1, c2, x)):
        buf_ref[k] = rows


def _lru_decode(proj_d, conv0, h0, cw, cb, wa, wx, ba, bx, lam):
    nd = proj_d.shape[0]
    full = lambda r: pl.BlockSpec((r, D_LRU), lambda i: (0, 0))
    conv = pl.BlockSpec((CONV_W - 1, nd, D_LRU), lambda i: (0, 0, 0))
    wspec = pl.BlockSpec((D_LRU // 256, 256, 256), lambda i: (0, 0, 0))
    return pl.pallas_call(
        _lru_decode_body,
        grid=(1,),
        in_specs=[
            pl.BlockSpec((nd, D_LRU), lambda i: (0, 1)),
            pl.BlockSpec((nd, D_LRU), lambda i: (0, 2)),
            conv, full(nd),
            full(CONV_W), full(1), wspec, wspec, full(1), full(1), full(1),
        ],
        out_specs=(full(nd), full(nd), conv),
        out_shape=(jax.ShapeDtypeStruct((nd, D_LRU), F32),
                   jax.ShapeDtypeStruct((nd, D_LRU), F32),
                   jax.ShapeDtypeStruct((CONV_W - 1, nd, D_LRU), F32)),
        name="lru_decode",
    )(proj_d, proj_d, conv0, h0, cw, cb, wa, wx, ba, bx, lam)


def _mix_decode_body(ys_ref, u_ref, lru_ref, x_ref, dsk_ref, wg_ref, bg_ref, gs_ref, gl_ref,
                     wo_ref, o_ref):
    o_ref[...] = (_mix_s5_part(ys_ref[...], u_ref[...], x_ref[...],
                               dsk_ref, wg_ref, bg_ref, gs_ref, wo_ref)
                  + _mix_lru_part(lru_ref[...], gl_ref, wo_ref))


def _mix_decode(ys_d, proj_d, lru_d, x1_d, dsk, wg, bg, gs, gl, wo):
    nd = x1_d.shape[0]
    full = lambda r, c: pl.BlockSpec((r, c), lambda i: (0, 0))
    return pl.pallas_call(
        _mix_decode_body,
        grid=(1,),
        in_specs=[full(nd, D_S5), full(nd, D_S5), full(nd, D_LRU), full(nd, D_MODEL),
                  full(1, D_S5), full(D_S5, D_S5), full(1, D_S5), full(1, D_S5), full(1, D_LRU),
                  full(D_MODEL, D_MODEL)],
        out_specs=full(nd, D_MODEL),
        out_shape=jax.ShapeDtypeStruct((nd, D_MODEL), F32),
        compiler_params=pltpu.CompilerParams(vmem_limit_bytes=VMEM_LIMIT),
        name="mix_decode",
    )(ys_d, proj_d, lru_d, x1_d, dsk, wg, bg, gs, gl, wo)


def _unpair(h, nb):
    tiles = S5_GROUPS // GROUPS_PER_TILE
    h5 = h.reshape(tiles, PAIRS_PER_TILE, nb, 2, S5_STATE)
    return jnp.transpose(h5, (2, 0, 3, 1, 4)).reshape(nb, S5_GROUPS, S5_STATE)


def _block_diag4(w):
    w4 = w.reshape(LRU_HEADS // 4, 4, LRU_HEAD_DIM, LRU_HEAD_DIM)
    eye = jnp.eye(4, dtype=w.dtype)
    return jnp.einsum("kaij,ab->kaibj", w4, eye).reshape(LRU_HEADS // 4, 256, 256)


def kernel(x_prompt, x_sample, state_s5_re, state_s5_im, state_lru_h, state_lru_conv, g_ffn1, w1_a, w3_a, w2_a, g_mix, w_in, lam_re, lam_im, log_dt, b_re, b_im, c_re, c_im, d_skip, w_glu, b_glu, conv_w, conv_b, w_a, b_a, w_x, b_x, lam_l, g_out_s5, g_out_lru, w_out, g_ffn2, w1_b, w3_b, w2_b, g_final):
    nb, seq, _ = x_prompt.shape
    nd = x_sample.shape[0]
    n_p = nb * seq
    row = lambda v: v.reshape(1, -1)
    assert w1_a.shape[0] == 1 and x_sample.shape[1] == 1
    assert nd == 128 and n_p % FFN_TM == 0 and n_p > FFN_TM
    assert seq % LRU_TT == 0 and seq % (8 * CHUNK) == 0

    xp = x_prompt.reshape(n_p, D_MODEL)
    xd = x_sample.reshape(nd, D_MODEL)

    n_i, n_f = n_p // FFN_TM, D_FF // FFN_TF
    x1_head, x1_d, w1_a16, w3_a16, w2_a16 = _ffn(
        xp, xd, row(g_ffn1[0]), w1_a[0], w3_a[0], w2_a[0], n_tiles=1)
    x1_p, w1_b16, w3_b16, w2_b16, w_in16 = _ffn(
        xp, None, row(g_ffn1[0]), w1_a16, w3_a16, w2_a16, head=x1_head,
        casts=(_cast_job(w1_b[0], n_i, n_f), _cast_job(w3_b[0], n_i, n_f),
               _row_cast_job(w2_b[0], n_i, n_f), _walk_cast_job(w_in[0], n_i, n_f)))
    nj = D_IN // PROJ_TN
    proj_p, proj_d, w_out16, w_glu16 = _inproj(
        x1_p, x1_d, row(g_mix[0]), w_in16,
        casts=(_cast_job(w_out[0], n_i, nj, bc=PROJ_TN), _cast_job(w_glu[0], n_i, nj, bc=PROJ_TN)))

    lmat, cpre, cpim, bd, cd, ar, ai, lr, li = _s5_prep(
        lam_re[0], lam_im[0], log_dt[0], b_re[0], b_im[0], c_re[0], c_im[0])
    ys_p, hf_re, hf_im = _s5_prompt(proj_p, lmat, cpre, cpim, ar, ai, nb=nb, seq=seq)

    to_gpb = lambda s: jnp.transpose(s, (1, 2, 0))
    ys_d, hd_re, hd_im = _s5_decode(proj_d, to_gpb(state_s5_re[0]), to_gpb(state_s5_im[0]),
                                    bd, cd, lr, li)

    wa_bd = _block_diag4(w_a[0]).astype(BF16)
    wx_bd = _block_diag4(w_x[0]).astype(BF16)
    lru_args = (conv_w[0], row(conv_b[0]), wa_bd, wx_bd, row(b_a[0]), row(b_x[0]), row(lam_l[0]))
    lru_d, hl_d, buf_d = _lru_decode(proj_d, jnp.transpose(state_lru_conv[0], (1, 0, 2)),
                                     state_lru_h[0], *lru_args)

    mix_args = (row(d_skip[0]), w_glu16, row(b_glu[0]), row(g_out_s5[0]), row(g_out_lru[0]), w_out16)
    x2_p, hl_p = _lru_mix_prompt(proj_p, ys_p, x1_p, *lru_args, *mix_args, nb=nb, seq=seq)
    x2_d = _mix_decode(ys_d, proj_d, lru_d, x1_d, *mix_args)
    y_p, y_d = _ffn(x2_p, x2_d, row(g_ffn2[0]), w1_b16, w3_b16, w2_b16, row(g_final))

    tail_p = proj_p.reshape(nb, seq, -1)[:, seq - (CONV_W - 1):, D_S5:D_S5 + D_LRU]
    return (
        y_p.reshape(nb, seq, D_MODEL),
        y_d.reshape(nd, 1, D_MODEL),
        _unpair(hf_re, nb)[None],
        _unpair(hf_im, nb)[None],
        hl_p.reshape(1, nb, D_LRU),
        tail_p[None],
        jnp.transpose(hd_re, (2, 0, 1))[None],
        jnp.transpose(hd_im, (2, 0, 1))[None],
        hl_d[None],
        jnp.transpose(buf_d, (1, 0, 2))[None],
    )
```
